```python
import numpy as np
import jax
import jax.numpy as jnp
from jax import lax

D_MODEL = 2048
BATCH = 2
SEQ = 8192
DEPTH = 2

GRID_W = 64
CTX_LEN = 256
EPS = 1e-6
N_BRANCH = 3
N_MOD = 6

N_HEADS = 16
N_KV_HEADS = 4
HEAD_DIM = 128
GROUP = N_HEADS // N_KV_HEADS
ROPE_PAIRS = HEAD_DIM // 4
ROPE_THETA = 10000.0
Q_BLOCK = 128
Q_WIDTH = N_HEADS * HEAD_DIM
KV_WIDTH = N_KV_HEADS * HEAD_DIM

LRU_WIDTH = D_MODEL
LRU_BLOCKS = 16
LRU_BLOCK = LRU_WIDTH // LRU_BLOCKS
LRU_CONV = 4
LRU_C = 8.0

SC_WIDTH = D_MODEL
SC_CONV = 3

PEER_HEADS = 8
PEER_NKEYS = 128
PEER_EXPERTS = PEER_NKEYS * PEER_NKEYS
PEER_QDIM = 256
PEER_TOPK = 16
PEER_BLOCK = 64

IN_SPLITS = (Q_WIDTH, KV_WIDTH, KV_WIDTH, LRU_WIDTH, LRU_WIDTH, SC_WIDTH, SC_WIDTH, SC_WIDTH, N_BRANCH * D_MODEL)
IN_WIDTH = Q_WIDTH + 2 * KV_WIDTH + 2 * LRU_WIDTH + 3 * SC_WIDTH + N_BRANCH * D_MODEL

kernel_name = 'hybrid_dit_gqa_rglru_shortconv_peer'


def rmsnorm(x, g):
    xf = x.astype(jnp.float32)
    y = xf * lax.rsqrt(jnp.mean(xf * xf, axis=-1, keepdims=True) + EPS)
    return (y * g.astype(jnp.float32)).astype(x.dtype)


def modulate(h, shift, scale):
    return h * (1.0 + scale) + shift


def split_cols(p):
    offsets = np.cumsum(IN_SPLITS)[:-1].tolist()
    return jnp.split(p, offsets, axis=-1)


def dwconv(x, w, b, left):
    k_w = w.shape[0]
    t = x.shape[1]
    xp = jnp.pad(x, ((0, 0), (left, k_w - 1 - left), (0, 0)))
    return b + sum(w[k] * xp[:, k:k + t] for k in range(k_w))


def axial_rope(rows):
    row = jnp.repeat(jnp.arange(rows, dtype=jnp.float32), GRID_W)
    col = jnp.tile(jnp.arange(GRID_W, dtype=jnp.float32), rows)
    inv = ROPE_THETA ** (-jnp.arange(ROPE_PAIRS, dtype=jnp.float32) / ROPE_PAIRS)
    ang = jnp.stack([row[:, None] * inv, col[:, None] * inv], axis=1)
    return jnp.cos(ang), jnp.sin(ang)


def apply_rope(t, cos, sin):
    b_, n_, h_, d_ = t.shape
    t = t.reshape(b_, n_, h_, 2, 2, ROPE_PAIRS)
    t1, t2 = t[..., 0, :], t[..., 1, :]
    cs = cos[:, None].astype(t.dtype)
    sn = sin[:, None].astype(t.dtype)
    out = jnp.stack([t1 * cs - t2 * sn, t2 * cs + t1 * sn], axis=-2)
    return out.reshape(b_, n_, h_, d_)


def attend(q, k, v):
    s = jnp.einsum('bqhgd,bkhd->bhgqk', q, k, preferred_element_type=jnp.float32) * (HEAD_DIM ** -0.5)
    p = jax.nn.softmax(s, axis=-1).astype(v.dtype)
    return jnp.einsum('bhgqk,bkhd->bqhgd', p, v)


def attention_branch(qx, kx, vx, qc, kc, vc, q_gain, k_gain, cos, sin, with_ctx):
    b_, t_, _ = qx.shape
    tc = kc.shape[1]
    ql = apply_rope(rmsnorm(qx.reshape(b_, t_, N_HEADS, HEAD_DIM), q_gain), cos, sin)
    kl = apply_rope(rmsnorm(kx.reshape(b_, t_, N_KV_HEADS, HEAD_DIM), k_gain), cos, sin)
    vl = vx.reshape(b_, t_, N_KV_HEADS, HEAD_DIM)
    kcn = rmsnorm(kc.reshape(b_, tc, N_KV_HEADS, HEAD_DIM), k_gain)
    vcn = vc.reshape(b_, tc, N_KV_HEADS, HEAD_DIM)
    k_all = jnp.concatenate([kcn, kl], axis=1)
    v_all = jnp.concatenate([vcn, vl], axis=1)
    n_blk = t_ // Q_BLOCK
    qb = ql.reshape(b_, n_blk, Q_BLOCK, N_KV_HEADS, GROUP, HEAD_DIM).transpose(1, 0, 2, 3, 4, 5)
    ob = lax.map(lambda q: attend(q, k_all, v_all), qb)
    y_l = ob.transpose(1, 0, 2, 3, 4, 5).reshape(b_, t_, Q_WIDTH)
    y_c = None
    if with_ctx:
        qcn = rmsnorm(qc.reshape(b_, tc, N_HEADS, HEAD_DIM), q_gain)
        y_c = attend(qcn.reshape(b_, tc, N_KV_HEADS, GROUP, HEAD_DIM), kcn, vcn).reshape(b_, tc, Q_WIDTH)
    return y_l, y_c


def _lin_comb(left, right):
    a_l, b_l = left
    a_r, b_r = right
    return a_l * a_r, a_r * b_l + b_r


def rglru_scan(u, wa, ba, wx, bx, lam, h0, reverse):
    b_, t_, c_ = u.shape
    uf = u.astype(jnp.float32)
    ub = uf.reshape(b_, t_, LRU_BLOCKS, LRU_BLOCK)
    r = jax.nn.sigmoid(jnp.einsum('btnc,ncd->btnd', ub, wa.astype(jnp.float32)).reshape(b_, t_, c_) + ba.astype(jnp.float32))
    i = jax.nn.sigmoid(jnp.einsum('btnc,ncd->btnd', ub, wx.astype(jnp.float32)).reshape(b_, t_, c_) + bx.astype(jnp.float32))
    log_a = -LRU_C * r * jax.nn.softplus(-lam.astype(jnp.float32))
    a = jnp.exp(log_a)
    drive = jnp.sqrt(-jnp.expm1(2.0 * log_a)) * (i * uf)
    edge = t_ - 1 if reverse else 0
    drive = drive.at[:, edge].add(a[:, edge] * h0)
    _, h = lax.associative_scan(_lin_comb, (a, drive), axis=1, reverse=reverse)
    return h


def lru_branch(xl, gl, xc, gc, conv_w, conv_b, wa, ba, wx, bx, lam, with_ctx):
    left = LRU_CONV // 2
    ul = dwconv(xl, conv_w, conv_b, left)
    uc = dwconv(xc, conv_w, conv_b, left)
    h_lat = 0.0
    h_ctx = 0.0
    for d, rev in enumerate((False, True)):
        hc_d = rglru_scan(uc, wa[d], ba[d], wx[d], bx[d], lam[d], jnp.zeros((xc.shape[0], LRU_WIDTH), jnp.float32), rev)
        h0 = hc_d[:, 0] if rev else hc_d[:, -1]
        h_lat = h_lat + rglru_scan(ul, wa[d], ba[d], wx[d], bx[d], lam[d], h0, rev)
        if with_ctx:
            h_ctx = h_ctx + hc_d
    y_l = h_lat.astype(xl.dtype) * jax.nn.gelu(gl)
    y_c = h_ctx.astype(xc.dtype) * jax.nn.gelu(gc) if with_ctx else None
    return y_l, y_c


def short_conv(bg, cg, u, w, b):
    return bg * dwconv(cg * u, w, b, SC_CONV // 2)


def merge_branches(att, lru, sc, gate_logits, w_o_attn, w_o_lru, w_o_sc, w_out):
    g_att, g_lru, g_sc = jnp.split(jax.nn.sigmoid(gate_logits), N_BRANCH, axis=-1)
    y = g_att * (att @ w_o_attn) + g_lru * (lru @ w_o_lru) + g_sc * (sc @ w_o_sc)
    return y @ w_out


def peer_ffn(h, w_q, k1, k2, w_u, w_v):
    b_, t_, d_ = h.shape
    n_blk = t_ // PEER_BLOCK
    hb = h.reshape(b_, n_blk, PEER_BLOCK, d_).transpose(1, 0, 2, 3)

    def block(hblk):
        q = (hblk @ w_q).reshape(b_, PEER_BLOCK, PEER_HEADS, 2, PEER_QDIM // 2)
        s1 = jnp.einsum('bthd,hnd->bthn', q[..., 0, :], k1, preferred_element_type=jnp.float32)
        s2 = jnp.einsum('bthd,hnd->bthn', q[..., 1, :], k2, preferred_element_type=jnp.float32)
        v1, i1 = lax.top_k(s1, PEER_TOPK)
        v2, i2 = lax.top_k(s2, PEER_TOPK)
        cand = (v1[..., :, None] + v2[..., None, :]).reshape(b_, PEER_BLOCK, PEER_HEADS, PEER_TOPK * PEER_TOPK)
        cidx = (i1[..., :, None] * PEER_NKEYS + i2[..., None, :]).reshape(b_, PEER_BLOCK, PEER_HEADS, PEER_TOPK * PEER_TOPK)
        score, pos = lax.top_k(cand, PEER_TOPK)
        idx = jnp.take_along_axis(cidx, pos, axis=-1)
        gate = jax.nn.softmax(score, axis=-1)
        u = w_u[idx]
        act = jax.nn.gelu(jnp.einsum('bthkd,btd->bthk', u, hblk))
        return jnp.einsum('bthk,bthkd->btd', (gate * act).astype(h.dtype), w_v[idx])

    out = lax.map(block, hb)
    return out.transpose(1, 0, 2, 3).reshape(b_, t_, d_)


def setup_inputs(seed: int = 0) -> dict:
    key = jax.random.key(seed)
    ks = iter(jax.random.split(key, 40))

    def nrm(shape, scale):
        return scale * jax.random.normal(next(ks), shape, jnp.float32)

    u = jax.random.uniform(next(ks), (DEPTH, 2, LRU_WIDTH), jnp.float32, minval=0.9, maxval=0.999)
    p = u ** (1.0 / LRU_C)
    lam = jnp.log(p) - jnp.log1p(-p)
    return {
        'x': nrm((BATCH, SEQ, D_MODEL), 1.0),
        'c': nrm((BATCH, D_MODEL), 1.0),
        'ctx': nrm((BATCH, CTX_LEN, D_MODEL), 1.0),
        'c_ctx': nrm((D_MODEL,), 1.0),
        'w_mod': nrm((DEPTH, D_MODEL, N_MOD * D_MODEL), 0.5 * D_MODEL ** -0.5),
        'b_mod': nrm((DEPTH, N_MOD * D_MODEL), 0.01),
        'norm_mix': 1.0 + nrm((DEPTH, D_MODEL), 0.02),
        'norm_ffn': 1.0 + nrm((DEPTH, D_MODEL), 0.02),
        'w_in': nrm((DEPTH, D_MODEL, IN_WIDTH), D_MODEL ** -0.5),
        'q_norm': 1.0 + nrm((DEPTH, HEAD_DIM), 0.02),
        'k_norm': 1.0 + nrm((DEPTH, HEAD_DIM), 0.02),
        'lru_conv_w': nrm((DEPTH, LRU_CONV, LRU_WIDTH), LRU_CONV ** -0.5),
        'lru_conv_b': nrm((DEPTH, LRU_WIDTH), 0.01),
        'lru_wa': nrm((DEPTH, 2, LRU_BLOCKS, LRU_BLOCK, LRU_BLOCK), LRU_BLOCK ** -0.5),
        'lru_ba': nrm((DEPTH, 2, LRU_WIDTH), 0.01),
        'lru_wx': nrm((DEPTH, 2, LRU_BLOCKS, LRU_BLOCK, LRU_BLOCK), LRU_BLOCK ** -0.5),
        'lru_bx': nrm((DEPTH, 2, LRU_WIDTH), 0.01),
        'lru_lambda': lam,
        'sc_conv_w': nrm((DEPTH, SC_CONV, SC_WIDTH), SC_CONV ** -0.5),
        'sc_conv_b': nrm((DEPTH, SC_WIDTH), 0.01),
        'w_o_attn': nrm((DEPTH, Q_WIDTH, D_MODEL), Q_WIDTH ** -0.5),
        'w_o_lru': nrm((DEPTH, LRU_WIDTH, D_MODEL), LRU_WIDTH ** -0.5),
        'w_o_sc': nrm((DEPTH, SC_WIDTH, D_MODEL), SC_WIDTH ** -0.5),
        'w_out': nrm((DEPTH, D_MODEL, D_MODEL), D_MODEL ** -0.5),
        'peer_wq': nrm((DEPTH, D_MODEL, PEER_HEADS * PEER_QDIM), D_MODEL ** -0.5),
        'peer_k1': nrm((DEPTH, PEER_HEADS, PEER_NKEYS, PEER_QDIM // 2), (PEER_QDIM // 2) ** -0.5),
        'peer_k2': nrm((DEPTH, PEER_HEADS, PEER_NKEYS, PEER_QDIM // 2), (PEER_QDIM // 2) ** -0.5),
        'peer_u': nrm((DEPTH, PEER_EXPERTS, D_MODEL), D_MODEL ** -0.5),
        'peer_v': nrm((DEPTH, PEER_EXPERTS, D_MODEL), 0.5),
    }


def reference(x, c, ctx, c_ctx, w_mod, b_mod, norm_mix, norm_ffn, w_in, q_norm, k_norm,
              lru_conv_w, lru_conv_b, lru_wa, lru_ba, lru_wx, lru_bx, lru_lambda,
              sc_conv_w, sc_conv_b, w_o_attn, w_o_lru, w_o_sc, w_out,
              peer_wq, peer_k1, peer_k2, peer_u, peer_v):
    rows = x.shape[1] // GRID_W
    cos, sin = axial_rope(rows)
    s_lat = jax.nn.silu(c)
    s_ctx = jax.nn.silu(c_ctx)
    for l in range(DEPTH):
        with_ctx = l < DEPTH - 1
        m_lat = jnp.split((s_lat @ w_mod[l] + b_mod[l])[:, None, :], N_MOD, axis=-1)
        m_ctx = jnp.split(s_ctx @ w_mod[l] + b_mod[l], N_MOD, axis=-1)

        hx = modulate(rmsnorm(x, norm_mix[l]), m_lat[0], m_lat[1])
        hc = modulate(rmsnorm(ctx, norm_mix[l]), m_ctx[0], m_ctx[1])
        qx, kx, vx, lx, lgx, bgx, cgx, ux, gx = split_cols(hx @ w_in[l])
        qc, kc, vc, lc, lgc, bgc, cgc, uc, gc = split_cols(hc @ w_in[l])
        att_x, att_c = attention_branch(qx, kx, vx, qc, kc, vc, q_norm[l], k_norm[l], cos, sin, with_ctx)
        lru_x, lru_c = lru_branch(lx, lgx, lc, lgc, lru_conv_w[l], lru_conv_b[l], lru_wa[l], lru_ba[l],
                                  lru_wx[l], lru_bx[l], lru_lambda[l], with_ctx)
        sc_x = short_conv(bgx, cgx, ux, sc_conv_w[l], sc_conv_b[l])
        x = x + m_lat[2] * merge_branches(att_x, lru_x, sc_x, gx, w_o_attn[l], w_o_lru[l], w_o_sc[l], w_out[l])
        if with_ctx:
            sc_c = short_conv(bgc, cgc, uc, sc_conv_w[l], sc_conv_b[l])
            ctx = ctx + m_ctx[2] * merge_branches(att_c, lru_c, sc_c, gc, w_o_attn[l], w_o_lru[l], w_o_sc[l], w_out[l])

        hx = modulate(rmsnorm(x, norm_ffn[l]), m_lat[3], m_lat[4])
        x = x + m_lat[5] * peer_ffn(hx, peer_wq[l], peer_k1[l], peer_k2[l], peer_u[l], peer_v[l])
        if with_ctx:
            hc = modulate(rmsnorm(ctx, norm_ffn[l]), m_ctx[3], m_ctx[4])
            ctx = ctx + m_ctx[5] * peer_ffn(hc, peer_wq[l], peer_k1[l], peer_k2[l], peer_u[l], peer_v[l])
    return x
```

```python
import functools
import math

import numpy as np
import jax
import jax.numpy as jnp
from jax import lax
from jax.experimental import pallas as pl
from jax.experimental.pallas import tpu as pltpu

F32 = jnp.float32
BF16 = jnp.bfloat16

GRID_W = 64
EPS = 1e-6
N_MOD = 6

N_HEADS = 16
N_KV_HEADS = 4
HEAD_DIM = 128
GROUP = N_HEADS // N_KV_HEADS
ROPE_PAIRS = HEAD_DIM // 4
ROPE_THETA = 10000.0

LRU_BLOCKS = 16
LRU_BLOCK = 128
LRU_CONV = 4
LRU_C = 8.0
SC_CONV = 3

PEER_HEADS = 8
PEER_NKEYS = 128
PEER_TOPK = 16

LANES = 128
SUBLANES = 8
HALO = SUBLANES
NEG_INF = float("-inf")
LOG2E = 1.4426950408889634
GELU_C = math.sqrt(2.0 / math.pi)


def _tile(n, target, mult):
    best = None
    for t in range(mult, min(n, target) + 1, mult):
        if n % t == 0:
            best = t
    assert best is not None, (n, target, mult)
    return best


def _cparams(sem, vmem_mib):
    return pltpu.CompilerParams(dimension_semantics=sem, vmem_limit_bytes=vmem_mib << 20)


def _gelu(x):
    return 0.5 * x * (1.0 + jnp.tanh(GELU_C * (x + 0.044715 * (x * x * x))))


def _row_select(mod_ref, b, is_ctx):
    return jnp.where(is_ctx, mod_ref[2:3, :], mod_ref[pl.ds(b, 1), :])


def _mod_kernel(s_ref, w_ref, b_ref, o_ref):
    s = s_ref[...]
    s = s * jax.nn.sigmoid(s)
    o_ref[0] = jnp.dot(s, w_ref[0], preferred_element_type=F32,
                       precision=lax.Precision.HIGHEST) + b_ref[0]


def _mod_call(s8, w_mod, b_mod):
    depth, d, n = w_mod.shape
    tn = _tile(n, 1024, LANES)
    return pl.pallas_call(
        _mod_kernel,
        grid=(depth, n // tn),
        in_specs=[pl.BlockSpec((8, d), lambda l, j: (0, 0)),
                  pl.BlockSpec((1, d, tn), lambda l, j: (l, 0, j)),
                  pl.BlockSpec((1, 1, tn), lambda l, j: (l, 0, j))],
        out_specs=pl.BlockSpec((1, 8, tn), lambda l, j: (l, 0, j)),
        out_shape=jax.ShapeDtypeStruct((depth, 8, n), F32),
        compiler_params=_cparams(("parallel", "parallel"), 32),
        name="mod",
    )(s8, w_mod, b_mod.reshape(depth, 1, n))


def _inproj_kernel(x_ref, shift_ref, scale_ref, g_ref, w_ref, o_ref, h_scr, *, tm, ctx_len):
    b = pl.program_id(0)
    i = pl.program_id(1)

    @pl.when(pl.program_id(2) == 0)
    def _():
        x = x_ref[0]
        var = jnp.mean(x * x, axis=-1, keepdims=True)
        y = x * lax.rsqrt(var + EPS) * g_ref[...]
        row = i * tm + lax.broadcasted_iota(jnp.int32, (tm, 1), 0)
        is_ctx = row < ctx_len
        sh = _row_select(shift_ref, b, is_ctx)
        sc = _row_select(scale_ref, b, is_ctx)
        h_scr[...] = (y * (1.0 + sc) + sh).astype(BF16)

    o_ref[0] = jnp.dot(h_scr[...], w_ref[...], preferred_element_type=F32).astype(BF16)


def _inproj_call(x, mod_l, gain, w_bf16, ctx_len):
    bsz, l, d = x.shape
    n = w_bf16.shape[1]
    tm = _tile(l, 704, 16)
    tn = 512
    kern = functools.partial(_inproj_kernel, tm=tm, ctx_len=ctx_len)
    return pl.pallas_call(
        kern,
        grid=(bsz, l // tm, n // tn),
        in_specs=[pl.BlockSpec((1, tm, d), lambda b, i, j: (b, i, 0)),
                  pl.BlockSpec((8, d), lambda b, i, j: (0, 0)),
                  pl.BlockSpec((8, d), lambda b, i, j: (0, 1)),
                  pl.BlockSpec((1, d), lambda b, i, j: (0, 0)),
                  pl.BlockSpec((d, tn), lambda b, i, j: (0, j))],
        out_specs=pl.BlockSpec((1, tm, tn), lambda b, i, j: (b, i, j)),
        out_shape=jax.ShapeDtypeStruct((bsz, l, n), BF16),
        scratch_shapes=[pltpu.VMEM((tm, d), BF16)],
        compiler_params=_cparams(("parallel", "parallel", "arbitrary"), 48),
        name="inproj",
    )(x, mod_l, mod_l, gain.reshape(1, d), w_bf16)


def _norm_rope(t, gain, cos, sin_signed, lane_lo):
    var = jnp.mean(t * t, axis=-1, keepdims=True)
    y = t * lax.rsqrt(var + EPS) * gain
    swapped = jnp.where(lane_lo, pltpu.roll(y, HEAD_DIM - ROPE_PAIRS, 1), pltpu.roll(y, ROPE_PAIRS, 1))
    return y * cos + swapped * sin_signed


def _qkprep_kernel(q_ref, k_ref, cos_ref, sin_ref, qg_ref, kg_ref, qo_ref, ko_ref, *, q_scale):
    cos = cos_ref[...]
    sin = sin_ref[...]
    lane = lax.broadcasted_iota(jnp.int32, cos.shape, 1)
    lane_lo = (lane & ROPE_PAIRS) == 0
    qg = qg_ref[...]
    kg = kg_ref[...]
    for h in range(N_HEADS):
        sl = slice(h * HEAD_DIM, (h + 1) * HEAD_DIM)
        t = q_ref[0, :, sl].astype(F32)
        qo_ref[0, :, sl] = (_norm_rope(t, qg, cos, sin, lane_lo) * q_scale).astype(BF16)
    for h in range(N_KV_HEADS):
        sl = slice(h * HEAD_DIM, (h + 1) * HEAD_DIM)
        t = k_ref[0, :, sl].astype(F32)
        ko_ref[0, :, sl] = _norm_rope(t, kg, cos, sin, lane_lo).astype(BF16)


def _qkprep_call(p, cos, sin_signed, q_gain, k_gain):
    bsz, l, _ = p.shape
    qw = N_HEADS * HEAD_DIM
    kw = N_KV_HEADS * HEAD_DIM
    tm = _tile(l, 768, 16)
    kern = functools.partial(_qkprep_kernel, q_scale=HEAD_DIM ** -0.5 * LOG2E)
    return pl.pallas_call(
        kern,
        grid=(bsz, l // tm),
        in_specs=[pl.BlockSpec((1, tm, qw), lambda b, i: (b, i, 0)),
                  pl.BlockSpec((1, tm, kw), lambda b, i: (b, i, qw // kw)),
                  pl.BlockSpec((tm, HEAD_DIM), lambda b, i: (i, 0)),
                  pl.BlockSpec((tm, HEAD_DIM), lambda b, i: (i, 0)),
                  pl.BlockSpec((1, HEAD_DIM), lambda b, i: (0, 0)),
                  pl.BlockSpec((1, HEAD_DIM), lambda b, i: (0, 0))],
        out_specs=[pl.BlockSpec((1, tm, qw), lambda b, i: (b, i, 0)),
                   pl.BlockSpec((1, tm, kw), lambda b, i: (b, i, 0))],
        out_shape=[jax.ShapeDtypeStruct((bsz, l, qw), BF16),
                   jax.ShapeDtypeStruct((bsz, l, kw), BF16)],
        compiler_params=_cparams(("parallel", "parallel"), 40),
        name="qkprep",
    )(p, p, cos, sin_signed, q_gain.reshape(1, HEAD_DIM), k_gain.reshape(1, HEAD_DIM))


def _attn_kernel(q_ref, k_ref, v_ref, o_ref, m_scr, l_scr, acc_scr, *, tq, tkc, ctx_len, n_lat_chunks):
    qi = pl.program_id(2)
    q = q_ref[0]
    qs = jnp.concatenate([q[:, h * HEAD_DIM:(h + 1) * HEAD_DIM] for h in range(GROUP)], axis=0)
    m_scr[...] = jnp.full(m_scr.shape, NEG_INF, F32)
    l_scr[...] = jnp.zeros(l_scr.shape, F32)
    acc_scr[...] = jnp.zeros(acc_scr.shape, F32)

    def chunk(start, size):
        k = k_ref[0, pl.ds(start, size), :]
        v = v_ref[0, pl.ds(start, size), :]
        s = lax.dot_general(qs, k, (((1,), (1,)), ((), ())), preferred_element_type=F32)
        m_old = m_scr[...]
        m_new = jnp.maximum(m_old, jnp.max(s, axis=-1, keepdims=True))
        alpha = jnp.exp2(m_old - m_new)
        p = jnp.exp2(s - m_new)
        l_scr[...] = alpha * l_scr[...] + jnp.sum(p, axis=-1, keepdims=True)
        acc_scr[...] = alpha * acc_scr[...] + jnp.dot(p.astype(BF16), v, preferred_element_type=F32)
        m_scr[...] = m_new

    chunk(0, ctx_len)
    n = jnp.where(qi < ctx_len // tq, 0, n_lat_chunks)

    def body(c, carry):
        chunk(pl.multiple_of(ctx_len + c * tkc, math.gcd(ctx_len, tkc)), tkc)
        return carry

    lax.fori_loop(0, n, body, 0)
    out = acc_scr[...] / l_scr[...]
    for h in range(GROUP):
        o_ref[0, :, h * HEAD_DIM:(h + 1) * HEAD_DIM] = out[h * tq:(h + 1) * tq, :].astype(BF16)


def _attn_call(qn, kn, p, ctx_len):
    bsz, l, qw = qn.shape
    tq = 256
    assert ctx_len % tq == 0 and l % tq == 0
    t_lat = l - ctx_len
    tkc = _tile(t_lat, 512, 256)
    gw = GROUP * HEAD_DIM
    v_col0 = (N_HEADS + N_KV_HEADS)
    kern = functools.partial(_attn_kernel, tq=tq, tkc=tkc, ctx_len=ctx_len, n_lat_chunks=t_lat // tkc)
    return pl.pallas_call(
        kern,
        grid=(bsz, N_KV_HEADS, l // tq),
        in_specs=[pl.BlockSpec((1, tq, gw), lambda b, g, i: (b, i, g)),
                  pl.BlockSpec((1, l, HEAD_DIM), lambda b, g, i: (b, 0, g)),
                  pl.BlockSpec((1, l, HEAD_DIM), lambda b, g, i: (b, 0, v_col0 + g))],
        out_specs=pl.BlockSpec((1, tq, gw), lambda b, g, i: (b, i, g)),
        out_shape=jax.ShapeDtypeStruct((bsz, l, qw), BF16),
        scratch_shapes=[pltpu.VMEM((GROUP * tq, 1), F32),
                        pltpu.VMEM((GROUP * tq, 1), F32),
                        pltpu.VMEM((GROUP * tq, HEAD_DIM), F32)],
        compiler_params=_cparams(("parallel", "parallel", "arbitrary"), 40),
        name="attn",
    )(qn, kn, p)


def _segment_bounds(row, ctx_len, seq_len):
    is_ctx = row < ctx_len
    first = jnp.where(is_ctx, 0, ctx_len)
    last = jnp.where(is_ctx, ctx_len - 1, seq_len - 1)
    return first, last


def _shift_down(x, prev, k, local):
    y = pltpu.roll(x, k, 0)
    for r in range(k):
        y = jnp.where(local == r, prev[HALO - k + r:HALO - k + r + 1, :], y)
    return y


def _shift_up(x, nxt, k, local, tt):
    y = pltpu.roll(x, tt - k, 0)
    for r in range(k):
        y = jnp.where(local == tt - k + r, nxt[r:r + 1, :], y)
    return y


def _lru_kernel(*refs, reverse, tt, ctx_len, seq_len):
    if reverse:
        (x_ref, xp_ref, xn_ref, cw_ref, cb_ref, wa_ref, ba_ref, wx_ref, bx_ref, lam_ref,
         hf_ref, g_ref, o_ref, carry_scr) = refs
    else:
        (x_ref, xp_ref, xn_ref, cw_ref, cb_ref, wa_ref, ba_ref, wx_ref, bx_ref, lam_ref,
         o_ref, carry_scr) = refs
    s = pl.program_id(2)
    nt = pl.num_programs(2)
    ti = jnp.where(s == 0, 0, nt - s) if reverse else s

    @pl.when(s == 0)
    def _():
        carry_scr[...] = jnp.zeros(carry_scr.shape, F32)

    x = x_ref[0].astype(F32)
    xp = xp_ref[0].astype(F32)
    xn = xn_ref[0].astype(F32)
    local = lax.broadcasted_iota(jnp.int32, (tt, 1), 0)
    row = ti * tt + local
    first, last = _segment_bounds(row, ctx_len, seq_len)
    cw = cw_ref[...]
    u = cb_ref[...] + cw[2:3, :] * x
    u = u + cw[0:1, :] * jnp.where(row - 2 >= first, _shift_down(x, xp, 2, local), 0.0)
    u = u + cw[1:2, :] * jnp.where(row - 1 >= first, _shift_down(x, xp, 1, local), 0.0)
    u = u + cw[3:4, :] * jnp.where(row + 1 <= last, _shift_up(x, xn, 1, local, tt), 0.0)

    ub = u.astype(BF16)
    r = jax.nn.sigmoid(jnp.dot(ub, wa_ref[0], preferred_element_type=F32) + ba_ref[...])
    i = jax.nn.sigmoid(jnp.dot(ub, wx_ref[0], preferred_element_type=F32) + bx_ref[...])
    nlam = -lam_ref[...]
    softplus = jnp.maximum(nlam, 0.0) + jnp.log1p(jnp.exp(-jnp.abs(nlam)))
    log_a = (-LRU_C) * r * softplus
    a = jnp.exp(log_a)
    d = jnp.sqrt(1.0 - a * a) * (i * u)

    k = 1
    while k < tt:
        if reverse:
            keep = local < tt - k
            a_n = jnp.where(keep, pltpu.roll(a, tt - k, 0), 1.0)
            d_n = jnp.where(keep, pltpu.roll(d, tt - k, 0), 0.0)
        else:
            keep = local >= k
            a_n = jnp.where(keep, pltpu.roll(a, k, 0), 1.0)
            d_n = jnp.where(keep, pltpu.roll(d, k, 0), 0.0)
        d = a * d_n + d
        a = a * a_n
        k *= 2
    h = d + a * carry_scr[...]
    carry_scr[...] = h[0:1, :] if reverse else h[tt - 1:tt, :]
    if reverse:
        o_ref[0] = ((hf_ref[0] + h) * _gelu(g_ref[0].astype(F32))).astype(BF16)
    else:
        o_ref[0] = h


def _lru_call(p, conv_w, conv_b, wa, ba, wx, bx, lam, ctx_len, reverse, h_fwd=None):
    bsz, l, _ = p.shape
    c = conv_w.shape[1]
    tt = 256
    assert ctx_len % tt == 0 and l % tt == 0
    nt = l // tt
    nb = c // LRU_BLOCK
    x_col0 = (N_HEADS + 2 * N_KV_HEADS)
    g_col0 = x_col0 + nb
    hb = tt // HALO
    n_hblk = l // HALO

    def tile_of(s):
        return jnp.where(s == 0, 0, nt - s) if reverse else s

    x_map = lambda b, cb, s: (b, tile_of(s), x_col0 + cb)
    prev_map = lambda b, cb, s: (b, jnp.maximum(tile_of(s) * hb - 1, 0), x_col0 + cb)
    next_map = lambda b, cb, s: (b, jnp.minimum((tile_of(s) + 1) * hb, n_hblk - 1), x_col0 + cb)
    vec_map = lambda b, cb, s: (0, cb)
    in_specs = [pl.BlockSpec((1, tt, LRU_BLOCK), x_map),
                pl.BlockSpec((1, HALO, LRU_BLOCK), prev_map),
                pl.BlockSpec((1, HALO, LRU_BLOCK), next_map),
                pl.BlockSpec((LRU_CONV, LRU_BLOCK), vec_map),
                pl.BlockSpec((1, LRU_BLOCK), vec_map),
                pl.BlockSpec((1, LRU_BLOCK, LRU_BLOCK), lambda b, cb, s: (cb, 0, 0)),
                pl.BlockSpec((1, LRU_BLOCK), vec_map),
                pl.BlockSpec((1, LRU_BLOCK, LRU_BLOCK), lambda b, cb, s: (cb, 0, 0)),
                pl.BlockSpec((1, LRU_BLOCK), vec_map),
                pl.BlockSpec((1, LRU_BLOCK), vec_map)]
    args = [p, p, p, conv_w, conv_b.reshape(1, c), wa.astype(BF16), ba.reshape(1, c),
            wx.astype(BF16), bx.reshape(1, c), lam.reshape(1, c)]
    out_map = lambda b, cb, s: (b, tile_of(s), cb)
    if reverse:
        in_specs += [pl.BlockSpec((1, tt, LRU_BLOCK), out_map),
                     pl.BlockSpec((1, tt, LRU_BLOCK), lambda b, cb, s: (b, tile_of(s), g_col0 + cb))]
        args += [h_fwd, p]
        out_dtype = BF16
    else:
        out_dtype = F32
    kern = functools.partial(_lru_kernel, reverse=reverse, tt=tt, ctx_len=ctx_len, seq_len=l)
    return pl.pallas_call(
        kern,
        grid=(bsz, nb, nt),
        in_specs=in_specs,
        out_specs=pl.BlockSpec((1, tt, LRU_BLOCK), out_map),
        out_shape=jax.ShapeDtypeStruct((bsz, l, c), out_dtype),
        scratch_shapes=[pltpu.VMEM((1, LRU_BLOCK), F32)],
        compiler_params=_cparams(("parallel", "parallel", "arbitrary"), 32),
        name="lru_rev" if reverse else "lru_fwd",
    )(*args)


def _sconv_kernel(bg_ref, cg_ref, u_ref, cgp_ref, up_ref, cgn_ref, un_ref, w_ref, b_ref, o_ref,
                  *, tt, ctx_len, seq_len):
    ti = pl.program_id(1)
    z = cg_ref[0].astype(F32) * u_ref[0].astype(F32)
    zp = cgp_ref[0].astype(F32) * up_ref[0].astype(F32)
    zn = cgn_ref[0].astype(F32) * un_ref[0].astype(F32)
    local = lax.broadcasted_iota(jnp.int32, (tt, 1), 0)
    row = ti * tt + local
    first, last = _segment_bounds(row, ctx_len, seq_len)
    w = w_ref[...]
    y = b_ref[...] + w[1:2, :] * z
    y = y + w[0:1, :] * jnp.where(row - 1 >= first, _shift_down(z, zp, 1, local), 0.0)
    y = y + w[2:3, :] * jnp.where(row + 1 <= last, _shift_up(z, zn, 1, local, tt), 0.0)
    o_ref[0] = (bg_ref[0].astype(F32) * y).astype(BF16)


def _sconv_call(p, w, b, ctx_len):
    bsz, l, _ = p.shape
    c = w.shape[1]
    tt = _tile(l, 768, 16)
    tc = 512
    ncb = c // tc
    b_col0 = (N_HEADS * HEAD_DIM + 2 * N_KV_HEADS * HEAD_DIM + 2 * c) // tc
    c_col0 = b_col0 + ncb
    u_col0 = c_col0 + ncb
    hb = tt // HALO
    n_hblk = l // HALO
    prev = lambda i: jnp.maximum(i * hb - 1, 0)
    nxt = lambda i: jnp.minimum((i + 1) * hb, n_hblk - 1)
    kern = functools.partial(_sconv_kernel, tt=tt, ctx_len=ctx_len, seq_len=l)
    return pl.pallas_call(
        kern,
        grid=(bsz, l // tt, ncb),
        in_specs=[pl.BlockSpec((1, tt, tc), lambda b_, i, j: (b_, i, b_col0 + j)),
                  pl.BlockSpec((1, tt, tc), lambda b_, i, j: (b_, i, c_col0 + j)),
                  pl.BlockSpec((1, tt, tc), lambda b_, i, j: (b_, i, u_col0 + j)),
                  pl.BlockSpec((1, HALO, tc), lambda b_, i, j: (b_, prev(i), c_col0 + j)),
                  pl.BlockSpec((1, HALO, tc), lambda b_, i, j: (b_, prev(i), u_col0 + j)),
                  pl.BlockSpec((1, HALO, tc), lambda b_, i, j: (b_, nxt(i), c_col0 + j)),
                  pl.BlockSpec((1, HALO, tc), lambda b_, i, j: (b_, nxt(i), u_col0 + j)),
                  pl.BlockSpec((SC_CONV, tc), lambda b_, i, j: (0, j)),
                  pl.BlockSpec((1, tc), lambda b_, i, j: (0, j))],
        out_specs=pl.BlockSpec((1, tt, tc), lambda b_, i, j: (b_, i, j)),
        out_shape=jax.ShapeDtypeStruct((bsz, l, c), BF16),
        compiler_params=_cparams(("parallel", "parallel", "parallel"), 32),
        name="sconv",
    )(p, p, p, p, p, p, p, w, b.reshape(1, c))


def _merge_kernel(xa_ref, xl_ref, xs_ref, ga_ref, gl_ref, gs_ref, wa_ref, wl_ref, ws_ref, o_ref):
    y = jax.nn.sigmoid(ga_ref[0].astype(F32)) * jnp.dot(xa_ref[0], wa_ref[...], preferred_element_type=F32)
    y = y + jax.nn.sigmoid(gl_ref[0].astype(F32)) * jnp.dot(xl_ref[0], wl_ref[...], preferred_element_type=F32)
    y = y + jax.nn.sigmoid(gs_ref[0].astype(F32)) * jnp.dot(xs_ref[0], ws_ref[...], preferred_element_type=F32)
    o_ref[0] = y.astype(BF16)


def _merge_call(x_att, x_lru, x_sc, p, w_att, w_lru, w_sc):
    bsz, l, d = x_att.shape
    tm = _tile(l, 768, 16)
    tn = 512
    g_col0 = (p.shape[2] - 3 * d) // tn
    nj = d // tn
    xspec = pl.BlockSpec((1, tm, d), lambda b, i, j: (b, i, 0))
    wspec = pl.BlockSpec((d, tn), lambda b, i, j: (0, j))
    gspec = lambda k: pl.BlockSpec((1, tm, tn), lambda b, i, j: (b, i, g_col0 + k * nj + j))
    return pl.pallas_call(
        _merge_kernel,
        grid=(bsz, l // tm, nj),
        in_specs=[xspec, xspec, xspec, gspec(0), gspec(1), gspec(2), wspec, wspec, wspec],
        out_specs=pl.BlockSpec((1, tm, tn), lambda b, i, j: (b, i, j)),
        out_shape=jax.ShapeDtypeStruct((bsz, l, d), BF16),
        compiler_params=_cparams(("parallel", "parallel", "arbitrary"), 48),
        name="merge",
    )(x_att, x_lru, x_sc, p, p, p, w_att, w_lru, w_sc)


def _outproj_kernel(y_ref, w_ref, x_ref, gate_ref, shift_ref, scale_ref, g_ref, xo_ref, ht_ref,
                    *, tm, ctx_len):
    b = pl.program_id(0)
    i = pl.program_id(1)
    row = i * tm + lax.broadcasted_iota(jnp.int32, (tm, 1), 0)
    is_ctx = row < ctx_len
    acc = jnp.dot(y_ref[0], w_ref[...], preferred_element_type=F32)
    xn = x_ref[0] + _row_select(gate_ref, b, is_ctx) * acc
    xo_ref[0] = xn
    var = jnp.mean(xn * xn, axis=-1, keepdims=True)
    h = xn * lax.rsqrt(var + EPS) * g_ref[...]
    h = h * (1.0 + _row_select(scale_ref, b, is_ctx)) + _row_select(shift_ref, b, is_ctx)
    ht_ref[0] = h.T.astype(BF16)


def _outproj_call(y, w_out, x, mod_l, gain, ctx_len):
    bsz, l, d = x.shape
    tm = _tile(l, 384, LANES)
    kern = functools.partial(_outproj_kernel, tm=tm, ctx_len=ctx_len)
    mspec = lambda k: pl.BlockSpec((8, d), lambda b, i: (0, k))
    return pl.pallas_call(
        kern,
        grid=(bsz, l // tm),
        in_specs=[pl.BlockSpec((1, tm, d), lambda b, i: (b, i, 0)),
                  pl.BlockSpec((d, d), lambda b, i: (0, 0)),
                  pl.BlockSpec((1, tm, d), lambda b, i: (b, i, 0)),
                  mspec(2), mspec(3), mspec(4),
                  pl.BlockSpec((1, d), lambda b, i: (0, 0))],
        out_specs=[pl.BlockSpec((1, tm, d), lambda b, i: (b, i, 0)),
                   pl.BlockSpec((1, d, tm), lambda b, i: (b, 0, i))],
        out_shape=[jax.ShapeDtypeStruct((bsz, l, d), F32),
                   jax.ShapeDtypeStruct((bsz, d, l), BF16)],
        compiler_params=_cparams(("parallel", "parallel"), 48),
        name="outproj",
    )(y, w_out, x, mod_l, mod_l, mod_l, gain.reshape(1, d))


_CAND_ROWS = tuple(PEER_TOPK // (i + 1) for i in range(PEER_TOPK))


def _top16(s, v_scr):
    n = s.shape[0]
    key = lax.broadcasted_iota(jnp.int32, s.shape, 0)

    def body(r, carry):
        work, rank = carry
        m = jnp.max(work, axis=0, keepdims=True)
        v_scr[pl.ds(r, 1), :] = m
        first = jnp.min(jnp.where(work == m, key, n), axis=0, keepdims=True)
        sel = key == first
        return jnp.where(sel, NEG_INF, work), jnp.where(sel, r.astype(F32), rank)

    _, rank = lax.fori_loop(0, PEER_TOPK, body, (s, jnp.full(s.shape, float(PEER_TOPK), F32)))
    return rank


def _peer_topk_kernel(ht_ref, wq_ref, k1_ref, k2_ref, cnt_ref, e1_ref, rk_ref, e2_ref,
                      q_scr, v1_scr, v2_scr, *, tm):
    h = pl.program_id(2)
    half = PEER_NKEYS

    @pl.when(h == 0)
    def _():
        q_scr[...] = jnp.dot(wq_ref[...], ht_ref[0], preferred_element_type=F32).astype(BF16)

    base = pl.multiple_of(h * 2 * half, 2 * half)
    sub = lax.broadcasted_iota(jnp.int32, (SUBLANES, LANES), 0)
    big = PEER_TOPK * PEER_TOPK

    def lane_group(gidx, carry):
        ls = pl.ds(pl.multiple_of(gidx * LANES, LANES), LANES)
        s1 = jnp.dot(k1_ref[0], q_scr[pl.ds(base, half), ls], preferred_element_type=F32)
        s2 = jnp.dot(k2_ref[0], q_scr[pl.ds(base + half, half), ls], preferred_element_type=F32)
        rank1 = _top16(s1, v1_scr)
        rank2 = _top16(s2, v2_scr)
        v1 = v1_scr[...]
        v2 = v2_scr[...]

        pieces, poss = [], []
        for i in range(SUBLANES):
            for j0 in range(0, _CAND_ROWS[i], SUBLANES):
                c = v1[i:i + 1, :] + v2[j0:j0 + SUBLANES, :]
                valid = sub + j0 < _CAND_ROWS[i]
                pieces.append(jnp.where(valid, c, NEG_INF))
                poss.append(jnp.where(valid, i * PEER_TOPK + j0 + sub, big))
        pieces.append(v1[SUBLANES:, :] + v2[0:1, :])
        poss.append((sub + SUBLANES) * PEER_TOPK)
        orig = list(pieces)
        npc = len(pieces)

        def pick(_, carry2):
            cs, sels = carry2
            m = functools.reduce(jnp.maximum, cs)
            m = jnp.max(m, axis=0, keepdims=True)
            cand_pos = functools.reduce(jnp.minimum, [jnp.where(c == m, p_, big) for c, p_ in zip(cs, poss)])
            first = jnp.min(cand_pos, axis=0, keepdims=True)
            hit = [p_ == first for p_ in poss]
            cs = tuple(jnp.where(hh, NEG_INF, c) for hh, c in zip(hit, cs))
            sels = tuple(jnp.where(hh, 1.0, s_) for hh, s_ in zip(hit, sels))
            return cs, sels

        zeros = tuple(jnp.zeros((SUBLANES, LANES), F32) for _ in range(npc))
        _, sels = lax.fori_loop(0, PEER_TOPK, pick, (tuple(pieces), zeros))

        top = v1[0:1, :] + v2[0:1, :]
        zsum = functools.reduce(
            lambda a_, b_: a_ + b_,
            [jnp.where(s_ > 0.0, jnp.exp(o - top), 0.0) for s_, o in zip(sels, orig)])
        zinv = 1.0 / jnp.sum(zsum, axis=0, keepdims=True)

        counts = []
        pi = 0
        for i in range(SUBLANES):
            c = None
            for j0 in range(0, _CAND_ROWS[i], SUBLANES):
                part = jnp.sum(sels[pi], axis=0, keepdims=True)
                c = part if c is None else c + part
                pi += 1
            counts.append(c)
        tail = sels[pi]
        for i in range(SUBLANES, PEER_TOPK):
            counts.append(tail[i - SUBLANES:i - SUBLANES + 1, :])
        cnt = jnp.zeros(rank1.shape, F32)
        for i in range(PEER_TOPK):
            cnt = jnp.where(rank1 == float(i), counts[i], cnt)

        cnt_ref[0, 0, :, ls] = cnt
        e1_ref[0, 0, :, ls] = jnp.exp(s1 - v1[0:1, :]) * zinv
        rk_ref[0, 0, :, ls] = rank2
        e2_ref[0, 0, :, ls] = jnp.exp(s2 - v2[0:1, :])
        return carry

    lax.fori_loop(0, tm // LANES, lane_group, 0)


def _peer_topk_call(ht, wq_t, k1, k2):
    bsz, d, l = ht.shape
    tm = _tile(l, 384, LANES)
    kern = functools.partial(_peer_topk_kernel, tm=tm)
    ospec = pl.BlockSpec((1, 1, PEER_NKEYS, tm), lambda b, i, h: (b, h, 0, i))
    oshape = jax.ShapeDtypeStruct((bsz, PEER_HEADS, PEER_NKEYS, l), F32)
    kspec = pl.BlockSpec((1, PEER_NKEYS, PEER_NKEYS), lambda b, i, h: (h, 0, 0))
    return pl.pallas_call(
        kern,
        grid=(bsz, l // tm, PEER_HEADS),
        in_specs=[pl.BlockSpec((1, d, tm), lambda b, i, h: (b, 0, i)),
                  pl.BlockSpec(wq_t.shape, lambda b, i, h: (0, 0)),
                  kspec, kspec],
        out_specs=[ospec, ospec, ospec, ospec],
        out_shape=[oshape, oshape, oshape, oshape],
        scratch_shapes=[pltpu.VMEM((wq_t.shape[0], tm), BF16),
                        pltpu.VMEM((PEER_TOPK, LANES), F32),
                        pltpu.VMEM((PEER_TOPK, LANES), F32)],
        compiler_params=_cparams(("parallel", "parallel", "arbitrary"), 40),
        name="peer_topk",
    )(ht, wq_t, k1, k2)


def _peer_dense_kernel(ht_ref, u_ref, vt_ref, cnt_ref, e1_ref, rk_ref, e2_ref, o_ref, wz_scr, *, n_sub):
    e = pl.program_id(2)

    @pl.when(e == 0)
    def _():
        o_ref[...] = jnp.zeros(o_ref.shape, F32)

    s = jnp.dot(u_ref[...], ht_ref[0], preferred_element_type=F32)
    for a in range(n_sub):
        rows = slice(a * PEER_NKEYS, (a + 1) * PEER_NKEYS)
        w = None
        for h in range(PEER_HEADS):
            cnt = cnt_ref[0, h, a:a + 1, :]
            e1 = e1_ref[0, h, a:a + 1, :]
            term = jnp.where(rk_ref[0, h] < cnt, e2_ref[0, h] * e1, 0.0)
            w = term if w is None else w + term
        wz_scr[rows, :] = (w * _gelu(s[rows, :])).astype(BF16)
    o_ref[0] += jnp.dot(vt_ref[...], wz_scr[...], preferred_element_type=F32)


def _peer_dense_call(ht, u_bf16, vt_bf16, cnt, e1n, rank2, e2):
    bsz, d, l = ht.shape
    n_exp = u_bf16.shape[0]
    tm = _tile(l, 768, LANES)
    n_sub = SUBLANES
    te = n_sub * PEER_NKEYS
    kern = functools.partial(_peer_dense_kernel, n_sub=n_sub)
    aspec = pl.BlockSpec((1, PEER_HEADS, n_sub, tm), lambda b, i, e: (b, 0, e, i))
    fspec = pl.BlockSpec((1, PEER_HEADS, PEER_NKEYS, tm), lambda b, i, e: (b, 0, 0, i))
    return pl.pallas_call(
        kern,
        grid=(bsz, l // tm, n_exp // te),
        in_specs=[pl.BlockSpec((1, d, tm), lambda b, i, e: (b, 0, i)),
                  pl.BlockSpec((te, d), lambda b, i, e: (e, 0)),
                  pl.BlockSpec((d, te), lambda b, i, e: (0, e)),
                  aspec, aspec, fspec, fspec],
        out_specs=pl.BlockSpec((1, d, tm), lambda b, i, e: (b, 0, i)),
        out_shape=jax.ShapeDtypeStruct((bsz, d, l), F32),
        scratch_shapes=[pltpu.VMEM((te, tm), BF16)],
        compiler_params=_cparams(("parallel", "parallel", "arbitrary"), 58),
        name="peer_dense",
    )(ht, u_bf16, vt_bf16, cnt, e1n, rank2, e2)


def _resid_kernel(x_ref, yt_ref, gate_ref, o_ref, *, tm, ctx_len):
    b = pl.program_id(0)
    i = pl.program_id(1)
    row = i * tm + lax.broadcasted_iota(jnp.int32, (tm, 1), 0)
    o_ref[0] = x_ref[0] + _row_select(gate_ref, b, row < ctx_len) * yt_ref[0].T


def _resid_call(x, yt, mod_l, ctx_len):
    bsz, l, d = x.shape
    tm = _tile(l, 384, LANES)
    kern = functools.partial(_resid_kernel, tm=tm, ctx_len=ctx_len)
    return pl.pallas_call(
        kern,
        grid=(bsz, l // tm),
        in_specs=[pl.BlockSpec((1, tm, d), lambda b, i: (b, i, 0)),
                  pl.BlockSpec((1, d, tm), lambda b, i: (b, 0, i)),
                  pl.BlockSpec((8, d), lambda b, i: (0, 5))],
        out_specs=pl.BlockSpec((1, tm, d), lambda b, i: (b, i, 0)),
        out_shape=jax.ShapeDtypeStruct((bsz, l, d), F32),
        compiler_params=_cparams(("parallel", "parallel"), 40),
        name="resid",
    )(x, yt, mod_l)


def _rope_tables(ctx_len, t_lat):
    rows = t_lat // GRID_W
    row = jnp.repeat(jnp.arange(rows, dtype=F32), GRID_W)
    col = jnp.tile(jnp.arange(GRID_W, dtype=F32), rows)
    inv = ROPE_THETA ** (-jnp.arange(ROPE_PAIRS, dtype=F32) / ROPE_PAIRS)
    ang = jnp.concatenate([row[:, None] * inv] * 2 + [col[:, None] * inv] * 2, axis=1)
    ang = jnp.concatenate([jnp.zeros((ctx_len, HEAD_DIM), F32), ang], axis=0)
    sign = jnp.where((jnp.arange(HEAD_DIM) & ROPE_PAIRS) == 0, -1.0, 1.0).astype(F32)
    return jnp.cos(ang), jnp.sin(ang) * sign


def kernel(x, c, ctx, c_ctx, w_mod, b_mod, norm_mix, norm_ffn, w_in, q_norm, k_norm, lru_conv_w, lru_conv_b, lru_wa, lru_ba, lru_wx, lru_bx, lru_lambda, sc_conv_w, sc_conv_b, w_o_attn, w_o_lru, w_o_sc, w_out, peer_wq, peer_k1, peer_k2, peer_u, peer_v):
    bsz, t_lat, d = x.shape
    ctx_len = ctx.shape[1]
    depth = w_mod.shape[0]
    assert bsz == 2, "modulation rows are laid out as [latent 0, latent 1, context]"

    xs = jnp.concatenate([ctx, x], axis=1)
    s8 = jnp.concatenate([c, c_ctx[None, :], jnp.zeros((8 - bsz - 1, d), F32)], axis=0)
    mod = _mod_call(s8, w_mod, b_mod)
    cos, sin_signed = _rope_tables(ctx_len, t_lat)

    for l in range(depth):
        mod_l = mod[l]
        p = _inproj_call(xs, mod_l, norm_mix[l], w_in[l].astype(BF16), ctx_len)
        qn, kn = _qkprep_call(p, cos, sin_signed, q_norm[l], k_norm[l])
        x_att = _attn_call(qn, kn, p, ctx_len)
        lru_args = (lru_conv_w[l], lru_conv_b[l])
        h_fwd = _lru_call(p, *lru_args, lru_wa[l, 0], lru_ba[l, 0], lru_wx[l, 0], lru_bx[l, 0],
                          lru_lambda[l, 0], ctx_len, reverse=False)
        x_lru = _lru_call(p, *lru_args, lru_wa[l, 1], lru_ba[l, 1], lru_wx[l, 1], lru_bx[l, 1],
                          lru_lambda[l, 1], ctx_len, reverse=True, h_fwd=h_fwd)
        x_sc = _sconv_call(p, sc_conv_w[l], sc_conv_b[l], ctx_len)
        y = _merge_call(x_att, x_lru, x_sc, p, w_o_attn[l].astype(BF16), w_o_lru[l].astype(BF16),
                        w_o_sc[l].astype(BF16))
        xs, ht = _outproj_call(y, w_out[l].astype(BF16), xs, mod_l, norm_ffn[l], ctx_len)
        cnt, e1n, rank2, e2 = _peer_topk_call(ht, peer_wq[l].T.astype(BF16), peer_k1[l].astype(BF16),
                                              peer_k2[l].astype(BF16))
        yt = _peer_dense_call(ht, peer_u[l].astype(BF16), peer_v[l].T.astype(BF16), cnt, e1n, rank2, e2)
        xs = _resid_call(xs, yt, mod_l, ctx_len)
    return xs[:, ctx_len:, :]
```

```python
import functools
import math

import numpy as np
import jax
import jax.numpy as jnp
from jax import lax
from jax.experimental import pallas as pl
from jax.experimental.pallas import tpu as pltpu

F32 = jnp.float32
BF16 = jnp.bfloat16

GRID_W = 64
EPS = 1e-6
N_MOD = 6

N_HEADS = 16
N_KV_HEADS = 4
HEAD_DIM = 128
GROUP = N_HEADS // N_KV_HEADS
ROPE_PAIRS = HEAD_DIM // 4
ROPE_THETA = 10000.0

LRU_BLOCKS = 16
LRU_BLOCK = 128
LRU_CONV = 4
LRU_C = 8.0
SC_CONV = 3

PEER_HEADS = 8
PEER_NKEYS = 128
PEER_TOPK = 16

LANES = 128
SUBLANES = 8
HALO = SUBLANES
NEG_INF = float("-inf")
LOG2E = 1.4426950408889634
GELU_C = math.sqrt(2.0 / math.pi)
Q_SCALE = HEAD_DIM ** -0.5 * LOG2E
SHIFT_MARGIN = 1.02
MIN_DENOM = 2.0 ** -100


def _tile(n, target, mult):
    best = None
    for t in range(mult, min(n, target) + 1, mult):
        if n % t == 0:
            best = t
    assert best is not None, (n, target, mult)
    return best


def _cparams(sem, vmem_mib):
    return pltpu.CompilerParams(dimension_semantics=sem, vmem_limit_bytes=vmem_mib << 20)


def _gelu(x):
    return 0.5 * x * (1.0 + jnp.tanh(GELU_C * (x + 0.044715 * (x * x * x))))


def _row_select(mod_ref, b, is_ctx):
    return jnp.where(is_ctx, mod_ref[2:3, :], mod_ref[pl.ds(b, 1), :])


def _mod_kernel(s_ref, w_ref, b_ref, o_ref):
    s = s_ref[...]
    s = s * jax.nn.sigmoid(s)
    o_ref[0] = jnp.dot(s, w_ref[0], preferred_element_type=F32,
                       precision=lax.Precision.HIGHEST) + b_ref[0]


def _mod_call(s8, w_mod, b_mod):
    depth, d, n = w_mod.shape
    tn = _tile(n, 1024, LANES)
    return pl.pallas_call(
        _mod_kernel,
        grid=(depth, n // tn),
        in_specs=[pl.BlockSpec((8, d), lambda l, j: (0, 0)),
                  pl.BlockSpec((1, d, tn), lambda l, j: (l, 0, j)),
                  pl.BlockSpec((1, 1, tn), lambda l, j: (l, 0, j))],
        out_specs=pl.BlockSpec((1, 8, tn), lambda l, j: (l, 0, j)),
        out_shape=jax.ShapeDtypeStruct((depth, 8, n), F32),
        compiler_params=_cparams(("parallel", "parallel"), 32),
        name="mod",
    )(s8, w_mod, b_mod.reshape(depth, 1, n))


def _inproj_kernel(x_ref, shift_ref, scale_ref, g_ref, w_ref, o_ref, h_scr, *, tm, ctx_len):
    b = pl.program_id(0)
    i = pl.program_id(1)

    @pl.when(pl.program_id(2) == 0)
    def _():
        x = x_ref[0]
        var = jnp.mean(x * x, axis=-1, keepdims=True)
        y = x * lax.rsqrt(var + EPS) * g_ref[...]
        row = i * tm + lax.broadcasted_iota(jnp.int32, (tm, 1), 0)
        is_ctx = row < ctx_len
        sh = _row_select(shift_ref, b, is_ctx)
        sc = _row_select(scale_ref, b, is_ctx)
        h_scr[...] = (y * (1.0 + sc) + sh).astype(BF16)

    o_ref[0] = jnp.dot(h_scr[...], w_ref[...], preferred_element_type=F32).astype(BF16)


def _inproj_call(x, mod_l, gain, w_bf16, ctx_len):
    bsz, l, d = x.shape
    n = w_bf16.shape[1]
    tm = _tile(l, 704, 16)
    tn = 512
    kern = functools.partial(_inproj_kernel, tm=tm, ctx_len=ctx_len)
    return pl.pallas_call(
        kern,
        grid=(bsz, l // tm, n // tn),
        in_specs=[pl.BlockSpec((1, tm, d), lambda b, i, j: (b, i, 0)),
                  pl.BlockSpec((8, d), lambda b, i, j: (0, 0)),
                  pl.BlockSpec((8, d), lambda b, i, j: (0, 1)),
                  pl.BlockSpec((1, d), lambda b, i, j: (0, 0)),
                  pl.BlockSpec((d, tn), lambda b, i, j: (0, j))],
        out_specs=pl.BlockSpec((1, tm, tn), lambda b, i, j: (b, i, j)),
        out_shape=jax.ShapeDtypeStruct((bsz, l, n), BF16),
        scratch_shapes=[pltpu.VMEM((tm, d), BF16)],
        compiler_params=_cparams(("parallel", "parallel", "arbitrary"), 48),
        name="inproj",
    )(x, mod_l, mod_l, gain.reshape(1, d), w_bf16)


def _norm_rope(t, gain, cos, sin_signed, lane_lo):
    var = jnp.mean(t * t, axis=-1, keepdims=True)
    y = t * lax.rsqrt(var + EPS) * gain
    swapped = jnp.where(lane_lo, pltpu.roll(y, HEAD_DIM - ROPE_PAIRS, 1), pltpu.roll(y, ROPE_PAIRS, 1))
    return y * cos + swapped * sin_signed


def _qkprep_kernel(q_ref, k_ref, v_ref, cos_ref, sin_ref, qg_ref, kg_ref, qo_ref, ko_ref, vo_ref):
    cos = cos_ref[...]
    sin = sin_ref[...]
    lane = lax.broadcasted_iota(jnp.int32, cos.shape, 1)
    lane_lo = (lane & ROPE_PAIRS) == 0
    qg = qg_ref[...]
    kg = kg_ref[...]
    for h in range(N_HEADS):
        sl = slice(h * HEAD_DIM, (h + 1) * HEAD_DIM)
        t = q_ref[0, :, sl].astype(F32)
        qo_ref[0, :, sl] = (_norm_rope(t, qg, cos, sin, lane_lo) * Q_SCALE).astype(BF16)
    k_tail = jnp.where(lane == 0, 1.0, 0.0).astype(BF16)
    v_tail = jnp.ones(cos.shape, BF16)
    for h in range(N_KV_HEADS):
        sl = slice(h * HEAD_DIM, (h + 1) * HEAD_DIM)
        lo = slice(2 * h * HEAD_DIM, (2 * h + 1) * HEAD_DIM)
        hi = slice((2 * h + 1) * HEAD_DIM, (2 * h + 2) * HEAD_DIM)
        t = k_ref[0, :, sl].astype(F32)
        ko_ref[0, :, lo] = _norm_rope(t, kg, cos, sin, lane_lo).astype(BF16)
        ko_ref[0, :, hi] = k_tail
        vo_ref[0, :, lo] = v_ref[0, :, sl]
        vo_ref[0, :, hi] = v_tail


def _qkprep_call(p, cos, sin_signed, q_gain, k_gain):
    bsz, l, _ = p.shape
    qw = N_HEADS * HEAD_DIM
    kw = N_KV_HEADS * HEAD_DIM
    tm = _tile(l, 768, 16)
    return pl.pallas_call(
        _qkprep_kernel,
        grid=(bsz, l // tm),
        in_specs=[pl.BlockSpec((1, tm, qw), lambda b, i: (b, i, 0)),
                  pl.BlockSpec((1, tm, kw), lambda b, i: (b, i, qw // kw)),
                  pl.BlockSpec((1, tm, kw), lambda b, i: (b, i, qw // kw + 1)),
                  pl.BlockSpec((tm, HEAD_DIM), lambda b, i: (i, 0)),
                  pl.BlockSpec((tm, HEAD_DIM), lambda b, i: (i, 0)),
                  pl.BlockSpec((1, HEAD_DIM), lambda b, i: (0, 0)),
                  pl.BlockSpec((1, HEAD_DIM), lambda b, i: (0, 0))],
        out_specs=[pl.BlockSpec((1, tm, qw), lambda b, i: (b, i, 0)),
                   pl.BlockSpec((1, tm, 2 * kw), lambda b, i: (b, i, 0)),
                   pl.BlockSpec((1, tm, 2 * kw), lambda b, i: (b, i, 0))],
        out_shape=[jax.ShapeDtypeStruct((bsz, l, qw), BF16),
                   jax.ShapeDtypeStruct((bsz, l, 2 * kw), BF16),
                   jax.ShapeDtypeStruct((bsz, l, 2 * kw), BF16)],
        compiler_params=_cparams(("parallel", "parallel"), 40),
        name="qkprep",
    )(p, p, p, cos, sin_signed, q_gain.reshape(1, HEAD_DIM), k_gain.reshape(1, HEAD_DIM))


def _attn_kernel(q_ref, k_ref, v_ref, qg_ref, kg_ref, o_ref, qs_scr, acc_scr, m_scr, l_scr,
                 *, tq, tkc, ctx_len, n_lat_chunks):
    qi = pl.program_id(2)
    q = q_ref[0]
    gq = jnp.max(jnp.abs(qg_ref[...]), axis=-1, keepdims=True)
    gk = jnp.max(jnp.abs(kg_ref[...]), axis=-1, keepdims=True)
    bound = (SHIFT_MARGIN * HEAD_DIM * Q_SCALE) * gq * gk
    lane = lax.broadcasted_iota(jnp.int32, (1, HEAD_DIM), 1)
    tail = jnp.where(lane == 0, -bound, 0.0).astype(BF16)
    for h in range(GROUP):
        qs_scr[h * tq:(h + 1) * tq, :HEAD_DIM] = q[:, h * HEAD_DIM:(h + 1) * HEAD_DIM]
        qs_scr[h * tq:(h + 1) * tq, HEAD_DIM:] = jnp.broadcast_to(tail, (tq, HEAD_DIM))
    n = jnp.where(qi < ctx_len // tq, 0, n_lat_chunks)

    def lat_start(c):
        return pl.multiple_of(ctx_len + c * tkc, math.gcd(ctx_len, tkc))

    def write(out):
        for h in range(GROUP):
            o_ref[0, :, h * HEAD_DIM:(h + 1) * HEAD_DIM] = out[h * tq:(h + 1) * tq, :].astype(BF16)

    def shifted_chunk(start, size):
        s = lax.dot_general(qs_scr[...], k_ref[0, pl.ds(start, size), :], (((1,), (1,)), ((), ())),
                            preferred_element_type=F32)
        p = jnp.exp2(s).astype(BF16)
        acc_scr[...] += jnp.dot(p, v_ref[0, pl.ds(start, size), :], preferred_element_type=F32)

    acc_scr[...] = jnp.zeros(acc_scr.shape, F32)
    shifted_chunk(0, ctx_len)

    def shifted_body(c, carry):
        shifted_chunk(lat_start(c), tkc)
        return carry

    lax.fori_loop(0, n, shifted_body, 0)
    acc = acc_scr[...]
    den = acc[:, HEAD_DIM:]
    write(acc[:, :HEAD_DIM] / den)

    @pl.when(jnp.logical_not(jnp.min(den) >= MIN_DENOM))
    def _():
        m_scr[...] = jnp.full(m_scr.shape, NEG_INF, F32)
        l_scr[...] = jnp.zeros(l_scr.shape, F32)
        acc_scr[...] = jnp.zeros(acc_scr.shape, F32)

        def online_chunk(start, size):
            k = k_ref[0, pl.ds(start, size), :HEAD_DIM]
            v = v_ref[0, pl.ds(start, size), :HEAD_DIM]
            s = lax.dot_general(qs_scr[:, :HEAD_DIM], k, (((1,), (1,)), ((), ())),
                                preferred_element_type=F32)
            m_old = m_scr[...]
            m_new = jnp.maximum(m_old, jnp.max(s, axis=-1, keepdims=True))
            alpha = jnp.exp2(m_old - m_new)
            p = jnp.exp2(s - m_new)
            l_scr[...] = alpha * l_scr[...] + jnp.sum(p, axis=-1, keepdims=True)
            acc_scr[:, :HEAD_DIM] = alpha * acc_scr[:, :HEAD_DIM] + jnp.dot(
                p.astype(BF16), v, preferred_element_type=F32)
            m_scr[...] = m_new

        online_chunk(0, ctx_len)

        def online_body(c, carry):
            online_chunk(lat_start(c), tkc)
            return carry

        lax.fori_loop(0, n, online_body, 0)
        write(acc_scr[:, :HEAD_DIM] / l_scr[...])


def _attn_call(qn, ke, ve, q_gain, k_gain, ctx_len):
    bsz, l, qw = qn.shape
    tq = 256
    assert ctx_len % tq == 0 and l % tq == 0
    t_lat = l - ctx_len
    tkc = _tile(t_lat, 512, 256)
    gw = GROUP * HEAD_DIM
    ew = 2 * HEAD_DIM
    kern = functools.partial(_attn_kernel, tq=tq, tkc=tkc, ctx_len=ctx_len, n_lat_chunks=t_lat // tkc)
    return pl.pallas_call(
        kern,
        grid=(bsz, N_KV_HEADS, l // tq),
        in_specs=[pl.BlockSpec((1, tq, gw), lambda b, g, i: (b, i, g)),
                  pl.BlockSpec((1, l, ew), lambda b, g, i: (b, 0, g)),
                  pl.BlockSpec((1, l, ew), lambda b, g, i: (b, 0, g)),
                  pl.BlockSpec((1, HEAD_DIM), lambda b, g, i: (0, 0)),
                  pl.BlockSpec((1, HEAD_DIM), lambda b, g, i: (0, 0))],
        out_specs=pl.BlockSpec((1, tq, gw), lambda b, g, i: (b, i, g)),
        out_shape=jax.ShapeDtypeStruct((bsz, l, qw), BF16),
        scratch_shapes=[pltpu.VMEM((GROUP * tq, ew), BF16),
                        pltpu.VMEM((GROUP * tq, ew), F32),
                        pltpu.VMEM((GROUP * tq, 1), F32),
                        pltpu.VMEM((GROUP * tq, 1), F32)],
        compiler_params=_cparams(("parallel", "parallel", "arbitrary"), 48),
        name="attn",
    )(qn, ke, ve, q_gain.reshape(1, HEAD_DIM), k_gain.reshape(1, HEAD_DIM))


def _segment_bounds(row, ctx_len, seq_len):
    is_ctx = row < ctx_len
    first = jnp.where(is_ctx, 0, ctx_len)
    last = jnp.where(is_ctx, ctx_len - 1, seq_len - 1)
    return first, last


def _shift_down(x, prev, k, local):
    y = pltpu.roll(x, k, 0)
    for r in range(k):
        y = jnp.where(local == r, prev[HALO - k + r:HALO - k + r + 1, :], y)
    return y


def _shift_up(x, nxt, k, local, tt):
    y = pltpu.roll(x, tt - k, 0)
    for r in range(k):
        y = jnp.where(local == tt - k + r, nxt[r:r + 1, :], y)
    return y


def _lru_kernel(*refs, reverse, tt, ctx_len, seq_len):
    if reverse:
        (x_ref, xp_ref, xn_ref, cw_ref, cb_ref, wa_ref, ba_ref, wx_ref, bx_ref, lam_ref,
         hf_ref, g_ref, o_ref, carry_scr) = refs
    else:
        (x_ref, xp_ref, xn_ref, cw_ref, cb_ref, wa_ref, ba_ref, wx_ref, bx_ref, lam_ref,
         o_ref, carry_scr) = refs
    s = pl.program_id(2)
    nt = pl.num_programs(2)
    ti = jnp.where(s == 0, 0, nt - s) if reverse else s

    @pl.when(s == 0)
    def _():
        carry_scr[...] = jnp.zeros(carry_scr.shape, F32)

    x = x_ref[0].astype(F32)
    xp = xp_ref[0].astype(F32)
    xn = xn_ref[0].astype(F32)
    local = lax.broadcasted_iota(jnp.int32, (tt, 1), 0)
    row = ti * tt + local
    first, last = _segment_bounds(row, ctx_len, seq_len)
    cw = cw_ref[...]
    u = cb_ref[...] + cw[2:3, :] * x
    u = u + cw[0:1, :] * jnp.where(row - 2 >= first, _shift_down(x, xp, 2, local), 0.0)
    u = u + cw[1:2, :] * jnp.where(row - 1 >= first, _shift_down(x, xp, 1, local), 0.0)
    u = u + cw[3:4, :] * jnp.where(row + 1 <= last, _shift_up(x, xn, 1, local, tt), 0.0)

    ub = u.astype(BF16)
    r = jax.nn.sigmoid(jnp.dot(ub, wa_ref[0], preferred_element_type=F32) + ba_ref[...])
    i = jax.nn.sigmoid(jnp.dot(ub, wx_ref[0], preferred_element_type=F32) + bx_ref[...])
    nlam = -lam_ref[...]
    softplus = jnp.maximum(nlam, 0.0) + jnp.log1p(jnp.exp(-jnp.abs(nlam)))
    log_a = (-LRU_C) * r * softplus
    a = jnp.exp(log_a)
    d = jnp.sqrt(1.0 - a * a) * (i * u)

    k = 1
    while k < tt:
        if reverse:
            keep = local < tt - k
            a_n = jnp.where(keep, pltpu.roll(a, tt - k, 0), 1.0)
            d_n = jnp.where(keep, pltpu.roll(d, tt - k, 0), 0.0)
        else:
            keep = local >= k
            a_n = jnp.where(keep, pltpu.roll(a, k, 0), 1.0)
            d_n = jnp.where(keep, pltpu.roll(d, k, 0), 0.0)
        d = a * d_n + d
        a = a * a_n
        k *= 2
    h = d + a * carry_scr[...]
    carry_scr[...] = h[0:1, :] if reverse else h[tt - 1:tt, :]
    if reverse:
        o_ref[0] = ((hf_ref[0] + h) * _gelu(g_ref[0].astype(F32))).astype(BF16)
    else:
        o_ref[0] = h


def _lru_call(p, conv_w, conv_b, wa, ba, wx, bx, lam, ctx_len, reverse, h_fwd=None):
    bsz, l, _ = p.shape
    c = conv_w.shape[1]
    tt = 256
    assert ctx_len % tt == 0 and l % tt == 0
    nt = l // tt
    nb = c // LRU_BLOCK
    x_col0 = (N_HEADS + 2 * N_KV_HEADS)
    g_col0 = x_col0 + nb
    hb = tt // HALO
    n_hblk = l // HALO

    def tile_of(s):
        return jnp.where(s == 0, 0, nt - s) if reverse else s

    x_map = lambda b, cb, s: (b, tile_of(s), x_col0 + cb)
    prev_map = lambda b, cb, s: (b, jnp.maximum(tile_of(s) * hb - 1, 0), x_col0 + cb)
    next_map = lambda b, cb, s: (b, jnp.minimum((tile_of(s) + 1) * hb, n_hblk - 1), x_col0 + cb)
    vec_map = lambda b, cb, s: (0, cb)
    in_specs = [pl.BlockSpec((1, tt, LRU_BLOCK), x_map),
                pl.BlockSpec((1, HALO, LRU_BLOCK), prev_map),
                pl.BlockSpec((1, HALO, LRU_BLOCK), next_map),
                pl.BlockSpec((LRU_CONV, LRU_BLOCK), vec_map),
                pl.BlockSpec((1, LRU_BLOCK), vec_map),
                pl.BlockSpec((1, LRU_BLOCK, LRU_BLOCK), lambda b, cb, s: (cb, 0, 0)),
                pl.BlockSpec((1, LRU_BLOCK), vec_map),
                pl.BlockSpec((1, LRU_BLOCK, LRU_BLOCK), lambda b, cb, s: (cb, 0, 0)),
                pl.BlockSpec((1, LRU_BLOCK), vec_map),
                pl.BlockSpec((1, LRU_BLOCK), vec_map)]
    args = [p, p, p, conv_w, conv_b.reshape(1, c), wa.astype(BF16), ba.reshape(1, c),
            wx.astype(BF16), bx.reshape(1, c), lam.reshape(1, c)]
    out_map = lambda b, cb, s: (b, tile_of(s), cb)
    if reverse:
        in_specs += [pl.BlockSpec((1, tt, LRU_BLOCK), out_map),
                     pl.BlockSpec((1, tt, LRU_BLOCK), lambda b, cb, s: (b, tile_of(s), g_col0 + cb))]
        args += [h_fwd, p]
        out_dtype = BF16
    else:
        out_dtype = F32
    kern = functools.partial(_lru_kernel, reverse=reverse, tt=tt, ctx_len=ctx_len, seq_len=l)
    return pl.pallas_call(
        kern,
        grid=(bsz, nb, nt),
        in_specs=in_specs,
        out_specs=pl.BlockSpec((1, tt, LRU_BLOCK), out_map),
        out_shape=jax.ShapeDtypeStruct((bsz, l, c), out_dtype),
        scratch_shapes=[pltpu.VMEM((1, LRU_BLOCK), F32)],
        compiler_params=_cparams(("parallel", "parallel", "arbitrary"), 32),
        name="lru_rev" if reverse else "lru_fwd",
    )(*args)


def _sconv_kernel(bg_ref, cg_ref, u_ref, cgp_ref, up_ref, cgn_ref, un_ref, w_ref, b_ref, o_ref,
                  *, tt, ctx_len, seq_len):
    ti = pl.program_id(1)
    z = cg_ref[0].astype(F32) * u_ref[0].astype(F32)
    zp = cgp_ref[0].astype(F32) * up_ref[0].astype(F32)
    zn = cgn_ref[0].astype(F32) * un_ref[0].astype(F32)
    local = lax.broadcasted_iota(jnp.int32, (tt, 1), 0)
    row = ti * tt + local
    first, last = _segment_bounds(row, ctx_len, seq_len)
    w = w_ref[...]
    y = b_ref[...] + w[1:2, :] * z
    y = y + w[0:1, :] * jnp.where(row - 1 >= first, _shift_down(z, zp, 1, local), 0.0)
    y = y + w[2:3, :] * jnp.where(row + 1 <= last, _shift_up(z, zn, 1, local, tt), 0.0)
    o_ref[0] = (bg_ref[0].astype(F32) * y).astype(BF16)


def _sconv_call(p, w, b, ctx_len):
    bsz, l, _ = p.shape
    c = w.shape[1]
    tt = _tile(l, 768, 16)
    tc = 512
    ncb = c // tc
    b_col0 = (N_HEADS * HEAD_DIM + 2 * N_KV_HEADS * HEAD_DIM + 2 * c) // tc
    c_col0 = b_col0 + ncb
    u_col0 = c_col0 + ncb
    hb = tt // HALO
    n_hblk = l // HALO
    prev = lambda i: jnp.maximum(i * hb - 1, 0)
    nxt = lambda i: jnp.minimum((i + 1) * hb, n_hblk - 1)
    kern = functools.partial(_sconv_kernel, tt=tt, ctx_len=ctx_len, seq_len=l)
    return pl.pallas_call(
        kern,
        grid=(bsz, l // tt, ncb),
        in_specs=[pl.BlockSpec((1, tt, tc), lambda b_, i, j: (b_, i, b_col0 + j)),
                  pl.BlockSpec((1, tt, tc), lambda b_, i, j: (b_, i, c_col0 + j)),
                  pl.BlockSpec((1, tt, tc), lambda b_, i, j: (b_, i, u_col0 + j)),
                  pl.BlockSpec((1, HALO, tc), lambda b_, i, j: (b_, prev(i), c_col0 + j)),
                  pl.BlockSpec((1, HALO, tc), lambda b_, i, j: (b_, prev(i), u_col0 + j)),
                  pl.BlockSpec((1, HALO, tc), lambda b_, i, j: (b_, nxt(i), c_col0 + j)),
                  pl.BlockSpec((1, HALO, tc), lambda b_, i, j: (b_, nxt(i), u_col0 + j)),
                  pl.BlockSpec((SC_CONV, tc), lambda b_, i, j: (0, j)),
                  pl.BlockSpec((1, tc), lambda b_, i, j: (0, j))],
        out_specs=pl.BlockSpec((1, tt, tc), lambda b_, i, j: (b_, i, j)),
        out_shape=jax.ShapeDtypeStruct((bsz, l, c), BF16),
        compiler_params=_cparams(("parallel", "parallel", "parallel"), 32),
        name="sconv",
    )(p, p, p, p, p, p, p, w, b.reshape(1, c))


def _merge_kernel(xa_ref, xl_ref, xs_ref, ga_ref, gl_ref, gs_ref, wa_ref, wl_ref, ws_ref, o_ref):
    y = jax.nn.sigmoid(ga_ref[0].astype(F32)) * jnp.dot(xa_ref[0], wa_ref[...], preferred_element_type=F32)
    y = y + jax.nn.sigmoid(gl_ref[0].astype(F32)) * jnp.dot(xl_ref[0], wl_ref[...], preferred_element_type=F32)
    y = y + jax.nn.sigmoid(gs_ref[0].astype(F32)) * jnp.dot(xs_ref[0], ws_ref[...], preferred_element_type=F32)
    o_ref[0] = y.astype(BF16)


def _merge_call(x_att, x_lru, x_sc, p, w_att, w_lru, w_sc):
    bsz, l, d = x_att.shape
    tm = _tile(l, 768, 16)
    tn = 512
    g_col0 = (p.shape[2] - 3 * d) // tn
    nj = d // tn
    xspec = pl.BlockSpec((1, tm, d), lambda b, i, j: (b, i, 0))
    wspec = pl.BlockSpec((d, tn), lambda b, i, j: (0, j))
    gspec = lambda k: pl.BlockSpec((1, tm, tn), lambda b, i, j: (b, i, g_col0 + k * nj + j))
    return pl.pallas_call(
        _merge_kernel,
        grid=(bsz, l // tm, nj),
        in_specs=[xspec, xspec, xspec, gspec(0), gspec(1), gspec(2), wspec, wspec, wspec],
        out_specs=pl.BlockSpec((1, tm, tn), lambda b, i, j: (b, i, j)),
        out_shape=jax.ShapeDtypeStruct((bsz, l, d), BF16),
        compiler_params=_cparams(("parallel", "parallel", "arbitrary"), 48),
        name="merge",
    )(x_att, x_lru, x_sc, p, p, p, w_att, w_lru, w_sc)


def _outproj_kernel(y_ref, w_ref, x_ref, gate_ref, shift_ref, scale_ref, g_ref, xo_ref, ht_ref,
                    *, tm, ctx_len):
    b = pl.program_id(0)
    i = pl.program_id(1)
    row = i * tm + lax.broadcasted_iota(jnp.int32, (tm, 1), 0)
    is_ctx = row < ctx_len
    acc = jnp.dot(y_ref[0], w_ref[...], preferred_element_type=F32)
    xn = x_ref[0] + _row_select(gate_ref, b, is_ctx) * acc
    xo_ref[0] = xn
    var = jnp.mean(xn * xn, axis=-1, keepdims=True)
    h = xn * lax.rsqrt(var + EPS) * g_ref[...]
    h = h * (1.0 + _row_select(scale_ref, b, is_ctx)) + _row_select(shift_ref, b, is_ctx)
    ht_ref[0] = h.T.astype(BF16)


def _outproj_call(y, w_out, x, mod_l, gain, ctx_len):
    bsz, l, d = x.shape
    tm = _tile(l, 384, LANES)
    kern = functools.partial(_outproj_kernel, tm=tm, ctx_len=ctx_len)
    mspec = lambda k: pl.BlockSpec((8, d), lambda b, i: (0, k))
    return pl.pallas_call(
        kern,
        grid=(bsz, l // tm),
        in_specs=[pl.BlockSpec((1, tm, d), lambda b, i: (b, i, 0)),
                  pl.BlockSpec((d, d), lambda b, i: (0, 0)),
                  pl.BlockSpec((1, tm, d), lambda b, i: (b, i, 0)),
                  mspec(2), mspec(3), mspec(4),
                  pl.BlockSpec((1, d), lambda b, i: (0, 0))],
        out_specs=[pl.BlockSpec((1, tm, d), lambda b, i: (b, i, 0)),
                   pl.BlockSpec((1, d, tm), lambda b, i: (b, 0, i))],
        out_shape=[jax.ShapeDtypeStruct((bsz, l, d), F32),
                   jax.ShapeDtypeStruct((bsz, d, l), BF16)],
        compiler_params=_cparams(("parallel", "parallel"), 48),
        name="outproj",
    )(y, w_out, x, mod_l, mod_l, mod_l, gain.reshape(1, d))


_CAND_ROWS = tuple(PEER_TOPK // (i + 1) for i in range(PEER_TOPK))


def _top16(s, v_scr):
    n = s.shape[0]
    key = lax.broadcasted_iota(jnp.int32, s.shape, 0)

    def body(r, carry):
        work, rank = carry
        m = jnp.max(work, axis=0, keepdims=True)
        v_scr[pl.ds(r, 1), :] = m
        first = jnp.min(jnp.where(work == m, key, n), axis=0, keepdims=True)
        sel = key == first
        return jnp.where(sel, NEG_INF, work), jnp.where(sel, jnp.asarray(r, F32), rank)

    _, rank = lax.fori_loop(0, PEER_TOPK, body, (s, jnp.full(s.shape, float(PEER_TOPK), F32)))
    return rank


def _peer_topk_kernel(ht_ref, wq_ref, k1_ref, k2_ref, cnt_ref, e1_ref, rk_ref, e2_ref,
                      q_scr, v1_scr, v2_scr, *, tm):
    h = pl.program_id(2)
    half = PEER_NKEYS

    @pl.when(h == 0)
    def _():
        q_scr[...] = jnp.dot(wq_ref[...], ht_ref[0], preferred_element_type=F32).astype(BF16)

    base = pl.multiple_of(h * 2 * half, 2 * half)
    sub = lax.broadcasted_iota(jnp.int32, (SUBLANES, LANES), 0)
    big = PEER_TOPK * PEER_TOPK

    def lane_group(gidx, carry):
        ls = pl.ds(pl.multiple_of(gidx * LANES, LANES), LANES)
        s1 = jnp.dot(k1_ref[0], q_scr[pl.ds(base, half), ls], preferred_element_type=F32)
        s2 = jnp.dot(k2_ref[0], q_scr[pl.ds(base + half, half), ls], preferred_element_type=F32)
        rank1 = _top16(s1, v1_scr)
        rank2 = _top16(s2, v2_scr)
        v1 = v1_scr[...]
        v2 = v2_scr[...]

        pieces, poss = [], []
        for i in range(SUBLANES):
            for j0 in range(0, _CAND_ROWS[i], SUBLANES):
                c = v1[i:i + 1, :] + v2[j0:j0 + SUBLANES, :]
                valid = sub + j0 < _CAND_ROWS[i]
                pieces.append(jnp.where(valid, c, NEG_INF))
                poss.append(jnp.where(valid, i * PEER_TOPK + j0 + sub, big))
        pieces.append(v1[SUBLANES:, :] + v2[0:1, :])
        poss.append((sub + SUBLANES) * PEER_TOPK)
        orig = list(pieces)
        npc = len(pieces)

        def pick(_, carry2):
            cs, sels = carry2
            m = functools.reduce(jnp.maximum, cs)
            m = jnp.max(m, axis=0, keepdims=True)
            cand_pos = functools.reduce(jnp.minimum, [jnp.where(c == m, p_, big) for c, p_ in zip(cs, poss)])
            first = jnp.min(cand_pos, axis=0, keepdims=True)
            hit = [p_ == first for p_ in poss]
            cs = tuple(jnp.where(hh, NEG_INF, c) for hh, c in zip(hit, cs))
            sels = tuple(jnp.where(hh, 1.0, s_) for hh, s_ in zip(hit, sels))
            return cs, sels

        zeros = tuple(jnp.zeros((SUBLANES, LANES), F32) for _ in range(npc))
        _, sels = lax.fori_loop(0, PEER_TOPK, pick, (tuple(pieces), zeros))

        top = v1[0:1, :] + v2[0:1, :]
        zsum = functools.reduce(
            lambda a_, b_: a_ + b_,
            [jnp.where(s_ > 0.0, jnp.exp(o - top), 0.0) for s_, o in zip(sels, orig)])
        zinv = 1.0 / jnp.sum(zsum, axis=0, keepdims=True)

        counts = []
        pi = 0
        for i in range(SUBLANES):
            c = None
            for j0 in range(0, _CAND_ROWS[i], SUBLANES):
                part = jnp.sum(sels[pi], axis=0, keepdims=True)
                c = part if c is None else c + part
                pi += 1
            counts.append(c)
        tail = sels[pi]
        for i in range(SUBLANES, PEER_TOPK):
            counts.append(tail[i - SUBLANES:i - SUBLANES + 1, :])
        cnt = jnp.zeros(rank1.shape, F32)
        for i in range(PEER_TOPK):
            cnt = jnp.where(rank1 == float(i), counts[i], cnt)

        cnt_ref[0, 0, :, ls] = cnt
        e1_ref[0, 0, :, ls] = jnp.exp(s1 - v1[0:1, :]) * zinv
        rk_ref[0, 0, :, ls] = rank2
        e2_ref[0, 0, :, ls] = jnp.exp(s2 - v2[0:1, :])
        return carry

    lax.fori_loop(0, tm // LANES, lane_group, 0)


def _peer_topk_call(ht, wq_t, k1, k2):
    bsz, d, l = ht.shape
    tm = _tile(l, 384, LANES)
    kern = functools.partial(_peer_topk_kernel, tm=tm)
    ospec = pl.BlockSpec((1, 1, PEER_NKEYS, tm), lambda b, i, h: (b, h, 0, i))
    oshape = jax.ShapeDtypeStruct((bsz, PEER_HEADS, PEER_NKEYS, l), F32)
    kspec = pl.BlockSpec((1, PEER_NKEYS, PEER_NKEYS), lambda b, i, h: (h, 0, 0))
    return pl.pallas_call(
        kern,
        grid=(bsz, l // tm, PEER_HEADS),
        in_specs=[pl.BlockSpec((1, d, tm), lambda b, i, h: (b, 0, i)),
                  pl.BlockSpec(wq_t.shape, lambda b, i, h: (0, 0)),
                  kspec, kspec],
        out_specs=[ospec, ospec, ospec, ospec],
        out_shape=[oshape, oshape, oshape, oshape],
        scratch_shapes=[pltpu.VMEM((wq_t.shape[0], tm), BF16),
                        pltpu.VMEM((PEER_TOPK, LANES), F32),
                        pltpu.VMEM((PEER_TOPK, LANES), F32)],
        compiler_params=_cparams(("parallel", "parallel", "arbitrary"), 40),
        name="peer_topk",
    )(ht, wq_t, k1, k2)


def _peer_dense_kernel(ht_ref, u_ref, vt_ref, cnt_ref, e1_ref, rk_ref, e2_ref, o_ref, wz_scr, *, n_sub):
    e = pl.program_id(2)

    @pl.when(e == 0)
    def _():
        o_ref[...] = jnp.zeros(o_ref.shape, F32)

    s = jnp.dot(u_ref[...], ht_ref[0], preferred_element_type=F32)
    for a in range(n_sub):
        rows = slice(a * PEER_NKEYS, (a + 1) * PEER_NKEYS)
        w = None
        for h in range(PEER_HEADS):
            cnt = cnt_ref[0, h, a:a + 1, :]
            e1 = e1_ref[0, h, a:a + 1, :]
            term = jnp.where(rk_ref[0, h] < cnt, e2_ref[0, h] * e1, 0.0)
            w = term if w is None else w + term
        wz_scr[rows, :] = (w * _gelu(s[rows, :])).astype(BF16)
    o_ref[0] += jnp.dot(vt_ref[...], wz_scr[...], preferred_element_type=F32)


def _peer_dense_call(ht, u_bf16, vt_bf16, cnt, e1n, rank2, e2):
    bsz, d, l = ht.shape
    n_exp = u_bf16.shape[0]
    tm = _tile(l, 768, LANES)
    n_sub = SUBLANES
    te = n_sub * PEER_NKEYS
    kern = functools.partial(_peer_dense_kernel, n_sub=n_sub)
    aspec = pl.BlockSpec((1, PEER_HEADS, n_sub, tm), lambda b, i, e: (b, 0, e, i))
    fspec = pl.BlockSpec((1, PEER_HEADS, PEER_NKEYS, tm), lambda b, i, e: (b, 0, 0, i))
    return pl.pallas_call(
        kern,
        grid=(bsz, l // tm, n_exp // te),
        in_specs=[pl.BlockSpec((1, d, tm), lambda b, i, e: (b, 0, i)),
                  pl.BlockSpec((te, d), lambda b, i, e: (e, 0)),
                  pl.BlockSpec((d, te), lambda b, i, e: (0, e)),
                  aspec, aspec, fspec, fspec],
        out_specs=pl.BlockSpec((1, d, tm), lambda b, i, e: (b, 0, i)),
        out_shape=jax.ShapeDtypeStruct((bsz, d, l), F32),
        scratch_shapes=[pltpu.VMEM((te, tm), BF16)],
        compiler_params=_cparams(("parallel", "parallel", "arbitrary"), 58),
        name="peer_dense",
    )(ht, u_bf16, vt_bf16, cnt, e1n, rank2, e2)


def _resid_kernel(x_ref, yt_ref, gate_ref, o_ref, *, tm, ctx_len):
    b = pl.program_id(0)
    i = pl.program_id(1)
    row = i * tm + lax.broadcasted_iota(jnp.int32, (tm, 1), 0)
    o_ref[0] = x_ref[0] + _row_select(gate_ref, b, row < ctx_len) * yt_ref[0].T


def _resid_call(x, yt, mod_l, ctx_len):
    bsz, l, d = x.shape
    tm = _tile(l, 384, LANES)
    kern = functools.partial(_resid_kernel, tm=tm, ctx_len=ctx_len)
    return pl.pallas_call(
        kern,
        grid=(bsz, l // tm),
        in_specs=[pl.BlockSpec((1, tm, d), lambda b, i: (b, i, 0)),
                  pl.BlockSpec((1, d, tm), lambda b, i: (b, 0, i)),
                  pl.BlockSpec((8, d), lambda b, i: (0, 5))],
        out_specs=pl.BlockSpec((1, tm, d), lambda b, i: (b, i, 0)),
        out_shape=jax.ShapeDtypeStruct((bsz, l, d), F32),
        compiler_params=_cparams(("parallel", "parallel"), 40),
        name="resid",
    )(x, yt, mod_l)


def _rope_tables(ctx_len, t_lat):
    rows = t_lat // GRID_W
    row = jnp.repeat(jnp.arange(rows, dtype=F32), GRID_W)
    col = jnp.tile(jnp.arange(GRID_W, dtype=F32), rows)
    inv = ROPE_THETA ** (-jnp.arange(ROPE_PAIRS, dtype=F32) / ROPE_PAIRS)
    ang = jnp.concatenate([row[:, None] * inv] * 2 + [col[:, None] * inv] * 2, axis=1)
    ang = jnp.concatenate([jnp.zeros((ctx_len, HEAD_DIM), F32), ang], axis=0)
    sign = jnp.where((jnp.arange(HEAD_DIM) & ROPE_PAIRS) == 0, -1.0, 1.0).astype(F32)
    return jnp.cos(ang), jnp.sin(ang) * sign


def kernel(x, c, ctx, c_ctx, w_mod, b_mod, norm_mix, norm_ffn, w_in, q_norm, k_norm, lru_conv_w, lru_conv_b, lru_wa, lru_ba, lru_wx, lru_bx, lru_lambda, sc_conv_w, sc_conv_b, w_o_attn, w_o_lru, w_o_sc, w_out, peer_wq, peer_k1, peer_k2, peer_u, peer_v):
    bsz, t_lat, d = x.shape
    ctx_len = ctx.shape[1]
    depth = w_mod.shape[0]
    assert bsz == 2, "modulation rows are laid out as [latent 0, latent 1, context]"

    xs = jnp.concatenate([ctx, x], axis=1)
    s8 = jnp.concatenate([c, c_ctx[None, :], jnp.zeros((8 - bsz - 1, d), F32)], axis=0)
    mod = _mod_call(s8, w_mod, b_mod)
    cos, sin_signed = _rope_tables(ctx_len, t_lat)

    for l in range(depth):
        mod_l = mod[l]
        p = _inproj_call(xs, mod_l, norm_mix[l], w_in[l].astype(BF16), ctx_len)
        qn, ke, ve = _qkprep_call(p, cos, sin_signed, q_norm[l], k_norm[l])
        x_att = _attn_call(qn, ke, ve, q_norm[l], k_norm[l], ctx_len)
        lru_args = (lru_conv_w[l], lru_conv_b[l])
        h_fwd = _lru_call(p, *lru_args, lru_wa[l, 0], lru_ba[l, 0], lru_wx[l, 0], lru_bx[l, 0],
                          lru_lambda[l, 0], ctx_len, reverse=False)
        x_lru = _lru_call(p, *lru_args, lru_wa[l, 1], lru_ba[l, 1], lru_wx[l, 1], lru_bx[l, 1],
                          lru_lambda[l, 1], ctx_len, reverse=True, h_fwd=h_fwd)
        x_sc = _sconv_call(p, sc_conv_w[l], sc_conv_b[l], ctx_len)
        y = _merge_call(x_att, x_lru, x_sc, p, w_o_attn[l].astype(BF16), w_o_lru[l].astype(BF16),
                        w_o_sc[l].astype(BF16))
        xs, ht = _outproj_call(y, w_out[l].astype(BF16), xs, mod_l, norm_ffn[l], ctx_len)
        cnt, e1n, rank2, e2 = _peer_topk_call(ht, peer_wq[l].T.astype(BF16), peer_k1[l].astype(BF16),
                                              peer_k2[l].astype(BF16))
        yt = _peer_dense_call(ht, peer_u[l].astype(BF16), peer_v[l].T.astype(BF16), cnt, e1n, rank2, e2)
        xs = _resid_call(xs, yt, mod_l, ctx_len)
    return xs[:, ctx_len:, :]
```

```python
import functools
import math

import numpy as np
import jax
import jax.numpy as jnp
from jax import lax
from jax.experimental import pallas as pl
from jax.experimental.pallas import tpu as pltpu

F32 = jnp.float32
BF16 = jnp.bfloat16

GRID_W = 64
EPS = 1e-6
N_MOD = 6

N_HEADS = 16
N_KV_HEADS = 4
HEAD_DIM = 128
GROUP = N_HEADS // N_KV_HEADS
ROPE_PAIRS = HEAD_DIM // 4
ROPE_THETA = 10000.0

LRU_BLOCKS = 16
LRU_BLOCK = 128
LRU_CONV = 4
LRU_C = 8.0
SC_CONV = 3

PEER_HEADS = 8
PEER_NKEYS = 128
PEER_TOPK = 16

LANES = 128
SUBLANES = 8
HALO = SUBLANES
NEG_INF = float("-inf")
LOG2E = 1.4426950408889634
GELU_C = math.sqrt(2.0 / math.pi)
Q_SCALE = HEAD_DIM ** -0.5 * LOG2E
SHIFT_MARGIN = 1.02
MIN_DENOM = 2.0 ** -100


def _tile(n, target, mult):
    best = None
    for t in range(mult, min(n, target) + 1, mult):
        if n % t == 0:
            best = t
    assert best is not None, (n, target, mult)
    return best


def _cparams(sem, vmem_mib):
    return pltpu.CompilerParams(dimension_semantics=sem, vmem_limit_bytes=vmem_mib << 20)


def _gelu(x):
    return 0.5 * x * (1.0 + jnp.tanh(GELU_C * (x + 0.044715 * (x * x * x))))


def _row_select(mod_ref, b, is_ctx):
    return jnp.where(is_ctx, mod_ref[2:3, :], mod_ref[pl.ds(b, 1), :])


def _mod_kernel(s_ref, w_ref, b_ref, o_ref):
    s = s_ref[...]
    s = s * jax.nn.sigmoid(s)
    o_ref[0] = jnp.dot(s, w_ref[0], preferred_element_type=F32,
                       precision=lax.Precision.HIGHEST) + b_ref[0]


def _mod_call(s8, w_mod, b_mod):
    depth, d, n = w_mod.shape
    tn = _tile(n, 1024, LANES)
    return pl.pallas_call(
        _mod_kernel,
        grid=(depth, n // tn),
        in_specs=[pl.BlockSpec((8, d), lambda l, j: (0, 0)),
                  pl.BlockSpec((1, d, tn), lambda l, j: (l, 0, j)),
                  pl.BlockSpec((1, 1, tn), lambda l, j: (l, 0, j))],
        out_specs=pl.BlockSpec((1, 8, tn), lambda l, j: (l, 0, j)),
        out_shape=jax.ShapeDtypeStruct((depth, 8, n), F32),
        compiler_params=_cparams(("parallel", "parallel"), 32),
        name="mod",
    )(s8, w_mod, b_mod.reshape(depth, 1, n))


def _inproj_kernel(x_ref, shift_ref, scale_ref, g_ref, w_ref, o_ref, h_scr, *, tm, ctx_len):
    b = pl.program_id(0)
    i = pl.program_id(1)

    @pl.when(pl.program_id(2) == 0)
    def _():
        x = x_ref[0]
        var = jnp.mean(x * x, axis=-1, keepdims=True)
        y = x * lax.rsqrt(var + EPS) * g_ref[...]
        row = i * tm + lax.broadcasted_iota(jnp.int32, (tm, 1), 0)
        is_ctx = row < ctx_len
        sh = _row_select(shift_ref, b, is_ctx)
        sc = _row_select(scale_ref, b, is_ctx)
        h_scr[...] = (y * (1.0 + sc) + sh).astype(BF16)

    o_ref[0] = jnp.dot(h_scr[...], w_ref[...], preferred_element_type=F32).astype(BF16)


def _inproj_call(x, mod_l, gain, w_bf16, ctx_len):
    bsz, l, d = x.shape
    n = w_bf16.shape[1]
    tm = _tile(l, 704, 16)
    tn = 1024
    kern = functools.partial(_inproj_kernel, tm=tm, ctx_len=ctx_len)
    return pl.pallas_call(
        kern,
        grid=(bsz, l // tm, n // tn),
        in_specs=[pl.BlockSpec((1, tm, d), lambda b, i, j: (b, i, 0)),
                  pl.BlockSpec((8, d), lambda b, i, j: (0, 0)),
                  pl.BlockSpec((8, d), lambda b, i, j: (0, 1)),
                  pl.BlockSpec((1, d), lambda b, i, j: (0, 0)),
                  pl.BlockSpec((d, tn), lambda b, i, j: (0, j))],
        out_specs=pl.BlockSpec((1, tm, tn), lambda b, i, j: (b, i, j)),
        out_shape=jax.ShapeDtypeStruct((bsz, l, n), BF16),
        scratch_shapes=[pltpu.VMEM((tm, d), BF16)],
        compiler_params=_cparams(("parallel", "parallel", "arbitrary"), 48),
        name="inproj",
    )(x, mod_l, mod_l, gain.reshape(1, d), w_bf16)


def _norm_rope(t, gain, cos, sin_signed, lane_lo):
    var = jnp.mean(t * t, axis=-1, keepdims=True)
    y = t * lax.rsqrt(var + EPS) * gain
    swapped = jnp.where(lane_lo, pltpu.roll(y, HEAD_DIM - ROPE_PAIRS, 1), pltpu.roll(y, ROPE_PAIRS, 1))
    return y * cos + swapped * sin_signed


def _qkprep_kernel(q_ref, k_ref, v_ref, cos_ref, sin_ref, qg_ref, kg_ref, qo_ref, ko_ref, vo_ref):
    cos = cos_ref[...]
    sin = sin_ref[...]
    lane = lax.broadcasted_iota(jnp.int32, cos.shape, 1)
    lane_lo = (lane & ROPE_PAIRS) == 0
    qg = qg_ref[...]
    kg = kg_ref[...]
    for h in range(N_HEADS):
        sl = slice(h * HEAD_DIM, (h + 1) * HEAD_DIM)
        t = q_ref[0, :, sl].astype(F32)
        qo_ref[0, :, sl] = (_norm_rope(t, qg, cos, sin, lane_lo) * Q_SCALE).astype(BF16)
    k_tail = jnp.where(lane == 0, 1.0, 0.0).astype(BF16)
    v_tail = jnp.ones(cos.shape, BF16)
    for h in range(N_KV_HEADS):
        sl = slice(h * HEAD_DIM, (h + 1) * HEAD_DIM)
        lo = slice(2 * h * HEAD_DIM, (2 * h + 1) * HEAD_DIM)
        hi = slice((2 * h + 1) * HEAD_DIM, (2 * h + 2) * HEAD_DIM)
        t = k_ref[0, :, sl].astype(F32)
        ko_ref[0, :, lo] = _norm_rope(t, kg, cos, sin, lane_lo).astype(BF16)
        ko_ref[0, :, hi] = k_tail
        vo_ref[0, :, lo] = v_ref[0, :, sl]
        vo_ref[0, :, hi] = v_tail


def _qkprep_call(p, cos, sin_signed, q_gain, k_gain):
    bsz, l, _ = p.shape
    qw = N_HEADS * HEAD_DIM
    kw = N_KV_HEADS * HEAD_DIM
    tm = _tile(l, 768, 16)
    return pl.pallas_call(
        _qkprep_kernel,
        grid=(bsz, l // tm),
        in_specs=[pl.BlockSpec((1, tm, qw), lambda b, i: (b, i, 0)),
                  pl.BlockSpec((1, tm, kw), lambda b, i: (b, i, qw // kw)),
                  pl.BlockSpec((1, tm, kw), lambda b, i: (b, i, qw // kw + 1)),
                  pl.BlockSpec((tm, HEAD_DIM), lambda b, i: (i, 0)),
                  pl.BlockSpec((tm, HEAD_DIM), lambda b, i: (i, 0)),
                  pl.BlockSpec((1, HEAD_DIM), lambda b, i: (0, 0)),
                  pl.BlockSpec((1, HEAD_DIM), lambda b, i: (0, 0))],
        out_specs=[pl.BlockSpec((1, tm, qw), lambda b, i: (b, i, 0)),
                   pl.BlockSpec((1, tm, 2 * kw), lambda b, i: (b, i, 0)),
                   pl.BlockSpec((1, tm, 2 * kw), lambda b, i: (b, i, 0))],
        out_shape=[jax.ShapeDtypeStruct((bsz, l, qw), BF16),
                   jax.ShapeDtypeStruct((bsz, l, 2 * kw), BF16),
                   jax.ShapeDtypeStruct((bsz, l, 2 * kw), BF16)],
        compiler_params=_cparams(("parallel", "parallel"), 40),
        name="qkprep",
    )(p, p, p, cos, sin_signed, q_gain.reshape(1, HEAD_DIM), k_gain.reshape(1, HEAD_DIM))


def _attn_kernel(q_ref, k_ref, v_ref, qg_ref, kg_ref, o_ref, qs_scr, acc_scr, sa_scr, sb_scr, m_scr, l_scr,
                 *, tq, tkc, ctx_len, n_lat_chunks):
    qi = pl.program_id(2)
    q = q_ref[0]
    gq = jnp.max(jnp.abs(qg_ref[...]), axis=-1, keepdims=True)
    gk = jnp.max(jnp.abs(kg_ref[...]), axis=-1, keepdims=True)
    bound = (SHIFT_MARGIN * HEAD_DIM * Q_SCALE) * gq * gk
    lane = lax.broadcasted_iota(jnp.int32, (1, HEAD_DIM), 1)
    tail = jnp.where(lane == 0, -bound, 0.0).astype(BF16)
    for h in range(GROUP):
        qs_scr[h * tq:(h + 1) * tq, :HEAD_DIM] = q[:, h * HEAD_DIM:(h + 1) * HEAD_DIM]
        qs_scr[h * tq:(h + 1) * tq, HEAD_DIM:] = jnp.broadcast_to(tail, (tq, HEAD_DIM))
    n = jnp.where(qi < ctx_len // tq, 0, n_lat_chunks)

    def lat_start(c):
        return pl.multiple_of(ctx_len + c * tkc, math.gcd(ctx_len, tkc))

    def write(out):
        for h in range(GROUP):
            o_ref[0, :, h * HEAD_DIM:(h + 1) * HEAD_DIM] = out[h * tq:(h + 1) * tq, :].astype(BF16)

    def scores(start, size):
        return lax.dot_general(qs_scr[...], k_ref[0, pl.ds(start, size), :], (((1,), (1,)), ((), ())),
                               preferred_element_type=F32)

    def accumulate(s, start, size):
        p = jnp.exp2(s).astype(BF16)
        acc_scr[...] += jnp.dot(p, v_ref[0, pl.ds(start, size), :], preferred_element_type=F32)

    acc_scr[...] = jnp.zeros(acc_scr.shape, F32)
    accumulate(scores(0, ctx_len), 0, ctx_len)

    @pl.when(n > 0)
    def _():
        sa_scr[...] = scores(lat_start(0), tkc)

    def shifted_body(j, carry):
        c0 = 2 * j
        sb_scr[...] = scores(lat_start(c0 + 1), tkc)
        accumulate(sa_scr[...], lat_start(c0), tkc)
        sa_scr[...] = scores(lat_start(jnp.minimum(c0 + 2, n_lat_chunks - 1)), tkc)
        accumulate(sb_scr[...], lat_start(c0 + 1), tkc)
        return carry

    lax.fori_loop(0, n // 2, shifted_body, 0)
    acc = acc_scr[...]
    den = acc[:, HEAD_DIM:]
    write(acc[:, :HEAD_DIM] / den)

    @pl.when(jnp.logical_not(jnp.min(den) >= MIN_DENOM))
    def _():
        m_scr[...] = jnp.full(m_scr.shape, NEG_INF, F32)
        l_scr[...] = jnp.zeros(l_scr.shape, F32)
        acc_scr[...] = jnp.zeros(acc_scr.shape, F32)

        def online_chunk(start, size):
            k = k_ref[0, pl.ds(start, size), :HEAD_DIM]
            v = v_ref[0, pl.ds(start, size), :HEAD_DIM]
            s = lax.dot_general(qs_scr[:, :HEAD_DIM], k, (((1,), (1,)), ((), ())),
                                preferred_element_type=F32)
            m_old = m_scr[...]
            m_new = jnp.maximum(m_old, jnp.max(s, axis=-1, keepdims=True))
            alpha = jnp.exp2(m_old - m_new)
            p = jnp.exp2(s - m_new)
            l_scr[...] = alpha * l_scr[...] + jnp.sum(p, axis=-1, keepdims=True)
            acc_scr[:, :HEAD_DIM] = alpha * acc_scr[:, :HEAD_DIM] + jnp.dot(
                p.astype(BF16), v, preferred_element_type=F32)
            m_scr[...] = m_new

        online_chunk(0, ctx_len)

        def online_body(c, carry):
            online_chunk(lat_start(c), tkc)
            return carry

        lax.fori_loop(0, n, online_body, 0)
        write(acc_scr[:, :HEAD_DIM] / l_scr[...])


def _attn_call(qn, ke, ve, q_gain, k_gain, ctx_len):
    bsz, l, qw = qn.shape
    tq = 256
    assert ctx_len % tq == 0 and l % tq == 0
    t_lat = l - ctx_len
    tkc = _tile(t_lat // 2, 512, 256)
    gw = GROUP * HEAD_DIM
    ew = 2 * HEAD_DIM
    kern = functools.partial(_attn_kernel, tq=tq, tkc=tkc, ctx_len=ctx_len, n_lat_chunks=t_lat // tkc)
    return pl.pallas_call(
        kern,
        grid=(bsz, N_KV_HEADS, l // tq),
        in_specs=[pl.BlockSpec((1, tq, gw), lambda b, g, i: (b, i, g)),
                  pl.BlockSpec((1, l, ew), lambda b, g, i: (b, 0, g)),
                  pl.BlockSpec((1, l, ew), lambda b, g, i: (b, 0, g)),
                  pl.BlockSpec((1, HEAD_DIM), lambda b, g, i: (0, 0)),
                  pl.BlockSpec((1, HEAD_DIM), lambda b, g, i: (0, 0))],
        out_specs=pl.BlockSpec((1, tq, gw), lambda b, g, i: (b, i, g)),
        out_shape=jax.ShapeDtypeStruct((bsz, l, qw), BF16),
        scratch_shapes=[pltpu.VMEM((GROUP * tq, ew), BF16),
                        pltpu.VMEM((GROUP * tq, ew), F32),
                        pltpu.VMEM((GROUP * tq, tkc), F32),
                        pltpu.VMEM((GROUP * tq, tkc), F32),
                        pltpu.VMEM((GROUP * tq, 1), F32),
                        pltpu.VMEM((GROUP * tq, 1), F32)],
        compiler_params=_cparams(("parallel", "parallel", "arbitrary"), 48),
        name="attn",
    )(qn, ke, ve, q_gain.reshape(1, HEAD_DIM), k_gain.reshape(1, HEAD_DIM))


def _segment_bounds(row, ctx_len, seq_len):
    is_ctx = row < ctx_len
    first = jnp.where(is_ctx, 0, ctx_len)
    last = jnp.where(is_ctx, ctx_len - 1, seq_len - 1)
    return first, last


def _shift_down(x, prev, k, local):
    y = pltpu.roll(x, k, 0)
    for r in range(k):
        y = jnp.where(local == r, prev[HALO - k + r:HALO - k + r + 1, :], y)
    return y


def _shift_up(x, nxt, k, local, tt):
    y = pltpu.roll(x, tt - k, 0)
    for r in range(k):
        y = jnp.where(local == tt - k + r, nxt[r:r + 1, :], y)
    return y


def _halo_shifts(x, prev, nxt):
    tt = x.shape[0]
    sub = lax.broadcasted_iota(jnp.int32, (SUBLANES, x.shape[1]), 0)

    def down(k):
        y = pltpu.roll(x, k, 0)
        head = y[:SUBLANES]
        for r in range(k):
            head = jnp.where(sub == r, prev[HALO - k + r:HALO - k + r + 1, :], head)
        return jnp.concatenate([head, y[SUBLANES:]], axis=0)

    def up(k):
        y = pltpu.roll(x, tt - k, 0)
        tail = y[tt - SUBLANES:]
        for r in range(k):
            tail = jnp.where(sub == SUBLANES - k + r, nxt[r:r + 1, :], tail)
        return jnp.concatenate([y[:tt - SUBLANES], tail], axis=0)

    return down, up


def _scan_tile(a, d, carry, reverse):
    n_groups = a.shape[0] // SUBLANES
    sub = lax.broadcasted_iota(jnp.int32, (SUBLANES, a.shape[1]), 0)
    groups = []
    for v in range(n_groups):
        av = a[v * SUBLANES:(v + 1) * SUBLANES]
        dv = d[v * SUBLANES:(v + 1) * SUBLANES]
        for k in (1, 2, 4):
            keep = sub < SUBLANES - k if reverse else sub >= k
            shift = SUBLANES - k if reverse else k
            a_n = jnp.where(keep, pltpu.roll(av, shift, 0), 1.0)
            d_n = jnp.where(keep, pltpu.roll(dv, shift, 0), 0.0)
            dv = av * d_n + dv
            av = av * a_n
        groups.append((av, dv))
    hs = [None] * n_groups
    for v in (reversed(range(n_groups)) if reverse else range(n_groups)):
        av, dv = groups[v]
        hv = dv + av * carry
        carry = hv[0:1] if reverse else hv[SUBLANES - 1:SUBLANES]
        hs[v] = hv
    return jnp.concatenate(hs, axis=0), carry


def _lru_kernel(*refs, reverse, tt, ctx_len, n_sub):
    if reverse:
        (x_ref, xp_ref, xn_ref, cw_ref, cb_ref, wa_ref, ba_ref, wx_ref, bx_ref, lam_ref,
         hf_ref, g_ref, o_ref, carry_scr) = refs
    else:
        (x_ref, xp_ref, xn_ref, cw_ref, cb_ref, wa_ref, ba_ref, wx_ref, bx_ref, lam_ref,
         o_ref, carry_scr) = refs
    s = pl.program_id(2)
    nt = pl.num_programs(2)
    ti = jnp.where(s == 0, 0, nt - s) if reverse else s

    @pl.when(s == 0)
    def _():
        carry_scr[...] = jnp.zeros(carry_scr.shape, F32)

    prev_ok = jnp.logical_and(ti != 0, ti * tt != ctx_len)
    next_ok = jnp.logical_and(ti != nt - 1, (ti + 1) * tt != ctx_len)
    x = x_ref[0].astype(F32)
    xp = jnp.where(prev_ok, xp_ref[0].astype(F32), 0.0)
    xn = jnp.where(next_ok, xn_ref[0].astype(F32), 0.0)
    down, up = _halo_shifts(x, xp, xn)
    cw = cw_ref[...]
    u = cb_ref[...] + cw[2:3, :] * x + cw[0:1, :] * down(2) + cw[1:2, :] * down(1) + cw[3:4, :] * up(1)
    nlam = -lam_ref[...]
    softplus = jnp.maximum(nlam, 0.0) + jnp.log1p(jnp.exp(-jnp.abs(nlam)))

    for j in range(n_sub):
        ls = slice(j * LRU_BLOCK, (j + 1) * LRU_BLOCK)
        uj = u[:, ls]
        ub = uj.astype(BF16)
        r = jax.nn.sigmoid(jnp.dot(ub, wa_ref[j], preferred_element_type=F32) + ba_ref[:, ls])
        i = jax.nn.sigmoid(jnp.dot(ub, wx_ref[j], preferred_element_type=F32) + bx_ref[:, ls])
        a = jnp.exp((-LRU_C) * r * softplus[:, ls])
        d = jnp.sqrt(1.0 - a * a) * (i * uj)
        h, carry = _scan_tile(a, d, carry_scr[:, ls], reverse)
        carry_scr[:, ls] = carry
        if reverse:
            o_ref[0, :, ls] = ((hf_ref[0, :, ls] + h) * _gelu(g_ref[0, :, ls].astype(F32))).astype(BF16)
        else:
            o_ref[0, :, ls] = h


def _lru_call(p, conv_w, conv_b, wa, ba, wx, bx, lam, ctx_len, reverse, h_fwd=None):
    bsz, l, _ = p.shape
    c = conv_w.shape[1]
    tt = 256
    assert ctx_len % tt == 0 and l % tt == 0
    nt = l // tt
    n_sub = 2
    cw = n_sub * LRU_BLOCK
    nb = c // cw
    x_col0 = (N_HEADS + 2 * N_KV_HEADS) * HEAD_DIM // cw
    g_col0 = x_col0 + nb
    hb = tt // HALO
    n_hblk = l // HALO

    def tile_of(s):
        return jnp.where(s == 0, 0, nt - s) if reverse else s

    x_map = lambda b, cb, s: (b, tile_of(s), x_col0 + cb)
    prev_map = lambda b, cb, s: (b, jnp.maximum(tile_of(s) * hb - 1, 0), x_col0 + cb)
    next_map = lambda b, cb, s: (b, jnp.minimum((tile_of(s) + 1) * hb, n_hblk - 1), x_col0 + cb)
    vec_map = lambda b, cb, s: (0, cb)
    w_spec = pl.BlockSpec((n_sub, LRU_BLOCK, LRU_BLOCK), lambda b, cb, s: (cb, 0, 0))
    in_specs = [pl.BlockSpec((1, tt, cw), x_map),
                pl.BlockSpec((1, HALO, cw), prev_map),
                pl.BlockSpec((1, HALO, cw), next_map),
                pl.BlockSpec((LRU_CONV, cw), vec_map),
                pl.BlockSpec((1, cw), vec_map),
                w_spec,
                pl.BlockSpec((1, cw), vec_map),
                w_spec,
                pl.BlockSpec((1, cw), vec_map),
                pl.BlockSpec((1, cw), vec_map)]
    args = [p, p, p, conv_w, conv_b.reshape(1, c), wa.astype(BF16), ba.reshape(1, c),
            wx.astype(BF16), bx.reshape(1, c), lam.reshape(1, c)]
    out_map = lambda b, cb, s: (b, tile_of(s), cb)
    if reverse:
        in_specs += [pl.BlockSpec((1, tt, cw), out_map),
                     pl.BlockSpec((1, tt, cw), lambda b, cb, s: (b, tile_of(s), g_col0 + cb))]
        args += [h_fwd, p]
        out_dtype = BF16
    else:
        out_dtype = F32
    kern = functools.partial(_lru_kernel, reverse=reverse, tt=tt, ctx_len=ctx_len, n_sub=n_sub)
    return pl.pallas_call(
        kern,
        grid=(bsz, nb, nt),
        in_specs=in_specs,
        out_specs=pl.BlockSpec((1, tt, cw), out_map),
        out_shape=jax.ShapeDtypeStruct((bsz, l, c), out_dtype),
        scratch_shapes=[pltpu.VMEM((1, cw), F32)],
        compiler_params=_cparams(("parallel", "parallel", "arbitrary"), 32),
        name="lru_rev" if reverse else "lru_fwd",
    )(*args)


def _sconv_kernel(bg_ref, cg_ref, u_ref, cgp_ref, up_ref, cgn_ref, un_ref, w_ref, b_ref, o_ref,
                  *, tt, ctx_len, seq_len):
    ti = pl.program_id(1)
    z = cg_ref[0].astype(F32) * u_ref[0].astype(F32)
    zp = cgp_ref[0].astype(F32) * up_ref[0].astype(F32)
    zn = cgn_ref[0].astype(F32) * un_ref[0].astype(F32)
    local = lax.broadcasted_iota(jnp.int32, (tt, 1), 0)
    row = ti * tt + local
    first, last = _segment_bounds(row, ctx_len, seq_len)
    w = w_ref[...]
    y = b_ref[...] + w[1:2, :] * z
    y = y + w[0:1, :] * jnp.where(row - 1 >= first, _shift_down(z, zp, 1, local), 0.0)
    y = y + w[2:3, :] * jnp.where(row + 1 <= last, _shift_up(z, zn, 1, local, tt), 0.0)
    o_ref[0] = (bg_ref[0].astype(F32) * y).astype(BF16)


def _sconv_call(p, w, b, ctx_len):
    bsz, l, _ = p.shape
    c = w.shape[1]
    tt = _tile(l, 768, 16)
    tc = 512
    ncb = c // tc
    b_col0 = (N_HEADS * HEAD_DIM + 2 * N_KV_HEADS * HEAD_DIM + 2 * c) // tc
    c_col0 = b_col0 + ncb
    u_col0 = c_col0 + ncb
    hb = tt // HALO
    n_hblk = l // HALO
    prev = lambda i: jnp.maximum(i * hb - 1, 0)
    nxt = lambda i: jnp.minimum((i + 1) * hb, n_hblk - 1)
    kern = functools.partial(_sconv_kernel, tt=tt, ctx_len=ctx_len, seq_len=l)
    return pl.pallas_call(
        kern,
        grid=(bsz, l // tt, ncb),
        in_specs=[pl.BlockSpec((1, tt, tc), lambda b_, i, j: (b_, i, b_col0 + j)),
                  pl.BlockSpec((1, tt, tc), lambda b_, i, j: (b_, i, c_col0 + j)),
                  pl.BlockSpec((1, tt, tc), lambda b_, i, j: (b_, i, u_col0 + j)),
                  pl.BlockSpec((1, HALO, tc), lambda b_, i, j: (b_, prev(i), c_col0 + j)),
                  pl.BlockSpec((1, HALO, tc), lambda b_, i, j: (b_, prev(i), u_col0 + j)),
                  pl.BlockSpec((1, HALO, tc), lambda b_, i, j: (b_, nxt(i), c_col0 + j)),
                  pl.BlockSpec((1, HALO, tc), lambda b_, i, j: (b_, nxt(i), u_col0 + j)),
                  pl.BlockSpec((SC_CONV, tc), lambda b_, i, j: (0, j)),
                  pl.BlockSpec((1, tc), lambda b_, i, j: (0, j))],
        out_specs=pl.BlockSpec((1, tt, tc), lambda b_, i, j: (b_, i, j)),
        out_shape=jax.ShapeDtypeStruct((bsz, l, c), BF16),
        compiler_params=_cparams(("parallel", "parallel", "parallel"), 32),
        name="sconv",
    )(p, p, p, p, p, p, p, w, b.reshape(1, c))


def _merge_kernel(xa_ref, xl_ref, xs_ref, ga_ref, gl_ref, gs_ref, wa_ref, wl_ref, ws_ref, o_ref):
    y = jax.nn.sigmoid(ga_ref[0].astype(F32)) * jnp.dot(xa_ref[0], wa_ref[...], preferred_element_type=F32)
    y = y + jax.nn.sigmoid(gl_ref[0].astype(F32)) * jnp.dot(xl_ref[0], wl_ref[...], preferred_element_type=F32)
    y = y + jax.nn.sigmoid(gs_ref[0].astype(F32)) * jnp.dot(xs_ref[0], ws_ref[...], preferred_element_type=F32)
    o_ref[0] = y.astype(BF16)


def _merge_call(x_att, x_lru, x_sc, p, w_att, w_lru, w_sc):
    bsz, l, d = x_att.shape
    tm = _tile(l, 768, 16)
    tn = 512
    g_col0 = (p.shape[2] - 3 * d) // tn
    nj = d // tn
    xspec = pl.BlockSpec((1, tm, d), lambda b, i, j: (b, i, 0))
    wspec = pl.BlockSpec((d, tn), lambda b, i, j: (0, j))
    gspec = lambda k: pl.BlockSpec((1, tm, tn), lambda b, i, j: (b, i, g_col0 + k * nj + j))
    return pl.pallas_call(
        _merge_kernel,
        grid=(bsz, l // tm, nj),
        in_specs=[xspec, xspec, xspec, gspec(0), gspec(1), gspec(2), wspec, wspec, wspec],
        out_specs=pl.BlockSpec((1, tm, tn), lambda b, i, j: (b, i, j)),
        out_shape=jax.ShapeDtypeStruct((bsz, l, d), BF16),
        compiler_params=_cparams(("parallel", "parallel", "arbitrary"), 48),
        name="merge",
    )(x_att, x_lru, x_sc, p, p, p, w_att, w_lru, w_sc)


def _outproj_kernel(y_ref, w_ref, x_ref, gate_ref, shift_ref, scale_ref, g_ref, xo_ref, ht_ref,
                    *, tm, ctx_len):
    b = pl.program_id(0)
    i = pl.program_id(1)
    row = i * tm + lax.broadcasted_iota(jnp.int32, (tm, 1), 0)
    is_ctx = row < ctx_len
    acc = jnp.dot(y_ref[0], w_ref[...], preferred_element_type=F32)
    xn = x_ref[0] + _row_select(gate_ref, b, is_ctx) * acc
    xo_ref[0] = xn
    var = jnp.mean(xn * xn, axis=-1, keepdims=True)
    h = xn * lax.rsqrt(var + EPS) * g_ref[...]
    h = h * (1.0 + _row_select(scale_ref, b, is_ctx)) + _row_select(shift_ref, b, is_ctx)
    ht_ref[0] = h.T.astype(BF16)


def _outproj_call(y, w_out, x, mod_l, gain, ctx_len):
    bsz, l, d = x.shape
    tm = _tile(l, 384, LANES)
    kern = functools.partial(_outproj_kernel, tm=tm, ctx_len=ctx_len)
    mspec = lambda k: pl.BlockSpec((8, d), lambda b, i: (0, k))
    return pl.pallas_call(
        kern,
        grid=(bsz, l // tm),
        in_specs=[pl.BlockSpec((1, tm, d), lambda b, i: (b, i, 0)),
                  pl.BlockSpec((d, d), lambda b, i: (0, 0)),
                  pl.BlockSpec((1, tm, d), lambda b, i: (b, i, 0)),
                  mspec(2), mspec(3), mspec(4),
                  pl.BlockSpec((1, d), lambda b, i: (0, 0))],
        out_specs=[pl.BlockSpec((1, tm, d), lambda b, i: (b, i, 0)),
                   pl.BlockSpec((1, d, tm), lambda b, i: (b, 0, i))],
        out_shape=[jax.ShapeDtypeStruct((bsz, l, d), F32),
                   jax.ShapeDtypeStruct((bsz, d, l), BF16)],
        compiler_params=_cparams(("parallel", "parallel"), 48),
        name="outproj",
    )(y, w_out, x, mod_l, mod_l, mod_l, gain.reshape(1, d))


_CAND_ROWS = tuple(PEER_TOPK // (i + 1) for i in range(PEER_TOPK))


def _top16(s, v_scr, exact):
    n = s.shape[0]
    key = lax.broadcasted_iota(jnp.int32, s.shape, 0)

    def body(r, carry):
        work, rank = carry
        m = jnp.max(work, axis=0, keepdims=True)
        v_scr[pl.ds(r, 1), :] = m
        sel = work == m
        if exact:
            first = jnp.min(jnp.where(sel, key, n), axis=0, keepdims=True)
            sel = key == first
        return jnp.where(sel, NEG_INF, work), jnp.where(sel, jnp.asarray(r, F32), rank)

    _, rank = lax.fori_loop(0, PEER_TOPK, body, (s, jnp.full(s.shape, float(PEER_TOPK), F32)))
    return rank


def _peer_topk_kernel(ht_ref, wq_ref, k1_ref, k2_ref, cnt_ref, e1_ref, rk_ref, e2_ref,
                      q_scr, v1_scr, v2_scr, *, tm):
    h = pl.program_id(2)
    half = PEER_NKEYS

    @pl.when(h == 0)
    def _():
        q_scr[...] = jnp.dot(wq_ref[...], ht_ref[0], preferred_element_type=F32).astype(BF16)

    base = pl.multiple_of(h * 2 * half, 2 * half)
    sub = lax.broadcasted_iota(jnp.int32, (SUBLANES, LANES), 0)
    big = PEER_TOPK * PEER_TOPK

    def compute(ls, exact):
        s1 = jnp.dot(k1_ref[0], q_scr[pl.ds(base, half), ls], preferred_element_type=F32)
        s2 = jnp.dot(k2_ref[0], q_scr[pl.ds(base + half, half), ls], preferred_element_type=F32)
        rank1 = _top16(s1, v1_scr, exact)
        rank2 = _top16(s2, v2_scr, exact)
        v1 = v1_scr[...]
        v2 = v2_scr[...]

        pieces, poss = [], []
        for i in range(SUBLANES):
            for j0 in range(0, _CAND_ROWS[i], SUBLANES):
                c = v1[i:i + 1, :] + v2[j0:j0 + SUBLANES, :]
                valid = sub + j0 < _CAND_ROWS[i]
                pieces.append(jnp.where(valid, c, NEG_INF))
                poss.append(jnp.where(valid, i * PEER_TOPK + j0 + sub, big))
        pieces.append(v1[SUBLANES:, :] + v2[0:1, :])
        poss.append((sub + SUBLANES) * PEER_TOPK)
        orig = list(pieces)
        npc = len(pieces)

        def pick(_, carry2):
            cs, sels = carry2
            m = functools.reduce(jnp.maximum, cs)
            m = jnp.max(m, axis=0, keepdims=True)
            hit = [c == m for c in cs]
            if exact:
                cand_pos = functools.reduce(jnp.minimum, [jnp.where(hh, p_, big) for hh, p_ in zip(hit, poss)])
                first = jnp.min(cand_pos, axis=0, keepdims=True)
                hit = [p_ == first for p_ in poss]
            cs = tuple(jnp.where(hh, NEG_INF, c) for hh, c in zip(hit, cs))
            sels = tuple(jnp.where(hh, 1.0, s_) for hh, s_ in zip(hit, sels))
            return cs, sels

        zeros = tuple(jnp.zeros((SUBLANES, LANES), F32) for _ in range(npc))
        _, sels = lax.fori_loop(0, PEER_TOPK, pick, (tuple(pieces), zeros))

        top = v1[0:1, :] + v2[0:1, :]
        zsum = functools.reduce(
            lambda a_, b_: a_ + b_,
            [jnp.where(s_ > 0.0, jnp.exp(o - top), 0.0) for s_, o in zip(sels, orig)])
        zinv = 1.0 / jnp.sum(zsum, axis=0, keepdims=True)

        counts = []
        pi = 0
        for i in range(SUBLANES):
            c = None
            for j0 in range(0, _CAND_ROWS[i], SUBLANES):
                part = jnp.sum(sels[pi], axis=0, keepdims=True)
                c = part if c is None else c + part
                pi += 1
            counts.append(c)
        tail = sels[pi]
        for i in range(SUBLANES, PEER_TOPK):
            counts.append(tail[i - SUBLANES:i - SUBLANES + 1, :])
        cnt = jnp.zeros(rank1.shape, F32)
        for i in range(PEER_TOPK):
            cnt = jnp.where(rank1 == float(i), counts[i], cnt)

        cnt_ref[0, 0, :, ls] = cnt
        e1_ref[0, 0, :, ls] = jnp.exp(s1 - v1[0:1, :]) * zinv
        rk_ref[0, 0, :, ls] = rank2.astype(BF16)
        e2_ref[0, 0, :, ls] = jnp.exp(s2 - v2[0:1, :]).astype(BF16)
        ranked = jnp.where(rank1 < PEER_TOPK, 1.0, 0.0)
        ranked = jnp.maximum(jnp.sum(ranked, axis=0, keepdims=True),
                             jnp.sum(jnp.where(rank2 < PEER_TOPK, 1.0, 0.0), axis=0, keepdims=True))
        picked = jnp.sum(functools.reduce(lambda a_, b_: a_ + b_, sels), axis=0, keepdims=True)
        return jnp.max(jnp.maximum(ranked, picked))

    def lane_group(gidx, carry):
        ls = pl.ds(pl.multiple_of(gidx * LANES, LANES), LANES)
        most = compute(ls, exact=False)

        @pl.when(most > PEER_TOPK)
        def _():
            compute(ls, exact=True)

        return carry

    lax.fori_loop(0, tm // LANES, lane_group, 0)


def _peer_topk_call(ht, wq_t, k1, k2):
    bsz, d, l = ht.shape
    tm = _tile(l, 384, LANES)
    kern = functools.partial(_peer_topk_kernel, tm=tm)
    ospec = pl.BlockSpec((1, 1, PEER_NKEYS, tm), lambda b, i, h: (b, h, 0, i))
    oshape = jax.ShapeDtypeStruct((bsz, PEER_HEADS, PEER_NKEYS, l), F32)
    kspec = pl.BlockSpec((1, PEER_NKEYS, PEER_NKEYS), lambda b, i, h: (h, 0, 0))
    return pl.pallas_call(
        kern,
        grid=(bsz, l // tm, PEER_HEADS),
        in_specs=[pl.BlockSpec((1, d, tm), lambda b, i, h: (b, 0, i)),
                  pl.BlockSpec(wq_t.shape, lambda b, i, h: (0, 0)),
                  kspec, kspec],
        out_specs=[ospec, ospec, ospec, ospec],
        out_shape=[oshape, oshape, jax.ShapeDtypeStruct(oshape.shape, BF16),
                   jax.ShapeDtypeStruct(oshape.shape, BF16)],
        scratch_shapes=[pltpu.VMEM((wq_t.shape[0], tm), BF16),
                        pltpu.VMEM((PEER_TOPK, LANES), F32),
                        pltpu.VMEM((PEER_TOPK, LANES), F32)],
        compiler_params=_cparams(("parallel", "parallel", "arbitrary"), 40),
        name="peer_topk",
    )(ht, wq_t, k1, k2)


def _peer_dense_kernel(ht_ref, u_ref, vt_ref, cnt_ref, e1_ref, rk_ref, e2_ref, o_ref, wz_scr, *, n_sub):
    e = pl.program_id(2)

    @pl.when(e == 0)
    def _():
        o_ref[...] = jnp.zeros(o_ref.shape, F32)

    s = jnp.dot(u_ref[...], ht_ref[0], preferred_element_type=F32)
    for a in range(n_sub):
        rows = slice(a * PEER_NKEYS, (a + 1) * PEER_NKEYS)
        w = None
        for h in range(PEER_HEADS):
            cnt = cnt_ref[0, h, a:a + 1, :].astype(BF16)
            e1 = e1_ref[0, h, a:a + 1, :].astype(BF16)
            term = jnp.where(rk_ref[0, h] < cnt, e2_ref[0, h] * e1, 0.0)
            w = term if w is None else w + term
        wz_scr[rows, :] = w * _gelu(s[rows, :]).astype(BF16)
    o_ref[0] += jnp.dot(vt_ref[...], wz_scr[...], preferred_element_type=F32)


def _peer_dense_call(ht, u_bf16, vt_bf16, cnt, e1n, rank2, e2):
    bsz, d, l = ht.shape
    n_exp = u_bf16.shape[0]
    tm = _tile(l, 768, LANES)
    n_sub = SUBLANES
    te = n_sub * PEER_NKEYS
    kern = functools.partial(_peer_dense_kernel, n_sub=n_sub)
    aspec = pl.BlockSpec((1, PEER_HEADS, n_sub, tm), lambda b, i, e: (b, 0, e, i))
    fspec = pl.BlockSpec((1, PEER_HEADS, PEER_NKEYS, tm), lambda b, i, e: (b, 0, 0, i))
    return pl.pallas_call(
        kern,
        grid=(bsz, l // tm, n_exp // te),
        in_specs=[pl.BlockSpec((1, d, tm), lambda b, i, e: (b, 0, i)),
                  pl.BlockSpec((te, d), lambda b, i, e: (e, 0)),
                  pl.BlockSpec((d, te), lambda b, i, e: (0, e)),
                  aspec, aspec, fspec, fspec],
        out_specs=pl.BlockSpec((1, d, tm), lambda b, i, e: (b, 0, i)),
        out_shape=jax.ShapeDtypeStruct((bsz, d, l), F32),
        scratch_shapes=[pltpu.VMEM((te, tm), BF16)],
        compiler_params=_cparams(("parallel", "parallel", "arbitrary"), 58),
        name="peer_dense",
    )(ht, u_bf16, vt_bf16, cnt, e1n, rank2, e2)


def _resid_kernel(x_ref, yt_ref, gate_ref, o_ref, *, tm, ctx_len):
    b = pl.program_id(0)
    i = pl.program_id(1)
    row = i * tm + lax.broadcasted_iota(jnp.int32, (tm, 1), 0)
    o_ref[0] = x_ref[0] + _row_select(gate_ref, b, row < ctx_len) * yt_ref[0].T


def _resid_call(x, yt, mod_l, ctx_len):
    bsz, l, d = x.shape
    tm = _tile(l, 384, LANES)
    kern = functools.partial(_resid_kernel, tm=tm, ctx_len=ctx_len)
    return pl.pallas_call(
        kern,
        grid=(bsz, l // tm),
        in_specs=[pl.BlockSpec((1, tm, d), lambda b, i: (b, i, 0)),
                  pl.BlockSpec((1, d, tm), lambda b, i: (b, 0, i)),
                  pl.BlockSpec((8, d), lambda b, i: (0, 5))],
        out_specs=pl.BlockSpec((1, tm, d), lambda b, i: (b, i, 0)),
        out_shape=jax.ShapeDtypeStruct((bsz, l, d), F32),
        compiler_params=_cparams(("parallel", "parallel"), 40),
        name="resid",
    )(x, yt, mod_l)


def _rope_tables(ctx_len, t_lat):
    rows = t_lat // GRID_W
    row = jnp.repeat(jnp.arange(rows, dtype=F32), GRID_W)
    col = jnp.tile(jnp.arange(GRID_W, dtype=F32), rows)
    inv = ROPE_THETA ** (-jnp.arange(ROPE_PAIRS, dtype=F32) / ROPE_PAIRS)
    ang = jnp.concatenate([row[:, None] * inv] * 2 + [col[:, None] * inv] * 2, axis=1)
    ang = jnp.concatenate([jnp.zeros((ctx_len, HEAD_DIM), F32), ang], axis=0)
    sign = jnp.where((jnp.arange(HEAD_DIM) & ROPE_PAIRS) == 0, -1.0, 1.0).astype(F32)
    return jnp.cos(ang), jnp.sin(ang) * sign


def kernel(x, c, ctx, c_ctx, w_mod, b_mod, norm_mix, norm_ffn, w_in, q_norm, k_norm, lru_conv_w, lru_conv_b, lru_wa, lru_ba, lru_wx, lru_bx, lru_lambda, sc_conv_w, sc_conv_b, w_o_attn, w_o_lru, w_o_sc, w_out, peer_wq, peer_k1, peer_k2, peer_u, peer_v):
    bsz, t_lat, d = x.shape
    ctx_len = ctx.shape[1]
    depth = w_mod.shape[0]
    assert bsz == 2, "modulation rows are laid out as [latent 0, latent 1, context]"

    xs = jnp.concatenate([ctx, x], axis=1)
    s8 = jnp.concatenate([c, c_ctx[None, :], jnp.zeros((8 - bsz - 1, d), F32)], axis=0)
    mod = _mod_call(s8, w_mod, b_mod)
    cos, sin_signed = _rope_tables(ctx_len, t_lat)

    for l in range(depth):
        mod_l = mod[l]
        p = _inproj_call(xs, mod_l, norm_mix[l], w_in[l].astype(BF16), ctx_len)
        qn, ke, ve = _qkprep_call(p, cos, sin_signed, q_norm[l], k_norm[l])
        x_att = _attn_call(qn, ke, ve, q_norm[l], k_norm[l], ctx_len)
        lru_args = (lru_conv_w[l], lru_conv_b[l])
        h_fwd = _lru_call(p, *lru_args, lru_wa[l, 0], lru_ba[l, 0], lru_wx[l, 0], lru_bx[l, 0],
                          lru_lambda[l, 0], ctx_len, reverse=False)
        x_lru = _lru_call(p, *lru_args, lru_wa[l, 1], lru_ba[l, 1], lru_wx[l, 1], lru_bx[l, 1],
                          lru_lambda[l, 1], ctx_len, reverse=True, h_fwd=h_fwd)
        x_sc = _sconv_call(p, sc_conv_w[l], sc_conv_b[l], ctx_len)
        y = _merge_call(x_att, x_lru, x_sc, p, w_o_attn[l].astype(BF16), w_o_lru[l].astype(BF16),
                        w_o_sc[l].astype(BF16))
        xs, ht = _outproj_call(y, w_out[l].astype(BF16), xs, mod_l, norm_ffn[l], ctx_len)
        cnt, e1n, rank2, e2 = _peer_topk_call(ht, peer_wq[l].T.astype(BF16), peer_k1[l].astype(BF16),
                                              peer_k2[l].astype(BF16))
        yt = _peer_dense_call(ht, peer_u[l].astype(BF16), peer_v[l].T.astype(BF16), cnt, e1n, rank2, e2)
        xs = _resid_call(xs, yt, mod_l, ctx_len)
    return xs[:, ctx_len:, :]
```

```python
import functools
import math

import numpy as np
import jax
import jax.numpy as jnp
from jax import lax
from jax.experimental import pallas as pl
from jax.experimental.pallas import tpu as pltpu

F32 = jnp.float32
BF16 = jnp.bfloat16

GRID_W = 64
EPS = 1e-6
N_MOD = 6

N_HEADS = 16
N_KV_HEADS = 4
HEAD_DIM = 128
GROUP = N_HEADS // N_KV_HEADS
ROPE_PAIRS = HEAD_DIM // 4
ROPE_THETA = 10000.0

LRU_BLOCKS = 16
LRU_BLOCK = 128
LRU_CONV = 4
LRU_C = 8.0
SC_CONV = 3

PEER_HEADS = 8
PEER_NKEYS = 128
PEER_TOPK = 16

LANES = 128
SUBLANES = 8
HALO = SUBLANES
NEG_INF = float("-inf")
LOG2E = 1.4426950408889634
GELU_C = math.sqrt(2.0 / math.pi)
Q_SCALE = HEAD_DIM ** -0.5 * LOG2E
SHIFT_MARGIN = 1.02
MIN_DENOM = 2.0 ** -100


def _tile(n, target, mult):
    best = None
    for t in range(mult, min(n, target) + 1, mult):
        if n % t == 0:
            best = t
    assert best is not None, (n, target, mult)
    return best


def _cparams(sem, vmem_mib):
    return pltpu.CompilerParams(dimension_semantics=sem, vmem_limit_bytes=vmem_mib << 20)


def _gelu(x):
    return 0.5 * x * (1.0 + jnp.tanh(GELU_C * (x + 0.044715 * (x * x * x))))


def _row_select(mod_ref, b, is_ctx):
    return jnp.where(is_ctx, mod_ref[2:3, :], mod_ref[pl.ds(b, 1), :])


def _mod_kernel(s_ref, w_ref, b_ref, o_ref):
    s = s_ref[...]
    s = s * jax.nn.sigmoid(s)
    o_ref[0] = jnp.dot(s, w_ref[0], preferred_element_type=F32,
                       precision=lax.Precision.HIGHEST) + b_ref[0]


def _mod_call(s8, w_mod, b_mod):
    depth, d, n = w_mod.shape
    tn = _tile(n, 1024, LANES)
    return pl.pallas_call(
        _mod_kernel,
        grid=(depth, n // tn),
        in_specs=[pl.BlockSpec((8, d), lambda l, j: (0, 0)),
                  pl.BlockSpec((1, d, tn), lambda l, j: (l, 0, j)),
                  pl.BlockSpec((1, 1, tn), lambda l, j: (l, 0, j))],
        out_specs=pl.BlockSpec((1, 8, tn), lambda l, j: (l, 0, j)),
        out_shape=jax.ShapeDtypeStruct((depth, 8, n), F32),
        compiler_params=_cparams(("parallel", "parallel"), 32),
        name="mod",
    )(s8, w_mod, b_mod.reshape(depth, 1, n))


def _inproj_kernel(x_ref, shift_ref, scale_ref, g_ref, w_ref, o_ref, h_scr, *, tm, ctx_len):
    b = pl.program_id(0)
    i = pl.program_id(1)

    @pl.when(pl.program_id(2) == 0)
    def _():
        x = x_ref[0]
        var = jnp.mean(x * x, axis=-1, keepdims=True)
        y = x * lax.rsqrt(var + EPS) * g_ref[...]
        row = i * tm + lax.broadcasted_iota(jnp.int32, (tm, 1), 0)
        is_ctx = row < ctx_len
        sh = _row_select(shift_ref, b, is_ctx)
        sc = _row_select(scale_ref, b, is_ctx)
        h_scr[...] = (y * (1.0 + sc) + sh).astype(BF16)

    o_ref[0] = jnp.dot(h_scr[...], w_ref[...], preferred_element_type=F32).astype(BF16)


def _inproj_call(x, mod, layer, gain, w_bf16, ctx_len):
    bsz, l, d = x.shape
    n = w_bf16.shape[2]
    tm = _tile(l, 704, 16)
    tn = 1024
    kern = functools.partial(_inproj_kernel, tm=tm, ctx_len=ctx_len)
    return pl.pallas_call(
        kern,
        grid=(bsz, l // tm, n // tn),
        in_specs=[pl.BlockSpec((1, tm, d), lambda b, i, j: (b, i, 0)),
                  pl.BlockSpec((None, 8, d), lambda b, i, j: (layer, 0, 0)),
                  pl.BlockSpec((None, 8, d), lambda b, i, j: (layer, 0, 1)),
                  pl.BlockSpec((1, d), lambda b, i, j: (0, 0)),
                  pl.BlockSpec((None, d, tn), lambda b, i, j: (layer, 0, j))],
        out_specs=pl.BlockSpec((1, tm, tn), lambda b, i, j: (b, i, j)),
        out_shape=jax.ShapeDtypeStruct((bsz, l, n), BF16),
        scratch_shapes=[pltpu.VMEM((tm, d), BF16)],
        compiler_params=_cparams(("parallel", "parallel", "arbitrary"), 48),
        name="inproj",
    )(x, mod, mod, gain.reshape(1, d), w_bf16)


def _norm_rope(t, gain, cos, sin_signed, lane_lo):
    var = jnp.mean(t * t, axis=-1, keepdims=True)
    y = t * lax.rsqrt(var + EPS) * gain
    swapped = jnp.where(lane_lo, pltpu.roll(y, HEAD_DIM - ROPE_PAIRS, 1), pltpu.roll(y, ROPE_PAIRS, 1))
    return y * cos + swapped * sin_signed


def _qkprep_kernel(q_ref, k_ref, v_ref, cos_ref, sin_ref, qg_ref, kg_ref, qo_ref, ko_ref, vo_ref):
    cos = cos_ref[...]
    sin = sin_ref[...]
    lane = lax.broadcasted_iota(jnp.int32, cos.shape, 1)
    lane_lo = (lane & ROPE_PAIRS) == 0
    qg = qg_ref[...]
    kg = kg_ref[...]
    for h in range(N_HEADS):
        sl = slice(h * HEAD_DIM, (h + 1) * HEAD_DIM)
        t = q_ref[0, :, sl].astype(F32)
        qo_ref[0, :, sl] = (_norm_rope(t, qg, cos, sin, lane_lo) * Q_SCALE).astype(BF16)
    k_tail = jnp.where(lane == 0, 1.0, 0.0).astype(BF16)
    v_tail = jnp.ones(cos.shape, BF16)
    for h in range(N_KV_HEADS):
        sl = slice(h * HEAD_DIM, (h + 1) * HEAD_DIM)
        lo = slice(2 * h * HEAD_DIM, (2 * h + 1) * HEAD_DIM)
        hi = slice((2 * h + 1) * HEAD_DIM, (2 * h + 2) * HEAD_DIM)
        t = k_ref[0, :, sl].astype(F32)
        ko_ref[0, :, lo] = _norm_rope(t, kg, cos, sin, lane_lo).astype(BF16)
        ko_ref[0, :, hi] = k_tail
        vo_ref[0, :, lo] = v_ref[0, :, sl]
        vo_ref[0, :, hi] = v_tail


def _qkprep_call(p, cos, sin_signed, q_gain, k_gain):
    bsz, l, _ = p.shape
    qw = N_HEADS * HEAD_DIM
    kw = N_KV_HEADS * HEAD_DIM
    tm = _tile(l, 768, 16)
    return pl.pallas_call(
        _qkprep_kernel,
        grid=(bsz, l // tm),
        in_specs=[pl.BlockSpec((1, tm, qw), lambda b, i: (b, i, 0)),
                  pl.BlockSpec((1, tm, kw), lambda b, i: (b, i, qw // kw)),
                  pl.BlockSpec((1, tm, kw), lambda b, i: (b, i, qw // kw + 1)),
                  pl.BlockSpec((tm, HEAD_DIM), lambda b, i: (i, 0)),
                  pl.BlockSpec((tm, HEAD_DIM), lambda b, i: (i, 0)),
                  pl.BlockSpec((1, HEAD_DIM), lambda b, i: (0, 0)),
                  pl.BlockSpec((1, HEAD_DIM), lambda b, i: (0, 0))],
        out_specs=[pl.BlockSpec((1, tm, qw), lambda b, i: (b, i, 0)),
                   pl.BlockSpec((1, tm, 2 * kw), lambda b, i: (b, i, 0)),
                   pl.BlockSpec((1, tm, 2 * kw), lambda b, i: (b, i, 0))],
        out_shape=[jax.ShapeDtypeStruct((bsz, l, qw), BF16),
                   jax.ShapeDtypeStruct((bsz, l, 2 * kw), BF16),
                   jax.ShapeDtypeStruct((bsz, l, 2 * kw), BF16)],
        compiler_params=_cparams(("parallel", "parallel"), 40),
        name="qkprep",
    )(p, p, p, cos, sin_signed, q_gain.reshape(1, HEAD_DIM), k_gain.reshape(1, HEAD_DIM))


def _attn_kernel(q_ref, k_ref, v_ref, qg_ref, kg_ref, o_ref, qs_scr, acc_scr, sa_scr, sb_scr, m_scr, l_scr,
                 *, tq, tkc, ctx_len, n_lat_chunks):
    qi = pl.program_id(2)
    q = q_ref[0]
    gq = jnp.max(jnp.abs(qg_ref[...]), axis=-1, keepdims=True)
    gk = jnp.max(jnp.abs(kg_ref[...]), axis=-1, keepdims=True)
    bound = (SHIFT_MARGIN * HEAD_DIM * Q_SCALE) * gq * gk
    lane = lax.broadcasted_iota(jnp.int32, (1, HEAD_DIM), 1)
    tail = jnp.where(lane == 0, -bound, 0.0).astype(BF16)
    for h in range(GROUP):
        qs_scr[h * tq:(h + 1) * tq, :HEAD_DIM] = q[:, h * HEAD_DIM:(h + 1) * HEAD_DIM]
        qs_scr[h * tq:(h + 1) * tq, HEAD_DIM:] = jnp.broadcast_to(tail, (tq, HEAD_DIM))
    n = jnp.where(qi < ctx_len // tq, 0, n_lat_chunks)

    def lat_start(c):
        return pl.multiple_of(ctx_len + c * tkc, math.gcd(ctx_len, tkc))

    def write(out):
        for h in range(GROUP):
            o_ref[0, :, h * HEAD_DIM:(h + 1) * HEAD_DIM] = out[h * tq:(h + 1) * tq, :].astype(BF16)

    def scores(start, size):
        return lax.dot_general(qs_scr[...], k_ref[0, pl.ds(start, size), :], (((1,), (1,)), ((), ())),
                               preferred_element_type=F32)

    def accumulate(s, start, size):
        p = jnp.exp2(s).astype(BF16)
        acc_scr[...] += jnp.dot(p, v_ref[0, pl.ds(start, size), :], preferred_element_type=F32)

    acc_scr[...] = jnp.zeros(acc_scr.shape, F32)
    accumulate(scores(0, ctx_len), 0, ctx_len)

    @pl.when(n > 0)
    def _():
        sa_scr[...] = scores(lat_start(0), tkc)

    def shifted_body(j, carry):
        c0 = 2 * j
        sb_scr[...] = scores(lat_start(c0 + 1), tkc)
        accumulate(sa_scr[...], lat_start(c0), tkc)
        sa_scr[...] = scores(lat_start(jnp.minimum(c0 + 2, n_lat_chunks - 1)), tkc)
        accumulate(sb_scr[...], lat_start(c0 + 1), tkc)
        return carry

    lax.fori_loop(0, n // 2, shifted_body, 0)
    acc = acc_scr[...]
    den = acc[:, HEAD_DIM:]
    write(acc[:, :HEAD_DIM] / den)

    @pl.when(jnp.logical_not(jnp.min(den) >= MIN_DENOM))
    def _():
        m_scr[...] = jnp.full(m_scr.shape, NEG_INF, F32)
        l_scr[...] = jnp.zeros(l_scr.shape, F32)
        acc_scr[...] = jnp.zeros(acc_scr.shape, F32)

        def online_chunk(start, size):
            k = k_ref[0, pl.ds(start, size), :HEAD_DIM]
            v = v_ref[0, pl.ds(start, size), :HEAD_DIM]
            s = lax.dot_general(qs_scr[:, :HEAD_DIM], k, (((1,), (1,)), ((), ())),
                                preferred_element_type=F32)
            m_old = m_scr[...]
            m_new = jnp.maximum(m_old, jnp.max(s, axis=-1, keepdims=True))
            alpha = jnp.exp2(m_old - m_new)
            p = jnp.exp2(s - m_new)
            l_scr[...] = alpha * l_scr[...] + jnp.sum(p, axis=-1, keepdims=True)
            acc_scr[:, :HEAD_DIM] = alpha * acc_scr[:, :HEAD_DIM] + jnp.dot(
                p.astype(BF16), v, preferred_element_type=F32)
            m_scr[...] = m_new

        online_chunk(0, ctx_len)

        def online_body(c, carry):
            online_chunk(lat_start(c), tkc)
            return carry

        lax.fori_loop(0, n, online_body, 0)
        write(acc_scr[:, :HEAD_DIM] / l_scr[...])


def _attn_call(qn, ke, ve, q_gain, k_gain, ctx_len):
    bsz, l, qw = qn.shape
    tq = 256
    assert ctx_len % tq == 0 and l % tq == 0
    t_lat = l - ctx_len
    tkc = _tile(t_lat // 2, 512, 256)
    gw = GROUP * HEAD_DIM
    ew = 2 * HEAD_DIM
    kern = functools.partial(_attn_kernel, tq=tq, tkc=tkc, ctx_len=ctx_len, n_lat_chunks=t_lat // tkc)
    return pl.pallas_call(
        kern,
        grid=(bsz, N_KV_HEADS, l // tq),
        in_specs=[pl.BlockSpec((1, tq, gw), lambda b, g, i: (b, i, g)),
                  pl.BlockSpec((1, l, ew), lambda b, g, i: (b, 0, g)),
                  pl.BlockSpec((1, l, ew), lambda b, g, i: (b, 0, g)),
                  pl.BlockSpec((1, HEAD_DIM), lambda b, g, i: (0, 0)),
                  pl.BlockSpec((1, HEAD_DIM), lambda b, g, i: (0, 0))],
        out_specs=pl.BlockSpec((1, tq, gw), lambda b, g, i: (b, i, g)),
        out_shape=jax.ShapeDtypeStruct((bsz, l, qw), BF16),
        scratch_shapes=[pltpu.VMEM((GROUP * tq, ew), BF16),
                        pltpu.VMEM((GROUP * tq, ew), F32),
                        pltpu.VMEM((GROUP * tq, tkc), F32),
                        pltpu.VMEM((GROUP * tq, tkc), F32),
                        pltpu.VMEM((GROUP * tq, 1), F32),
                        pltpu.VMEM((GROUP * tq, 1), F32)],
        compiler_params=_cparams(("parallel", "parallel", "arbitrary"), 48),
        name="attn",
    )(qn, ke, ve, q_gain.reshape(1, HEAD_DIM), k_gain.reshape(1, HEAD_DIM))


def _segment_bounds(row, ctx_len, seq_len):
    is_ctx = row < ctx_len
    first = jnp.where(is_ctx, 0, ctx_len)
    last = jnp.where(is_ctx, ctx_len - 1, seq_len - 1)
    return first, last


def _shift_down(x, prev, k, local):
    y = pltpu.roll(x, k, 0)
    for r in range(k):
        y = jnp.where(local == r, prev[HALO - k + r:HALO - k + r + 1, :], y)
    return y


def _shift_up(x, nxt, k, local, tt):
    y = pltpu.roll(x, tt - k, 0)
    for r in range(k):
        y = jnp.where(local == tt - k + r, nxt[r:r + 1, :], y)
    return y


def _halo_shifts(x, prev, nxt):
    tt = x.shape[0]
    sub = lax.broadcasted_iota(jnp.int32, (SUBLANES, x.shape[1]), 0)

    def down(k):
        y = pltpu.roll(x, k, 0)
        head = y[:SUBLANES]
        for r in range(k):
            head = jnp.where(sub == r, prev[HALO - k + r:HALO - k + r + 1, :], head)
        return jnp.concatenate([head, y[SUBLANES:]], axis=0)

    def up(k):
        y = pltpu.roll(x, tt - k, 0)
        tail = y[tt - SUBLANES:]
        for r in range(k):
            tail = jnp.where(sub == SUBLANES - k + r, nxt[r:r + 1, :], tail)
        return jnp.concatenate([y[:tt - SUBLANES], tail], axis=0)

    return down, up


def _scan_tile(a, d, carry, reverse):
    n_groups = a.shape[0] // SUBLANES
    sub = lax.broadcasted_iota(jnp.int32, (SUBLANES, a.shape[1]), 0)
    groups = []
    for v in range(n_groups):
        av = a[v * SUBLANES:(v + 1) * SUBLANES]
        dv = d[v * SUBLANES:(v + 1) * SUBLANES]
        for k in (1, 2, 4):
            keep = sub < SUBLANES - k if reverse else sub >= k
            shift = SUBLANES - k if reverse else k
            a_n = jnp.where(keep, pltpu.roll(av, shift, 0), 1.0)
            d_n = jnp.where(keep, pltpu.roll(dv, shift, 0), 0.0)
            dv = av * d_n + dv
            av = av * a_n
        groups.append((av, dv))
    hs = [None] * n_groups
    for v in (reversed(range(n_groups)) if reverse else range(n_groups)):
        av, dv = groups[v]
        hv = dv + av * carry
        carry = hv[0:1] if reverse else hv[SUBLANES - 1:SUBLANES]
        hs[v] = hv
    return jnp.concatenate(hs, axis=0), carry


def _lru_kernel(*refs, reverse, tt, ctx_len, n_sub):
    if reverse:
        (x_ref, xp_ref, xn_ref, cw_ref, cb_ref, wa_ref, ba_ref, wx_ref, bx_ref, lam_ref,
         hf_ref, g_ref, o_ref, carry_scr) = refs
    else:
        (x_ref, xp_ref, xn_ref, cw_ref, cb_ref, wa_ref, ba_ref, wx_ref, bx_ref, lam_ref,
         o_ref, carry_scr) = refs
    s = pl.program_id(2)
    nt = pl.num_programs(2)
    ti = jnp.where(s == 0, 0, nt - s) if reverse else s

    @pl.when(s == 0)
    def _():
        carry_scr[...] = jnp.zeros(carry_scr.shape, F32)

    prev_ok = jnp.logical_and(ti != 0, ti * tt != ctx_len)
    next_ok = jnp.logical_and(ti != nt - 1, (ti + 1) * tt != ctx_len)
    x = x_ref[0].astype(F32)
    xp = jnp.where(prev_ok, xp_ref[0].astype(F32), 0.0)
    xn = jnp.where(next_ok, xn_ref[0].astype(F32), 0.0)
    down, up = _halo_shifts(x, xp, xn)
    cw = cw_ref[...]
    u = cb_ref[...] + cw[2:3, :] * x + cw[0:1, :] * down(2) + cw[1:2, :] * down(1) + cw[3:4, :] * up(1)
    nlam = -lam_ref[...]
    softplus = jnp.maximum(nlam, 0.0) + jnp.log1p(jnp.exp(-jnp.abs(nlam)))

    for j in range(n_sub):
        ls = slice(j * LRU_BLOCK, (j + 1) * LRU_BLOCK)
        uj = u[:, ls]
        ub = uj.astype(BF16)
        r = jax.nn.sigmoid(jnp.dot(ub, wa_ref[j], preferred_element_type=F32) + ba_ref[:, ls])
        i = jax.nn.sigmoid(jnp.dot(ub, wx_ref[j], preferred_element_type=F32) + bx_ref[:, ls])
        a = jnp.exp((-LRU_C) * r * softplus[:, ls])
        d = jnp.sqrt(1.0 - a * a) * (i * uj)
        h, carry = _scan_tile(a, d, carry_scr[:, ls], reverse)
        carry_scr[:, ls] = carry
        if reverse:
            o_ref[0, :, ls] = ((hf_ref[0, :, ls] + h) * _gelu(g_ref[0, :, ls].astype(F32))).astype(BF16)
        else:
            o_ref[0, :, ls] = h


def _lru_call(p, conv_w, conv_b, wa, ba, wx, bx, lam, ctx_len, reverse, h_fwd=None):
    bsz, l, _ = p.shape
    c = conv_w.shape[1]
    tt = 256
    assert ctx_len % tt == 0 and l % tt == 0
    nt = l // tt
    n_sub = 4
    cw = n_sub * LRU_BLOCK
    nb = c // cw
    x_col0 = (N_HEADS + 2 * N_KV_HEADS) * HEAD_DIM // cw
    g_col0 = x_col0 + nb
    hb = tt // HALO
    n_hblk = l // HALO

    def tile_of(s):
        return jnp.where(s == 0, 0, nt - s) if reverse else s

    x_map = lambda b, cb, s: (b, tile_of(s), x_col0 + cb)
    prev_map = lambda b, cb, s: (b, jnp.maximum(tile_of(s) * hb - 1, 0), x_col0 + cb)
    next_map = lambda b, cb, s: (b, jnp.minimum((tile_of(s) + 1) * hb, n_hblk - 1), x_col0 + cb)
    vec_map = lambda b, cb, s: (0, cb)
    w_spec = pl.BlockSpec((n_sub, LRU_BLOCK, LRU_BLOCK), lambda b, cb, s: (cb, 0, 0))
    in_specs = [pl.BlockSpec((1, tt, cw), x_map),
                pl.BlockSpec((1, HALO, cw), prev_map),
                pl.BlockSpec((1, HALO, cw), next_map),
                pl.BlockSpec((LRU_CONV, cw), vec_map),
                pl.BlockSpec((1, cw), vec_map),
                w_spec,
                pl.BlockSpec((1, cw), vec_map),
                w_spec,
                pl.BlockSpec((1, cw), vec_map),
                pl.BlockSpec((1, cw), vec_map)]
    args = [p, p, p, conv_w, conv_b.reshape(1, c), wa.astype(BF16), ba.reshape(1, c),
            wx.astype(BF16), bx.reshape(1, c), lam.reshape(1, c)]
    out_map = lambda b, cb, s: (b, tile_of(s), cb)
    if reverse:
        in_specs += [pl.BlockSpec((1, tt, cw), out_map),
                     pl.BlockSpec((1, tt, cw), lambda b, cb, s: (b, tile_of(s), g_col0 + cb))]
        args += [h_fwd, p]
        out_dtype = BF16
    else:
        out_dtype = F32
    kern = functools.partial(_lru_kernel, reverse=reverse, tt=tt, ctx_len=ctx_len, n_sub=n_sub)
    return pl.pallas_call(
        kern,
        grid=(bsz, nb, nt),
        in_specs=in_specs,
        out_specs=pl.BlockSpec((1, tt, cw), out_map),
        out_shape=jax.ShapeDtypeStruct((bsz, l, c), out_dtype),
        scratch_shapes=[pltpu.VMEM((1, cw), F32)],
        compiler_params=_cparams(("parallel", "parallel", "arbitrary"), 32),
        name="lru_rev" if reverse else "lru_fwd",
    )(*args)


def _sconv_kernel(bg_ref, cg_ref, u_ref, cgp_ref, up_ref, cgn_ref, un_ref, w_ref, b_ref, o_ref,
                  *, tt, ctx_len, seq_len):
    ti = pl.program_id(1)
    z = cg_ref[0].astype(F32) * u_ref[0].astype(F32)
    zp = cgp_ref[0].astype(F32) * up_ref[0].astype(F32)
    zn = cgn_ref[0].astype(F32) * un_ref[0].astype(F32)
    local = lax.broadcasted_iota(jnp.int32, (tt, 1), 0)
    row = ti * tt + local
    first, last = _segment_bounds(row, ctx_len, seq_len)
    w = w_ref[...]
    y = b_ref[...] + w[1:2, :] * z
    y = y + w[0:1, :] * jnp.where(row - 1 >= first, _shift_down(z, zp, 1, local), 0.0)
    y = y + w[2:3, :] * jnp.where(row + 1 <= last, _shift_up(z, zn, 1, local, tt), 0.0)
    o_ref[0] = (bg_ref[0].astype(F32) * y).astype(BF16)


def _sconv_call(p, w, b, ctx_len):
    bsz, l, _ = p.shape
    c = w.shape[1]
    tt = _tile(l, 768, 16)
    tc = 512
    ncb = c // tc
    b_col0 = (N_HEADS * HEAD_DIM + 2 * N_KV_HEADS * HEAD_DIM + 2 * c) // tc
    c_col0 = b_col0 + ncb
    u_col0 = c_col0 + ncb
    hb = tt // HALO
    n_hblk = l // HALO
    prev = lambda i: jnp.maximum(i * hb - 1, 0)
    nxt = lambda i: jnp.minimum((i + 1) * hb, n_hblk - 1)
    kern = functools.partial(_sconv_kernel, tt=tt, ctx_len=ctx_len, seq_len=l)
    return pl.pallas_call(
        kern,
        grid=(bsz, l // tt, ncb),
        in_specs=[pl.BlockSpec((1, tt, tc), lambda b_, i, j: (b_, i, b_col0 + j)),
                  pl.BlockSpec((1, tt, tc), lambda b_, i, j: (b_, i, c_col0 + j)),
                  pl.BlockSpec((1, tt, tc), lambda b_, i, j: (b_, i, u_col0 + j)),
                  pl.BlockSpec((1, HALO, tc), lambda b_, i, j: (b_, prev(i), c_col0 + j)),
                  pl.BlockSpec((1, HALO, tc), lambda b_, i, j: (b_, prev(i), u_col0 + j)),
                  pl.BlockSpec((1, HALO, tc), lambda b_, i, j: (b_, nxt(i), c_col0 + j)),
                  pl.BlockSpec((1, HALO, tc), lambda b_, i, j: (b_, nxt(i), u_col0 + j)),
                  pl.BlockSpec((SC_CONV, tc), lambda b_, i, j: (0, j)),
                  pl.BlockSpec((1, tc), lambda b_, i, j: (0, j))],
        out_specs=pl.BlockSpec((1, tt, tc), lambda b_, i, j: (b_, i, j)),
        out_shape=jax.ShapeDtypeStruct((bsz, l, c), BF16),
        compiler_params=_cparams(("parallel", "parallel", "parallel"), 32),
        name="sconv",
    )(p, p, p, p, p, p, p, w, b.reshape(1, c))


def _merge_kernel(xa_ref, xl_ref, xs_ref, ga_ref, gl_ref, gs_ref, wa_ref, wl_ref, ws_ref, o_ref):
    y = jax.nn.sigmoid(ga_ref[0].astype(F32)) * jnp.dot(xa_ref[0], wa_ref[...], preferred_element_type=F32)
    y = y + jax.nn.sigmoid(gl_ref[0].astype(F32)) * jnp.dot(xl_ref[0], wl_ref[...], preferred_element_type=F32)
    y = y + jax.nn.sigmoid(gs_ref[0].astype(F32)) * jnp.dot(xs_ref[0], ws_ref[...], preferred_element_type=F32)
    o_ref[0] = y.astype(BF16)


def _merge_call(x_att, x_lru, x_sc, p, layer, w_att, w_lru, w_sc):
    bsz, l, d = x_att.shape
    tm = _tile(l, 768, 16)
    tn = 512
    g_col0 = (p.shape[2] - 3 * d) // tn
    nj = d // tn
    xspec = pl.BlockSpec((1, tm, d), lambda b, i, j: (b, i, 0))
    wspec = pl.BlockSpec((None, d, tn), lambda b, i, j: (layer, 0, j))
    gspec = lambda k: pl.BlockSpec((1, tm, tn), lambda b, i, j: (b, i, g_col0 + k * nj + j))
    return pl.pallas_call(
        _merge_kernel,
        grid=(bsz, l // tm, nj),
        in_specs=[xspec, xspec, xspec, gspec(0), gspec(1), gspec(2), wspec, wspec, wspec],
        out_specs=pl.BlockSpec((1, tm, tn), lambda b, i, j: (b, i, j)),
        out_shape=jax.ShapeDtypeStruct((bsz, l, d), BF16),
        compiler_params=_cparams(("parallel", "parallel", "arbitrary"), 48),
        name="merge",
    )(x_att, x_lru, x_sc, p, p, p, w_att, w_lru, w_sc)


def _outproj_kernel(y_ref, w_ref, x_ref, gate_ref, shift_ref, scale_ref, g_ref, xo_ref, ht_ref,
                    *, tm, ctx_len):
    b = pl.program_id(0)
    i = pl.program_id(1)
    row = i * tm + lax.broadcasted_iota(jnp.int32, (tm, 1), 0)
    is_ctx = row < ctx_len
    acc = jnp.dot(y_ref[0], w_ref[...], preferred_element_type=F32)
    xn = x_ref[0] + _row_select(gate_ref, b, is_ctx) * acc
    xo_ref[0] = xn
    var = jnp.mean(xn * xn, axis=-1, keepdims=True)
    h = xn * lax.rsqrt(var + EPS) * g_ref[...]
    h = h * (1.0 + _row_select(scale_ref, b, is_ctx)) + _row_select(shift_ref, b, is_ctx)
    ht_ref[0] = h.T.astype(BF16)


def _outproj_call(y, w_out, x, mod, layer, gain, ctx_len):
    bsz, l, d = x.shape
    tm = _tile(l, 384, LANES)
    kern = functools.partial(_outproj_kernel, tm=tm, ctx_len=ctx_len)
    mspec = lambda k: pl.BlockSpec((None, 8, d), lambda b, i: (layer, 0, k))
    return pl.pallas_call(
        kern,
        grid=(bsz, l // tm),
        in_specs=[pl.BlockSpec((1, tm, d), lambda b, i: (b, i, 0)),
                  pl.BlockSpec((None, d, d), lambda b, i: (layer, 0, 0)),
                  pl.BlockSpec((1, tm, d), lambda b, i: (b, i, 0)),
                  mspec(2), mspec(3), mspec(4),
                  pl.BlockSpec((1, d), lambda b, i: (0, 0))],
        out_specs=[pl.BlockSpec((1, tm, d), lambda b, i: (b, i, 0)),
                   pl.BlockSpec((1, d, tm), lambda b, i: (b, 0, i))],
        out_shape=[jax.ShapeDtypeStruct((bsz, l, d), F32),
                   jax.ShapeDtypeStruct((bsz, d, l), BF16)],
        compiler_params=_cparams(("parallel", "parallel"), 48),
        name="outproj",
    )(y, w_out, x, mod, mod, mod, gain.reshape(1, d))


_CAND_ROWS = tuple(PEER_TOPK // (i + 1) for i in range(PEER_TOPK))


def _top16(s, v_scr, exact):
    n = s.shape[0]
    key = lax.broadcasted_iota(jnp.int32, s.shape, 0)

    def body(r, carry):
        work, rank = carry
        m = jnp.max(work, axis=0, keepdims=True)
        v_scr[pl.ds(r, 1), :] = m
        sel = work == m
        if exact:
            first = jnp.min(jnp.where(sel, key, n), axis=0, keepdims=True)
            sel = key == first
        return jnp.where(sel, NEG_INF, work), jnp.where(sel, jnp.asarray(r, F32), rank)

    _, rank = lax.fori_loop(0, PEER_TOPK, body, (s, jnp.full(s.shape, float(PEER_TOPK), F32)))
    return rank


def _peer_topk_kernel(ht_ref, wq_ref, k1_ref, k2_ref, cnt_ref, e1_ref, rk_ref, e2_ref,
                      q_scr, v1_scr, v2_scr, *, tm):
    half = PEER_NKEYS
    q_scr[...] = jnp.dot(wq_ref[...], ht_ref[0], preferred_element_type=F32).astype(BF16)
    sub = lax.broadcasted_iota(jnp.int32, (SUBLANES, LANES), 0)
    big = PEER_TOPK * PEER_TOPK

    def compute(h, ls, exact):
        base = pl.multiple_of(h * 2 * half, 2 * half)
        s1 = jnp.dot(k1_ref[h], q_scr[pl.ds(base, half), ls], preferred_element_type=F32)
        s2 = jnp.dot(k2_ref[h], q_scr[pl.ds(base + half, half), ls], preferred_element_type=F32)
        rank1 = _top16(s1, v1_scr, exact)
        rank2 = _top16(s2, v2_scr, exact)
        v1 = v1_scr[...]
        v2 = v2_scr[...]

        pieces, poss = [], []
        for i in range(SUBLANES):
            for j0 in range(0, _CAND_ROWS[i], SUBLANES):
                c = v1[i:i + 1, :] + v2[j0:j0 + SUBLANES, :]
                valid = sub + j0 < _CAND_ROWS[i]
                pieces.append(jnp.where(valid, c, NEG_INF))
                poss.append(jnp.where(valid, i * PEER_TOPK + j0 + sub, big))
        pieces.append(v1[SUBLANES:, :] + v2[0:1, :])
        poss.append((sub + SUBLANES) * PEER_TOPK)
        orig = list(pieces)
        npc = len(pieces)

        def pick(_, carry2):
            cs, sels = carry2
            m = functools.reduce(jnp.maximum, cs)
            m = jnp.max(m, axis=0, keepdims=True)
            hit = [c == m for c in cs]
            if exact:
                cand_pos = functools.reduce(jnp.minimum, [jnp.where(hh, p_, big) for hh, p_ in zip(hit, poss)])
                first = jnp.min(cand_pos, axis=0, keepdims=True)
                hit = [p_ == first for p_ in poss]
            cs = tuple(jnp.where(hh, NEG_INF, c) for hh, c in zip(hit, cs))
            sels = tuple(jnp.where(hh, 1.0, s_) for hh, s_ in zip(hit, sels))
            return cs, sels

        zeros = tuple(jnp.zeros((SUBLANES, LANES), F32) for _ in range(npc))
        _, sels = lax.fori_loop(0, PEER_TOPK, pick, (tuple(pieces), zeros))

        top = v1[0:1, :] + v2[0:1, :]
        zsum = functools.reduce(
            lambda a_, b_: a_ + b_,
            [jnp.where(s_ > 0.0, jnp.exp(o - top), 0.0) for s_, o in zip(sels, orig)])
        zinv = 1.0 / jnp.sum(zsum, axis=0, keepdims=True)

        counts = []
        pi = 0
        for i in range(SUBLANES):
            c = None
            for j0 in range(0, _CAND_ROWS[i], SUBLANES):
                part = jnp.sum(sels[pi], axis=0, keepdims=True)
                c = part if c is None else c + part
                pi += 1
            counts.append(c)
        tail = sels[pi]
        for i in range(SUBLANES, PEER_TOPK):
            counts.append(tail[i - SUBLANES:i - SUBLANES + 1, :])
        cnt = jnp.zeros(rank1.shape, F32)
        for i in range(PEER_TOPK):
            cnt = jnp.where(rank1 == float(i), counts[i], cnt)

        cnt_ref[0, h, :, ls] = cnt
        e1_ref[0, h, :, ls] = jnp.exp(s1 - v1[0:1, :]) * zinv
        rk_ref[0, h, :, ls] = rank2.astype(BF16)
        e2_ref[0, h, :, ls] = jnp.exp(s2 - v2[0:1, :]).astype(BF16)
        ranked = jnp.where(rank1 < PEER_TOPK, 1.0, 0.0)
        ranked = jnp.maximum(jnp.sum(ranked, axis=0, keepdims=True),
                             jnp.sum(jnp.where(rank2 < PEER_TOPK, 1.0, 0.0), axis=0, keepdims=True))
        picked = jnp.sum(functools.reduce(lambda a_, b_: a_ + b_, sels), axis=0, keepdims=True)
        return jnp.max(jnp.maximum(ranked, picked))

    n_groups = tm // LANES

    def head_lane_group(idx, carry):
        h = idx // n_groups
        ls = pl.ds(pl.multiple_of((idx % n_groups) * LANES, LANES), LANES)
        most = compute(h, ls, exact=False)

        @pl.when(most > PEER_TOPK)
        def _():
            compute(h, ls, exact=True)

        return carry

    lax.fori_loop(0, PEER_HEADS * n_groups, head_lane_group, 0)


def _peer_topk_call(ht, layer, wq_t, k1, k2):
    bsz, d, l = ht.shape
    qd = wq_t.shape[1]
    tm = _tile(l, 384, LANES)
    kern = functools.partial(_peer_topk_kernel, tm=tm)
    ospec = pl.BlockSpec((1, PEER_HEADS, PEER_NKEYS, tm), lambda b, i: (b, 0, 0, i))
    oshape = jax.ShapeDtypeStruct((bsz, PEER_HEADS, PEER_NKEYS, l), F32)
    kspec = pl.BlockSpec((None, PEER_HEADS, PEER_NKEYS, PEER_NKEYS), lambda b, i: (layer, 0, 0, 0))
    return pl.pallas_call(
        kern,
        grid=(bsz, l // tm),
        in_specs=[pl.BlockSpec((1, d, tm), lambda b, i: (b, 0, i)),
                  pl.BlockSpec((None, qd, d), lambda b, i: (layer, 0, 0)),
                  kspec, kspec],
        out_specs=[ospec, ospec, ospec, ospec],
        out_shape=[oshape, oshape, jax.ShapeDtypeStruct(oshape.shape, BF16),
                   jax.ShapeDtypeStruct(oshape.shape, BF16)],
        scratch_shapes=[pltpu.VMEM((qd, tm), BF16),
                        pltpu.VMEM((PEER_TOPK, LANES), F32),
                        pltpu.VMEM((PEER_TOPK, LANES), F32)],
        compiler_params=_cparams(("parallel", "parallel"), 40),
        name="peer_topk",
    )(ht, wq_t, k1, k2)


def _peer_dense_kernel(ht_ref, u_ref, vt_ref, cnt_ref, e1_ref, rk_ref, e2_ref, o_ref, wz_scr, *, n_sub):
    e = pl.program_id(2)

    @pl.when(e == 0)
    def _():
        o_ref[...] = jnp.zeros(o_ref.shape, F32)

    s = jnp.dot(u_ref[...], ht_ref[0], preferred_element_type=F32)
    for a in range(n_sub):
        rows = slice(a * PEER_NKEYS, (a + 1) * PEER_NKEYS)
        w = None
        for h in range(PEER_HEADS):
            cnt = cnt_ref[0, h, a:a + 1, :].astype(BF16)
            e1 = e1_ref[0, h, a:a + 1, :].astype(BF16)
            term = jnp.where(rk_ref[0, h] < cnt, e2_ref[0, h] * e1, 0.0)
            w = term if w is None else w + term
        wz_scr[rows, :] = w * _gelu(s[rows, :]).astype(BF16)
    o_ref[0] += jnp.dot(vt_ref[...], wz_scr[...], preferred_element_type=F32)


def _peer_dense_call(ht, layer, u_bf16, vt_bf16, cnt, e1n, rank2, e2):
    bsz, d, l = ht.shape
    n_exp = u_bf16.shape[1]
    tm = _tile(l, 768, LANES)
    n_sub = SUBLANES
    te = n_sub * PEER_NKEYS
    kern = functools.partial(_peer_dense_kernel, n_sub=n_sub)
    aspec = pl.BlockSpec((1, PEER_HEADS, n_sub, tm), lambda b, i, e: (b, 0, e, i))
    fspec = pl.BlockSpec((1, PEER_HEADS, PEER_NKEYS, tm), lambda b, i, e: (b, 0, 0, i))
    return pl.pallas_call(
        kern,
        grid=(bsz, l // tm, n_exp // te),
        in_specs=[pl.BlockSpec((1, d, tm), lambda b, i, e: (b, 0, i)),
                  pl.BlockSpec((None, te, d), lambda b, i, e: (layer, e, 0)),
                  pl.BlockSpec((None, d, te), lambda b, i, e: (layer, 0, e)),
                  aspec, aspec, fspec, fspec],
        out_specs=pl.BlockSpec((1, d, tm), lambda b, i, e: (b, 0, i)),
        out_shape=jax.ShapeDtypeStruct((bsz, d, l), F32),
        scratch_shapes=[pltpu.VMEM((te, tm), BF16)],
        compiler_params=_cparams(("parallel", "parallel", "arbitrary"), 58),
        name="peer_dense",
    )(ht, u_bf16, vt_bf16, cnt, e1n, rank2, e2)


def _resid_kernel(x_ref, yt_ref, gate_ref, o_ref, *, tm, ctx_len):
    b = pl.program_id(0)
    i = pl.program_id(1)
    row = i * tm + lax.broadcasted_iota(jnp.int32, (tm, 1), 0)
    o_ref[0] = x_ref[0] + _row_select(gate_ref, b, row < ctx_len) * yt_ref[0].T


def _resid_call(x, yt, mod, layer, ctx_len):
    bsz, l, d = x.shape
    tm = _tile(l, 384, LANES)
    kern = functools.partial(_resid_kernel, tm=tm, ctx_len=ctx_len)
    return pl.pallas_call(
        kern,
        grid=(bsz, l // tm),
        in_specs=[pl.BlockSpec((1, tm, d), lambda b, i: (b, i, 0)),
                  pl.BlockSpec((1, d, tm), lambda b, i: (b, 0, i)),
                  pl.BlockSpec((None, 8, d), lambda b, i: (layer, 0, 5))],
        out_specs=pl.BlockSpec((1, tm, d), lambda b, i: (b, i, 0)),
        out_shape=jax.ShapeDtypeStruct((bsz, l, d), F32),
        compiler_params=_cparams(("parallel", "parallel"), 40),
        name="resid",
    )(x, yt, mod)


def _rope_tables(ctx_len, t_lat):
    rows = t_lat // GRID_W
    row = jnp.repeat(jnp.arange(rows, dtype=F32), GRID_W)
    col = jnp.tile(jnp.arange(GRID_W, dtype=F32), rows)
    inv = ROPE_THETA ** (-jnp.arange(ROPE_PAIRS, dtype=F32) / ROPE_PAIRS)
    ang = jnp.concatenate([row[:, None] * inv] * 2 + [col[:, None] * inv] * 2, axis=1)
    ang = jnp.concatenate([jnp.zeros((ctx_len, HEAD_DIM), F32), ang], axis=0)
    sign = jnp.where((jnp.arange(HEAD_DIM) & ROPE_PAIRS) == 0, -1.0, 1.0).astype(F32)
    return jnp.cos(ang), jnp.sin(ang) * sign


def kernel(x, c, ctx, c_ctx, w_mod, b_mod, norm_mix, norm_ffn, w_in, q_norm, k_norm, lru_conv_w, lru_conv_b, lru_wa, lru_ba, lru_wx, lru_bx, lru_lambda, sc_conv_w, sc_conv_b, w_o_attn, w_o_lru, w_o_sc, w_out, peer_wq, peer_k1, peer_k2, peer_u, peer_v):
    bsz, t_lat, d = x.shape
    ctx_len = ctx.shape[1]
    depth = w_mod.shape[0]
    assert bsz == 2, "modulation rows are laid out as [latent 0, latent 1, context]"

    xs = jnp.concatenate([ctx, x], axis=1)
    s8 = jnp.concatenate([c, c_ctx[None, :], jnp.zeros((8 - bsz - 1, d), F32)], axis=0)
    mod = _mod_call(s8, w_mod, b_mod)
    cos, sin_signed = _rope_tables(ctx_len, t_lat)

    w_in_b = w_in.astype(BF16)
    w_att_b, w_lru_b, w_sc_b = w_o_attn.astype(BF16), w_o_lru.astype(BF16), w_o_sc.astype(BF16)
    w_out_b = w_out.astype(BF16)
    wq_t = jnp.swapaxes(peer_wq, 1, 2).astype(BF16)
    k1_b, k2_b = peer_k1.astype(BF16), peer_k2.astype(BF16)
    u_b = peer_u.astype(BF16)
    vt_b = jnp.swapaxes(peer_v, 1, 2).astype(BF16)

    for l in range(depth):
        p = _inproj_call(xs, mod, l, norm_mix[l], w_in_b, ctx_len)
        qn, ke, ve = _qkprep_call(p, cos, sin_signed, q_norm[l], k_norm[l])
        x_att = _attn_call(qn, ke, ve, q_norm[l], k_norm[l], ctx_len)
        lru_args = (lru_conv_w[l], lru_conv_b[l])
        h_fwd = _lru_call(p, *lru_args, lru_wa[l, 0], lru_ba[l, 0], lru_wx[l, 0], lru_bx[l, 0],
                          lru_lambda[l, 0], ctx_len, reverse=False)
        x_lru = _lru_call(p, *lru_args, lru_wa[l, 1], lru_ba[l, 1], lru_wx[l, 1], lru_bx[l, 1],
                          lru_lambda[l, 1], ctx_len, reverse=True, h_fwd=h_fwd)
        x_sc = _sconv_call(p, sc_conv_w[l], sc_conv_b[l], ctx_len)
        y = _merge_call(x_att, x_lru, x_sc, p, l, w_att_b, w_lru_b, w_sc_b)
        xs, ht = _outproj_call(y, w_out_b, xs, mod, l, norm_ffn[l], ctx_len)
        cnt, e1n, rank2, e2 = _peer_topk_call(ht, l, wq_t, k1_b, k2_b)
        yt = _peer_dense_call(ht, l, u_b, vt_b, cnt, e1n, rank2, e2)
        xs = _resid_call(xs, yt, mod, l, ctx_len)
    return xs[:, ctx_len:, :]
```

```python
import functools
import math

import numpy as np
import jax
import jax.numpy as jnp
from jax import lax
from jax.experimental import pallas as pl
from jax.experimental.pallas import tpu as pltpu

F32 = jnp.float32
BF16 = jnp.bfloat16

GRID_W = 64
EPS = 1e-6
N_MOD = 6

N_HEADS = 16
N_KV_HEADS = 4
HEAD_DIM = 128
GROUP = N_HEADS // N_KV_HEADS
ROPE_PAIRS = HEAD_DIM // 4
ROPE_THETA = 10000.0

LRU_BLOCKS = 16
LRU_BLOCK = 128
LRU_CONV = 4
LRU_C = 8.0
SC_CONV = 3

PEER_HEADS = 8
PEER_NKEYS = 128
PEER_TOPK = 16

LANES = 128
SUBLANES = 8
HALO = SUBLANES
NEG_INF = float("-inf")
LOG2E = 1.4426950408889634
GELU_C = math.sqrt(2.0 / math.pi)
Q_SCALE = HEAD_DIM ** -0.5 * LOG2E
SHIFT_MARGIN = 1.02
MIN_DENOM = 2.0 ** -100


def _tile(n, target, mult):
    best = None
    for t in range(mult, min(n, target) + 1, mult):
        if n % t == 0:
            best = t
    assert best is not None, (n, target, mult)
    return best


def _cparams(sem, vmem_mib):
    return pltpu.CompilerParams(dimension_semantics=sem, vmem_limit_bytes=vmem_mib << 20)


def _gelu(x):
    return 0.5 * x * (1.0 + jnp.tanh(GELU_C * (x + 0.044715 * (x * x * x))))


def _row_select(mod_ref, b, is_ctx):
    return jnp.where(is_ctx, mod_ref[2:3, :], mod_ref[pl.ds(b, 1), :])


def _mod_kernel(s_ref, w_ref, b_ref, o_ref):
    s = s_ref[...]
    s = s * jax.nn.sigmoid(s)
    o_ref[0] = jnp.dot(s, w_ref[0], preferred_element_type=F32,
                       precision=lax.Precision.HIGHEST) + b_ref[0]


def _mod_call(s8, w_mod, b_mod):
    depth, d, n = w_mod.shape
    tn = _tile(n, 1024, LANES)
    return pl.pallas_call(
        _mod_kernel,
        grid=(depth, n // tn),
        in_specs=[pl.BlockSpec((8, d), lambda l, j: (0, 0)),
                  pl.BlockSpec((1, d, tn), lambda l, j: (l, 0, j)),
                  pl.BlockSpec((1, 1, tn), lambda l, j: (l, 0, j))],
        out_specs=pl.BlockSpec((1, 8, tn), lambda l, j: (l, 0, j)),
        out_shape=jax.ShapeDtypeStruct((depth, 8, n), F32),
        compiler_params=_cparams(("parallel", "parallel"), 32),
        name="mod",
    )(s8, w_mod, b_mod.reshape(depth, 1, n))


def _inproj_kernel(x_ref, shift_ref, scale_ref, g_ref, w_ref, o_ref, h_scr, *, tm, ctx_len):
    b = pl.program_id(0)
    i = pl.program_id(1)

    @pl.when(pl.program_id(2) == 0)
    def _():
        x = x_ref[0]
        var = jnp.mean(x * x, axis=-1, keepdims=True)
        y = x * lax.rsqrt(var + EPS) * g_ref[...]
        row = i * tm + lax.broadcasted_iota(jnp.int32, (tm, 1), 0)
        is_ctx = row < ctx_len
        sh = _row_select(shift_ref, b, is_ctx)
        sc = _row_select(scale_ref, b, is_ctx)
        h_scr[...] = (y * (1.0 + sc) + sh).astype(BF16)

    o_ref[0] = jnp.dot(h_scr[...], w_ref[...], preferred_element_type=F32).astype(BF16)


def _inproj_call(x, mod, layer, gain, w_bf16, ctx_len):
    bsz, l, d = x.shape
    n = w_bf16.shape[2]
    tm = _tile(l, 704, 16)
    tn = 1024
    kern = functools.partial(_inproj_kernel, tm=tm, ctx_len=ctx_len)
    return pl.pallas_call(
        kern,
        grid=(bsz, l // tm, n // tn),
        in_specs=[pl.BlockSpec((1, tm, d), lambda b, i, j: (b, i, 0)),
                  pl.BlockSpec((None, 8, d), lambda b, i, j: (layer, 0, 0)),
                  pl.BlockSpec((None, 8, d), lambda b, i, j: (layer, 0, 1)),
                  pl.BlockSpec((1, d), lambda b, i, j: (0, 0)),
                  pl.BlockSpec((None, d, tn), lambda b, i, j: (layer, 0, j))],
        out_specs=pl.BlockSpec((1, tm, tn), lambda b, i, j: (b, i, j)),
        out_shape=jax.ShapeDtypeStruct((bsz, l, n), BF16),
        scratch_shapes=[pltpu.VMEM((tm, d), BF16)],
        compiler_params=_cparams(("parallel", "parallel", "arbitrary"), 48),
        name="inproj",
    )(x, mod, mod, gain.reshape(1, d), w_bf16)


def _norm_rope(t, gain, cos, sin_signed, lane_lo):
    var = jnp.mean(t * t, axis=-1, keepdims=True)
    y = t * lax.rsqrt(var + EPS) * gain
    swapped = jnp.where(lane_lo, pltpu.roll(y, HEAD_DIM - ROPE_PAIRS, 1), pltpu.roll(y, ROPE_PAIRS, 1))
    return y * cos + swapped * sin_signed


def _qkprep_kernel(q_ref, k_ref, v_ref, cos_ref, sin_ref, qg_ref, kg_ref, qo_ref, ko_ref, vo_ref):
    cos = cos_ref[...]
    sin = sin_ref[...]
    lane = lax.broadcasted_iota(jnp.int32, cos.shape, 1)
    lane_lo = (lane & ROPE_PAIRS) == 0
    qg = qg_ref[...]
    kg = kg_ref[...]
    for h in range(N_HEADS):
        sl = slice(h * HEAD_DIM, (h + 1) * HEAD_DIM)
        t = q_ref[0, :, sl].astype(F32)
        qo_ref[0, :, sl] = (_norm_rope(t, qg, cos, sin, lane_lo) * Q_SCALE).astype(BF16)
    k_tail = jnp.where(lane == 0, 1.0, 0.0).astype(BF16)
    v_tail = jnp.ones(cos.shape, BF16)
    for h in range(N_KV_HEADS):
        sl = slice(h * HEAD_DIM, (h + 1) * HEAD_DIM)
        lo = slice(2 * h * HEAD_DIM, (2 * h + 1) * HEAD_DIM)
        hi = slice((2 * h + 1) * HEAD_DIM, (2 * h + 2) * HEAD_DIM)
        t = k_ref[0, :, sl].astype(F32)
        ko_ref[0, :, lo] = _norm_rope(t, kg, cos, sin, lane_lo).astype(BF16)
        ko_ref[0, :, hi] = k_tail
        vo_ref[0, :, lo] = v_ref[0, :, sl]
        vo_ref[0, :, hi] = v_tail


def _qkprep_call(p, cos, sin_signed, q_gain, k_gain):
    bsz, l, _ = p.shape
    qw = N_HEADS * HEAD_DIM
    kw = N_KV_HEADS * HEAD_DIM
    tm = _tile(l, 768, 16)
    return pl.pallas_call(
        _qkprep_kernel,
        grid=(bsz, l // tm),
        in_specs=[pl.BlockSpec((1, tm, qw), lambda b, i: (b, i, 0)),
                  pl.BlockSpec((1, tm, kw), lambda b, i: (b, i, qw // kw)),
                  pl.BlockSpec((1, tm, kw), lambda b, i: (b, i, qw // kw + 1)),
                  pl.BlockSpec((tm, HEAD_DIM), lambda b, i: (i, 0)),
                  pl.BlockSpec((tm, HEAD_DIM), lambda b, i: (i, 0)),
                  pl.BlockSpec((1, HEAD_DIM), lambda b, i: (0, 0)),
                  pl.BlockSpec((1, HEAD_DIM), lambda b, i: (0, 0))],
        out_specs=[pl.BlockSpec((1, tm, qw), lambda b, i: (b, i, 0)),
                   pl.BlockSpec((1, tm, 2 * kw), lambda b, i: (b, i, 0)),
                   pl.BlockSpec((1, tm, 2 * kw), lambda b, i: (b, i, 0))],
        out_shape=[jax.ShapeDtypeStruct((bsz, l, qw), BF16),
                   jax.ShapeDtypeStruct((bsz, l, 2 * kw), BF16),
                   jax.ShapeDtypeStruct((bsz, l, 2 * kw), BF16)],
        compiler_params=_cparams(("parallel", "parallel"), 40),
        name="qkprep",
    )(p, p, p, cos, sin_signed, q_gain.reshape(1, HEAD_DIM), k_gain.reshape(1, HEAD_DIM))


def _attn_kernel(q_ref, k_ref, v_ref, qg_ref, kg_ref, o_ref, qs_scr, acc_scr, sa_scr, sb_scr, m_scr, l_scr,
                 *, tq, tkc, ctx_len, n_lat_chunks):
    qi = pl.program_id(2)
    q = q_ref[0]
    gq = jnp.max(jnp.abs(qg_ref[...]), axis=-1, keepdims=True)
    gk = jnp.max(jnp.abs(kg_ref[...]), axis=-1, keepdims=True)
    bound = (SHIFT_MARGIN * HEAD_DIM * Q_SCALE) * gq * gk
    lane = lax.broadcasted_iota(jnp.int32, (1, HEAD_DIM), 1)
    tail = jnp.where(lane == 0, -bound, 0.0).astype(BF16)
    for h in range(GROUP):
        qs_scr[h * tq:(h + 1) * tq, :HEAD_DIM] = q[:, h * HEAD_DIM:(h + 1) * HEAD_DIM]
        qs_scr[h * tq:(h + 1) * tq, HEAD_DIM:] = jnp.broadcast_to(tail, (tq, HEAD_DIM))
    n = jnp.where(qi < ctx_len // tq, 0, n_lat_chunks)

    def lat_start(c):
        return pl.multiple_of(ctx_len + c * tkc, math.gcd(ctx_len, tkc))

    def write(out):
        for h in range(GROUP):
            o_ref[0, :, h * HEAD_DIM:(h + 1) * HEAD_DIM] = out[h * tq:(h + 1) * tq, :].astype(BF16)

    def scores(start, size):
        return lax.dot_general(qs_scr[...], k_ref[0, pl.ds(start, size), :], (((1,), (1,)), ((), ())),
                               preferred_element_type=F32)

    def accumulate(s, start, size):
        p = jnp.exp2(s).astype(BF16)
        acc_scr[...] += jnp.dot(p, v_ref[0, pl.ds(start, size), :], preferred_element_type=F32)

    acc_scr[...] = jnp.zeros(acc_scr.shape, F32)
    sa_scr[...] = scores(lat_start(0), tkc)
    accumulate(scores(0, ctx_len), 0, ctx_len)

    def shifted_body(j, carry):
        c0 = 2 * j
        sb_scr[...] = scores(lat_start(c0 + 1), tkc)
        accumulate(sa_scr[...], lat_start(c0), tkc)
        sa_scr[...] = scores(lat_start(jnp.minimum(c0 + 2, n_lat_chunks - 1)), tkc)
        accumulate(sb_scr[...], lat_start(c0 + 1), tkc)
        return carry

    lax.fori_loop(0, n // 2, shifted_body, 0)
    acc = acc_scr[...]
    den = acc[:, HEAD_DIM:]
    write(acc[:, :HEAD_DIM] / den)

    @pl.when(jnp.logical_not(jnp.min(den) >= MIN_DENOM))
    def _():
        m_scr[...] = jnp.full(m_scr.shape, NEG_INF, F32)
        l_scr[...] = jnp.zeros(l_scr.shape, F32)
        acc_scr[...] = jnp.zeros(acc_scr.shape, F32)

        def online_chunk(start, size):
            k = k_ref[0, pl.ds(start, size), :HEAD_DIM]
            v = v_ref[0, pl.ds(start, size), :HEAD_DIM]
            s = lax.dot_general(qs_scr[:, :HEAD_DIM], k, (((1,), (1,)), ((), ())),
                                preferred_element_type=F32)
            m_old = m_scr[...]
            m_new = jnp.maximum(m_old, jnp.max(s, axis=-1, keepdims=True))
            alpha = jnp.exp2(m_old - m_new)
            p = jnp.exp2(s - m_new)
            l_scr[...] = alpha * l_scr[...] + jnp.sum(p, axis=-1, keepdims=True)
            acc_scr[:, :HEAD_DIM] = alpha * acc_scr[:, :HEAD_DIM] + jnp.dot(
                p.astype(BF16), v, preferred_element_type=F32)
            m_scr[...] = m_new

        online_chunk(0, ctx_len)

        def online_body(c, carry):
            online_chunk(lat_start(c), tkc)
            return carry

        lax.fori_loop(0, n, online_body, 0)
        write(acc_scr[:, :HEAD_DIM] / l_scr[...])


def _attn_call(qn, ke, ve, q_gain, k_gain, ctx_len):
    bsz, l, qw = qn.shape
    tq = 256
    assert ctx_len % tq == 0 and l % tq == 0
    t_lat = l - ctx_len
    tkc = _tile(t_lat // 2, 512, 256)
    gw = GROUP * HEAD_DIM
    ew = 2 * HEAD_DIM
    kern = functools.partial(_attn_kernel, tq=tq, tkc=tkc, ctx_len=ctx_len, n_lat_chunks=t_lat // tkc)
    return pl.pallas_call(
        kern,
        grid=(bsz, N_KV_HEADS, l // tq),
        in_specs=[pl.BlockSpec((1, tq, gw), lambda b, g, i: (b, i, g)),
                  pl.BlockSpec((1, l, ew), lambda b, g, i: (b, 0, g)),
                  pl.BlockSpec((1, l, ew), lambda b, g, i: (b, 0, g)),
                  pl.BlockSpec((1, HEAD_DIM), lambda b, g, i: (0, 0)),
                  pl.BlockSpec((1, HEAD_DIM), lambda b, g, i: (0, 0))],
        out_specs=pl.BlockSpec((1, tq, gw), lambda b, g, i: (b, i, g)),
        out_shape=jax.ShapeDtypeStruct((bsz, l, qw), BF16),
        scratch_shapes=[pltpu.VMEM((GROUP * tq, ew), BF16),
                        pltpu.VMEM((GROUP * tq, ew), F32),
                        pltpu.VMEM((GROUP * tq, tkc), F32),
                        pltpu.VMEM((GROUP * tq, tkc), F32),
                        pltpu.VMEM((GROUP * tq, 1), F32),
                        pltpu.VMEM((GROUP * tq, 1), F32)],
        compiler_params=_cparams(("parallel", "parallel", "arbitrary"), 48),
        name="attn",
    )(qn, ke, ve, q_gain.reshape(1, HEAD_DIM), k_gain.reshape(1, HEAD_DIM))


def _segment_bounds(row, ctx_len, seq_len):
    is_ctx = row < ctx_len
    first = jnp.where(is_ctx, 0, ctx_len)
    last = jnp.where(is_ctx, ctx_len - 1, seq_len - 1)
    return first, last


def _shift_down(x, prev, k, local):
    y = pltpu.roll(x, k, 0)
    for r in range(k):
        y = jnp.where(local == r, prev[HALO - k + r:HALO - k + r + 1, :], y)
    return y


def _shift_up(x, nxt, k, local, tt):
    y = pltpu.roll(x, tt - k, 0)
    for r in range(k):
        y = jnp.where(local == tt - k + r, nxt[r:r + 1, :], y)
    return y


def _halo_shifts(x, prev, nxt):
    tt = x.shape[0]
    sub = lax.broadcasted_iota(jnp.int32, (SUBLANES, x.shape[1]), 0)

    def down(k):
        y = pltpu.roll(x, k, 0)
        head = y[:SUBLANES]
        for r in range(k):
            head = jnp.where(sub == r, prev[HALO - k + r:HALO - k + r + 1, :], head)
        return jnp.concatenate([head, y[SUBLANES:]], axis=0)

    def up(k):
        y = pltpu.roll(x, tt - k, 0)
        tail = y[tt - SUBLANES:]
        for r in range(k):
            tail = jnp.where(sub == SUBLANES - k + r, nxt[r:r + 1, :], tail)
        return jnp.concatenate([y[:tt - SUBLANES], tail], axis=0)

    return down, up


def _scan_tile(a, d, carry, reverse):
    n_groups = a.shape[0] // SUBLANES
    sub = lax.broadcasted_iota(jnp.int32, (SUBLANES, a.shape[1]), 0)
    groups = []
    for v in range(n_groups):
        av = a[v * SUBLANES:(v + 1) * SUBLANES]
        dv = d[v * SUBLANES:(v + 1) * SUBLANES]
        for k in (1, 2, 4):
            keep = sub < SUBLANES - k if reverse else sub >= k
            shift = SUBLANES - k if reverse else k
            a_n = jnp.where(keep, pltpu.roll(av, shift, 0), 1.0)
            d_n = jnp.where(keep, pltpu.roll(dv, shift, 0), 0.0)
            dv = av * d_n + dv
            av = av * a_n
        groups.append((av, dv))
    hs = [None] * n_groups
    for v in (reversed(range(n_groups)) if reverse else range(n_groups)):
        av, dv = groups[v]
        hv = dv + av * carry
        carry = hv[0:1] if reverse else hv[SUBLANES - 1:SUBLANES]
        hs[v] = hv
    return jnp.concatenate(hs, axis=0), carry


def _lru_kernel(*refs, reverse, tt, ctx_len, n_sub):
    if reverse:
        (x_ref, xp_ref, xn_ref, cw_ref, cb_ref, wa_ref, ba_ref, wx_ref, bx_ref, lam_ref,
         hf_ref, g_ref, o_ref, carry_scr) = refs
    else:
        (x_ref, xp_ref, xn_ref, cw_ref, cb_ref, wa_ref, ba_ref, wx_ref, bx_ref, lam_ref,
         o_ref, carry_scr) = refs
    s = pl.program_id(2)
    nt = pl.num_programs(2)
    ti = jnp.where(s == 0, 0, nt - s) if reverse else s

    @pl.when(s == 0)
    def _():
        carry_scr[...] = jnp.zeros(carry_scr.shape, F32)

    prev_ok = jnp.logical_and(ti != 0, ti * tt != ctx_len)
    next_ok = jnp.logical_and(ti != nt - 1, (ti + 1) * tt != ctx_len)
    x = x_ref[0].astype(F32)
    xp = jnp.where(prev_ok, xp_ref[0].astype(F32), 0.0)
    xn = jnp.where(next_ok, xn_ref[0].astype(F32), 0.0)
    down, up = _halo_shifts(x, xp, xn)
    cw = cw_ref[...]
    u = cb_ref[...] + cw[2:3, :] * x + cw[0:1, :] * down(2) + cw[1:2, :] * down(1) + cw[3:4, :] * up(1)
    nlam = -lam_ref[...]
    softplus = jnp.maximum(nlam, 0.0) + jnp.log1p(jnp.exp(-jnp.abs(nlam)))

    for j in range(n_sub):
        ls = slice(j * LRU_BLOCK, (j + 1) * LRU_BLOCK)
        uj = u[:, ls]
        ub = uj.astype(BF16)
        r = jax.nn.sigmoid(jnp.dot(ub, wa_ref[j], preferred_element_type=F32) + ba_ref[:, ls])
        i = jax.nn.sigmoid(jnp.dot(ub, wx_ref[j], preferred_element_type=F32) + bx_ref[:, ls])
        a = jnp.exp((-LRU_C) * r * softplus[:, ls])
        d = jnp.sqrt(1.0 - a * a) * (i * uj)
        h, carry = _scan_tile(a, d, carry_scr[:, ls], reverse)
        carry_scr[:, ls] = carry
        if reverse:
            o_ref[0, :, ls] = ((hf_ref[0, :, ls] + h) * _gelu(g_ref[0, :, ls].astype(F32))).astype(BF16)
        else:
            o_ref[0, :, ls] = h


def _lru_call(p, conv_w, conv_b, wa, ba, wx, bx, lam, ctx_len, reverse, h_fwd=None):
    bsz, l, _ = p.shape
    c = conv_w.shape[1]
    tt = 256
    assert ctx_len % tt == 0 and l % tt == 0
    nt = l // tt
    n_sub = 4
    cw = n_sub * LRU_BLOCK
    nb = c // cw
    x_col0 = (N_HEADS + 2 * N_KV_HEADS) * HEAD_DIM // cw
    g_col0 = x_col0 + nb
    hb = tt // HALO
    n_hblk = l // HALO

    def tile_of(s):
        return jnp.where(s == 0, 0, nt - s) if reverse else s

    x_map = lambda b, cb, s: (b, tile_of(s), x_col0 + cb)
    prev_map = lambda b, cb, s: (b, jnp.maximum(tile_of(s) * hb - 1, 0), x_col0 + cb)
    next_map = lambda b, cb, s: (b, jnp.minimum((tile_of(s) + 1) * hb, n_hblk - 1), x_col0 + cb)
    vec_map = lambda b, cb, s: (0, cb)
    w_spec = pl.BlockSpec((n_sub, LRU_BLOCK, LRU_BLOCK), lambda b, cb, s: (cb, 0, 0))
    in_specs = [pl.BlockSpec((1, tt, cw), x_map),
                pl.BlockSpec((1, HALO, cw), prev_map),
                pl.BlockSpec((1, HALO, cw), next_map),
                pl.BlockSpec((LRU_CONV, cw), vec_map),
                pl.BlockSpec((1, cw), vec_map),
                w_spec,
                pl.BlockSpec((1, cw), vec_map),
                w_spec,
                pl.BlockSpec((1, cw), vec_map),
                pl.BlockSpec((1, cw), vec_map)]
    args = [p, p, p, conv_w, conv_b.reshape(1, c), wa.astype(BF16), ba.reshape(1, c),
            wx.astype(BF16), bx.reshape(1, c), lam.reshape(1, c)]
    out_map = lambda b, cb, s: (b, tile_of(s), cb)
    if reverse:
        in_specs += [pl.BlockSpec((1, tt, cw), out_map),
                     pl.BlockSpec((1, tt, cw), lambda b, cb, s: (b, tile_of(s), g_col0 + cb))]
        args += [h_fwd, p]
        out_dtype = BF16
    else:
        out_dtype = F32
    kern = functools.partial(_lru_kernel, reverse=reverse, tt=tt, ctx_len=ctx_len, n_sub=n_sub)
    return pl.pallas_call(
        kern,
        grid=(bsz, nb, nt),
        in_specs=in_specs,
        out_specs=pl.BlockSpec((1, tt, cw), out_map),
        out_shape=jax.ShapeDtypeStruct((bsz, l, c), out_dtype),
        scratch_shapes=[pltpu.VMEM((1, cw), F32)],
        compiler_params=_cparams(("parallel", "parallel", "arbitrary"), 32),
        name="lru_rev" if reverse else "lru_fwd",
    )(*args)


def _sconv_kernel(bg_ref, cg_ref, u_ref, cgp_ref, up_ref, cgn_ref, un_ref, w_ref, b_ref, o_ref,
                  *, tt, ctx_len, seq_len):
    ti = pl.program_id(1)
    z = cg_ref[0].astype(F32) * u_ref[0].astype(F32)
    zp = cgp_ref[0].astype(F32) * up_ref[0].astype(F32)
    zn = cgn_ref[0].astype(F32) * un_ref[0].astype(F32)
    local = lax.broadcasted_iota(jnp.int32, (tt, 1), 0)
    row = ti * tt + local
    first, last = _segment_bounds(row, ctx_len, seq_len)
    w = w_ref[...]
    y = b_ref[...] + w[1:2, :] * z
    y = y + w[0:1, :] * jnp.where(row - 1 >= first, _shift_down(z, zp, 1, local), 0.0)
    y = y + w[2:3, :] * jnp.where(row + 1 <= last, _shift_up(z, zn, 1, local, tt), 0.0)
    o_ref[0] = (bg_ref[0].astype(F32) * y).astype(BF16)


def _sconv_call(p, w, b, ctx_len):
    bsz, l, _ = p.shape
    c = w.shape[1]
    tt = _tile(l, 768, 16)
    tc = 512
    ncb = c // tc
    b_col0 = (N_HEADS * HEAD_DIM + 2 * N_KV_HEADS * HEAD_DIM + 2 * c) // tc
    c_col0 = b_col0 + ncb
    u_col0 = c_col0 + ncb
    hb = tt // HALO
    n_hblk = l // HALO
    prev = lambda i: jnp.maximum(i * hb - 1, 0)
    nxt = lambda i: jnp.minimum((i + 1) * hb, n_hblk - 1)
    kern = functools.partial(_sconv_kernel, tt=tt, ctx_len=ctx_len, seq_len=l)
    return pl.pallas_call(
        kern,
        grid=(bsz, l // tt, ncb),
        in_specs=[pl.BlockSpec((1, tt, tc), lambda b_, i, j: (b_, i, b_col0 + j)),
                  pl.BlockSpec((1, tt, tc), lambda b_, i, j: (b_, i, c_col0 + j)),
                  pl.BlockSpec((1, tt, tc), lambda b_, i, j: (b_, i, u_col0 + j)),
                  pl.BlockSpec((1, HALO, tc), lambda b_, i, j: (b_, prev(i), c_col0 + j)),
                  pl.BlockSpec((1, HALO, tc), lambda b_, i, j: (b_, prev(i), u_col0 + j)),
                  pl.BlockSpec((1, HALO, tc), lambda b_, i, j: (b_, nxt(i), c_col0 + j)),
                  pl.BlockSpec((1, HALO, tc), lambda b_, i, j: (b_, nxt(i), u_col0 + j)),
                  pl.BlockSpec((SC_CONV, tc), lambda b_, i, j: (0, j)),
                  pl.BlockSpec((1, tc), lambda b_, i, j: (0, j))],
        out_specs=pl.BlockSpec((1, tt, tc), lambda b_, i, j: (b_, i, j)),
        out_shape=jax.ShapeDtypeStruct((bsz, l, c), BF16),
        compiler_params=_cparams(("parallel", "parallel", "parallel"), 32),
        name="sconv",
    )(p, p, p, p, p, p, p, w, b.reshape(1, c))


def _merge_kernel(xa_ref, xl_ref, xs_ref, ga_ref, gl_ref, gs_ref, wa_ref, wl_ref, ws_ref, o_ref):
    y = jax.nn.sigmoid(ga_ref[0].astype(F32)) * jnp.dot(xa_ref[0], wa_ref[...], preferred_element_type=F32)
    y = y + jax.nn.sigmoid(gl_ref[0].astype(F32)) * jnp.dot(xl_ref[0], wl_ref[...], preferred_element_type=F32)
    y = y + jax.nn.sigmoid(gs_ref[0].astype(F32)) * jnp.dot(xs_ref[0], ws_ref[...], preferred_element_type=F32)
    o_ref[0] = y.astype(BF16)


def _merge_call(x_att, x_lru, x_sc, p, layer, w_att, w_lru, w_sc):
    bsz, l, d = x_att.shape
    tm = _tile(l, 768, 16)
    tn = 512
    g_col0 = (p.shape[2] - 3 * d) // tn
    nj = d // tn
    xspec = pl.BlockSpec((1, tm, d), lambda b, i, j: (b, i, 0))
    wspec = pl.BlockSpec((None, d, tn), lambda b, i, j: (layer, 0, j))
    gspec = lambda k: pl.BlockSpec((1, tm, tn), lambda b, i, j: (b, i, g_col0 + k * nj + j))
    return pl.pallas_call(
        _merge_kernel,
        grid=(bsz, l // tm, nj),
        in_specs=[xspec, xspec, xspec, gspec(0), gspec(1), gspec(2), wspec, wspec, wspec],
        out_specs=pl.BlockSpec((1, tm, tn), lambda b, i, j: (b, i, j)),
        out_shape=jax.ShapeDtypeStruct((bsz, l, d), BF16),
        compiler_params=_cparams(("parallel", "parallel", "arbitrary"), 48),
        name="merge",
    )(x_att, x_lru, x_sc, p, p, p, w_att, w_lru, w_sc)


def _outproj_kernel(y_ref, w_ref, x_ref, gate_ref, shift_ref, scale_ref, g_ref, xo_ref, ht_ref,
                    *, tm, ctx_len):
    b = pl.program_id(0)
    i = pl.program_id(1)
    row = i * tm + lax.broadcasted_iota(jnp.int32, (tm, 1), 0)
    is_ctx = row < ctx_len
    acc = jnp.dot(y_ref[0], w_ref[...], preferred_element_type=F32)
    xn = x_ref[0] + _row_select(gate_ref, b, is_ctx) * acc
    xo_ref[0] = xn
    var = jnp.mean(xn * xn, axis=-1, keepdims=True)
    h = xn * lax.rsqrt(var + EPS) * g_ref[...]
    h = h * (1.0 + _row_select(scale_ref, b, is_ctx)) + _row_select(shift_ref, b, is_ctx)
    ht_ref[0] = h.T.astype(BF16)


def _outproj_call(y, w_out, x, mod, layer, gain, ctx_len):
    bsz, l, d = x.shape
    tm = _tile(l, 384, LANES)
    kern = functools.partial(_outproj_kernel, tm=tm, ctx_len=ctx_len)
    mspec = lambda k: pl.BlockSpec((None, 8, d), lambda b, i: (layer, 0, k))
    return pl.pallas_call(
        kern,
        grid=(bsz, l // tm),
        in_specs=[pl.BlockSpec((1, tm, d), lambda b, i: (b, i, 0)),
                  pl.BlockSpec((None, d, d), lambda b, i: (layer, 0, 0)),
                  pl.BlockSpec((1, tm, d), lambda b, i: (b, i, 0)),
                  mspec(2), mspec(3), mspec(4),
                  pl.BlockSpec((1, d), lambda b, i: (0, 0))],
        out_specs=[pl.BlockSpec((1, tm, d), lambda b, i: (b, i, 0)),
                   pl.BlockSpec((1, d, tm), lambda b, i: (b, 0, i))],
        out_shape=[jax.ShapeDtypeStruct((bsz, l, d), F32),
                   jax.ShapeDtypeStruct((bsz, d, l), BF16)],
        compiler_params=_cparams(("parallel", "parallel"), 48),
        name="outproj",
    )(y, w_out, x, mod, mod, mod, gain.reshape(1, d))


_CAND_ROWS = tuple(PEER_TOPK // (i + 1) for i in range(PEER_TOPK))


def _top16(s, v_scr):
    n = s.shape[0]
    key = lax.broadcasted_iota(jnp.int32, s.shape, 0)

    def body(r, carry):
        work, rank = carry
        m = jnp.max(work, axis=0, keepdims=True)
        v_scr[pl.ds(r, 1), :] = m
        first = jnp.min(jnp.where(work == m, key, n), axis=0, keepdims=True)
        sel = key == first
        return jnp.where(sel, NEG_INF, work), jnp.where(sel, jnp.asarray(r, F32), rank)

    _, rank = lax.fori_loop(0, PEER_TOPK, body, (s, jnp.full(s.shape, float(PEER_TOPK), F32)))
    return rank


MARK = -(2.0 ** 127)


def _top16_distinct(ss, v_scrs):
    def body(r, works):
        mark = jnp.asarray(r, F32) * (MARK / 32.0) + MARK
        out = []
        for work, v_scr in zip(works, v_scrs):
            m = jnp.max(work, axis=0, keepdims=True)
            v_scr[pl.ds(r, 1), :] = m
            out.append(jnp.where(work == m, mark, work))
        return tuple(out)

    works = lax.fori_loop(0, PEER_TOPK, body, tuple(ss))
    return [jnp.where(w <= MARK, (MARK - w) * (-32.0 / MARK), float(PEER_TOPK)) for w in works]


def _peer_topk_kernel(ht_ref, wq_ref, k1_ref, k2_ref, cnt_ref, e1_ref, rk_ref, e2_ref,
                      q_scr, v1_scr, v2_scr, *, tm):
    half = PEER_NKEYS
    q_scr[...] = jnp.dot(wq_ref[...], ht_ref[0], preferred_element_type=F32).astype(BF16)
    sub = lax.broadcasted_iota(jnp.int32, (SUBLANES, LANES), 0)
    big = PEER_TOPK * PEER_TOPK

    def compute(h, ls, exact):
        base = pl.multiple_of(h * 2 * half, 2 * half)
        s1 = jnp.dot(k1_ref[h], q_scr[pl.ds(base, half), ls], preferred_element_type=F32)
        s2 = jnp.dot(k2_ref[h], q_scr[pl.ds(base + half, half), ls], preferred_element_type=F32)
        if exact:
            rank1 = _top16(s1, v1_scr)
            rank2 = _top16(s2, v2_scr)
        else:
            rank1, rank2 = _top16_distinct((s1, s2), (v1_scr, v2_scr))
        v1 = v1_scr[...]
        v2 = v2_scr[...]

        pieces, poss = [], []
        for i in range(SUBLANES):
            for j0 in range(0, _CAND_ROWS[i], SUBLANES):
                c = v1[i:i + 1, :] + v2[j0:j0 + SUBLANES, :]
                valid = sub + j0 < _CAND_ROWS[i]
                pieces.append(jnp.where(valid, c, NEG_INF))
                poss.append(jnp.where(valid, i * PEER_TOPK + j0 + sub, big))
        pieces.append(v1[SUBLANES:, :] + v2[0:1, :])
        poss.append((sub + SUBLANES) * PEER_TOPK)
        orig = list(pieces)
        npc = len(pieces)

        def pick(_, carry2):
            cs, sels = carry2
            m = functools.reduce(jnp.maximum, cs)
            m = jnp.max(m, axis=0, keepdims=True)
            cand_pos = functools.reduce(jnp.minimum, [jnp.where(c == m, p_, big) for c, p_ in zip(cs, poss)])
            first = jnp.min(cand_pos, axis=0, keepdims=True)
            hit = [p_ == first for p_ in poss]
            cs = tuple(jnp.where(hh, NEG_INF, c) for hh, c in zip(hit, cs))
            sels = tuple(jnp.where(hh, 1.0, s_) for hh, s_ in zip(hit, sels))
            return cs, sels

        def pick_distinct(_, cs):
            m = functools.reduce(jnp.maximum, cs)
            m = jnp.max(m, axis=0, keepdims=True)
            return tuple(jnp.where(c == m, MARK, c) for c in cs)

        if exact:
            zeros = tuple(jnp.zeros((SUBLANES, LANES), F32) for _ in range(npc))
            _, sels = lax.fori_loop(0, PEER_TOPK, pick, (tuple(pieces), zeros))
        else:
            marked = lax.fori_loop(0, PEER_TOPK, pick_distinct, tuple(pieces))
            sels = tuple(jnp.where(c == MARK, 1.0, 0.0) for c in marked)

        top = v1[0:1, :] + v2[0:1, :]
        zsum = functools.reduce(
            lambda a_, b_: a_ + b_,
            [jnp.where(s_ > 0.0, jnp.exp(o - top), 0.0) for s_, o in zip(sels, orig)])
        zinv = 1.0 / jnp.sum(zsum, axis=0, keepdims=True)

        counts = []
        pi = 0
        for i in range(SUBLANES):
            c = None
            for j0 in range(0, _CAND_ROWS[i], SUBLANES):
                part = jnp.sum(sels[pi], axis=0, keepdims=True)
                c = part if c is None else c + part
                pi += 1
            counts.append(c)
        tail = sels[pi]
        for i in range(SUBLANES, PEER_TOPK):
            counts.append(tail[i - SUBLANES:i - SUBLANES + 1, :])
        cnt = jnp.zeros(rank1.shape, F32)
        for i in range(PEER_TOPK):
            cnt = jnp.where(rank1 == float(i), counts[i], cnt)

        cnt_ref[0, h, :, ls] = cnt
        e1_ref[0, h, :, ls] = jnp.exp(s1 - v1[0:1, :]) * zinv
        rk_ref[0, h, :, ls] = rank2.astype(BF16)
        e2_ref[0, h, :, ls] = jnp.exp(s2 - v2[0:1, :]).astype(BF16)
        ranked = jnp.where(rank1 < PEER_TOPK, 1.0, 0.0)
        ranked = jnp.maximum(jnp.sum(ranked, axis=0, keepdims=True),
                             jnp.sum(jnp.where(rank2 < PEER_TOPK, 1.0, 0.0), axis=0, keepdims=True))
        picked = jnp.sum(functools.reduce(lambda a_, b_: a_ + b_, sels), axis=0, keepdims=True)
        return jnp.max(jnp.maximum(ranked, picked))

    n_groups = tm // LANES

    def head_lane_group(idx, carry):
        h = idx // n_groups
        ls = pl.ds(pl.multiple_of((idx % n_groups) * LANES, LANES), LANES)
        most = compute(h, ls, exact=False)

        @pl.when(most > PEER_TOPK)
        def _():
            compute(h, ls, exact=True)

        return carry

    lax.fori_loop(0, PEER_HEADS * n_groups, head_lane_group, 0)


def _peer_topk_call(ht, layer, wq_t, k1, k2):
    bsz, d, l = ht.shape
    qd = wq_t.shape[1]
    tm = _tile(l, 384, LANES)
    kern = functools.partial(_peer_topk_kernel, tm=tm)
    ospec = pl.BlockSpec((1, PEER_HEADS, PEER_NKEYS, tm), lambda b, i: (b, 0, 0, i))
    oshape = jax.ShapeDtypeStruct((bsz, PEER_HEADS, PEER_NKEYS, l), F32)
    kspec = pl.BlockSpec((None, PEER_HEADS, PEER_NKEYS, PEER_NKEYS), lambda b, i: (layer, 0, 0, 0))
    return pl.pallas_call(
        kern,
        grid=(bsz, l // tm),
        in_specs=[pl.BlockSpec((1, d, tm), lambda b, i: (b, 0, i)),
                  pl.BlockSpec((None, qd, d), lambda b, i: (layer, 0, 0)),
                  kspec, kspec],
        out_specs=[ospec, ospec, ospec, ospec],
        out_shape=[oshape, oshape, jax.ShapeDtypeStruct(oshape.shape, BF16),
                   jax.ShapeDtypeStruct(oshape.shape, BF16)],
        scratch_shapes=[pltpu.VMEM((qd, tm), BF16),
                        pltpu.VMEM((PEER_TOPK, LANES), F32),
                        pltpu.VMEM((PEER_TOPK, LANES), F32)],
        compiler_params=_cparams(("parallel", "parallel"), 40),
        name="peer_topk",
    )(ht, wq_t, k1, k2)


def _peer_dense_kernel(ht_ref, u_ref, vt_ref, cnt_ref, e1_ref, rk_ref, e2_ref, o_ref, wz_scr, *, n_sub):
    e = pl.program_id(2)

    @pl.when(e == 0)
    def _():
        o_ref[...] = jnp.zeros(o_ref.shape, F32)

    s = jnp.dot(u_ref[...], ht_ref[0], preferred_element_type=F32)
    for a in range(n_sub):
        rows = slice(a * PEER_NKEYS, (a + 1) * PEER_NKEYS)
        w = None
        for h in range(PEER_HEADS):
            cnt = cnt_ref[0, h, a:a + 1, :].astype(BF16)
            e1 = e1_ref[0, h, a:a + 1, :].astype(BF16)
            term = jnp.where(rk_ref[0, h] < cnt, e2_ref[0, h] * e1, 0.0)
            w = term if w is None else w + term
        wz_scr[rows, :] = w * _gelu(s[rows, :]).astype(BF16)
    o_ref[0] += jnp.dot(vt_ref[...], wz_scr[...], preferred_element_type=F32)


def _peer_dense_call(ht, layer, u_bf16, vt_bf16, cnt, e1n, rank2, e2):
    bsz, d, l = ht.shape
    n_exp = u_bf16.shape[1]
    tm = _tile(l, 768, LANES)
    n_sub = SUBLANES
    te = n_sub * PEER_NKEYS
    kern = functools.partial(_peer_dense_kernel, n_sub=n_sub)
    aspec = pl.BlockSpec((1, PEER_HEADS, n_sub, tm), lambda b, i, e: (b, 0, e, i))
    fspec = pl.BlockSpec((1, PEER_HEADS, PEER_NKEYS, tm), lambda b, i, e: (b, 0, 0, i))
    return pl.pallas_call(
        kern,
        grid=(bsz, l // tm, n_exp // te),
        in_specs=[pl.BlockSpec((1, d, tm), lambda b, i, e: (b, 0, i)),
                  pl.BlockSpec((None, te, d), lambda b, i, e: (layer, e, 0)),
                  pl.BlockSpec((None, d, te), lambda b, i, e: (layer, 0, e)),
                  aspec, aspec, fspec, fspec],
        out_specs=pl.BlockSpec((1, d, tm), lambda b, i, e: (b, 0, i)),
        out_shape=jax.ShapeDtypeStruct((bsz, d, l), F32),
        scratch_shapes=[pltpu.VMEM((te, tm), BF16)],
        compiler_params=_cparams(("parallel", "parallel", "arbitrary"), 58),
        name="peer_dense",
    )(ht, u_bf16, vt_bf16, cnt, e1n, rank2, e2)


def _resid_kernel(x_ref, yt_ref, gate_ref, o_ref, *, tm, ctx_len):
    b = pl.program_id(0)
    i = pl.program_id(1)
    row = i * tm + lax.broadcasted_iota(jnp.int32, (tm, 1), 0)
    o_ref[0] = x_ref[0] + _row_select(gate_ref, b, row < ctx_len) * yt_ref[0].T


def _resid_call(x, yt, mod, layer, ctx_len):
    bsz, l, d = x.shape
    tm = _tile(l, 384, LANES)
    kern = functools.partial(_resid_kernel, tm=tm, ctx_len=ctx_len)
    return pl.pallas_call(
        kern,
        grid=(bsz, l // tm),
        in_specs=[pl.BlockSpec((1, tm, d), lambda b, i: (b, i, 0)),
                  pl.BlockSpec((1, d, tm), lambda b, i: (b, 0, i)),
                  pl.BlockSpec((None, 8, d), lambda b, i: (layer, 0, 5))],
        out_specs=pl.BlockSpec((1, tm, d), lambda b, i: (b, i, 0)),
        out_shape=jax.ShapeDtypeStruct((bsz, l, d), F32),
        compiler_params=_cparams(("parallel", "parallel"), 40),
        name="resid",
    )(x, yt, mod)


def _rope_tables(ctx_len, t_lat):
    rows = t_lat // GRID_W
    row = jnp.repeat(jnp.arange(rows, dtype=F32), GRID_W)
    col = jnp.tile(jnp.arange(GRID_W, dtype=F32), rows)
    inv = ROPE_THETA ** (-jnp.arange(ROPE_PAIRS, dtype=F32) / ROPE_PAIRS)
    ang = jnp.concatenate([row[:, None] * inv] * 2 + [col[:, None] * inv] * 2, axis=1)
    ang = jnp.concatenate([jnp.zeros((ctx_len, HEAD_DIM), F32), ang], axis=0)
    sign = jnp.where((jnp.arange(HEAD_DIM) & ROPE_PAIRS) == 0, -1.0, 1.0).astype(F32)
    return jnp.cos(ang), jnp.sin(ang) * sign


def kernel(x, c, ctx, c_ctx, w_mod, b_mod, norm_mix, norm_ffn, w_in, q_norm, k_norm, lru_conv_w, lru_conv_b, lru_wa, lru_ba, lru_wx, lru_bx, lru_lambda, sc_conv_w, sc_conv_b, w_o_attn, w_o_lru, w_o_sc, w_out, peer_wq, peer_k1, peer_k2, peer_u, peer_v):
    bsz, t_lat, d = x.shape
    ctx_len = ctx.shape[1]
    depth = w_mod.shape[0]
    assert bsz == 2, "modulation rows are laid out as [latent 0, latent 1, context]"

    xs = jnp.concatenate([ctx, x], axis=1)
    s8 = jnp.concatenate([c, c_ctx[None, :], jnp.zeros((8 - bsz - 1, d), F32)], axis=0)
    mod = _mod_call(s8, w_mod, b_mod)
    cos, sin_signed = _rope_tables(ctx_len, t_lat)

    w_in_b = w_in.astype(BF16)
    w_att_b, w_lru_b, w_sc_b = w_o_attn.astype(BF16), w_o_lru.astype(BF16), w_o_sc.astype(BF16)
    w_out_b = w_out.astype(BF16)
    wq_t = jnp.swapaxes(peer_wq, 1, 2).astype(BF16)
    k1_b, k2_b = peer_k1.astype(BF16), peer_k2.astype(BF16)
    u_b = peer_u.astype(BF16)
    vt_b = jnp.swapaxes(peer_v, 1, 2).astype(BF16)

    for l in range(depth):
        p = _inproj_call(xs, mod, l, norm_mix[l], w_in_b, ctx_len)
        qn, ke, ve = _qkprep_call(p, cos, sin_signed, q_norm[l], k_norm[l])
        x_att = _attn_call(qn, ke, ve, q_norm[l], k_norm[l], ctx_len)
        lru_args = (lru_conv_w[l], lru_conv_b[l])
        h_fwd = _lru_call(p, *lru_args, lru_wa[l, 0], lru_ba[l, 0], lru_wx[l, 0], lru_bx[l, 0],
                          lru_lambda[l, 0], ctx_len, reverse=False)
        x_lru = _lru_call(p, *lru_args, lru_wa[l, 1], lru_ba[l, 1], lru_wx[l, 1], lru_bx[l, 1],
                          lru_lambda[l, 1], ctx_len, reverse=True, h_fwd=h_fwd)
        x_sc = _sconv_call(p, sc_conv_w[l], sc_conv_b[l], ctx_len)
        y = _merge_call(x_att, x_lru, x_sc, p, l, w_att_b, w_lru_b, w_sc_b)
        xs, ht = _outproj_call(y, w_out_b, xs, mod, l, norm_ffn[l], ctx_len)
        cnt, e1n, rank2, e2 = _peer_topk_call(ht, l, wq_t, k1_b, k2_b)
        yt = _peer_dense_call(ht, l, u_b, vt_b, cnt, e1n, rank2, e2)
        xs = _resid_call(xs, yt, mod, l, ctx_len)
    return xs[:, ctx_len:, :]
```

```python
import functools
import math

import numpy as np
import jax
import jax.numpy as jnp
from jax import lax
from jax.experimental import pallas as pl
from jax.experimental.pallas import tpu as pltpu

F32 = jnp.float32
BF16 = jnp.bfloat16

GRID_W = 64
EPS = 1e-6
N_MOD = 6

N_HEADS = 16
N_KV_HEADS = 4
HEAD_DIM = 128
GROUP = N_HEADS // N_KV_HEADS
ROPE_PAIRS = HEAD_DIM // 4
ROPE_THETA = 10000.0

LRU_BLOCKS = 16
LRU_BLOCK = 128
LRU_CONV = 4
LRU_C = 8.0
SC_CONV = 3

PEER_HEADS = 8
PEER_NKEYS = 128
PEER_TOPK = 16

LANES = 128
SUBLANES = 8
HALO = SUBLANES
NEG_INF = float("-inf")
LOG2E = 1.4426950408889634
GELU_C = math.sqrt(2.0 / math.pi)
Q_SCALE = HEAD_DIM ** -0.5 * LOG2E
SHIFT_MARGIN = 1.02
MIN_DENOM = 2.0 ** -100


def _tile(n, target, mult):
    best = None
    for t in range(mult, min(n, target) + 1, mult):
        if n % t == 0:
            best = t
    assert best is not None, (n, target, mult)
    return best


def _cparams(sem, vmem_mib):
    return pltpu.CompilerParams(dimension_semantics=sem, vmem_limit_bytes=vmem_mib << 20)


def _gelu(x):
    return 0.5 * x * (1.0 + jnp.tanh(GELU_C * (x + 0.044715 * (x * x * x))))


def _row_select(mod_ref, b, is_ctx):
    return jnp.where(is_ctx, mod_ref[2:3, :], mod_ref[pl.ds(b, 1), :])


def _mod_kernel(s_ref, w_ref, b_ref, o_ref):
    s = s_ref[...]
    s = s * jax.nn.sigmoid(s)
    o_ref[0] = jnp.dot(s, w_ref[0], preferred_element_type=F32,
                       precision=lax.Precision.HIGHEST) + b_ref[0]


def _mod_call(s8, w_mod, b_mod):
    depth, d, n = w_mod.shape
    tn = _tile(n, 1024, LANES)
    return pl.pallas_call(
        _mod_kernel,
        grid=(depth, n // tn),
        in_specs=[pl.BlockSpec((8, d), lambda l, j: (0, 0)),
                  pl.BlockSpec((1, d, tn), lambda l, j: (l, 0, j)),
                  pl.BlockSpec((1, 1, tn), lambda l, j: (l, 0, j))],
        out_specs=pl.BlockSpec((1, 8, tn), lambda l, j: (l, 0, j)),
        out_shape=jax.ShapeDtypeStruct((depth, 8, n), F32),
        compiler_params=_cparams(("parallel", "parallel"), 32),
        name="mod",
    )(s8, w_mod, b_mod.reshape(depth, 1, n))


def _inproj_kernel(x_ref, shift_ref, scale_ref, g_ref, w_ref, o_ref, h_scr, *, tm, ctx_len):
    b = pl.program_id(0)
    i = pl.program_id(1)

    @pl.when(pl.program_id(2) == 0)
    def _():
        x = x_ref[0]
        var = jnp.mean(x * x, axis=-1, keepdims=True)
        y = x * lax.rsqrt(var + EPS) * g_ref[...]
        row = i * tm + lax.broadcasted_iota(jnp.int32, (tm, 1), 0)
        is_ctx = row < ctx_len
        sh = _row_select(shift_ref, b, is_ctx)
        sc = _row_select(scale_ref, b, is_ctx)
        h_scr[...] = (y * (1.0 + sc) + sh).astype(BF16)

    o_ref[0] = jnp.dot(h_scr[...], w_ref[...], preferred_element_type=F32).astype(BF16)


def _inproj_call(x, mod, layer, gain, w_bf16, ctx_len):
    bsz, l, d = x.shape
    n = w_bf16.shape[2]
    tm = _tile(l, 704, 16)
    tn = 1024
    kern = functools.partial(_inproj_kernel, tm=tm, ctx_len=ctx_len)
    return pl.pallas_call(
        kern,
        grid=(bsz, l // tm, n // tn),
        in_specs=[pl.BlockSpec((1, tm, d), lambda b, i, j: (b, i, 0)),
                  pl.BlockSpec((None, 8, d), lambda b, i, j: (layer, 0, 0)),
                  pl.BlockSpec((None, 8, d), lambda b, i, j: (layer, 0, 1)),
                  pl.BlockSpec((1, d), lambda b, i, j: (0, 0)),
                  pl.BlockSpec((None, d, tn), lambda b, i, j: (layer, 0, j))],
        out_specs=pl.BlockSpec((1, tm, tn), lambda b, i, j: (b, i, j)),
        out_shape=jax.ShapeDtypeStruct((bsz, l, n), BF16),
        scratch_shapes=[pltpu.VMEM((tm, d), BF16)],
        compiler_params=_cparams(("parallel", "parallel", "arbitrary"), 48),
        name="inproj",
    )(x, mod, mod, gain.reshape(1, d), w_bf16)


def _norm_rope(t, gain, cos, sin_signed, lane_lo):
    var = jnp.mean(t * t, axis=-1, keepdims=True)
    y = t * lax.rsqrt(var + EPS) * gain
    swapped = jnp.where(lane_lo, pltpu.roll(y, HEAD_DIM - ROPE_PAIRS, 1), pltpu.roll(y, ROPE_PAIRS, 1))
    return y * cos + swapped * sin_signed


def _qkprep_kernel(q_ref, k_ref, v_ref, cos_ref, sin_ref, qg_ref, kg_ref, qo_ref, ko_ref, vo_ref):
    cos = cos_ref[...]
    sin = sin_ref[...]
    lane = lax.broadcasted_iota(jnp.int32, cos.shape, 1)
    lane_lo = (lane & ROPE_PAIRS) == 0
    qg = qg_ref[...]
    kg = kg_ref[...]
    for h in range(N_HEADS):
        sl = slice(h * HEAD_DIM, (h + 1) * HEAD_DIM)
        t = q_ref[0, :, sl].astype(F32)
        qo_ref[0, :, sl] = (_norm_rope(t, qg, cos, sin, lane_lo) * Q_SCALE).astype(BF16)
    k_tail = jnp.where(lane == 0, 1.0, 0.0).astype(BF16)
    v_tail = jnp.ones(cos.shape, BF16)
    for h in range(N_KV_HEADS):
        sl = slice(h * HEAD_DIM, (h + 1) * HEAD_DIM)
        lo = slice(2 * h * HEAD_DIM, (2 * h + 1) * HEAD_DIM)
        hi = slice((2 * h + 1) * HEAD_DIM, (2 * h + 2) * HEAD_DIM)
        t = k_ref[0, :, sl].astype(F32)
        ko_ref[0, :, lo] = _norm_rope(t, kg, cos, sin, lane_lo).astype(BF16)
        ko_ref[0, :, hi] = k_tail
        vo_ref[0, :, lo] = v_ref[0, :, sl]
        vo_ref[0, :, hi] = v_tail


def _qkprep_call(p, cos, sin_signed, q_gain, k_gain):
    bsz, l, _ = p.shape
    qw = N_HEADS * HEAD_DIM
    kw = N_KV_HEADS * HEAD_DIM
    tm = _tile(l, 768, 16)
    return pl.pallas_call(
        _qkprep_kernel,
        grid=(bsz, l // tm),
        in_specs=[pl.BlockSpec((1, tm, qw), lambda b, i: (b, i, 0)),
                  pl.BlockSpec((1, tm, kw), lambda b, i: (b, i, qw // kw)),
                  pl.BlockSpec((1, tm, kw), lambda b, i: (b, i, qw // kw + 1)),
                  pl.BlockSpec((tm, HEAD_DIM), lambda b, i: (i, 0)),
                  pl.BlockSpec((tm, HEAD_DIM), lambda b, i: (i, 0)),
                  pl.BlockSpec((1, HEAD_DIM), lambda b, i: (0, 0)),
                  pl.BlockSpec((1, HEAD_DIM), lambda b, i: (0, 0))],
        out_specs=[pl.BlockSpec((1, tm, qw), lambda b, i: (b, i, 0)),
                   pl.BlockSpec((1, tm, 2 * kw), lambda b, i: (b, i, 0)),
                   pl.BlockSpec((1, tm, 2 * kw), lambda b, i: (b, i, 0))],
        out_shape=[jax.ShapeDtypeStruct((bsz, l, qw), BF16),
                   jax.ShapeDtypeStruct((bsz, l, 2 * kw), BF16),
                   jax.ShapeDtypeStruct((bsz, l, 2 * kw), BF16)],
        compiler_params=_cparams(("parallel", "parallel"), 40),
        name="qkprep",
    )(p, p, p, cos, sin_signed, q_gain.reshape(1, HEAD_DIM), k_gain.reshape(1, HEAD_DIM))


def _attn_kernel(q_ref, k_ref, v_ref, qg_ref, kg_ref, o_ref, qs_scr, acc_scr, sa_scr, sb_scr, m_scr, l_scr,
                 *, tq, tkc, ctx_len, n_lat_chunks):
    qi = pl.program_id(2)
    q = q_ref[0]
    gq = jnp.max(jnp.abs(qg_ref[...]), axis=-1, keepdims=True)
    gk = jnp.max(jnp.abs(kg_ref[...]), axis=-1, keepdims=True)
    bound = (SHIFT_MARGIN * HEAD_DIM * Q_SCALE) * gq * gk
    lane = lax.broadcasted_iota(jnp.int32, (1, HEAD_DIM), 1)
    tail = jnp.where(lane == 0, -bound, 0.0).astype(BF16)
    for h in range(GROUP):
        qs_scr[h * tq:(h + 1) * tq, :HEAD_DIM] = q[:, h * HEAD_DIM:(h + 1) * HEAD_DIM]
        qs_scr[h * tq:(h + 1) * tq, HEAD_DIM:] = jnp.broadcast_to(tail, (tq, HEAD_DIM))
    n = jnp.where(qi < ctx_len // tq, 0, n_lat_chunks)

    def lat_start(c):
        return pl.multiple_of(ctx_len + c * tkc, math.gcd(ctx_len, tkc))

    def write(out):
        for h in range(GROUP):
            o_ref[0, :, h * HEAD_DIM:(h + 1) * HEAD_DIM] = out[h * tq:(h + 1) * tq, :].astype(BF16)

    def scores(start, size):
        return lax.dot_general(qs_scr[...], k_ref[0, pl.ds(start, size), :], (((1,), (1,)), ((), ())),
                               preferred_element_type=F32)

    def accumulate(s, start, size):
        p = jnp.exp2(s).astype(BF16)
        acc_scr[...] += jnp.dot(p, v_ref[0, pl.ds(start, size), :], preferred_element_type=F32)

    acc_scr[...] = jnp.zeros(acc_scr.shape, F32)
    sa_scr[...] = scores(lat_start(0), tkc)
    accumulate(scores(0, ctx_len), 0, ctx_len)

    def chunk_pair(j, look_ahead):
        c0 = 2 * j
        sb_scr[...] = scores(lat_start(c0 + 1), tkc)
        accumulate(sa_scr[...], lat_start(c0), tkc)
        if look_ahead:
            sa_scr[...] = scores(lat_start(c0 + 2), tkc)
        accumulate(sb_scr[...], lat_start(c0 + 1), tkc)

    def shifted_body(j, carry):
        chunk_pair(j, look_ahead=True)
        return carry

    lax.fori_loop(0, n // 2 - 1, shifted_body, 0)

    @pl.when(n > 0)
    def _():
        chunk_pair(n // 2 - 1, look_ahead=False)

    acc = acc_scr[...]
    den = acc[:, HEAD_DIM:]
    write(acc[:, :HEAD_DIM] / den)

    @pl.when(jnp.logical_not(jnp.min(den) >= MIN_DENOM))
    def _():
        m_scr[...] = jnp.full(m_scr.shape, NEG_INF, F32)
        l_scr[...] = jnp.zeros(l_scr.shape, F32)
        acc_scr[...] = jnp.zeros(acc_scr.shape, F32)

        def online_chunk(start, size):
            k = k_ref[0, pl.ds(start, size), :HEAD_DIM]
            v = v_ref[0, pl.ds(start, size), :HEAD_DIM]
            s = lax.dot_general(qs_scr[:, :HEAD_DIM], k, (((1,), (1,)), ((), ())),
                                preferred_element_type=F32)
            m_old = m_scr[...]
            m_new = jnp.maximum(m_old, jnp.max(s, axis=-1, keepdims=True))
            alpha = jnp.exp2(m_old - m_new)
            p = jnp.exp2(s - m_new)
            l_scr[...] = alpha * l_scr[...] + jnp.sum(p, axis=-1, keepdims=True)
            acc_scr[:, :HEAD_DIM] = alpha * acc_scr[:, :HEAD_DIM] + jnp.dot(
                p.astype(BF16), v, preferred_element_type=F32)
            m_scr[...] = m_new

        online_chunk(0, ctx_len)

        def online_body(c, carry):
            online_chunk(lat_start(c), tkc)
            return carry

        lax.fori_loop(0, n, online_body, 0)
        write(acc_scr[:, :HEAD_DIM] / l_scr[...])


def _attn_call(qn, ke, ve, q_gain, k_gain, ctx_len):
    bsz, l, qw = qn.shape
    tq = 256
    assert ctx_len % tq == 0 and l % tq == 0
    t_lat = l - ctx_len
    tkc = _tile(t_lat // 2, 512, 256)
    gw = GROUP * HEAD_DIM
    ew = 2 * HEAD_DIM
    kern = functools.partial(_attn_kernel, tq=tq, tkc=tkc, ctx_len=ctx_len, n_lat_chunks=t_lat // tkc)
    return pl.pallas_call(
        kern,
        grid=(bsz, N_KV_HEADS, l // tq),
        in_specs=[pl.BlockSpec((1, tq, gw), lambda b, g, i: (b, i, g)),
                  pl.BlockSpec((1, l, ew), lambda b, g, i: (b, 0, g)),
                  pl.BlockSpec((1, l, ew), lambda b, g, i: (b, 0, g)),
                  pl.BlockSpec((1, HEAD_DIM), lambda b, g, i: (0, 0)),
                  pl.BlockSpec((1, HEAD_DIM), lambda b, g, i: (0, 0))],
        out_specs=pl.BlockSpec((1, tq, gw), lambda b, g, i: (b, i, g)),
        out_shape=jax.ShapeDtypeStruct((bsz, l, qw), BF16),
        scratch_shapes=[pltpu.VMEM((GROUP * tq, ew), BF16),
                        pltpu.VMEM((GROUP * tq, ew), F32),
                        pltpu.VMEM((GROUP * tq, tkc), F32),
                        pltpu.VMEM((GROUP * tq, tkc), F32),
                        pltpu.VMEM((GROUP * tq, 1), F32),
                        pltpu.VMEM((GROUP * tq, 1), F32)],
        compiler_params=_cparams(("parallel", "parallel", "arbitrary"), 48),
        name="attn",
    )(qn, ke, ve, q_gain.reshape(1, HEAD_DIM), k_gain.reshape(1, HEAD_DIM))


def _segment_bounds(row, ctx_len, seq_len):
    is_ctx = row < ctx_len
    first = jnp.where(is_ctx, 0, ctx_len)
    last = jnp.where(is_ctx, ctx_len - 1, seq_len - 1)
    return first, last


def _shift_down(x, prev, k, local):
    y = pltpu.roll(x, k, 0)
    for r in range(k):
        y = jnp.where(local == r, prev[HALO - k + r:HALO - k + r + 1, :], y)
    return y


def _shift_up(x, nxt, k, local, tt):
    y = pltpu.roll(x, tt - k, 0)
    for r in range(k):
        y = jnp.where(local == tt - k + r, nxt[r:r + 1, :], y)
    return y


def _halo_shifts(x, prev, nxt):
    tt = x.shape[0]
    sub = lax.broadcasted_iota(jnp.int32, (SUBLANES, x.shape[1]), 0)

    def down(k):
        y = pltpu.roll(x, k, 0)
        head = y[:SUBLANES]
        for r in range(k):
            head = jnp.where(sub == r, prev[HALO - k + r:HALO - k + r + 1, :], head)
        return jnp.concatenate([head, y[SUBLANES:]], axis=0)

    def up(k):
        y = pltpu.roll(x, tt - k, 0)
        tail = y[tt - SUBLANES:]
        for r in range(k):
            tail = jnp.where(sub == SUBLANES - k + r, nxt[r:r + 1, :], tail)
        return jnp.concatenate([y[:tt - SUBLANES], tail], axis=0)

    return down, up


def _scan_tile(a, d, carry, reverse):
    n_groups = a.shape[0] // SUBLANES
    sub = lax.broadcasted_iota(jnp.int32, (SUBLANES, a.shape[1]), 0)
    groups = []
    for v in range(n_groups):
        av = a[v * SUBLANES:(v + 1) * SUBLANES]
        dv = d[v * SUBLANES:(v + 1) * SUBLANES]
        for k in (1, 2, 4):
            keep = sub < SUBLANES - k if reverse else sub >= k
            shift = SUBLANES - k if reverse else k
            a_n = jnp.where(keep, pltpu.roll(av, shift, 0), 1.0)
            d_n = jnp.where(keep, pltpu.roll(dv, shift, 0), 0.0)
            dv = av * d_n + dv
            av = av * a_n
        groups.append((av, dv))
    hs = [None] * n_groups
    for v in (reversed(range(n_groups)) if reverse else range(n_groups)):
        av, dv = groups[v]
        hv = dv + av * carry
        carry = hv[0:1] if reverse else hv[SUBLANES - 1:SUBLANES]
        hs[v] = hv
    return jnp.concatenate(hs, axis=0), carry


def _lru_kernel(*refs, reverse, tt, ctx_len, n_sub):
    if reverse:
        (x_ref, xp_ref, xn_ref, cw_ref, cb_ref, wa_ref, ba_ref, wx_ref, bx_ref, lam_ref,
         hf_ref, g_ref, o_ref, carry_scr) = refs
    else:
        (x_ref, xp_ref, xn_ref, cw_ref, cb_ref, wa_ref, ba_ref, wx_ref, bx_ref, lam_ref,
         o_ref, carry_scr) = refs
    s = pl.program_id(2)
    nt = pl.num_programs(2)
    ti = jnp.where(s == 0, 0, nt - s) if reverse else s

    @pl.when(s == 0)
    def _():
        carry_scr[...] = jnp.zeros(carry_scr.shape, F32)

    prev_ok = jnp.logical_and(ti != 0, ti * tt != ctx_len)
    next_ok = jnp.logical_and(ti != nt - 1, (ti + 1) * tt != ctx_len)
    x = x_ref[0].astype(F32)
    xp = jnp.where(prev_ok, xp_ref[0].astype(F32), 0.0)
    xn = jnp.where(next_ok, xn_ref[0].astype(F32), 0.0)
    down, up = _halo_shifts(x, xp, xn)
    cw = cw_ref[...]
    u = cb_ref[...] + cw[2:3, :] * x + cw[0:1, :] * down(2) + cw[1:2, :] * down(1) + cw[3:4, :] * up(1)
    nlam = -lam_ref[...]
    softplus = jnp.maximum(nlam, 0.0) + jnp.log1p(jnp.exp(-jnp.abs(nlam)))

    for j in range(n_sub):
        ls = slice(j * LRU_BLOCK, (j + 1) * LRU_BLOCK)
        uj = u[:, ls]
        ub = uj.astype(BF16)
        r = jax.nn.sigmoid(jnp.dot(ub, wa_ref[j], preferred_element_type=F32) + ba_ref[:, ls])
        i = jax.nn.sigmoid(jnp.dot(ub, wx_ref[j], preferred_element_type=F32) + bx_ref[:, ls])
        a = jnp.exp((-LRU_C) * r * softplus[:, ls])
        d = jnp.sqrt(1.0 - a * a) * (i * uj)
        h, carry = _scan_tile(a, d, carry_scr[:, ls], reverse)
        carry_scr[:, ls] = carry
        if reverse:
            o_ref[0, :, ls] = ((hf_ref[0, :, ls] + h) * _gelu(g_ref[0, :, ls].astype(F32))).astype(BF16)
        else:
            o_ref[0, :, ls] = h


def _lru_call(p, conv_w, conv_b, wa, ba, wx, bx, lam, ctx_len, reverse, h_fwd=None):
    bsz, l, _ = p.shape
    c = conv_w.shape[1]
    tt = 256
    assert ctx_len % tt == 0 and l % tt == 0
    nt = l // tt
    n_sub = 4
    cw = n_sub * LRU_BLOCK
    nb = c // cw
    x_col0 = (N_HEADS + 2 * N_KV_HEADS) * HEAD_DIM // cw
    g_col0 = x_col0 + nb
    hb = tt // HALO
    n_hblk = l // HALO

    def tile_of(s):
        return jnp.where(s == 0, 0, nt - s) if reverse else s

    x_map = lambda b, cb, s: (b, tile_of(s), x_col0 + cb)
    prev_map = lambda b, cb, s: (b, jnp.maximum(tile_of(s) * hb - 1, 0), x_col0 + cb)
    next_map = lambda b, cb, s: (b, jnp.minimum((tile_of(s) + 1) * hb, n_hblk - 1), x_col0 + cb)
    vec_map = lambda b, cb, s: (0, cb)
    w_spec = pl.BlockSpec((n_sub, LRU_BLOCK, LRU_BLOCK), lambda b, cb, s: (cb, 0, 0))
    in_specs = [pl.BlockSpec((1, tt, cw), x_map),
                pl.BlockSpec((1, HALO, cw), prev_map),
                pl.BlockSpec((1, HALO, cw), next_map),
                pl.BlockSpec((LRU_CONV, cw), vec_map),
                pl.BlockSpec((1, cw), vec_map),
                w_spec,
                pl.BlockSpec((1, cw), vec_map),
                w_spec,
                pl.BlockSpec((1, cw), vec_map),
                pl.BlockSpec((1, cw), vec_map)]
    args = [p, p, p, conv_w, conv_b.reshape(1, c), wa.astype(BF16), ba.reshape(1, c),
            wx.astype(BF16), bx.reshape(1, c), lam.reshape(1, c)]
    out_map = lambda b, cb, s: (b, tile_of(s), cb)
    if reverse:
        in_specs += [pl.BlockSpec((1, tt, cw), out_map),
                     pl.BlockSpec((1, tt, cw), lambda b, cb, s: (b, tile_of(s), g_col0 + cb))]
        args += [h_fwd, p]
        out_dtype = BF16
    else:
        out_dtype = F32
    kern = functools.partial(_lru_kernel, reverse=reverse, tt=tt, ctx_len=ctx_len, n_sub=n_sub)
    return pl.pallas_call(
        kern,
        grid=(bsz, nb, nt),
        in_specs=in_specs,
        out_specs=pl.BlockSpec((1, tt, cw), out_map),
        out_shape=jax.ShapeDtypeStruct((bsz, l, c), out_dtype),
        scratch_shapes=[pltpu.VMEM((1, cw), F32)],
        compiler_params=_cparams(("parallel", "parallel", "arbitrary"), 32),
        name="lru_rev" if reverse else "lru_fwd",
    )(*args)


def _sconv_kernel(bg_ref, cg_ref, u_ref, cgp_ref, up_ref, cgn_ref, un_ref, w_ref, b_ref, o_ref,
                  *, tt, ctx_len, seq_len):
    ti = pl.program_id(1)
    z = cg_ref[0].astype(F32) * u_ref[0].astype(F32)
    zp = cgp_ref[0].astype(F32) * up_ref[0].astype(F32)
    zn = cgn_ref[0].astype(F32) * un_ref[0].astype(F32)
    local = lax.broadcasted_iota(jnp.int32, (tt, 1), 0)
    row = ti * tt + local
    first, last = _segment_bounds(row, ctx_len, seq_len)
    w = w_ref[...]
    y = b_ref[...] + w[1:2, :] * z
    y = y + w[0:1, :] * jnp.where(row - 1 >= first, _shift_down(z, zp, 1, local), 0.0)
    y = y + w[2:3, :] * jnp.where(row + 1 <= last, _shift_up(z, zn, 1, local, tt), 0.0)
    o_ref[0] = (bg_ref[0].astype(F32) * y).astype(BF16)


def _sconv_call(p, w, b, ctx_len):
    bsz, l, _ = p.shape
    c = w.shape[1]
    tt = _tile(l, 768, 16)
    tc = 512
    ncb = c // tc
    b_col0 = (N_HEADS * HEAD_DIM + 2 * N_KV_HEADS * HEAD_DIM + 2 * c) // tc
    c_col0 = b_col0 + ncb
    u_col0 = c_col0 + ncb
    hb = tt // HALO
    n_hblk = l // HALO
    prev = lambda i: jnp.maximum(i * hb - 1, 0)
    nxt = lambda i: jnp.minimum((i + 1) * hb, n_hblk - 1)
    kern = functools.partial(_sconv_kernel, tt=tt, ctx_len=ctx_len, seq_len=l)
    return pl.pallas_call(
        kern,
        grid=(bsz, l // tt, ncb),
        in_specs=[pl.BlockSpec((1, tt, tc), lambda b_, i, j: (b_, i, b_col0 + j)),
                  pl.BlockSpec((1, tt, tc), lambda b_, i, j: (b_, i, c_col0 + j)),
                  pl.BlockSpec((1, tt, tc), lambda b_, i, j: (b_, i, u_col0 + j)),
                  pl.BlockSpec((1, HALO, tc), lambda b_, i, j: (b_, prev(i), c_col0 + j)),
                  pl.BlockSpec((1, HALO, tc), lambda b_, i, j: (b_, prev(i), u_col0 + j)),
                  pl.BlockSpec((1, HALO, tc), lambda b_, i, j: (b_, nxt(i), c_col0 + j)),
                  pl.BlockSpec((1, HALO, tc), lambda b_, i, j: (b_, nxt(i), u_col0 + j)),
                  pl.BlockSpec((SC_CONV, tc), lambda b_, i, j: (0, j)),
                  pl.BlockSpec((1, tc), lambda b_, i, j: (0, j))],
        out_specs=pl.BlockSpec((1, tt, tc), lambda b_, i, j: (b_, i, j)),
        out_shape=jax.ShapeDtypeStruct((bsz, l, c), BF16),
        compiler_params=_cparams(("parallel", "parallel", "parallel"), 32),
        name="sconv",
    )(p, p, p, p, p, p, p, w, b.reshape(1, c))


def _merge_kernel(xa_ref, xl_ref, xs_ref, ga_ref, gl_ref, gs_ref, wa_ref, wl_ref, ws_ref, o_ref):
    y = jax.nn.sigmoid(ga_ref[0].astype(F32)) * jnp.dot(xa_ref[0], wa_ref[...], preferred_element_type=F32)
    y = y + jax.nn.sigmoid(gl_ref[0].astype(F32)) * jnp.dot(xl_ref[0], wl_ref[...], preferred_element_type=F32)
    y = y + jax.nn.sigmoid(gs_ref[0].astype(F32)) * jnp.dot(xs_ref[0], ws_ref[...], preferred_element_type=F32)
    o_ref[0] = y.astype(BF16)


def _merge_call(x_att, x_lru, x_sc, p, layer, w_att, w_lru, w_sc):
    bsz, l, d = x_att.shape
    tm = _tile(l, 768, 16)
    tn = 512
    g_col0 = (p.shape[2] - 3 * d) // tn
    nj = d // tn
    xspec = pl.BlockSpec((1, tm, d), lambda b, i, j: (b, i, 0))
    wspec = pl.BlockSpec((None, d, tn), lambda b, i, j: (layer, 0, j))
    gspec = lambda k: pl.BlockSpec((1, tm, tn), lambda b, i, j: (b, i, g_col0 + k * nj + j))
    return pl.pallas_call(
        _merge_kernel,
        grid=(bsz, l // tm, nj),
        in_specs=[xspec, xspec, xspec, gspec(0), gspec(1), gspec(2), wspec, wspec, wspec],
        out_specs=pl.BlockSpec((1, tm, tn), lambda b, i, j: (b, i, j)),
        out_shape=jax.ShapeDtypeStruct((bsz, l, d), BF16),
        compiler_params=_cparams(("parallel", "parallel", "arbitrary"), 48),
        name="merge",
    )(x_att, x_lru, x_sc, p, p, p, w_att, w_lru, w_sc)


def _outproj_kernel(y_ref, w_ref, x_ref, gate_ref, shift_ref, scale_ref, g_ref, xo_ref, ht_ref,
                    *, tm, ctx_len):
    b = pl.program_id(0)
    i = pl.program_id(1)
    row = i * tm + lax.broadcasted_iota(jnp.int32, (tm, 1), 0)
    is_ctx = row < ctx_len
    acc = jnp.dot(y_ref[0], w_ref[...], preferred_element_type=F32)
    xn = x_ref[0] + _row_select(gate_ref, b, is_ctx) * acc
    xo_ref[0] = xn
    var = jnp.mean(xn * xn, axis=-1, keepdims=True)
    h = xn * lax.rsqrt(var + EPS) * g_ref[...]
    h = h * (1.0 + _row_select(scale_ref, b, is_ctx)) + _row_select(shift_ref, b, is_ctx)
    ht_ref[0] = h.T.astype(BF16)


def _outproj_call(y, w_out, x, mod, layer, gain, ctx_len):
    bsz, l, d = x.shape
    tm = _tile(l, 384, LANES)
    kern = functools.partial(_outproj_kernel, tm=tm, ctx_len=ctx_len)
    mspec = lambda k: pl.BlockSpec((None, 8, d), lambda b, i: (layer, 0, k))
    return pl.pallas_call(
        kern,
        grid=(bsz, l // tm),
        in_specs=[pl.BlockSpec((1, tm, d), lambda b, i: (b, i, 0)),
                  pl.BlockSpec((None, d, d), lambda b, i: (layer, 0, 0)),
                  pl.BlockSpec((1, tm, d), lambda b, i: (b, i, 0)),
                  mspec(2), mspec(3), mspec(4),
                  pl.BlockSpec((1, d), lambda b, i: (0, 0))],
        out_specs=[pl.BlockSpec((1, tm, d), lambda b, i: (b, i, 0)),
                   pl.BlockSpec((1, d, tm), lambda b, i: (b, 0, i))],
        out_shape=[jax.ShapeDtypeStruct((bsz, l, d), F32),
                   jax.ShapeDtypeStruct((bsz, d, l), BF16)],
        compiler_params=_cparams(("parallel", "parallel"), 48),
        name="outproj",
    )(y, w_out, x, mod, mod, mod, gain.reshape(1, d))


_CAND_ROWS = tuple(PEER_TOPK // (i + 1) for i in range(PEER_TOPK))


def _top16(s, v_scr):
    n = s.shape[0]
    key = lax.broadcasted_iota(jnp.int32, s.shape, 0)

    def body(r, carry):
        work, rank = carry
        m = jnp.max(work, axis=0, keepdims=True)
        v_scr[pl.ds(r, 1), :] = m
        first = jnp.min(jnp.where(work == m, key, n), axis=0, keepdims=True)
        sel = key == first
        return jnp.where(sel, NEG_INF, work), jnp.where(sel, jnp.asarray(r, F32), rank)

    _, rank = lax.fori_loop(0, PEER_TOPK, body, (s, jnp.full(s.shape, float(PEER_TOPK), F32)))
    return rank


MARK = -(2.0 ** 127)


def _top16_distinct(ss, v_scrs):
    def body(r, works):
        mark = jnp.asarray(r, F32) * (MARK / 32.0) + MARK
        out = []
        for work, v_scr in zip(works, v_scrs):
            m = jnp.max(work, axis=0, keepdims=True)
            v_scr[pl.ds(r, 1), :] = m
            out.append(jnp.where(work == m, mark, work))
        return tuple(out)

    works = lax.fori_loop(0, PEER_TOPK, body, tuple(ss))
    return [jnp.where(w <= MARK, (MARK - w) * (-32.0 / MARK), float(PEER_TOPK)) for w in works]


def _peer_topk_kernel(ht_ref, wq_ref, k1_ref, k2_ref, cnt_ref, e1_ref, rk_ref, e2_ref,
                      q_scr, v_scr, *, tm, n_par):
    half = PEER_NKEYS
    q_scr[...] = jnp.dot(wq_ref[...], ht_ref[0], preferred_element_type=F32).astype(BF16)
    sub = lax.broadcasted_iota(jnp.int32, (SUBLANES, LANES), 0)
    big = PEER_TOPK * PEER_TOPK

    def candidates(v1, v2):
        pieces, poss = [], []
        for i in range(SUBLANES):
            for j0 in range(0, _CAND_ROWS[i], SUBLANES):
                c = v1[i:i + 1, :] + v2[j0:j0 + SUBLANES, :]
                valid = sub + j0 < _CAND_ROWS[i]
                pieces.append(jnp.where(valid, c, NEG_INF))
                poss.append(jnp.where(valid, i * PEER_TOPK + j0 + sub, big))
        pieces.append(v1[SUBLANES:, :] + v2[0:1, :])
        poss.append((sub + SUBLANES) * PEER_TOPK)
        return pieces, poss

    def pick_exact(pieces, poss):
        def pick(_, carry2):
            cs, sels = carry2
            m = functools.reduce(jnp.maximum, cs)
            m = jnp.max(m, axis=0, keepdims=True)
            cand_pos = functools.reduce(jnp.minimum, [jnp.where(c == m, p_, big) for c, p_ in zip(cs, poss)])
            first = jnp.min(cand_pos, axis=0, keepdims=True)
            hit = [p_ == first for p_ in poss]
            cs = tuple(jnp.where(hh, NEG_INF, c) for hh, c in zip(hit, cs))
            sels = tuple(jnp.where(hh, 1.0, s_) for hh, s_ in zip(hit, sels))
            return cs, sels

        zeros = tuple(jnp.zeros((SUBLANES, LANES), F32) for _ in pieces)
        return lax.fori_loop(0, PEER_TOPK, pick, (tuple(pieces), zeros))[1]

    def pick_distinct(groups):
        npc = len(groups[0])

        def pick(_, cs):
            out = []
            for g in range(len(groups)):
                grp = cs[g * npc:(g + 1) * npc]
                m = jnp.max(functools.reduce(jnp.maximum, grp), axis=0, keepdims=True)
                out.extend(jnp.where(c == m, MARK, c) for c in grp)
            return tuple(out)

        marked = lax.fori_loop(0, PEER_TOPK, pick, tuple(c for grp in groups for c in grp))
        return [tuple(jnp.where(c == MARK, 1.0, 0.0) for c in marked[g * npc:(g + 1) * npc])
                for g in range(len(groups))]

    def compute(h, lss, exact):
        base = pl.multiple_of(h * 2 * half, 2 * half)
        n_g = len(lss)
        s1s = [jnp.dot(k1_ref[h], q_scr[pl.ds(base, half), ls], preferred_element_type=F32) for ls in lss]
        s2s = [jnp.dot(k2_ref[h], q_scr[pl.ds(base + half, half), ls], preferred_element_type=F32)
               for ls in lss]
        v1_refs = [v_scr.at[2 * g] for g in range(n_g)]
        v2_refs = [v_scr.at[2 * g + 1] for g in range(n_g)]
        if exact:
            rank1s = [_top16(s, r) for s, r in zip(s1s, v1_refs)]
            rank2s = [_top16(s, r) for s, r in zip(s2s, v2_refs)]
        else:
            ranks = _top16_distinct(tuple(s1s + s2s), tuple(v1_refs + v2_refs))
            rank1s, rank2s = ranks[:n_g], ranks[n_g:]
        v1s = [r[...] for r in v1_refs]
        v2s = [r[...] for r in v2_refs]
        cands = [candidates(v1, v2) for v1, v2 in zip(v1s, v2s)]
        if exact:
            all_sels = [pick_exact(pieces, poss) for pieces, poss in cands]
        else:
            all_sels = pick_distinct([pieces for pieces, _ in cands])
        most = None
        for g in range(n_g):
            most_g = finish(h, lss[g], s1s[g], s2s[g], rank1s[g], rank2s[g], v1s[g], v2s[g],
                            cands[g][0], all_sels[g])
            most = most_g if most is None else jnp.maximum(most, most_g)
        return most

    def finish(h, ls, s1, s2, rank1, rank2, v1, v2, orig, sels):
        top = v1[0:1, :] + v2[0:1, :]
        zsum = functools.reduce(
            lambda a_, b_: a_ + b_,
            [jnp.where(s_ > 0.0, jnp.exp(o - top), 0.0) for s_, o in zip(sels, orig)])
        zinv = 1.0 / jnp.sum(zsum, axis=0, keepdims=True)

        counts = []
        pi = 0
        for i in range(SUBLANES):
            c = None
            for j0 in range(0, _CAND_ROWS[i], SUBLANES):
                part = jnp.sum(sels[pi], axis=0, keepdims=True)
                c = part if c is None else c + part
                pi += 1
            counts.append(c)
        tail = sels[pi]
        for i in range(SUBLANES, PEER_TOPK):
            counts.append(tail[i - SUBLANES:i - SUBLANES + 1, :])
        cnt = jnp.zeros(rank1.shape, F32)
        for i in range(PEER_TOPK):
            cnt = jnp.where(rank1 == float(i), counts[i], cnt)

        cnt_ref[0, h, :, ls] = cnt
        e1_ref[0, h, :, ls] = jnp.exp(s1 - v1[0:1, :]) * zinv
        rk_ref[0, h, :, ls] = rank2.astype(BF16)
        e2_ref[0, h, :, ls] = jnp.exp(s2 - v2[0:1, :]).astype(BF16)
        ranked = jnp.where(rank1 < PEER_TOPK, 1.0, 0.0)
        ranked = jnp.maximum(jnp.sum(ranked, axis=0, keepdims=True),
                             jnp.sum(jnp.where(rank2 < PEER_TOPK, 1.0, 0.0), axis=0, keepdims=True))
        picked = jnp.sum(functools.reduce(lambda a_, b_: a_ + b_, sels), axis=0, keepdims=True)
        return jnp.max(jnp.maximum(ranked, picked))

    n_trips = tm // (n_par * LANES)

    def head_lane_groups(idx, carry):
        h = idx // n_trips
        first = (idx % n_trips) * n_par
        lss = [pl.ds(pl.multiple_of((first + g) * LANES, LANES), LANES) for g in range(n_par)]
        most = compute(h, lss, exact=False)

        @pl.when(most > PEER_TOPK)
        def _():
            compute(h, lss, exact=True)

        return carry

    lax.fori_loop(0, PEER_HEADS * n_trips, head_lane_groups, 0)


def _peer_topk_call(ht, layer, wq_t, k1, k2):
    bsz, d, l = ht.shape
    qd = wq_t.shape[1]
    n_par = 2
    tm = _tile(l, 768, n_par * LANES)
    kern = functools.partial(_peer_topk_kernel, tm=tm, n_par=n_par)
    ospec = pl.BlockSpec((1, PEER_HEADS, PEER_NKEYS, tm), lambda b, i: (b, 0, 0, i))
    oshape = jax.ShapeDtypeStruct((bsz, PEER_HEADS, PEER_NKEYS, l), F32)
    kspec = pl.BlockSpec((None, PEER_HEADS, PEER_NKEYS, PEER_NKEYS), lambda b, i: (layer, 0, 0, 0))
    return pl.pallas_call(
        kern,
        grid=(bsz, l // tm),
        in_specs=[pl.BlockSpec((1, d, tm), lambda b, i: (b, 0, i)),
                  pl.BlockSpec((None, qd, d), lambda b, i: (layer, 0, 0)),
                  kspec, kspec],
        out_specs=[ospec, ospec, ospec, ospec],
        out_shape=[oshape, oshape, jax.ShapeDtypeStruct(oshape.shape, BF16),
                   jax.ShapeDtypeStruct(oshape.shape, BF16)],
        scratch_shapes=[pltpu.VMEM((qd, tm), BF16),
                        pltpu.VMEM((2 * n_par, PEER_TOPK, LANES), F32)],
        compiler_params=_cparams(("parallel", "parallel"), 56),
        name="peer_topk",
    )(ht, wq_t, k1, k2)


def _peer_dense_kernel(ht_ref, u_ref, vt_ref, cnt_ref, e1_ref, rk_ref, e2_ref, o_ref, wz_scr, *, n_sub):
    e = pl.program_id(2)

    @pl.when(e == 0)
    def _():
        o_ref[...] = jnp.zeros(o_ref.shape, F32)

    s = jnp.dot(u_ref[...], ht_ref[0], preferred_element_type=F32)
    for a in range(n_sub):
        rows = slice(a * PEER_NKEYS, (a + 1) * PEER_NKEYS)
        w = None
        for h in range(PEER_HEADS):
            cnt = cnt_ref[0, h, a:a + 1, :].astype(BF16)
            e1 = e1_ref[0, h, a:a + 1, :].astype(BF16)
            term = jnp.where(rk_ref[0, h] < cnt, e2_ref[0, h] * e1, 0.0)
            w = term if w is None else w + term
        wz_scr[rows, :] = w * _gelu(s[rows, :]).astype(BF16)
    o_ref[0] += jnp.dot(vt_ref[...], wz_scr[...], preferred_element_type=F32)


def _peer_dense_call(ht, layer, u_bf16, vt_bf16, cnt, e1n, rank2, e2):
    bsz, d, l = ht.shape
    n_exp = u_bf16.shape[1]
    tm = _tile(l, 768, LANES)
    n_sub = SUBLANES
    te = n_sub * PEER_NKEYS
    kern = functools.partial(_peer_dense_kernel, n_sub=n_sub)
    aspec = pl.BlockSpec((1, PEER_HEADS, n_sub, tm), lambda b, i, e: (b, 0, e, i))
    fspec = pl.BlockSpec((1, PEER_HEADS, PEER_NKEYS, tm), lambda b, i, e: (b, 0, 0, i))
    return pl.pallas_call(
        kern,
        grid=(bsz, l // tm, n_exp // te),
        in_specs=[pl.BlockSpec((1, d, tm), lambda b, i, e: (b, 0, i)),
                  pl.BlockSpec((None, te, d), lambda b, i, e: (layer, e, 0)),
                  pl.BlockSpec((None, d, te), lambda b, i, e: (layer, 0, e)),
                  aspec, aspec, fspec, fspec],
        out_specs=pl.BlockSpec((1, d, tm), lambda b, i, e: (b, 0, i)),
        out_shape=jax.ShapeDtypeStruct((bsz, d, l), F32),
        scratch_shapes=[pltpu.VMEM((te, tm), BF16)],
        compiler_params=_cparams(("parallel", "parallel", "arbitrary"), 58),
        name="peer_dense",
    )(ht, u_bf16, vt_bf16, cnt, e1n, rank2, e2)


def _resid_kernel(x_ref, yt_ref, gate_ref, o_ref, *, tm, ctx_len):
    b = pl.program_id(0)
    i = pl.program_id(1)
    row = i * tm + lax.broadcasted_iota(jnp.int32, (tm, 1), 0)
    o_ref[0] = x_ref[0] + _row_select(gate_ref, b, row < ctx_len) * yt_ref[0].T


def _resid_call(x, yt, mod, layer, ctx_len):
    bsz, l, d = x.shape
    tm = _tile(l, 384, LANES)
    kern = functools.partial(_resid_kernel, tm=tm, ctx_len=ctx_len)
    return pl.pallas_call(
        kern,
        grid=(bsz, l // tm),
        in_specs=[pl.BlockSpec((1, tm, d), lambda b, i: (b, i, 0)),
                  pl.BlockSpec((1, d, tm), lambda b, i: (b, 0, i)),
                  pl.BlockSpec((None, 8, d), lambda b, i: (layer, 0, 5))],
        out_specs=pl.BlockSpec((1, tm, d), lambda b, i: (b, i, 0)),
        out_shape=jax.ShapeDtypeStruct((bsz, l, d), F32),
        compiler_params=_cparams(("parallel", "parallel"), 40),
        name="resid",
    )(x, yt, mod)


def _rope_tables(ctx_len, t_lat):
    rows = t_lat // GRID_W
    row = jnp.repeat(jnp.arange(rows, dtype=F32), GRID_W)
    col = jnp.tile(jnp.arange(GRID_W, dtype=F32), rows)
    inv = ROPE_THETA ** (-jnp.arange(ROPE_PAIRS, dtype=F32) / ROPE_PAIRS)
    ang = jnp.concatenate([row[:, None] * inv] * 2 + [col[:, None] * inv] * 2, axis=1)
    ang = jnp.concatenate([jnp.zeros((ctx_len, HEAD_DIM), F32), ang], axis=0)
    sign = jnp.where((jnp.arange(HEAD_DIM) & ROPE_PAIRS) == 0, -1.0, 1.0).astype(F32)
    return jnp.cos(ang), jnp.sin(ang) * sign


def kernel(x, c, ctx, c_ctx, w_mod, b_mod, norm_mix, norm_ffn, w_in, q_norm, k_norm, lru_conv_w, lru_conv_b, lru_wa, lru_ba, lru_wx, lru_bx, lru_lambda, sc_conv_w, sc_conv_b, w_o_attn, w_o_lru, w_o_sc, w_out, peer_wq, peer_k1, peer_k2, peer_u, peer_v):
    bsz, t_lat, d = x.shape
    ctx_len = ctx.shape[1]
    depth = w_mod.shape[0]
    assert bsz == 2, "modulation rows are laid out as [latent 0, latent 1, context]"

    xs = jnp.concatenate([ctx, x], axis=1)
    s8 = jnp.concatenate([c, c_ctx[None, :], jnp.zeros((8 - bsz - 1, d), F32)], axis=0)
    mod = _mod_call(s8, w_mod, b_mod)
    cos, sin_signed = _rope_tables(ctx_len, t_lat)

    w_in_b = w_in.astype(BF16)
    w_att_b, w_lru_b, w_sc_b = w_o_attn.astype(BF16), w_o_lru.astype(BF16), w_o_sc.astype(BF16)
    w_out_b = w_out.astype(BF16)
    wq_t = jnp.swapaxes(peer_wq, 1, 2).astype(BF16)
    k1_b, k2_b = peer_k1.astype(BF16), peer_k2.astype(BF16)
    u_b = peer_u.astype(BF16)
    vt_b = jnp.swapaxes(peer_v, 1, 2).astype(BF16)

    for l in range(depth):
        p = _inproj_call(xs, mod, l, norm_mix[l], w_in_b, ctx_len)
        qn, ke, ve = _qkprep_call(p, cos, sin_signed, q_norm[l], k_norm[l])
        x_att = _attn_call(qn, ke, ve, q_norm[l], k_norm[l], ctx_len)
        lru_args = (lru_conv_w[l], lru_conv_b[l])
        h_fwd = _lru_call(p, *lru_args, lru_wa[l, 0], lru_ba[l, 0], lru_wx[l, 0], lru_bx[l, 0],
                          lru_lambda[l, 0], ctx_len, reverse=False)
        x_lru = _lru_call(p, *lru_args, lru_wa[l, 1], lru_ba[l, 1], lru_wx[l, 1], lru_bx[l, 1],
                          lru_lambda[l, 1], ctx_len, reverse=True, h_fwd=h_fwd)
        x_sc = _sconv_call(p, sc_conv_w[l], sc_conv_b[l], ctx_len)
        y = _merge_call(x_att, x_lru, x_sc, p, l, w_att_b, w_lru_b, w_sc_b)
        xs, ht = _outproj_call(y, w_out_b, xs, mod, l, norm_ffn[l], ctx_len)
        cnt, e1n, rank2, e2 = _peer_topk_call(ht, l, wq_t, k1_b, k2_b)
        yt = _peer_dense_call(ht, l, u_b, vt_b, cnt, e1n, rank2, e2)
        xs = _resid_call(xs, yt, mod, l, ctx_len)
    return xs[:, ctx_len:, :]
```

```python
import functools
import math

import numpy as np
import jax
import jax.numpy as jnp
from jax import lax
from jax.experimental import pallas as pl
from jax.experimental.pallas import tpu as pltpu

F32 = jnp.float32
BF16 = jnp.bfloat16

GRID_W = 64
EPS = 1e-6
N_MOD = 6

N_HEADS = 16
N_KV_HEADS = 4
HEAD_DIM = 128
GROUP = N_HEADS // N_KV_HEADS
ROPE_PAIRS = HEAD_DIM // 4
ROPE_THETA = 10000.0

LRU_BLOCKS = 16
LRU_BLOCK = 128
LRU_CONV = 4
LRU_C = 8.0
SC_CONV = 3

PEER_HEADS = 8
PEER_NKEYS = 128
PEER_TOPK = 16

LANES = 128
SUBLANES = 8
HALO = SUBLANES
NEG_INF = float("-inf")
LOG2E = 1.4426950408889634
GELU_C = math.sqrt(2.0 / math.pi)
Q_SCALE = HEAD_DIM ** -0.5 * LOG2E
SHIFT_MARGIN = 1.02
MIN_DENOM = 2.0 ** -100


def _tile(n, target, mult):
    best = None
    for t in range(mult, min(n, target) + 1, mult):
        if n % t == 0:
            best = t
    assert best is not None, (n, target, mult)
    return best


def _cparams(sem, vmem_mib):
    return pltpu.CompilerParams(dimension_semantics=sem, vmem_limit_bytes=vmem_mib << 20)


def _gelu(x):
    return 0.5 * x * (1.0 + jnp.tanh(GELU_C * (x + 0.044715 * (x * x * x))))


def _row_select(mod_ref, b, is_ctx):
    return jnp.where(is_ctx, mod_ref[2:3, :], mod_ref[pl.ds(b, 1), :])


def _mod_kernel(s_ref, w_ref, b_ref, o_ref):
    s = s_ref[...]
    s = s * jax.nn.sigmoid(s)
    o_ref[0] = jnp.dot(s, w_ref[0], preferred_element_type=F32,
                       precision=lax.Precision.HIGHEST) + b_ref[0]


def _mod_call(s8, w_mod, b_mod):
    depth, d, n = w_mod.shape
    tn = _tile(n, 1024, LANES)
    return pl.pallas_call(
        _mod_kernel,
        grid=(depth, n // tn),
        in_specs=[pl.BlockSpec((8, d), lambda l, j: (0, 0)),
                  pl.BlockSpec((1, d, tn), lambda l, j: (l, 0, j)),
                  pl.BlockSpec((1, 1, tn), lambda l, j: (l, 0, j))],
        out_specs=pl.BlockSpec((1, 8, tn), lambda l, j: (l, 0, j)),
        out_shape=jax.ShapeDtypeStruct((depth, 8, n), F32),
        compiler_params=_cparams(("parallel", "parallel"), 32),
        name="mod",
    )(s8, w_mod, b_mod.reshape(depth, 1, n))


def _inproj_kernel(x_ref, shift_ref, scale_ref, g_ref, w_ref, o_ref, h_scr, *, tm, ctx_len):
    b = pl.program_id(0)
    i = pl.program_id(1)

    @pl.when(pl.program_id(2) == 0)
    def _():
        x = x_ref[0]
        var = jnp.mean(x * x, axis=-1, keepdims=True)
        y = x * lax.rsqrt(var + EPS) * g_ref[...]
        row = i * tm + lax.broadcasted_iota(jnp.int32, (tm, 1), 0)
        is_ctx = row < ctx_len
        sh = _row_select(shift_ref, b, is_ctx)
        sc = _row_select(scale_ref, b, is_ctx)
        h_scr[...] = (y * (1.0 + sc) + sh).astype(BF16)

    o_ref[0] = jnp.dot(h_scr[...], w_ref[...], preferred_element_type=F32).astype(BF16)


def _inproj_call(x, mod, layer, gain, w_bf16, ctx_len):
    bsz, l, d = x.shape
    n = w_bf16.shape[2]
    tm = _tile(l, 704, 16)
    tn = 1024
    kern = functools.partial(_inproj_kernel, tm=tm, ctx_len=ctx_len)
    return pl.pallas_call(
        kern,
        grid=(bsz, l // tm, n // tn),
        in_specs=[pl.BlockSpec((1, tm, d), lambda b, i, j: (b, i, 0)),
                  pl.BlockSpec((None, 8, d), lambda b, i, j: (layer, 0, 0)),
                  pl.BlockSpec((None, 8, d), lambda b, i, j: (layer, 0, 1)),
                  pl.BlockSpec((1, d), lambda b, i, j: (0, 0)),
                  pl.BlockSpec((None, d, tn), lambda b, i, j: (layer, 0, j))],
        out_specs=pl.BlockSpec((1, tm, tn), lambda b, i, j: (b, i, j)),
        out_shape=jax.ShapeDtypeStruct((bsz, l, n), BF16),
        scratch_shapes=[pltpu.VMEM((tm, d), BF16)],
        compiler_params=_cparams(("parallel", "parallel", "arbitrary"), 48),
        name="inproj",
    )(x, mod, mod, gain.reshape(1, d), w_bf16)


def _norm_rope(t, gain, cos, sin_signed, lane_lo):
    var = jnp.mean(t * t, axis=-1, keepdims=True)
    y = t * lax.rsqrt(var + EPS) * gain
    swapped = jnp.where(lane_lo, pltpu.roll(y, HEAD_DIM - ROPE_PAIRS, 1), pltpu.roll(y, ROPE_PAIRS, 1))
    return y * cos + swapped * sin_signed


def _qkprep_kernel(q_ref, k_ref, v_ref, cos_ref, sin_ref, qg_ref, kg_ref, qo_ref, ko_ref, vo_ref):
    cos = cos_ref[...]
    sin = sin_ref[...]
    lane = lax.broadcasted_iota(jnp.int32, cos.shape, 1)
    lane_lo = (lane & ROPE_PAIRS) == 0
    qg = qg_ref[...]
    kg = kg_ref[...]
    for h in range(N_HEADS):
        sl = slice(h * HEAD_DIM, (h + 1) * HEAD_DIM)
        t = q_ref[0, :, sl].astype(F32)
        qo_ref[0, :, sl] = (_norm_rope(t, qg, cos, sin, lane_lo) * Q_SCALE).astype(BF16)
    k_tail = jnp.where(lane == 0, 1.0, 0.0).astype(BF16)
    v_tail = jnp.ones(cos.shape, BF16)
    for h in range(N_KV_HEADS):
        sl = slice(h * HEAD_DIM, (h + 1) * HEAD_DIM)
        lo = slice(2 * h * HEAD_DIM, (2 * h + 1) * HEAD_DIM)
        hi = slice((2 * h + 1) * HEAD_DIM, (2 * h + 2) * HEAD_DIM)
        t = k_ref[0, :, sl].astype(F32)
        ko_ref[0, :, lo] = _norm_rope(t, kg, cos, sin, lane_lo).astype(BF16)
        ko_ref[0, :, hi] = k_tail
        vo_ref[0, :, lo] = v_ref[0, :, sl]
        vo_ref[0, :, hi] = v_tail


def _qkprep_call(p, cos, sin_signed, q_gain, k_gain):
    bsz, l, _ = p.shape
    qw = N_HEADS * HEAD_DIM
    kw = N_KV_HEADS * HEAD_DIM
    tm = _tile(l, 768, 16)
    return pl.pallas_call(
        _qkprep_kernel,
        grid=(bsz, l // tm),
        in_specs=[pl.BlockSpec((1, tm, qw), lambda b, i: (b, i, 0)),
                  pl.BlockSpec((1, tm, kw), lambda b, i: (b, i, qw // kw)),
                  pl.BlockSpec((1, tm, kw), lambda b, i: (b, i, qw // kw + 1)),
                  pl.BlockSpec((tm, HEAD_DIM), lambda b, i: (i, 0)),
                  pl.BlockSpec((tm, HEAD_DIM), lambda b, i: (i, 0)),
                  pl.BlockSpec((1, HEAD_DIM), lambda b, i: (0, 0)),
                  pl.BlockSpec((1, HEAD_DIM), lambda b, i: (0, 0))],
        out_specs=[pl.BlockSpec((1, tm, qw), lambda b, i: (b, i, 0)),
                   pl.BlockSpec((1, tm, 2 * kw), lambda b, i: (b, i, 0)),
                   pl.BlockSpec((1, tm, 2 * kw), lambda b, i: (b, i, 0))],
        out_shape=[jax.ShapeDtypeStruct((bsz, l, qw), BF16),
                   jax.ShapeDtypeStruct((bsz, l, 2 * kw), BF16),
                   jax.ShapeDtypeStruct((bsz, l, 2 * kw), BF16)],
        compiler_params=_cparams(("parallel", "parallel"), 40),
        name="qkprep",
    )(p, p, p, cos, sin_signed, q_gain.reshape(1, HEAD_DIM), k_gain.reshape(1, HEAD_DIM))


def _attn_kernel(q_ref, k_ref, v_ref, qg_ref, kg_ref, o_ref, qs_scr, acc_scr, sa_scr, sb_scr, m_scr, l_scr,
                 *, tq, tkc, ctx_len, n_lat_chunks):
    qi = pl.program_id(2)
    q = q_ref[0]
    gq = jnp.max(jnp.abs(qg_ref[...]), axis=-1, keepdims=True)
    gk = jnp.max(jnp.abs(kg_ref[...]), axis=-1, keepdims=True)
    bound = (SHIFT_MARGIN * HEAD_DIM * Q_SCALE) * gq * gk
    lane = lax.broadcasted_iota(jnp.int32, (1, HEAD_DIM), 1)
    tail = jnp.where(lane == 0, -bound, 0.0).astype(BF16)
    for h in range(GROUP):
        qs_scr[h * tq:(h + 1) * tq, :HEAD_DIM] = q[:, h * HEAD_DIM:(h + 1) * HEAD_DIM]
        qs_scr[h * tq:(h + 1) * tq, HEAD_DIM:] = jnp.broadcast_to(tail, (tq, HEAD_DIM))
    n = jnp.where(qi < ctx_len // tq, 0, n_lat_chunks)

    def lat_start(c):
        return pl.multiple_of(ctx_len + c * tkc, math.gcd(ctx_len, tkc))

    def write(out):
        for h in range(GROUP):
            o_ref[0, :, h * HEAD_DIM:(h + 1) * HEAD_DIM] = out[h * tq:(h + 1) * tq, :].astype(BF16)

    def scores(start, size):
        return lax.dot_general(qs_scr[...], k_ref[0, pl.ds(start, size), :], (((1,), (1,)), ((), ())),
                               preferred_element_type=F32)

    def accumulate(s, start, size):
        p = jnp.exp2(s).astype(BF16)
        acc_scr[...] += jnp.dot(p, v_ref[0, pl.ds(start, size), :], preferred_element_type=F32)

    acc_scr[...] = jnp.zeros(acc_scr.shape, F32)
    sa_scr[...] = scores(lat_start(0), tkc)
    accumulate(scores(0, ctx_len), 0, ctx_len)

    def chunk_pair(j, look_ahead):
        c0 = 2 * j
        sb_scr[...] = scores(lat_start(c0 + 1), tkc)
        accumulate(sa_scr[...], lat_start(c0), tkc)
        if look_ahead:
            sa_scr[...] = scores(lat_start(c0 + 2), tkc)
        accumulate(sb_scr[...], lat_start(c0 + 1), tkc)

    def shifted_body(j, carry):
        chunk_pair(j, look_ahead=True)
        return carry

    lax.fori_loop(0, n // 2 - 1, shifted_body, 0)

    @pl.when(n > 0)
    def _():
        chunk_pair(n // 2 - 1, look_ahead=False)

    acc = acc_scr[...]
    den = acc[:, HEAD_DIM:]
    write(acc[:, :HEAD_DIM] / den)

    @pl.when(jnp.logical_not(jnp.min(den) >= MIN_DENOM))
    def _():
        m_scr[...] = jnp.full(m_scr.shape, NEG_INF, F32)
        l_scr[...] = jnp.zeros(l_scr.shape, F32)
        acc_scr[...] = jnp.zeros(acc_scr.shape, F32)

        def online_chunk(start, size):
            k = k_ref[0, pl.ds(start, size), :HEAD_DIM]
            v = v_ref[0, pl.ds(start, size), :HEAD_DIM]
            s = lax.dot_general(qs_scr[:, :HEAD_DIM], k, (((1,), (1,)), ((), ())),
                                preferred_element_type=F32)
            m_old = m_scr[...]
            m_new = jnp.maximum(m_old, jnp.max(s, axis=-1, keepdims=True))
            alpha = jnp.exp2(m_old - m_new)
            p = jnp.exp2(s - m_new)
            l_scr[...] = alpha * l_scr[...] + jnp.sum(p, axis=-1, keepdims=True)
            acc_scr[:, :HEAD_DIM] = alpha * acc_scr[:, :HEAD_DIM] + jnp.dot(
                p.astype(BF16), v, preferred_element_type=F32)
            m_scr[...] = m_new

        online_chunk(0, ctx_len)

        def online_body(c, carry):
            online_chunk(lat_start(c), tkc)
            return carry

        lax.fori_loop(0, n, online_body, 0)
        write(acc_scr[:, :HEAD_DIM] / l_scr[...])


def _attn_call(qn, ke, ve, q_gain, k_gain, ctx_len):
    bsz, l, qw = qn.shape
    tq = 256
    assert ctx_len % tq == 0 and l % tq == 0
    t_lat = l - ctx_len
    tkc = _tile(t_lat // 2, 1024, 256)
    gw = GROUP * HEAD_DIM
    ew = 2 * HEAD_DIM
    kern = functools.partial(_attn_kernel, tq=tq, tkc=tkc, ctx_len=ctx_len, n_lat_chunks=t_lat // tkc)
    return pl.pallas_call(
        kern,
        grid=(bsz, N_KV_HEADS, l // tq),
        in_specs=[pl.BlockSpec((1, tq, gw), lambda b, g, i: (b, i, g)),
                  pl.BlockSpec((1, l, ew), lambda b, g, i: (b, 0, g)),
                  pl.BlockSpec((1, l, ew), lambda b, g, i: (b, 0, g)),
                  pl.BlockSpec((1, HEAD_DIM), lambda b, g, i: (0, 0)),
                  pl.BlockSpec((1, HEAD_DIM), lambda b, g, i: (0, 0))],
        out_specs=pl.BlockSpec((1, tq, gw), lambda b, g, i: (b, i, g)),
        out_shape=jax.ShapeDtypeStruct((bsz, l, qw), BF16),
        scratch_shapes=[pltpu.VMEM((GROUP * tq, ew), BF16),
                        pltpu.VMEM((GROUP * tq, ew), F32),
                        pltpu.VMEM((GROUP * tq, tkc), F32),
                        pltpu.VMEM((GROUP * tq, tkc), F32),
                        pltpu.VMEM((GROUP * tq, 1), F32),
                        pltpu.VMEM((GROUP * tq, 1), F32)],
        compiler_params=_cparams(("parallel", "parallel", "arbitrary"), 48),
        name="attn",
    )(qn, ke, ve, q_gain.reshape(1, HEAD_DIM), k_gain.reshape(1, HEAD_DIM))


def _segment_bounds(row, ctx_len, seq_len):
    is_ctx = row < ctx_len
    first = jnp.where(is_ctx, 0, ctx_len)
    last = jnp.where(is_ctx, ctx_len - 1, seq_len - 1)
    return first, last


def _shift_down(x, prev, k, local):
    y = pltpu.roll(x, k, 0)
    for r in range(k):
        y = jnp.where(local == r, prev[HALO - k + r:HALO - k + r + 1, :], y)
    return y


def _shift_up(x, nxt, k, local, tt):
    y = pltpu.roll(x, tt - k, 0)
    for r in range(k):
        y = jnp.where(local == tt - k + r, nxt[r:r + 1, :], y)
    return y


def _halo_shifts(x, prev, nxt):
    tt = x.shape[0]
    sub = lax.broadcasted_iota(jnp.int32, (SUBLANES, x.shape[1]), 0)

    def down(k):
        y = pltpu.roll(x, k, 0)
        head = y[:SUBLANES]
        for r in range(k):
            head = jnp.where(sub == r, prev[HALO - k + r:HALO - k + r + 1, :], head)
        return jnp.concatenate([head, y[SUBLANES:]], axis=0)

    def up(k):
        y = pltpu.roll(x, tt - k, 0)
        tail = y[tt - SUBLANES:]
        for r in range(k):
            tail = jnp.where(sub == SUBLANES - k + r, nxt[r:r + 1, :], tail)
        return jnp.concatenate([y[:tt - SUBLANES], tail], axis=0)

    return down, up


def _scan_tile(a, d, carry, reverse):
    n_groups = a.shape[0] // SUBLANES
    sub = lax.broadcasted_iota(jnp.int32, (SUBLANES, a.shape[1]), 0)
    groups = []
    for v in range(n_groups):
        av = a[v * SUBLANES:(v + 1) * SUBLANES]
        dv = d[v * SUBLANES:(v + 1) * SUBLANES]
        for k in (1, 2, 4):
            keep = sub < SUBLANES - k if reverse else sub >= k
            shift = SUBLANES - k if reverse else k
            a_n = jnp.where(keep, pltpu.roll(av, shift, 0), 1.0)
            d_n = jnp.where(keep, pltpu.roll(dv, shift, 0), 0.0)
            dv = av * d_n + dv
            av = av * a_n
        groups.append((av, dv))
    hs = [None] * n_groups
    for v in (reversed(range(n_groups)) if reverse else range(n_groups)):
        av, dv = groups[v]
        hv = dv + av * carry
        carry = hv[0:1] if reverse else hv[SUBLANES - 1:SUBLANES]
        hs[v] = hv
    return jnp.concatenate(hs, axis=0), carry


def _lru_kernel(*refs, reverse, tt, ctx_len, n_sub):
    if reverse:
        (x_ref, xp_ref, xn_ref, cw_ref, cb_ref, wa_ref, ba_ref, wx_ref, bx_ref, lam_ref,
         hf_ref, g_ref, o_ref, carry_scr) = refs
    else:
        (x_ref, xp_ref, xn_ref, cw_ref, cb_ref, wa_ref, ba_ref, wx_ref, bx_ref, lam_ref,
         o_ref, carry_scr) = refs
    s = pl.program_id(2)
    nt = pl.num_programs(2)
    ti = jnp.where(s == 0, 0, nt - s) if reverse else s

    @pl.when(s == 0)
    def _():
        carry_scr[...] = jnp.zeros(carry_scr.shape, F32)

    prev_ok = jnp.logical_and(ti != 0, ti * tt != ctx_len)
    next_ok = jnp.logical_and(ti != nt - 1, (ti + 1) * tt != ctx_len)
    x = x_ref[0].astype(F32)
    xp = jnp.where(prev_ok, xp_ref[0].astype(F32), 0.0)
    xn = jnp.where(next_ok, xn_ref[0].astype(F32), 0.0)
    down, up = _halo_shifts(x, xp, xn)
    cw = cw_ref[...]
    u = cb_ref[...] + cw[2:3, :] * x + cw[0:1, :] * down(2) + cw[1:2, :] * down(1) + cw[3:4, :] * up(1)
    nlam = -lam_ref[...]
    softplus = jnp.maximum(nlam, 0.0) + jnp.log1p(jnp.exp(-jnp.abs(nlam)))

    for j in range(n_sub):
        ls = slice(j * LRU_BLOCK, (j + 1) * LRU_BLOCK)
        uj = u[:, ls]
        ub = uj.astype(BF16)
        r = jax.nn.sigmoid(jnp.dot(ub, wa_ref[j], preferred_element_type=F32) + ba_ref[:, ls])
        i = jax.nn.sigmoid(jnp.dot(ub, wx_ref[j], preferred_element_type=F32) + bx_ref[:, ls])
        a = jnp.exp((-LRU_C) * r * softplus[:, ls])
        d = jnp.sqrt(1.0 - a * a) * (i * uj)
        h, carry = _scan_tile(a, d, carry_scr[:, ls], reverse)
        carry_scr[:, ls] = carry
        if reverse:
            o_ref[0, :, ls] = ((hf_ref[0, :, ls] + h) * _gelu(g_ref[0, :, ls].astype(F32))).astype(BF16)
        else:
            o_ref[0, :, ls] = h


def _lru_call(p, conv_w, conv_b, wa, ba, wx, bx, lam, ctx_len, reverse, h_fwd=None):
    bsz, l, _ = p.shape
    c = conv_w.shape[1]
    tt = 256
    assert ctx_len % tt == 0 and l % tt == 0
    nt = l // tt
    n_sub = 4
    cw = n_sub * LRU_BLOCK
    nb = c // cw
    x_col0 = (N_HEADS + 2 * N_KV_HEADS) * HEAD_DIM // cw
    g_col0 = x_col0 + nb
    hb = tt // HALO
    n_hblk = l // HALO

    def tile_of(s):
        return jnp.where(s == 0, 0, nt - s) if reverse else s

    x_map = lambda b, cb, s: (b, tile_of(s), x_col0 + cb)
    prev_map = lambda b, cb, s: (b, jnp.maximum(tile_of(s) * hb - 1, 0), x_col0 + cb)
    next_map = lambda b, cb, s: (b, jnp.minimum((tile_of(s) + 1) * hb, n_hblk - 1), x_col0 + cb)
    vec_map = lambda b, cb, s: (0, cb)
    w_spec = pl.BlockSpec((n_sub, LRU_BLOCK, LRU_BLOCK), lambda b, cb, s: (cb, 0, 0))
    in_specs = [pl.BlockSpec((1, tt, cw), x_map),
                pl.BlockSpec((1, HALO, cw), prev_map),
                pl.BlockSpec((1, HALO, cw), next_map),
                pl.BlockSpec((LRU_CONV, cw), vec_map),
                pl.BlockSpec((1, cw), vec_map),
                w_spec,
                pl.BlockSpec((1, cw), vec_map),
                w_spec,
                pl.BlockSpec((1, cw), vec_map),
                pl.BlockSpec((1, cw), vec_map)]
    args = [p, p, p, conv_w, conv_b.reshape(1, c), wa.astype(BF16), ba.reshape(1, c),
            wx.astype(BF16), bx.reshape(1, c), lam.reshape(1, c)]
    out_map = lambda b, cb, s: (b, tile_of(s), cb)
    if reverse:
        in_specs += [pl.BlockSpec((1, tt, cw), out_map),
                     pl.BlockSpec((1, tt, cw), lambda b, cb, s: (b, tile_of(s), g_col0 + cb))]
        args += [h_fwd, p]
        out_dtype = BF16
    else:
        out_dtype = F32
    kern = functools.partial(_lru_kernel, reverse=reverse, tt=tt, ctx_len=ctx_len, n_sub=n_sub)
    return pl.pallas_call(
        kern,
        grid=(bsz, nb, nt),
        in_specs=in_specs,
        out_specs=pl.BlockSpec((1, tt, cw), out_map),
        out_shape=jax.ShapeDtypeStruct((bsz, l, c), out_dtype),
        scratch_shapes=[pltpu.VMEM((1, cw), F32)],
        compiler_params=_cparams(("parallel", "parallel", "arbitrary"), 32),
        name="lru_rev" if reverse else "lru_fwd",
    )(*args)


def _sconv_kernel(bg_ref, cg_ref, u_ref, cgp_ref, up_ref, cgn_ref, un_ref, w_ref, b_ref, o_ref,
                  *, tt, ctx_len, seq_len):
    ti = pl.program_id(1)
    z = cg_ref[0].astype(F32) * u_ref[0].astype(F32)
    zp = cgp_ref[0].astype(F32) * up_ref[0].astype(F32)
    zn = cgn_ref[0].astype(F32) * un_ref[0].astype(F32)
    local = lax.broadcasted_iota(jnp.int32, (tt, 1), 0)
    row = ti * tt + local
    first, last = _segment_bounds(row, ctx_len, seq_len)
    w = w_ref[...]
    y = b_ref[...] + w[1:2, :] * z
    y = y + w[0:1, :] * jnp.where(row - 1 >= first, _shift_down(z, zp, 1, local), 0.0)
    y = y + w[2:3, :] * jnp.where(row + 1 <= last, _shift_up(z, zn, 1, local, tt), 0.0)
    o_ref[0] = (bg_ref[0].astype(F32) * y).astype(BF16)


def _sconv_call(p, w, b, ctx_len):
    bsz, l, _ = p.shape
    c = w.shape[1]
    tt = _tile(l, 768, 16)
    tc = 512
    ncb = c // tc
    b_col0 = (N_HEADS * HEAD_DIM + 2 * N_KV_HEADS * HEAD_DIM + 2 * c) // tc
    c_col0 = b_col0 + ncb
    u_col0 = c_col0 + ncb
    hb = tt // HALO
    n_hblk = l // HALO
    prev = lambda i: jnp.maximum(i * hb - 1, 0)
    nxt = lambda i: jnp.minimum((i + 1) * hb, n_hblk - 1)
    kern = functools.partial(_sconv_kernel, tt=tt, ctx_len=ctx_len, seq_len=l)
    return pl.pallas_call(
        kern,
        grid=(bsz, l // tt, ncb),
        in_specs=[pl.BlockSpec((1, tt, tc), lambda b_, i, j: (b_, i, b_col0 + j)),
                  pl.BlockSpec((1, tt, tc), lambda b_, i, j: (b_, i, c_col0 + j)),
                  pl.BlockSpec((1, tt, tc), lambda b_, i, j: (b_, i, u_col0 + j)),
                  pl.BlockSpec((1, HALO, tc), lambda b_, i, j: (b_, prev(i), c_col0 + j)),
                  pl.BlockSpec((1, HALO, tc), lambda b_, i, j: (b_, prev(i), u_col0 + j)),
                  pl.BlockSpec((1, HALO, tc), lambda b_, i, j: (b_, nxt(i), c_col0 + j)),
                  pl.BlockSpec((1, HALO, tc), lambda b_, i, j: (b_, nxt(i), u_col0 + j)),
                  pl.BlockSpec((SC_CONV, tc), lambda b_, i, j: (0, j)),
                  pl.BlockSpec((1, tc), lambda b_, i, j: (0, j))],
        out_specs=pl.BlockSpec((1, tt, tc), lambda b_, i, j: (b_, i, j)),
        out_shape=jax.ShapeDtypeStruct((bsz, l, c), BF16),
        compiler_params=_cparams(("parallel", "parallel", "parallel"), 32),
        name="sconv",
    )(p, p, p, p, p, p, p, w, b.reshape(1, c))


def _merge_kernel(xa_ref, xl_ref, xs_ref, ga_ref, gl_ref, gs_ref, wa_ref, wl_ref, ws_ref, o_ref):
    y = jax.nn.sigmoid(ga_ref[0].astype(F32)) * jnp.dot(xa_ref[0], wa_ref[...], preferred_element_type=F32)
    y = y + jax.nn.sigmoid(gl_ref[0].astype(F32)) * jnp.dot(xl_ref[0], wl_ref[...], preferred_element_type=F32)
    y = y + jax.nn.sigmoid(gs_ref[0].astype(F32)) * jnp.dot(xs_ref[0], ws_ref[...], preferred_element_type=F32)
    o_ref[0] = y.astype(BF16)


def _merge_call(x_att, x_lru, x_sc, p, layer, w_att, w_lru, w_sc):
    bsz, l, d = x_att.shape
    tm = _tile(l, 768, 16)
    tn = 512
    g_col0 = (p.shape[2] - 3 * d) // tn
    nj = d // tn
    xspec = pl.BlockSpec((1, tm, d), lambda b, i, j: (b, i, 0))
    wspec = pl.BlockSpec((None, d, tn), lambda b, i, j: (layer, 0, j))
    gspec = lambda k: pl.BlockSpec((1, tm, tn), lambda b, i, j: (b, i, g_col0 + k * nj + j))
    return pl.pallas_call(
        _merge_kernel,
        grid=(bsz, l // tm, nj),
        in_specs=[xspec, xspec, xspec, gspec(0), gspec(1), gspec(2), wspec, wspec, wspec],
        out_specs=pl.BlockSpec((1, tm, tn), lambda b, i, j: (b, i, j)),
        out_shape=jax.ShapeDtypeStruct((bsz, l, d), BF16),
        compiler_params=_cparams(("parallel", "parallel", "arbitrary"), 48),
        name="merge",
    )(x_att, x_lru, x_sc, p, p, p, w_att, w_lru, w_sc)


def _outproj_kernel(y_ref, w_ref, x_ref, gate_ref, shift_ref, scale_ref, g_ref, xo_ref, ht_ref,
                    *, tm, ctx_len):
    b = pl.program_id(0)
    i = pl.program_id(1)
    row = i * tm + lax.broadcasted_iota(jnp.int32, (tm, 1), 0)
    is_ctx = row < ctx_len
    acc = jnp.dot(y_ref[0], w_ref[...], preferred_element_type=F32)
    xn = x_ref[0] + _row_select(gate_ref, b, is_ctx) * acc
    xo_ref[0] = xn
    var = jnp.mean(xn * xn, axis=-1, keepdims=True)
    h = xn * lax.rsqrt(var + EPS) * g_ref[...]
    h = h * (1.0 + _row_select(scale_ref, b, is_ctx)) + _row_select(shift_ref, b, is_ctx)
    ht_ref[0] = h.T.astype(BF16)


def _outproj_call(y, w_out, x, mod, layer, gain, ctx_len):
    bsz, l, d = x.shape
    tm = _tile(l, 384, LANES)
    kern = functools.partial(_outproj_kernel, tm=tm, ctx_len=ctx_len)
    mspec = lambda k: pl.BlockSpec((None, 8, d), lambda b, i: (layer, 0, k))
    return pl.pallas_call(
        kern,
        grid=(bsz, l // tm),
        in_specs=[pl.BlockSpec((1, tm, d), lambda b, i: (b, i, 0)),
                  pl.BlockSpec((None, d, d), lambda b, i: (layer, 0, 0)),
                  pl.BlockSpec((1, tm, d), lambda b, i: (b, i, 0)),
                  mspec(2), mspec(3), mspec(4),
                  pl.BlockSpec((1, d), lambda b, i: (0, 0))],
        out_specs=[pl.BlockSpec((1, tm, d), lambda b, i: (b, i, 0)),
                   pl.BlockSpec((1, d, tm), lambda b, i: (b, 0, i))],
        out_shape=[jax.ShapeDtypeStruct((bsz, l, d), F32),
                   jax.ShapeDtypeStruct((bsz, d, l), BF16)],
        compiler_params=_cparams(("parallel", "parallel"), 48),
        name="outproj",
    )(y, w_out, x, mod, mod, mod, gain.reshape(1, d))


_CAND_ROWS = tuple(PEER_TOPK // (i + 1) for i in range(PEER_TOPK))


def _top16(s, v_scr):
    n = s.shape[0]
    key = lax.broadcasted_iota(jnp.int32, s.shape, 0)

    def body(r, carry):
        work, rank = carry
        m = jnp.max(work, axis=0, keepdims=True)
        v_scr[pl.ds(r, 1), :] = m
        first = jnp.min(jnp.where(work == m, key, n), axis=0, keepdims=True)
        sel = key == first
        return jnp.where(sel, NEG_INF, work), jnp.where(sel, jnp.asarray(r, F32), rank)

    _, rank = lax.fori_loop(0, PEER_TOPK, body, (s, jnp.full(s.shape, float(PEER_TOPK), F32)))
    return rank


MARK = -(2.0 ** 127)


def _top16_distinct(ss, v_scrs):
    def body(r, works):
        mark = jnp.asarray(r, F32) * (MARK / 32.0) + MARK
        out = []
        for work, v_scr in zip(works, v_scrs):
            m = jnp.max(work, axis=0, keepdims=True)
            v_scr[pl.ds(r, 1), :] = m
            out.append(jnp.where(work == m, mark, work))
        return tuple(out)

    works = lax.fori_loop(0, PEER_TOPK, body, tuple(ss))
    return [jnp.where(w <= MARK, (MARK - w) * (-32.0 / MARK), float(PEER_TOPK)) for w in works]


def _peer_topk_kernel(ht_ref, wq_ref, k1_ref, k2_ref, cnt_ref, e1_ref, rk_ref, e2_ref,
                      q_scr, v_scr, *, tm, n_par):
    half = PEER_NKEYS
    q_scr[...] = jnp.dot(wq_ref[...], ht_ref[0], preferred_element_type=F32).astype(BF16)
    sub = lax.broadcasted_iota(jnp.int32, (SUBLANES, LANES), 0)
    big = PEER_TOPK * PEER_TOPK

    def candidates(v1, v2):
        pieces, poss = [], []
        for i in range(SUBLANES):
            for j0 in range(0, _CAND_ROWS[i], SUBLANES):
                c = v1[i:i + 1, :] + v2[j0:j0 + SUBLANES, :]
                valid = sub + j0 < _CAND_ROWS[i]
                pieces.append(jnp.where(valid, c, NEG_INF))
                poss.append(jnp.where(valid, i * PEER_TOPK + j0 + sub, big))
        pieces.append(v1[SUBLANES:, :] + v2[0:1, :])
        poss.append((sub + SUBLANES) * PEER_TOPK)
        return pieces, poss

    def pick_exact(pieces, poss):
        def pick(_, carry2):
            cs, sels = carry2
            m = functools.reduce(jnp.maximum, cs)
            m = jnp.max(m, axis=0, keepdims=True)
            cand_pos = functools.reduce(jnp.minimum, [jnp.where(c == m, p_, big) for c, p_ in zip(cs, poss)])
            first = jnp.min(cand_pos, axis=0, keepdims=True)
            hit = [p_ == first for p_ in poss]
            cs = tuple(jnp.where(hh, NEG_INF, c) for hh, c in zip(hit, cs))
            sels = tuple(jnp.where(hh, 1.0, s_) for hh, s_ in zip(hit, sels))
            return cs, sels

        zeros = tuple(jnp.zeros((SUBLANES, LANES), F32) for _ in pieces)
        return lax.fori_loop(0, PEER_TOPK, pick, (tuple(pieces), zeros))[1]

    def pick_distinct(groups):
        npc = len(groups[0])

        def pick(_, cs):
            out = []
            for g in range(len(groups)):
                grp = cs[g * npc:(g + 1) * npc]
                m = jnp.max(functools.reduce(jnp.maximum, grp), axis=0, keepdims=True)
                out.extend(jnp.where(c == m, MARK, c) for c in grp)
            return tuple(out)

        marked = lax.fori_loop(0, PEER_TOPK, pick, tuple(c for grp in groups for c in grp))
        return [tuple(jnp.where(c == MARK, 1.0, 0.0) for c in marked[g * npc:(g + 1) * npc])
                for g in range(len(groups))]

    def compute(h, lss, exact):
        base = pl.multiple_of(h * 2 * half, 2 * half)
        n_g = len(lss)
        s1s = [jnp.dot(k1_ref[h], q_scr[pl.ds(base, half), ls], preferred_element_type=F32) for ls in lss]
        s2s = [jnp.dot(k2_ref[h], q_scr[pl.ds(base + half, half), ls], preferred_element_type=F32)
               for ls in lss]
        v1_refs = [v_scr.at[2 * g] for g in range(n_g)]
        v2_refs = [v_scr.at[2 * g + 1] for g in range(n_g)]
        if exact:
            rank1s = [_top16(s, r) for s, r in zip(s1s, v1_refs)]
            rank2s = [_top16(s, r) for s, r in zip(s2s, v2_refs)]
        else:
            ranks = _top16_distinct(tuple(s1s + s2s), tuple(v1_refs + v2_refs))
            rank1s, rank2s = ranks[:n_g], ranks[n_g:]
        v1s = [r[...] for r in v1_refs]
        v2s = [r[...] for r in v2_refs]
        cands = [candidates(v1, v2) for v1, v2 in zip(v1s, v2s)]
        if exact:
            all_sels = [pick_exact(pieces, poss) for pieces, poss in cands]
        else:
            all_sels = pick_distinct([pieces for pieces, _ in cands])
        most = None
        for g in range(n_g):
            most_g = finish(h, lss[g], s1s[g], s2s[g], rank1s[g], rank2s[g], v1s[g], v2s[g],
                            cands[g][0], all_sels[g])
            most = most_g if most is None else jnp.maximum(most, most_g)
        return most

    def finish(h, ls, s1, s2, rank1, rank2, v1, v2, orig, sels):
        top = v1[0:1, :] + v2[0:1, :]
        zsum = functools.reduce(
            lambda a_, b_: a_ + b_,
            [jnp.where(s_ > 0.0, jnp.exp(o - top), 0.0) for s_, o in zip(sels, orig)])
        zinv = 1.0 / jnp.sum(zsum, axis=0, keepdims=True)

        counts = []
        pi = 0
        for i in range(SUBLANES):
            c = None
            for j0 in range(0, _CAND_ROWS[i], SUBLANES):
                part = jnp.sum(sels[pi], axis=0, keepdims=True)
                c = part if c is None else c + part
                pi += 1
            counts.append(c)
        tail = sels[pi]
        for i in range(SUBLANES, PEER_TOPK):
            counts.append(tail[i - SUBLANES:i - SUBLANES + 1, :])
        cnt = jnp.zeros(rank1.shape, F32)
        for i in range(PEER_TOPK):
            cnt = jnp.where(rank1 == float(i), counts[i], cnt)

        cnt_ref[0, h, :, ls] = cnt
        e1_ref[0, h, :, ls] = jnp.exp(s1 - v1[0:1, :]) * zinv
        rk_ref[0, h, :, ls] = rank2.astype(BF16)
        e2_ref[0, h, :, ls] = jnp.exp(s2 - v2[0:1, :]).astype(BF16)
        ranked = jnp.where(rank1 < PEER_TOPK, 1.0, 0.0)
        ranked = jnp.maximum(jnp.sum(ranked, axis=0, keepdims=True),
                             jnp.sum(jnp.where(rank2 < PEER_TOPK, 1.0, 0.0), axis=0, keepdims=True))
        picked = jnp.sum(functools.reduce(lambda a_, b_: a_ + b_, sels), axis=0, keepdims=True)
        return jnp.max(jnp.maximum(ranked, picked))

    n_trips = tm // (n_par * LANES)

    def head_lane_groups(idx, carry):
        h = idx // n_trips
        first = (idx % n_trips) * n_par
        lss = [pl.ds(pl.multiple_of((first + g) * LANES, LANES), LANES) for g in range(n_par)]
        most = compute(h, lss, exact=False)

        @pl.when(most > PEER_TOPK)
        def _():
            compute(h, lss, exact=True)

        return carry

    lax.fori_loop(0, PEER_HEADS * n_trips, head_lane_groups, 0)


def _peer_topk_call(ht, layer, wq_t, k1, k2):
    bsz, d, l = ht.shape
    qd = wq_t.shape[1]
    n_par = 2
    tm = _tile(l, 768, n_par * LANES)
    kern = functools.partial(_peer_topk_kernel, tm=tm, n_par=n_par)
    ospec = pl.BlockSpec((1, PEER_HEADS, PEER_NKEYS, tm), lambda b, i: (b, 0, 0, i))
    oshape = jax.ShapeDtypeStruct((bsz, PEER_HEADS, PEER_NKEYS, l), F32)
    kspec = pl.BlockSpec((None, PEER_HEADS, PEER_NKEYS, PEER_NKEYS), lambda b, i: (layer, 0, 0, 0))
    return pl.pallas_call(
        kern,
        grid=(bsz, l // tm),
        in_specs=[pl.BlockSpec((1, d, tm), lambda b, i: (b, 0, i)),
                  pl.BlockSpec((None, qd, d), lambda b, i: (layer, 0, 0)),
                  kspec, kspec],
        out_specs=[ospec, ospec, ospec, ospec],
        out_shape=[oshape, oshape, jax.ShapeDtypeStruct(oshape.shape, BF16),
                   jax.ShapeDtypeStruct(oshape.shape, BF16)],
        scratch_shapes=[pltpu.VMEM((qd, tm), BF16),
                        pltpu.VMEM((2 * n_par, PEER_TOPK, LANES), F32)],
        compiler_params=_cparams(("parallel", "parallel"), 56),
        name="peer_topk",
    )(ht, wq_t, k1, k2)


def _peer_dense_kernel(ht_ref, u_ref, vt_ref, cnt_ref, e1_ref, rk_ref, e2_ref, o_ref, wz_scr, *, n_sub):
    e = pl.program_id(2)

    @pl.when(e == 0)
    def _():
        o_ref[...] = jnp.zeros(o_ref.shape, F32)

    ht = ht_ref[0]
    n_split = 2
    per = n_sub // n_split
    for part in range(n_split):
        s = jnp.dot(u_ref[part * per * PEER_NKEYS:(part + 1) * per * PEER_NKEYS, :], ht,
                    preferred_element_type=F32)
        for a in range(part * per, (part + 1) * per):
            rows = slice(a * PEER_NKEYS, (a + 1) * PEER_NKEYS)
            local = slice((a - part * per) * PEER_NKEYS, (a - part * per + 1) * PEER_NKEYS)
            w = None
            for h in range(PEER_HEADS):
                cnt = cnt_ref[0, h, a:a + 1, :].astype(BF16)
                e1 = e1_ref[0, h, a:a + 1, :].astype(BF16)
                term = jnp.where(rk_ref[0, h] < cnt, e2_ref[0, h] * e1, 0.0)
                w = term if w is None else w + term
            wz_scr[rows, :] = w * _gelu(s[local, :]).astype(BF16)
    o_ref[0] += jnp.dot(vt_ref[...], wz_scr[...], preferred_element_type=F32)


def _peer_dense_call(ht, layer, u_bf16, vt_bf16, cnt, e1n, rank2, e2):
    bsz, d, l = ht.shape
    n_exp = u_bf16.shape[1]
    tm = _tile(l, 768, LANES)
    n_sub = SUBLANES
    te = n_sub * PEER_NKEYS
    kern = functools.partial(_peer_dense_kernel, n_sub=n_sub)
    aspec = pl.BlockSpec((1, PEER_HEADS, n_sub, tm), lambda b, i, e: (b, 0, e, i))
    fspec = pl.BlockSpec((1, PEER_HEADS, PEER_NKEYS, tm), lambda b, i, e: (b, 0, 0, i))
    return pl.pallas_call(
        kern,
        grid=(bsz, l // tm, n_exp // te),
        in_specs=[pl.BlockSpec((1, d, tm), lambda b, i, e: (b, 0, i)),
                  pl.BlockSpec((None, te, d), lambda b, i, e: (layer, e, 0)),
                  pl.BlockSpec((None, d, te), lambda b, i, e: (layer, 0, e)),
                  aspec, aspec, fspec, fspec],
        out_specs=pl.BlockSpec((1, d, tm), lambda b, i, e: (b, 0, i)),
        out_shape=jax.ShapeDtypeStruct((bsz, d, l), F32),
        scratch_shapes=[pltpu.VMEM((te, tm), BF16)],
        compiler_params=_cparams(("parallel", "parallel", "arbitrary"), 58),
        name="peer_dense",
    )(ht, u_bf16, vt_bf16, cnt, e1n, rank2, e2)


def _resid_kernel(x_ref, yt_ref, gate_ref, o_ref, *, tm, ctx_len, first_tile):
    b = pl.program_id(0)
    i = pl.program_id(1) + first_tile
    row = i * tm + lax.broadcasted_iota(jnp.int32, (tm, 1), 0)
    o_ref[0] = x_ref[0] + _row_select(gate_ref, b, row < ctx_len) * yt_ref[0].T


def _resid_call(x, yt, mod, layer, ctx_len, latent_only):
    bsz, l, d = x.shape
    tm = _tile(math.gcd(l, ctx_len), 384, LANES)
    first_tile = ctx_len // tm if latent_only else 0
    n_tiles = l // tm - first_tile
    kern = functools.partial(_resid_kernel, tm=tm, ctx_len=ctx_len, first_tile=first_tile)
    return pl.pallas_call(
        kern,
        grid=(bsz, n_tiles),
        in_specs=[pl.BlockSpec((1, tm, d), lambda b, i: (b, i + first_tile, 0)),
                  pl.BlockSpec((1, d, tm), lambda b, i: (b, 0, i + first_tile)),
                  pl.BlockSpec((None, 8, d), lambda b, i: (layer, 0, 5))],
        out_specs=pl.BlockSpec((1, tm, d), lambda b, i: (b, i, 0)),
        out_shape=jax.ShapeDtypeStruct((bsz, n_tiles * tm, d), F32),
        compiler_params=_cparams(("parallel", "parallel"), 40),
        name="resid",
    )(x, yt, mod)


def _rope_tables(ctx_len, t_lat):
    rows = t_lat // GRID_W
    row = jnp.repeat(jnp.arange(rows, dtype=F32), GRID_W)
    col = jnp.tile(jnp.arange(GRID_W, dtype=F32), rows)
    inv = ROPE_THETA ** (-jnp.arange(ROPE_PAIRS, dtype=F32) / ROPE_PAIRS)
    ang = jnp.concatenate([row[:, None] * inv] * 2 + [col[:, None] * inv] * 2, axis=1)
    ang = jnp.concatenate([jnp.zeros((ctx_len, HEAD_DIM), F32), ang], axis=0)
    sign = jnp.where((jnp.arange(HEAD_DIM) & ROPE_PAIRS) == 0, -1.0, 1.0).astype(F32)
    return jnp.cos(ang), jnp.sin(ang) * sign


def kernel(x, c, ctx, c_ctx, w_mod, b_mod, norm_mix, norm_ffn, w_in, q_norm, k_norm, lru_conv_w, lru_conv_b, lru_wa, lru_ba, lru_wx, lru_bx, lru_lambda, sc_conv_w, sc_conv_b, w_o_attn, w_o_lru, w_o_sc, w_out, peer_wq, peer_k1, peer_k2, peer_u, peer_v):
    bsz, t_lat, d = x.shape
    ctx_len = ctx.shape[1]
    depth = w_mod.shape[0]
    assert bsz == 2, "modulation rows are laid out as [latent 0, latent 1, context]"

    xs = jnp.concatenate([ctx, x], axis=1)
    s8 = jnp.concatenate([c, c_ctx[None, :], jnp.zeros((8 - bsz - 1, d), F32)], axis=0)
    mod = _mod_call(s8, w_mod, b_mod)
    cos, sin_signed = _rope_tables(ctx_len, t_lat)

    w_in_b = w_in.astype(BF16)
    w_att_b, w_lru_b, w_sc_b = w_o_attn.astype(BF16), w_o_lru.astype(BF16), w_o_sc.astype(BF16)
    w_out_b = w_out.astype(BF16)
    wq_t = jnp.swapaxes(peer_wq, 1, 2).astype(BF16)
    k1_b, k2_b = peer_k1.astype(BF16), peer_k2.astype(BF16)
    u_b = peer_u.astype(BF16)
    vt_b = jnp.swapaxes(peer_v, 1, 2).astype(BF16)

    for l in range(depth):
        p = _inproj_call(xs, mod, l, norm_mix[l], w_in_b, ctx_len)
        qn, ke, ve = _qkprep_call(p, cos, sin_signed, q_norm[l], k_norm[l])
        x_att = _attn_call(qn, ke, ve, q_norm[l], k_norm[l], ctx_len)
        lru_args = (lru_conv_w[l], lru_conv_b[l])
        h_fwd = _lru_call(p, *lru_args, lru_wa[l, 0], lru_ba[l, 0], lru_wx[l, 0], lru_bx[l, 0],
                          lru_lambda[l, 0], ctx_len, reverse=False)
        x_lru = _lru_call(p, *lru_args, lru_wa[l, 1], lru_ba[l, 1], lru_wx[l, 1], lru_bx[l, 1],
                          lru_lambda[l, 1], ctx_len, reverse=True, h_fwd=h_fwd)
        x_sc = _sconv_call(p, sc_conv_w[l], sc_conv_b[l], ctx_len)
        y = _merge_call(x_att, x_lru, x_sc, p, l, w_att_b, w_lru_b, w_sc_b)
        xs, ht = _outproj_call(y, w_out_b, xs, mod, l, norm_ffn[l], ctx_len)
        cnt, e1n, rank2, e2 = _peer_topk_call(ht, l, wq_t, k1_b, k2_b)
        yt = _peer_dense_call(ht, l, u_b, vt_b, cnt, e1n, rank2, e2)
        xs = _resid_call(xs, yt, mod, l, ctx_len, latent_only=(l == depth - 1))
    return xs
```

```python
import functools
import math

import numpy as np
import jax
import jax.numpy as jnp
from jax import lax
from jax.experimental import pallas as pl
from jax.experimental.pallas import tpu as pltpu

F32 = jnp.float32
BF16 = jnp.bfloat16

GRID_W = 64
EPS = 1e-6
N_MOD = 6

N_HEADS = 16
N_KV_HEADS = 4
HEAD_DIM = 128
GROUP = N_HEADS // N_KV_HEADS
ROPE_PAIRS = HEAD_DIM // 4
ROPE_THETA = 10000.0

LRU_BLOCKS = 16
LRU_BLOCK = 128
LRU_CONV = 4
LRU_C = 8.0
SC_CONV = 3

PEER_HEADS = 8
PEER_NKEYS = 128
PEER_TOPK = 16

LANES = 128
SUBLANES = 8
HALO = SUBLANES
NEG_INF = float("-inf")
LOG2E = 1.4426950408889634
GELU_C = math.sqrt(2.0 / math.pi)
Q_SCALE = HEAD_DIM ** -0.5 * LOG2E
SHIFT_MARGIN = 1.02
MIN_DENOM = 2.0 ** -100


def _tile(n, target, mult):
    best = None
    for t in range(mult, min(n, target) + 1, mult):
        if n % t == 0:
            best = t
    assert best is not None, (n, target, mult)
    return best


def _cparams(sem, vmem_mib):
    return pltpu.CompilerParams(dimension_semantics=sem, vmem_limit_bytes=vmem_mib << 20)


def _gelu(x):
    return 0.5 * x * (1.0 + jnp.tanh(GELU_C * (x + 0.044715 * (x * x * x))))


def _row_select(mod_ref, b, is_ctx):
    return jnp.where(is_ctx, mod_ref[2:3, :], mod_ref[pl.ds(b, 1), :])


def _mod_kernel(s_ref, w_ref, b_ref, o_ref):
    s = s_ref[...]
    s = s * jax.nn.sigmoid(s)
    o_ref[0] = jnp.dot(s, w_ref[0], preferred_element_type=F32,
                       precision=lax.Precision.HIGHEST) + b_ref[0]


def _mod_call(s8, w_mod, b_mod):
    depth, d, n = w_mod.shape
    tn = _tile(n, 1024, LANES)
    return pl.pallas_call(
        _mod_kernel,
        grid=(depth, n // tn),
        in_specs=[pl.BlockSpec((8, d), lambda l, j: (0, 0)),
                  pl.BlockSpec((1, d, tn), lambda l, j: (l, 0, j)),
                  pl.BlockSpec((1, 1, tn), lambda l, j: (l, 0, j))],
        out_specs=pl.BlockSpec((1, 8, tn), lambda l, j: (l, 0, j)),
        out_shape=jax.ShapeDtypeStruct((depth, 8, n), F32),
        compiler_params=_cparams(("parallel", "parallel"), 32),
        name="mod",
    )(s8, w_mod, b_mod.reshape(depth, 1, n))


def _inproj_kernel(x_ref, shift_ref, scale_ref, g_ref, w_ref, o_ref, h_scr, *, tm, rows, ctx_len):
    b = pl.program_id(0)
    i = pl.program_id(1)

    @pl.when(pl.program_id(2) == 0)
    def _():
        def norm_rows(c, carry):
            r0 = pl.multiple_of(c * rows, rows)
            x = x_ref[0, pl.ds(r0, rows), :]
            var = jnp.mean(x * x, axis=-1, keepdims=True)
            y = x * lax.rsqrt(var + EPS) * g_ref[...]
            row = i * tm + r0 + lax.broadcasted_iota(jnp.int32, (rows, 1), 0)
            is_ctx = row < ctx_len
            sh = _row_select(shift_ref, b, is_ctx)
            sc = _row_select(scale_ref, b, is_ctx)
            h_scr[pl.ds(r0, rows), :] = (y * (1.0 + sc) + sh).astype(BF16)
            return carry

        lax.fori_loop(0, tm // rows, norm_rows, 0)

    o_ref[0] = jnp.dot(h_scr[...], w_ref[...], preferred_element_type=F32).astype(BF16)


def _inproj_call(x, mod, layer, gain, w_bf16, ctx_len):
    bsz, l, d = x.shape
    n = w_bf16.shape[2]
    tm = _tile(l, 1056, 16)
    rows = _tile(tm, 352, 16)
    tn = 1024
    kern = functools.partial(_inproj_kernel, tm=tm, rows=rows, ctx_len=ctx_len)
    return pl.pallas_call(
        kern,
        grid=(bsz, l // tm, n // tn),
        in_specs=[pl.BlockSpec((1, tm, d), lambda b, i, j: (b, i, 0)),
                  pl.BlockSpec((None, 8, d), lambda b, i, j: (layer, 0, 0)),
                  pl.BlockSpec((None, 8, d), lambda b, i, j: (layer, 0, 1)),
                  pl.BlockSpec((1, d), lambda b, i, j: (0, 0)),
                  pl.BlockSpec((None, d, tn), lambda b, i, j: (layer, 0, j))],
        out_specs=pl.BlockSpec((1, tm, tn), lambda b, i, j: (b, i, j)),
        out_shape=jax.ShapeDtypeStruct((bsz, l, n), BF16),
        scratch_shapes=[pltpu.VMEM((tm, d), BF16)],
        compiler_params=_cparams(("parallel", "parallel", "arbitrary"), 48),
        name="inproj",
    )(x, mod, mod, gain.reshape(1, d), w_bf16)


def _norm_rope(t, gain, cos, sin_signed, lane_lo):
    var = jnp.mean(t * t, axis=-1, keepdims=True)
    y = t * lax.rsqrt(var + EPS) * gain
    swapped = jnp.where(lane_lo, pltpu.roll(y, HEAD_DIM - ROPE_PAIRS, 1), pltpu.roll(y, ROPE_PAIRS, 1))
    return y * cos + swapped * sin_signed


def _qkprep_kernel(q_ref, k_ref, v_ref, cos_ref, sin_ref, qg_ref, kg_ref, qo_ref, ko_ref, vo_ref):
    cos = cos_ref[...]
    sin = sin_ref[...]
    lane = lax.broadcasted_iota(jnp.int32, cos.shape, 1)
    lane_lo = (lane & ROPE_PAIRS) == 0
    qg = qg_ref[...]
    kg = kg_ref[...]
    for h in range(N_HEADS):
        sl = slice(h * HEAD_DIM, (h + 1) * HEAD_DIM)
        t = q_ref[0, :, sl].astype(F32)
        qo_ref[0, :, sl] = (_norm_rope(t, qg, cos, sin, lane_lo) * Q_SCALE).astype(BF16)
    k_tail = jnp.where(lane == 0, 1.0, 0.0).astype(BF16)
    v_tail = jnp.ones(cos.shape, BF16)
    for h in range(N_KV_HEADS):
        sl = slice(h * HEAD_DIM, (h + 1) * HEAD_DIM)
        lo = slice(2 * h * HEAD_DIM, (2 * h + 1) * HEAD_DIM)
        hi = slice((2 * h + 1) * HEAD_DIM, (2 * h + 2) * HEAD_DIM)
        t = k_ref[0, :, sl].astype(F32)
        ko_ref[0, :, lo] = _norm_rope(t, kg, cos, sin, lane_lo).astype(BF16)
        ko_ref[0, :, hi] = k_tail
        vo_ref[0, :, lo] = v_ref[0, :, sl]
        vo_ref[0, :, hi] = v_tail


def _qkprep_call(p, cos, sin_signed, q_gain, k_gain):
    bsz, l, _ = p.shape
    qw = N_HEADS * HEAD_DIM
    kw = N_KV_HEADS * HEAD_DIM
    tm = _tile(l, 768, 16)
    return pl.pallas_call(
        _qkprep_kernel,
        grid=(bsz, l // tm),
        in_specs=[pl.BlockSpec((1, tm, qw), lambda b, i: (b, i, 0)),
                  pl.BlockSpec((1, tm, kw), lambda b, i: (b, i, qw // kw)),
                  pl.BlockSpec((1, tm, kw), lambda b, i: (b, i, qw // kw + 1)),
                  pl.BlockSpec((tm, HEAD_DIM), lambda b, i: (i, 0)),
                  pl.BlockSpec((tm, HEAD_DIM), lambda b, i: (i, 0)),
                  pl.BlockSpec((1, HEAD_DIM), lambda b, i: (0, 0)),
                  pl.BlockSpec((1, HEAD_DIM), lambda b, i: (0, 0))],
        out_specs=[pl.BlockSpec((1, tm, qw), lambda b, i: (b, i, 0)),
                   pl.BlockSpec((1, tm, 2 * kw), lambda b, i: (b, i, 0)),
                   pl.BlockSpec((1, tm, 2 * kw), lambda b, i: (b, i, 0))],
        out_shape=[jax.ShapeDtypeStruct((bsz, l, qw), BF16),
                   jax.ShapeDtypeStruct((bsz, l, 2 * kw), BF16),
                   jax.ShapeDtypeStruct((bsz, l, 2 * kw), BF16)],
        compiler_params=_cparams(("parallel", "parallel"), 40),
        name="qkprep",
    )(p, p, p, cos, sin_signed, q_gain.reshape(1, HEAD_DIM), k_gain.reshape(1, HEAD_DIM))


def _attn_kernel(q_ref, k_ref, v_ref, qg_ref, kg_ref, o_ref, qs_scr, acc_scr, sa_scr, sb_scr, m_scr, l_scr,
                 *, tq, tkc, ctx_len, n_lat_chunks):
    qi = pl.program_id(2)
    q = q_ref[0]
    gq = jnp.max(jnp.abs(qg_ref[...]), axis=-1, keepdims=True)
    gk = jnp.max(jnp.abs(kg_ref[...]), axis=-1, keepdims=True)
    bound = (SHIFT_MARGIN * HEAD_DIM * Q_SCALE) * gq * gk
    lane = lax.broadcasted_iota(jnp.int32, (1, HEAD_DIM), 1)
    tail = jnp.where(lane == 0, -bound, 0.0).astype(BF16)
    for h in range(GROUP):
        qs_scr[h * tq:(h + 1) * tq, :HEAD_DIM] = q[:, h * HEAD_DIM:(h + 1) * HEAD_DIM]
        qs_scr[h * tq:(h + 1) * tq, HEAD_DIM:] = jnp.broadcast_to(tail, (tq, HEAD_DIM))
    n = jnp.where(qi < ctx_len // tq, 0, n_lat_chunks)

    def lat_start(c):
        return pl.multiple_of(ctx_len + c * tkc, math.gcd(ctx_len, tkc))

    def write(out):
        for h in range(GROUP):
            o_ref[0, :, h * HEAD_DIM:(h + 1) * HEAD_DIM] = out[h * tq:(h + 1) * tq, :].astype(BF16)

    def scores(start, size):
        return lax.dot_general(qs_scr[...], k_ref[0, pl.ds(start, size), :], (((1,), (1,)), ((), ())),
                               preferred_element_type=F32)

    def accumulate(s, start, size):
        p = jnp.exp2(s).astype(BF16)
        acc_scr[...] += jnp.dot(p, v_ref[0, pl.ds(start, size), :], preferred_element_type=F32)

    acc_scr[...] = jnp.zeros(acc_scr.shape, F32)
    sa_scr[...] = scores(lat_start(0), tkc)
    accumulate(scores(0, ctx_len), 0, ctx_len)

    def chunk_pair(j, look_ahead):
        c0 = 2 * j
        sb_scr[...] = scores(lat_start(c0 + 1), tkc)
        accumulate(sa_scr[...], lat_start(c0), tkc)
        if look_ahead:
            sa_scr[...] = scores(lat_start(c0 + 2), tkc)
        accumulate(sb_scr[...], lat_start(c0 + 1), tkc)

    def shifted_body(j, carry):
        chunk_pair(j, look_ahead=True)
        return carry

    lax.fori_loop(0, n // 2 - 1, shifted_body, 0)

    @pl.when(n > 0)
    def _():
        chunk_pair(n // 2 - 1, look_ahead=False)

    acc = acc_scr[...]
    den = acc[:, HEAD_DIM:]
    write(acc[:, :HEAD_DIM] / den)

    @pl.when(jnp.logical_not(jnp.min(den) >= MIN_DENOM))
    def _():
        m_scr[...] = jnp.full(m_scr.shape, NEG_INF, F32)
        l_scr[...] = jnp.zeros(l_scr.shape, F32)
        acc_scr[...] = jnp.zeros(acc_scr.shape, F32)

        def online_chunk(start, size):
            k = k_ref[0, pl.ds(start, size), :HEAD_DIM]
            v = v_ref[0, pl.ds(start, size), :HEAD_DIM]
            s = lax.dot_general(qs_scr[:, :HEAD_DIM], k, (((1,), (1,)), ((), ())),
                                preferred_element_type=F32)
            m_old = m_scr[...]
            m_new = jnp.maximum(m_old, jnp.max(s, axis=-1, keepdims=True))
            alpha = jnp.exp2(m_old - m_new)
            p = jnp.exp2(s - m_new)
            l_scr[...] = alpha * l_scr[...] + jnp.sum(p, axis=-1, keepdims=True)
            acc_scr[:, :HEAD_DIM] = alpha * acc_scr[:, :HEAD_DIM] + jnp.dot(
                p.astype(BF16), v, preferred_element_type=F32)
            m_scr[...] = m_new

        online_chunk(0, ctx_len)

        def online_body(c, carry):
            online_chunk(lat_start(c), tkc)
            return carry

        lax.fori_loop(0, n, online_body, 0)
        write(acc_scr[:, :HEAD_DIM] / l_scr[...])


def _attn_call(qn, ke, ve, q_gain, k_gain, ctx_len):
    bsz, l, qw = qn.shape
    tq = 256
    assert ctx_len % tq == 0 and l % tq == 0
    t_lat = l - ctx_len
    tkc = _tile(t_lat // 2, 2048, 256)
    gw = GROUP * HEAD_DIM
    ew = 2 * HEAD_DIM
    kern = functools.partial(_attn_kernel, tq=tq, tkc=tkc, ctx_len=ctx_len, n_lat_chunks=t_lat // tkc)
    return pl.pallas_call(
        kern,
        grid=(bsz, N_KV_HEADS, l // tq),
        in_specs=[pl.BlockSpec((1, tq, gw), lambda b, g, i: (b, i, g)),
                  pl.BlockSpec((1, l, ew), lambda b, g, i: (b, 0, g)),
                  pl.BlockSpec((1, l, ew), lambda b, g, i: (b, 0, g)),
                  pl.BlockSpec((1, HEAD_DIM), lambda b, g, i: (0, 0)),
                  pl.BlockSpec((1, HEAD_DIM), lambda b, g, i: (0, 0))],
        out_specs=pl.BlockSpec((1, tq, gw), lambda b, g, i: (b, i, g)),
        out_shape=jax.ShapeDtypeStruct((bsz, l, qw), BF16),
        scratch_shapes=[pltpu.VMEM((GROUP * tq, ew), BF16),
                        pltpu.VMEM((GROUP * tq, ew), F32),
                        pltpu.VMEM((GROUP * tq, tkc), F32),
                        pltpu.VMEM((GROUP * tq, tkc), F32),
                        pltpu.VMEM((GROUP * tq, 1), F32),
                        pltpu.VMEM((GROUP * tq, 1), F32)],
        compiler_params=_cparams(("parallel", "parallel", "arbitrary"), 56),
        name="attn",
    )(qn, ke, ve, q_gain.reshape(1, HEAD_DIM), k_gain.reshape(1, HEAD_DIM))


def _segment_bounds(row, ctx_len, seq_len):
    is_ctx = row < ctx_len
    first = jnp.where(is_ctx, 0, ctx_len)
    last = jnp.where(is_ctx, ctx_len - 1, seq_len - 1)
    return first, last


def _shift_down(x, prev, k, local):
    y = pltpu.roll(x, k, 0)
    for r in range(k):
        y = jnp.where(local == r, prev[HALO - k + r:HALO - k + r + 1, :], y)
    return y


def _shift_up(x, nxt, k, local, tt):
    y = pltpu.roll(x, tt - k, 0)
    for r in range(k):
        y = jnp.where(local == tt - k + r, nxt[r:r + 1, :], y)
    return y


def _halo_shifts(x, prev, nxt):
    tt = x.shape[0]
    sub = lax.broadcasted_iota(jnp.int32, (SUBLANES, x.shape[1]), 0)

    def down(k):
        y = pltpu.roll(x, k, 0)
        head = y[:SUBLANES]
        for r in range(k):
            head = jnp.where(sub == r, prev[HALO - k + r:HALO - k + r + 1, :], head)
        return jnp.concatenate([head, y[SUBLANES:]], axis=0)

    def up(k):
        y = pltpu.roll(x, tt - k, 0)
        tail = y[tt - SUBLANES:]
        for r in range(k):
            tail = jnp.where(sub == SUBLANES - k + r, nxt[r:r + 1, :], tail)
        return jnp.concatenate([y[:tt - SUBLANES], tail], axis=0)

    return down, up


def _scan_tile(a, d, carry, reverse):
    n_groups = a.shape[0] // SUBLANES
    sub = lax.broadcasted_iota(jnp.int32, (SUBLANES, a.shape[1]), 0)
    groups = []
    for v in range(n_groups):
        av = a[v * SUBLANES:(v + 1) * SUBLANES]
        dv = d[v * SUBLANES:(v + 1) * SUBLANES]
        for k in (1, 2, 4):
            keep = sub < SUBLANES - k if reverse else sub >= k
            shift = SUBLANES - k if reverse else k
            a_n = jnp.where(keep, pltpu.roll(av, shift, 0), 1.0)
            d_n = jnp.where(keep, pltpu.roll(dv, shift, 0), 0.0)
            dv = av * d_n + dv
            av = av * a_n
        groups.append((av, dv))
    hs = [None] * n_groups
    for v in (reversed(range(n_groups)) if reverse else range(n_groups)):
        av, dv = groups[v]
        hv = dv + av * carry
        carry = hv[0:1] if reverse else hv[SUBLANES - 1:SUBLANES]
        hs[v] = hv
    return jnp.concatenate(hs, axis=0), carry


def _lru_kernel(*refs, reverse, tt, ctx_len, n_sub):
    if reverse:
        (x_ref, xp_ref, xn_ref, cw_ref, cb_ref, wa_ref, ba_ref, wx_ref, bx_ref, lam_ref,
         hf_ref, g_ref, o_ref, carry_scr) = refs
    else:
        (x_ref, xp_ref, xn_ref, cw_ref, cb_ref, wa_ref, ba_ref, wx_ref, bx_ref, lam_ref,
         o_ref, carry_scr) = refs
    s = pl.program_id(2)
    nt = pl.num_programs(2)
    ti = jnp.where(s == 0, 0, nt - s) if reverse else s

    @pl.when(s == 0)
    def _():
        carry_scr[...] = jnp.zeros(carry_scr.shape, F32)

    prev_ok = jnp.logical_and(ti != 0, ti * tt != ctx_len)
    next_ok = jnp.logical_and(ti != nt - 1, (ti + 1) * tt != ctx_len)
    x = x_ref[0].astype(F32)
    xp = jnp.where(prev_ok, xp_ref[0].astype(F32), 0.0)
    xn = jnp.where(next_ok, xn_ref[0].astype(F32), 0.0)
    down, up = _halo_shifts(x, xp, xn)
    cw = cw_ref[...]
    u = cb_ref[...] + cw[2:3, :] * x + cw[0:1, :] * down(2) + cw[1:2, :] * down(1) + cw[3:4, :] * up(1)
    nlam = -lam_ref[...]
    softplus = jnp.maximum(nlam, 0.0) + jnp.log1p(jnp.exp(-jnp.abs(nlam)))

    for j in range(n_sub):
        ls = slice(j * LRU_BLOCK, (j + 1) * LRU_BLOCK)
        uj = u[:, ls]
        ub = uj.astype(BF16)
        r = jax.nn.sigmoid(jnp.dot(ub, wa_ref[j], preferred_element_type=F32) + ba_ref[:, ls])
        i = jax.nn.sigmoid(jnp.dot(ub, wx_ref[j], preferred_element_type=F32) + bx_ref[:, ls])
        a = jnp.exp((-LRU_C) * r * softplus[:, ls])
        d = jnp.sqrt(1.0 - a * a) * (i * uj)
        h, carry = _scan_tile(a, d, carry_scr[:, ls], reverse)
        carry_scr[:, ls] = carry
        if reverse:
            o_ref[0, :, ls] = ((hf_ref[0, :, ls] + h) * _gelu(g_ref[0, :, ls].astype(F32))).astype(BF16)
        else:
            o_ref[0, :, ls] = h


def _lru_call(p, conv_w, conv_b, wa, ba, wx, bx, lam, ctx_len, reverse, h_fwd=None):
    bsz, l, _ = p.shape
    c = conv_w.shape[1]
    tt = 256
    assert ctx_len % tt == 0 and l % tt == 0
    nt = l // tt
    n_sub = 4
    cw = n_sub * LRU_BLOCK
    nb = c // cw
    x_col0 = (N_HEADS + 2 * N_KV_HEADS) * HEAD_DIM // cw
    g_col0 = x_col0 + nb
    hb = tt // HALO
    n_hblk = l // HALO

    def tile_of(s):
        return jnp.where(s == 0, 0, nt - s) if reverse else s

    x_map = lambda b, cb, s: (b, tile_of(s), x_col0 + cb)
    prev_map = lambda b, cb, s: (b, jnp.maximum(tile_of(s) * hb - 1, 0), x_col0 + cb)
    next_map = lambda b, cb, s: (b, jnp.minimum((tile_of(s) + 1) * hb, n_hblk - 1), x_col0 + cb)
    vec_map = lambda b, cb, s: (0, cb)
    w_spec = pl.BlockSpec((n_sub, LRU_BLOCK, LRU_BLOCK), lambda b, cb, s: (cb, 0, 0))
    in_specs = [pl.BlockSpec((1, tt, cw), x_map),
                pl.BlockSpec((1, HALO, cw), prev_map),
                pl.BlockSpec((1, HALO, cw), next_map),
                pl.BlockSpec((LRU_CONV, cw), vec_map),
                pl.BlockSpec((1, cw), vec_map),
                w_spec,
                pl.BlockSpec((1, cw), vec_map),
                w_spec,
                pl.BlockSpec((1, cw), vec_map),
                pl.BlockSpec((1, cw), vec_map)]
    args = [p, p, p, conv_w, conv_b.reshape(1, c), wa.astype(BF16), ba.reshape(1, c),
            wx.astype(BF16), bx.reshape(1, c), lam.reshape(1, c)]
    out_map = lambda b, cb, s: (b, tile_of(s), cb)
    if reverse:
        in_specs += [pl.BlockSpec((1, tt, cw), out_map),
                     pl.BlockSpec((1, tt, cw), lambda b, cb, s: (b, tile_of(s), g_col0 + cb))]
        args += [h_fwd, p]
        out_dtype = BF16
    else:
        out_dtype = F32
    kern = functools.partial(_lru_kernel, reverse=reverse, tt=tt, ctx_len=ctx_len, n_sub=n_sub)
    return pl.pallas_call(
        kern,
        grid=(bsz, nb, nt),
        in_specs=in_specs,
        out_specs=pl.BlockSpec((1, tt, cw), out_map),
        out_shape=jax.ShapeDtypeStruct((bsz, l, c), out_dtype),
        scratch_shapes=[pltpu.VMEM((1, cw), F32)],
        compiler_params=_cparams(("parallel", "parallel", "arbitrary"), 32),
        name="lru_rev" if reverse else "lru_fwd",
    )(*args)


def _sconv_kernel(bg_ref, cg_ref, u_ref, cgp_ref, up_ref, cgn_ref, un_ref, w_ref, b_ref, o_ref,
                  *, tt, ctx_len, seq_len):
    ti = pl.program_id(1)
    z = cg_ref[0].astype(F32) * u_ref[0].astype(F32)
    zp = cgp_ref[0].astype(F32) * up_ref[0].astype(F32)
    zn = cgn_ref[0].astype(F32) * un_ref[0].astype(F32)
    local = lax.broadcasted_iota(jnp.int32, (tt, 1), 0)
    row = ti * tt + local
    first, last = _segment_bounds(row, ctx_len, seq_len)
    w = w_ref[...]
    y = b_ref[...] + w[1:2, :] * z
    y = y + w[0:1, :] * jnp.where(row - 1 >= first, _shift_down(z, zp, 1, local), 0.0)
    y = y + w[2:3, :] * jnp.where(row + 1 <= last, _shift_up(z, zn, 1, local, tt), 0.0)
    o_ref[0] = (bg_ref[0].astype(F32) * y).astype(BF16)


def _sconv_call(p, w, b, ctx_len):
    bsz, l, _ = p.shape
    c = w.shape[1]
    tt = _tile(l, 768, 16)
    tc = 512
    ncb = c // tc
    b_col0 = (N_HEADS * HEAD_DIM + 2 * N_KV_HEADS * HEAD_DIM + 2 * c) // tc
    c_col0 = b_col0 + ncb
    u_col0 = c_col0 + ncb
    hb = tt // HALO
    n_hblk = l // HALO
    prev = lambda i: jnp.maximum(i * hb - 1, 0)
    nxt = lambda i: jnp.minimum((i + 1) * hb, n_hblk - 1)
    kern = functools.partial(_sconv_kernel, tt=tt, ctx_len=ctx_len, seq_len=l)
    return pl.pallas_call(
        kern,
        grid=(bsz, l // tt, ncb),
        in_specs=[pl.BlockSpec((1, tt, tc), lambda b_, i, j: (b_, i, b_col0 + j)),
                  pl.BlockSpec((1, tt, tc), lambda b_, i, j: (b_, i, c_col0 + j)),
                  pl.BlockSpec((1, tt, tc), lambda b_, i, j: (b_, i, u_col0 + j)),
                  pl.BlockSpec((1, HALO, tc), lambda b_, i, j: (b_, prev(i), c_col0 + j)),
                  pl.BlockSpec((1, HALO, tc), lambda b_, i, j: (b_, prev(i), u_col0 + j)),
                  pl.BlockSpec((1, HALO, tc), lambda b_, i, j: (b_, nxt(i), c_col0 + j)),
                  pl.BlockSpec((1, HALO, tc), lambda b_, i, j: (b_, nxt(i), u_col0 + j)),
                  pl.BlockSpec((SC_CONV, tc), lambda b_, i, j: (0, j)),
                  pl.BlockSpec((1, tc), lambda b_, i, j: (0, j))],
        out_specs=pl.BlockSpec((1, tt, tc), lambda b_, i, j: (b_, i, j)),
        out_shape=jax.ShapeDtypeStruct((bsz, l, c), BF16),
        compiler_params=_cparams(("parallel", "parallel", "parallel"), 32),
        name="sconv",
    )(p, p, p, p, p, p, p, w, b.reshape(1, c))


def _merge_kernel(xa_ref, xl_ref, xs_ref, ga_ref, gl_ref, gs_ref, wa_ref, wl_ref, ws_ref, o_ref):
    y = jax.nn.sigmoid(ga_ref[0].astype(F32)) * jnp.dot(xa_ref[0], wa_ref[...], preferred_element_type=F32)
    y = y + jax.nn.sigmoid(gl_ref[0].astype(F32)) * jnp.dot(xl_ref[0], wl_ref[...], preferred_element_type=F32)
    y = y + jax.nn.sigmoid(gs_ref[0].astype(F32)) * jnp.dot(xs_ref[0], ws_ref[...], preferred_element_type=F32)
    o_ref[0] = y.astype(BF16)


def _merge_call(x_att, x_lru, x_sc, p, layer, w_att, w_lru, w_sc):
    bsz, l, d = x_att.shape
    tm = _tile(l, 768, 16)
    tn = 512
    g_col0 = (p.shape[2] - 3 * d) // tn
    nj = d // tn
    xspec = pl.BlockSpec((1, tm, d), lambda b, i, j: (b, i, 0))
    wspec = pl.BlockSpec((None, d, tn), lambda b, i, j: (layer, 0, j))
    gspec = lambda k: pl.BlockSpec((1, tm, tn), lambda b, i, j: (b, i, g_col0 + k * nj + j))
    return pl.pallas_call(
        _merge_kernel,
        grid=(bsz, l // tm, nj),
        in_specs=[xspec, xspec, xspec, gspec(0), gspec(1), gspec(2), wspec, wspec, wspec],
        out_specs=pl.BlockSpec((1, tm, tn), lambda b, i, j: (b, i, j)),
        out_shape=jax.ShapeDtypeStruct((bsz, l, d), BF16),
        compiler_params=_cparams(("parallel", "parallel", "arbitrary"), 48),
        name="merge",
    )(x_att, x_lru, x_sc, p, p, p, w_att, w_lru, w_sc)


def _outproj_kernel(y_ref, w_ref, x_ref, gate_ref, shift_ref, scale_ref, g_ref, xo_ref, ht_ref,
                    *, tm, ctx_len):
    b = pl.program_id(0)
    i = pl.program_id(1)
    row = i * tm + lax.broadcasted_iota(jnp.int32, (tm, 1), 0)
    is_ctx = row < ctx_len
    acc = jnp.dot(y_ref[0], w_ref[...], preferred_element_type=F32)
    xn = x_ref[0] + _row_select(gate_ref, b, is_ctx) * acc
    xo_ref[0] = xn
    var = jnp.mean(xn * xn, axis=-1, keepdims=True)
    h = xn * lax.rsqrt(var + EPS) * g_ref[...]
    h = h * (1.0 + _row_select(scale_ref, b, is_ctx)) + _row_select(shift_ref, b, is_ctx)
    ht_ref[0] = h.T.astype(BF16)


def _outproj_call(y, w_out, x, mod, layer, gain, ctx_len):
    bsz, l, d = x.shape
    tm = _tile(l, 384, LANES)
    kern = functools.partial(_outproj_kernel, tm=tm, ctx_len=ctx_len)
    mspec = lambda k: pl.BlockSpec((None, 8, d), lambda b, i: (layer, 0, k))
    return pl.pallas_call(
        kern,
        grid=(bsz, l // tm),
        in_specs=[pl.BlockSpec((1, tm, d), lambda b, i: (b, i, 0)),
                  pl.BlockSpec((None, d, d), lambda b, i: (layer, 0, 0)),
                  pl.BlockSpec((1, tm, d), lambda b, i: (b, i, 0)),
                  mspec(2), mspec(3), mspec(4),
                  pl.BlockSpec((1, d), lambda b, i: (0, 0))],
        out_specs=[pl.BlockSpec((1, tm, d), lambda b, i: (b, i, 0)),
                   pl.BlockSpec((1, d, tm), lambda b, i: (b, 0, i))],
        out_shape=[jax.ShapeDtypeStruct((bsz, l, d), F32),
                   jax.ShapeDtypeStruct((bsz, d, l), BF16)],
        compiler_params=_cparams(("parallel", "parallel"), 48),
        name="outproj",
    )(y, w_out, x, mod, mod, mod, gain.reshape(1, d))


_CAND_ROWS = tuple(PEER_TOPK // (i + 1) for i in range(PEER_TOPK))


def _top16(s, v_scr):
    n = s.shape[0]
    key = lax.broadcasted_iota(jnp.int32, s.shape, 0)

    def body(r, carry):
        work, rank = carry
        m = jnp.max(work, axis=0, keepdims=True)
        v_scr[pl.ds(r, 1), :] = m
        first = jnp.min(jnp.where(work == m, key, n), axis=0, keepdims=True)
        sel = key == first
        return jnp.where(sel, NEG_INF, work), jnp.where(sel, jnp.asarray(r, F32), rank)

    _, rank = lax.fori_loop(0, PEER_TOPK, body, (s, jnp.full(s.shape, float(PEER_TOPK), F32)))
    return rank


MARK = -(2.0 ** 127)


def _top16_distinct(ss, v_scrs):
    def body(r, works):
        mark = jnp.asarray(r, F32) * (MARK / 32.0) + MARK
        out = []
        for work, v_scr in zip(works, v_scrs):
            m = jnp.max(work, axis=0, keepdims=True)
            v_scr[pl.ds(r, 1), :] = m
            out.append(jnp.where(work == m, mark, work))
        return tuple(out)

    works = lax.fori_loop(0, PEER_TOPK, body, tuple(ss))
    return [jnp.where(w <= MARK, (MARK - w) * (-32.0 / MARK), float(PEER_TOPK)) for w in works]


def _peer_topk_kernel(ht_ref, wq_ref, k1_ref, k2_ref, cnt_ref, e1_ref, rk_ref, e2_ref,
                      q_scr, v_scr, *, tm, n_par):
    half = PEER_NKEYS
    q_scr[...] = jnp.dot(wq_ref[...], ht_ref[0], preferred_element_type=F32).astype(BF16)
    sub = lax.broadcasted_iota(jnp.int32, (SUBLANES, LANES), 0)
    big = PEER_TOPK * PEER_TOPK

    def candidates(v1, v2):
        pieces, poss = [], []
        for i in range(SUBLANES):
            for j0 in range(0, _CAND_ROWS[i], SUBLANES):
                c = v1[i:i + 1, :] + v2[j0:j0 + SUBLANES, :]
                valid = sub + j0 < _CAND_ROWS[i]
                pieces.append(jnp.where(valid, c, NEG_INF))
                poss.append(jnp.where(valid, i * PEER_TOPK + j0 + sub, big))
        pieces.append(v1[SUBLANES:, :] + v2[0:1, :])
        poss.append((sub + SUBLANES) * PEER_TOPK)
        return pieces, poss

    def pick_exact(pieces, poss):
        def pick(_, carry2):
            cs, sels = carry2
            m = functools.reduce(jnp.maximum, cs)
            m = jnp.max(m, axis=0, keepdims=True)
            cand_pos = functools.reduce(jnp.minimum, [jnp.where(c == m, p_, big) for c, p_ in zip(cs, poss)])
            first = jnp.min(cand_pos, axis=0, keepdims=True)
            hit = [p_ == first for p_ in poss]
            cs = tuple(jnp.where(hh, NEG_INF, c) for hh, c in zip(hit, cs))
            sels = tuple(jnp.where(hh, 1.0, s_) for hh, s_ in zip(hit, sels))
            return cs, sels

        zeros = tuple(jnp.zeros((SUBLANES, LANES), F32) for _ in pieces)
        return lax.fori_loop(0, PEER_TOPK, pick, (tuple(pieces), zeros))[1]

    def pick_distinct(groups):
        npc = len(groups[0])

        def pick(_, cs):
            out = []
            for g in range(len(groups)):
                grp = cs[g * npc:(g + 1) * npc]
                m = jnp.max(functools.reduce(jnp.maximum, grp), axis=0, keepdims=True)
                out.extend(jnp.where(c == m, MARK, c) for c in grp)
            return tuple(out)

        marked = lax.fori_loop(0, PEER_TOPK, pick, tuple(c for grp in groups for c in grp))
        return [tuple(jnp.where(c == MARK, 1.0, 0.0) for c in marked[g * npc:(g + 1) * npc])
                for g in range(len(groups))]

    def compute(h, lss, exact):
        base = pl.multiple_of(h * 2 * half, 2 * half)
        n_g = len(lss)
        s1s = [jnp.dot(k1_ref[h], q_scr[pl.ds(base, half), ls], preferred_element_type=F32) for ls in lss]
        s2s = [jnp.dot(k2_ref[h], q_scr[pl.ds(base + half, half), ls], preferred_element_type=F32)
               for ls in lss]
        v1_refs = [v_scr.at[2 * g] for g in range(n_g)]
        v2_refs = [v_scr.at[2 * g + 1] for g in range(n_g)]
        if exact:
            rank1s = [_top16(s, r) for s, r in zip(s1s, v1_refs)]
            rank2s = [_top16(s, r) for s, r in zip(s2s, v2_refs)]
        else:
            ranks = _top16_distinct(tuple(s1s + s2s), tuple(v1_refs + v2_refs))
            rank1s, rank2s = ranks[:n_g], ranks[n_g:]
        v1s = [r[...] for r in v1_refs]
        v2s = [r[...] for r in v2_refs]
        cands = [candidates(v1, v2) for v1, v2 in zip(v1s, v2s)]
        if exact:
            all_sels = [pick_exact(pieces, poss) for pieces, poss in cands]
        else:
            all_sels = pick_distinct([pieces for pieces, _ in cands])
        most = None
        for g in range(n_g):
            most_g = finish(h, lss[g], s1s[g], s2s[g], rank1s[g], rank2s[g], v1s[g], v2s[g],
                            cands[g][0], all_sels[g])
            most = most_g if most is None else jnp.maximum(most, most_g)
        return most

    def finish(h, ls, s1, s2, rank1, rank2, v1, v2, orig, sels):
        top = v1[0:1, :] + v2[0:1, :]
        zsum = functools.reduce(
            lambda a_, b_: a_ + b_,
            [jnp.where(s_ > 0.0, jnp.exp(o - top), 0.0) for s_, o in zip(sels, orig)])
        zinv = 1.0 / jnp.sum(zsum, axis=0, keepdims=True)

        counts = []
        pi = 0
        for i in range(SUBLANES):
            c = None
            for j0 in range(0, _CAND_ROWS[i], SUBLANES):
                part = jnp.sum(sels[pi], axis=0, keepdims=True)
                c = part if c is None else c + part
                pi += 1
            counts.append(c)
        tail = sels[pi]
        for i in range(SUBLANES, PEER_TOPK):
            counts.append(tail[i - SUBLANES:i - SUBLANES + 1, :])
        cnt = jnp.zeros(rank1.shape, F32)
        for i in range(PEER_TOPK):
            cnt = jnp.where(rank1 == float(i), counts[i], cnt)

        cnt_ref[0, h, :, ls] = cnt
        e1_ref[0, h, :, ls] = jnp.exp(s1 - v1[0:1, :]) * zinv
        rk_ref[0, h, :, ls] = rank2.astype(BF16)
        e2_ref[0, h, :, ls] = jnp.exp(s2 - v2[0:1, :]).astype(BF16)
        ranked = jnp.where(rank1 < PEER_TOPK, 1.0, 0.0)
        ranked = jnp.maximum(jnp.sum(ranked, axis=0, keepdims=True),
                             jnp.sum(jnp.where(rank2 < PEER_TOPK, 1.0, 0.0), axis=0, keepdims=True))
        picked = jnp.sum(functools.reduce(lambda a_, b_: a_ + b_, sels), axis=0, keepdims=True)
        return jnp.max(jnp.maximum(ranked, picked))

    n_trips = tm // (n_par * LANES)

    def head_lane_groups(idx, carry):
        h = idx // n_trips
        first = (idx % n_trips) * n_par
        lss = [pl.ds(pl.multiple_of((first + g) * LANES, LANES), LANES) for g in range(n_par)]
        most = compute(h, lss, exact=False)

        @pl.when(most > PEER_TOPK)
        def _():
            compute(h, lss, exact=True)

        return carry

    lax.fori_loop(0, PEER_HEADS * n_trips, head_lane_groups, 0)


def _peer_topk_call(ht, layer, wq_t, k1, k2):
    bsz, d, l = ht.shape
    qd = wq_t.shape[1]
    n_par = 2
    tm = _tile(l, 768, n_par * LANES)
    kern = functools.partial(_peer_topk_kernel, tm=tm, n_par=n_par)
    ospec = pl.BlockSpec((1, PEER_HEADS, PEER_NKEYS, tm), lambda b, i: (b, 0, 0, i))
    oshape = jax.ShapeDtypeStruct((bsz, PEER_HEADS, PEER_NKEYS, l), F32)
    kspec = pl.BlockSpec((None, PEER_HEADS, PEER_NKEYS, PEER_NKEYS), lambda b, i: (layer, 0, 0, 0))
    return pl.pallas_call(
        kern,
        grid=(bsz, l // tm),
        in_specs=[pl.BlockSpec((1, d, tm), lambda b, i: (b, 0, i)),
                  pl.BlockSpec((None, qd, d), lambda b, i: (layer, 0, 0)),
                  kspec, kspec],
        out_specs=[ospec, ospec, ospec, ospec],
        out_shape=[oshape, oshape, jax.ShapeDtypeStruct(oshape.shape, BF16),
                   jax.ShapeDtypeStruct(oshape.shape, BF16)],
        scratch_shapes=[pltpu.VMEM((qd, tm), BF16),
                        pltpu.VMEM((2 * n_par, PEER_TOPK, LANES), F32)],
        compiler_params=_cparams(("parallel", "parallel"), 56),
        name="peer_topk",
    )(ht, wq_t, k1, k2)


def _peer_dense_kernel(ht_ref, u_ref, vt_ref, cnt_ref, e1_ref, rk_ref, e2_ref, o_ref, wz_scr, *, n_sub):
    e = pl.program_id(2)

    @pl.when(e == 0)
    def _():
        o_ref[...] = jnp.zeros(o_ref.shape, F32)

    ht = ht_ref[0]
    n_split = 2
    per = n_sub // n_split
    for part in range(n_split):
        s = jnp.dot(u_ref[part * per * PEER_NKEYS:(part + 1) * per * PEER_NKEYS, :], ht,
                    preferred_element_type=F32)
        for a in range(part * per, (part + 1) * per):
            rows = slice(a * PEER_NKEYS, (a + 1) * PEER_NKEYS)
            local = slice((a - part * per) * PEER_NKEYS, (a - part * per + 1) * PEER_NKEYS)
            w = None
            for h in range(PEER_HEADS):
                cnt = cnt_ref[0, h, a:a + 1, :].astype(BF16)
                e1 = e1_ref[0, h, a:a + 1, :].astype(BF16)
                term = jnp.where(rk_ref[0, h] < cnt, e2_ref[0, h] * e1, 0.0)
                w = term if w is None else w + term
            wz_scr[rows, :] = w * _gelu(s[local, :]).astype(BF16)
    o_ref[0] += jnp.dot(vt_ref[...], wz_scr[...], preferred_element_type=F32)


def _peer_dense_call(ht, layer, u_bf16, vt_bf16, cnt, e1n, rank2, e2):
    bsz, d, l = ht.shape
    n_exp = u_bf16.shape[1]
    tm = _tile(l, 768, LANES)
    n_sub = SUBLANES
    te = n_sub * PEER_NKEYS
    kern = functools.partial(_peer_dense_kernel, n_sub=n_sub)
    aspec = pl.BlockSpec((1, PEER_HEADS, n_sub, tm), lambda b, i, e: (b, 0, e, i))
    fspec = pl.BlockSpec((1, PEER_HEADS, PEER_NKEYS, tm), lambda b, i, e: (b, 0, 0, i))
    return pl.pallas_call(
        kern,
        grid=(bsz, l // tm, n_exp // te),
        in_specs=[pl.BlockSpec((1, d, tm), lambda b, i, e: (b, 0, i)),
                  pl.BlockSpec((None, te, d), lambda b, i, e: (layer, e, 0)),
                  pl.BlockSpec((None, d, te), lambda b, i, e: (layer, 0, e)),
                  aspec, aspec, fspec, fspec],
        out_specs=pl.BlockSpec((1, d, tm), lambda b, i, e: (b, 0, i)),
        out_shape=jax.ShapeDtypeStruct((bsz, d, l), F32),
        scratch_shapes=[pltpu.VMEM((te, tm), BF16)],
        compiler_params=_cparams(("parallel", "parallel", "arbitrary"), 58),
        name="peer_dense",
    )(ht, u_bf16, vt_bf16, cnt, e1n, rank2, e2)


def _resid_kernel(x_ref, yt_ref, gate_ref, o_ref, *, tm, ctx_len, first_tile):
    b = pl.program_id(0)
    i = pl.program_id(1) + first_tile
    row = i * tm + lax.broadcasted_iota(jnp.int32, (tm, 1), 0)
    o_ref[0] = x_ref[0] + _row_select(gate_ref, b, row < ctx_len) * yt_ref[0].T


def _resid_call(x, yt, mod, layer, ctx_len, latent_only):
    bsz, l, d = x.shape
    tm = _tile(math.gcd(l, ctx_len), 384, LANES)
    first_tile = ctx_len // tm if latent_only else 0
    n_tiles = l // tm - first_tile
    kern = functools.partial(_resid_kernel, tm=tm, ctx_len=ctx_len, first_tile=first_tile)
    return pl.pallas_call(
        kern,
        grid=(bsz, n_tiles),
        in_specs=[pl.BlockSpec((1, tm, d), lambda b, i: (b, i + first_tile, 0)),
                  pl.BlockSpec((1, d, tm), lambda b, i: (b, 0, i + first_tile)),
                  pl.BlockSpec((None, 8, d), lambda b, i: (layer, 0, 5))],
        out_specs=pl.BlockSpec((1, tm, d), lambda b, i: (b, i, 0)),
        out_shape=jax.ShapeDtypeStruct((bsz, n_tiles * tm, d), F32),
        compiler_params=_cparams(("parallel", "parallel"), 40),
        name="resid",
    )(x, yt, mod)


def _rope_tables(ctx_len, t_lat):
    rows = t_lat // GRID_W
    row = jnp.repeat(jnp.arange(rows, dtype=F32), GRID_W)
    col = jnp.tile(jnp.arange(GRID_W, dtype=F32), rows)
    inv = ROPE_THETA ** (-jnp.arange(ROPE_PAIRS, dtype=F32) / ROPE_PAIRS)
    ang = jnp.concatenate([row[:, None] * inv] * 2 + [col[:, None] * inv] * 2, axis=1)
    ang = jnp.concatenate([jnp.zeros((ctx_len, HEAD_DIM), F32), ang], axis=0)
    sign = jnp.where((jnp.arange(HEAD_DIM) & ROPE_PAIRS) == 0, -1.0, 1.0).astype(F32)
    return jnp.cos(ang), jnp.sin(ang) * sign


def kernel(x, c, ctx, c_ctx, w_mod, b_mod, norm_mix, norm_ffn, w_in, q_norm, k_norm, lru_conv_w, lru_conv_b, lru_wa, lru_ba, lru_wx, lru_bx, lru_lambda, sc_conv_w, sc_conv_b, w_o_attn, w_o_lru, w_o_sc, w_out, peer_wq, peer_k1, peer_k2, peer_u, peer_v):
    bsz, t_lat, d = x.shape
    ctx_len = ctx.shape[1]
    depth = w_mod.shape[0]
    assert bsz == 2, "modulation rows are laid out as [latent 0, latent 1, context]"

    xs = jnp.concatenate([ctx, x], axis=1)
    s8 = jnp.concatenate([c, c_ctx[None, :], jnp.zeros((8 - bsz - 1, d), F32)], axis=0)
    mod = _mod_call(s8, w_mod, b_mod)
    cos, sin_signed = _rope_tables(ctx_len, t_lat)

    w_in_b = w_in.astype(BF16)
    w_att_b, w_lru_b, w_sc_b = w_o_attn.astype(BF16), w_o_lru.astype(BF16), w_o_sc.astype(BF16)
    w_out_b = w_out.astype(BF16)
    wq_t = jnp.swapaxes(peer_wq, 1, 2).astype(BF16)
    k1_b, k2_b = peer_k1.astype(BF16), peer_k2.astype(BF16)
    u_b = peer_u.astype(BF16)
    vt_b = jnp.swapaxes(peer_v, 1, 2).astype(BF16)

    for l in range(depth):
        p = _inproj_call(xs, mod, l, norm_mix[l], w_in_b, ctx_len)
        qn, ke, ve = _qkprep_call(p, cos, sin_signed, q_norm[l], k_norm[l])
        x_att = _attn_call(qn, ke, ve, q_norm[l], k_norm[l], ctx_len)
        lru_args = (lru_conv_w[l], lru_conv_b[l])
        h_fwd = _lru_call(p, *lru_args, lru_wa[l, 0], lru_ba[l, 0], lru_wx[l, 0], lru_bx[l, 0],
                          lru_lambda[l, 0], ctx_len, reverse=False)
        x_lru = _lru_call(p, *lru_args, lru_wa[l, 1], lru_ba[l, 1], lru_wx[l, 1], lru_bx[l, 1],
                          lru_lambda[l, 1], ctx_len, reverse=True, h_fwd=h_fwd)
        x_sc = _sconv_call(p, sc_conv_w[l], sc_conv_b[l], ctx_len)
        y = _merge_call(x_att, x_lru, x_sc, p, l, w_att_b, w_lru_b, w_sc_b)
        xs, ht = _outproj_call(y, w_out_b, xs, mod, l, norm_ffn[l], ctx_len)
        cnt, e1n, rank2, e2 = _peer_topk_call(ht, l, wq_t, k1_b, k2_b)
        yt = _peer_dense_call(ht, l, u_b, vt_b, cnt, e1n, rank2, e2)
        xs = _resid_call(xs, yt, mod, l, ctx_len, latent_only=(l == depth - 1))
    return xs
```

```python
import functools
import math

import numpy as np
import jax
import jax.numpy as jnp
from jax import lax
from jax.experimental import pallas as pl
from jax.experimental.pallas import tpu as pltpu

F32 = jnp.float32
BF16 = jnp.bfloat16

GRID_W = 64
EPS = 1e-6
N_MOD = 6

N_HEADS = 16
N_KV_HEADS = 4
HEAD_DIM = 128
GROUP = N_HEADS // N_KV_HEADS
ROPE_PAIRS = HEAD_DIM // 4
ROPE_THETA = 10000.0

LRU_BLOCKS = 16
LRU_BLOCK = 128
LRU_CONV = 4
LRU_C = 8.0
SC_CONV = 3

PEER_HEADS = 8
PEER_NKEYS = 128
PEER_TOPK = 16

LANES = 128
SUBLANES = 8
HALO = SUBLANES
NEG_INF = float("-inf")
LOG2E = 1.4426950408889634
GELU_C = math.sqrt(2.0 / math.pi)
Q_SCALE = HEAD_DIM ** -0.5 * LOG2E
SHIFT_MARGIN = 1.02
MIN_DENOM = 2.0 ** -100


def _tile(n, target, mult):
    best = None
    for t in range(mult, min(n, target) + 1, mult):
        if n % t == 0:
            best = t
    assert best is not None, (n, target, mult)
    return best


def _cparams(sem, vmem_mib):
    return pltpu.CompilerParams(dimension_semantics=sem, vmem_limit_bytes=vmem_mib << 20)


def _gelu(x):
    return 0.5 * x * (1.0 + jnp.tanh(GELU_C * (x + 0.044715 * (x * x * x))))


def _row_select(mod_ref, b, is_ctx):
    return jnp.where(is_ctx, mod_ref[2:3, :], mod_ref[pl.ds(b, 1), :])


def _mod_kernel(s_ref, w_ref, b_ref, o_ref):
    s = s_ref[...]
    s = s * jax.nn.sigmoid(s)
    o_ref[0] = jnp.dot(s, w_ref[0], preferred_element_type=F32,
                       precision=lax.Precision.HIGHEST) + b_ref[0]


def _mod_call(s8, w_mod, b_mod):
    depth, d, n = w_mod.shape
    tn = _tile(n, 1024, LANES)
    return pl.pallas_call(
        _mod_kernel,
        grid=(depth, n // tn),
        in_specs=[pl.BlockSpec((8, d), lambda l, j: (0, 0)),
                  pl.BlockSpec((1, d, tn), lambda l, j: (l, 0, j)),
                  pl.BlockSpec((1, 1, tn), lambda l, j: (l, 0, j))],
        out_specs=pl.BlockSpec((1, 8, tn), lambda l, j: (l, 0, j)),
        out_shape=jax.ShapeDtypeStruct((depth, 8, n), F32),
        compiler_params=_cparams(("parallel", "parallel"), 32),
        name="mod",
    )(s8, w_mod, b_mod.reshape(depth, 1, n))


def _inproj_kernel(x_ref, shift_ref, scale_ref, g_ref, w_ref, o_ref, h_scr, *, tm, rows, ctx_len):
    b = pl.program_id(0)
    i = pl.program_id(1)

    @pl.when(pl.program_id(2) == 0)
    def _():
        def norm_rows(c, carry):
            r0 = pl.multiple_of(c * rows, rows)
            x = x_ref[0, pl.ds(r0, rows), :]
            var = jnp.mean(x * x, axis=-1, keepdims=True)
            y = x * lax.rsqrt(var + EPS) * g_ref[...]
            row = i * tm + r0 + lax.broadcasted_iota(jnp.int32, (rows, 1), 0)
            is_ctx = row < ctx_len
            sh = _row_select(shift_ref, b, is_ctx)
            sc = _row_select(scale_ref, b, is_ctx)
            h_scr[pl.ds(r0, rows), :] = (y * (1.0 + sc) + sh).astype(BF16)
            return carry

        lax.fori_loop(0, tm // rows, norm_rows, 0)

    o_ref[0] = jnp.dot(h_scr[...], w_ref[...], preferred_element_type=F32).astype(BF16)


def _inproj_call(x, mod, layer, gain, w_bf16, ctx_len):
    bsz, l, d = x.shape
    n = w_bf16.shape[2]
    tm = _tile(l, 1056, 16)
    rows = _tile(tm, 352, 16)
    tn = 1024
    kern = functools.partial(_inproj_kernel, tm=tm, rows=rows, ctx_len=ctx_len)
    return pl.pallas_call(
        kern,
        grid=(bsz, l // tm, n // tn),
        in_specs=[pl.BlockSpec((1, tm, d), lambda b, i, j: (b, i, 0)),
                  pl.BlockSpec((None, 8, d), lambda b, i, j: (layer, 0, 0)),
                  pl.BlockSpec((None, 8, d), lambda b, i, j: (layer, 0, 1)),
                  pl.BlockSpec((1, d), lambda b, i, j: (0, 0)),
                  pl.BlockSpec((None, d, tn), lambda b, i, j: (layer, 0, j))],
        out_specs=pl.BlockSpec((1, tm, tn), lambda b, i, j: (b, i, j)),
        out_shape=jax.ShapeDtypeStruct((bsz, l, n), BF16),
        scratch_shapes=[pltpu.VMEM((tm, d), BF16)],
        compiler_params=_cparams(("parallel", "parallel", "arbitrary"), 48),
        name="inproj",
    )(x, mod, mod, gain.reshape(1, d), w_bf16)


def _norm_rope(t, gain, cos, sin_signed, lane_lo):
    var = jnp.mean(t * t, axis=-1, keepdims=True)
    y = t * lax.rsqrt(var + EPS) * gain
    swapped = jnp.where(lane_lo, pltpu.roll(y, HEAD_DIM - ROPE_PAIRS, 1), pltpu.roll(y, ROPE_PAIRS, 1))
    return y * cos + swapped * sin_signed


def _qkprep_kernel(q_ref, k_ref, v_ref, cos_ref, sin_ref, qg_ref, kg_ref, qo_ref, ko_ref, vo_ref):
    cos = cos_ref[...]
    sin = sin_ref[...]
    lane = lax.broadcasted_iota(jnp.int32, cos.shape, 1)
    lane_lo = (lane & ROPE_PAIRS) == 0
    qg = qg_ref[...]
    kg = kg_ref[...]
    for h in range(N_HEADS):
        sl = slice(h * HEAD_DIM, (h + 1) * HEAD_DIM)
        t = q_ref[0, :, sl].astype(F32)
        qo_ref[0, :, sl] = (_norm_rope(t, qg, cos, sin, lane_lo) * Q_SCALE).astype(BF16)
    k_tail = jnp.where(lane == 0, 1.0, 0.0).astype(BF16)
    v_tail = jnp.ones(cos.shape, BF16)
    for h in range(N_KV_HEADS):
        sl = slice(h * HEAD_DIM, (h + 1) * HEAD_DIM)
        lo = slice(2 * h * HEAD_DIM, (2 * h + 1) * HEAD_DIM)
        hi = slice((2 * h + 1) * HEAD_DIM, (2 * h + 2) * HEAD_DIM)
        t = k_ref[0, :, sl].astype(F32)
        ko_ref[0, :, lo] = _norm_rope(t, kg, cos, sin, lane_lo).astype(BF16)
        ko_ref[0, :, hi] = k_tail
        vo_ref[0, :, lo] = v_ref[0, :, sl]
        vo_ref[0, :, hi] = v_tail


def _qkprep_call(p, cos, sin_signed, q_gain, k_gain):
    bsz, l, _ = p.shape
    qw = N_HEADS * HEAD_DIM
    kw = N_KV_HEADS * HEAD_DIM
    tm = _tile(l, 768, 16)
    return pl.pallas_call(
        _qkprep_kernel,
        grid=(bsz, l // tm),
        in_specs=[pl.BlockSpec((1, tm, qw), lambda b, i: (b, i, 0)),
                  pl.BlockSpec((1, tm, kw), lambda b, i: (b, i, qw // kw)),
                  pl.BlockSpec((1, tm, kw), lambda b, i: (b, i, qw // kw + 1)),
                  pl.BlockSpec((tm, HEAD_DIM), lambda b, i: (i, 0)),
                  pl.BlockSpec((tm, HEAD_DIM), lambda b, i: (i, 0)),
                  pl.BlockSpec((1, HEAD_DIM), lambda b, i: (0, 0)),
                  pl.BlockSpec((1, HEAD_DIM), lambda b, i: (0, 0))],
        out_specs=[pl.BlockSpec((1, tm, qw), lambda b, i: (b, i, 0)),
                   pl.BlockSpec((1, tm, 2 * kw), lambda b, i: (b, i, 0)),
                   pl.BlockSpec((1, tm, 2 * kw), lambda b, i: (b, i, 0))],
        out_shape=[jax.ShapeDtypeStruct((bsz, l, qw), BF16),
                   jax.ShapeDtypeStruct((bsz, l, 2 * kw), BF16),
                   jax.ShapeDtypeStruct((bsz, l, 2 * kw), BF16)],
        compiler_params=_cparams(("parallel", "parallel"), 40),
        name="qkprep",
    )(p, p, p, cos, sin_signed, q_gain.reshape(1, HEAD_DIM), k_gain.reshape(1, HEAD_DIM))


def _attn_kernel(q_ref, k_ref, v_ref, qg_ref, kg_ref, o_ref, qs_scr, acc_scr, sa_scr, sb_scr, m_scr, l_scr,
                 *, tq, tkc, ctx_len, n_lat_chunks):
    qi = pl.program_id(2)
    q = q_ref[0]
    gq = jnp.max(jnp.abs(qg_ref[...]), axis=-1, keepdims=True)
    gk = jnp.max(jnp.abs(kg_ref[...]), axis=-1, keepdims=True)
    bound = (SHIFT_MARGIN * HEAD_DIM * Q_SCALE) * gq * gk
    lane = lax.broadcasted_iota(jnp.int32, (1, HEAD_DIM), 1)
    tail = jnp.where(lane == 0, -bound, 0.0).astype(BF16)
    for h in range(GROUP):
        qs_scr[h * tq:(h + 1) * tq, :HEAD_DIM] = q[:, h * HEAD_DIM:(h + 1) * HEAD_DIM]
        qs_scr[h * tq:(h + 1) * tq, HEAD_DIM:] = jnp.broadcast_to(tail, (tq, HEAD_DIM))
    n = jnp.where(qi < ctx_len // tq, 0, n_lat_chunks)

    def lat_start(c):
        return pl.multiple_of(ctx_len + c * tkc, math.gcd(ctx_len, tkc))

    def write(out):
        for h in range(GROUP):
            o_ref[0, :, h * HEAD_DIM:(h + 1) * HEAD_DIM] = out[h * tq:(h + 1) * tq, :].astype(BF16)

    def scores(start, size):
        return lax.dot_general(qs_scr[...], k_ref[0, pl.ds(start, size), :], (((1,), (1,)), ((), ())),
                               preferred_element_type=F32)

    def accumulate(s, start, size):
        p = jnp.exp2(s).astype(BF16)
        acc_scr[...] += jnp.dot(p, v_ref[0, pl.ds(start, size), :], preferred_element_type=F32)

    acc_scr[...] = jnp.zeros(acc_scr.shape, F32)
    sa_scr[...] = scores(lat_start(0), tkc)
    accumulate(scores(0, ctx_len), 0, ctx_len)

    def chunk_pair(j, look_ahead):
        c0 = 2 * j
        sb_scr[...] = scores(lat_start(c0 + 1), tkc)
        accumulate(sa_scr[...], lat_start(c0), tkc)
        if look_ahead:
            sa_scr[...] = scores(lat_start(c0 + 2), tkc)
        accumulate(sb_scr[...], lat_start(c0 + 1), tkc)

    def shifted_body(j, carry):
        chunk_pair(j, look_ahead=True)
        return carry

    lax.fori_loop(0, n // 2 - 1, shifted_body, 0)

    @pl.when(n > 0)
    def _():
        chunk_pair(n // 2 - 1, look_ahead=False)

    acc = acc_scr[...]
    den = acc[:, HEAD_DIM:]
    write(acc[:, :HEAD_DIM] / den)

    @pl.when(jnp.logical_not(jnp.min(den) >= MIN_DENOM))
    def _():
        m_scr[...] = jnp.full(m_scr.shape, NEG_INF, F32)
        l_scr[...] = jnp.zeros(l_scr.shape, F32)
        acc_scr[...] = jnp.zeros(acc_scr.shape, F32)

        def online_chunk(start, size):
            k = k_ref[0, pl.ds(start, size), :HEAD_DIM]
            v = v_ref[0, pl.ds(start, size), :HEAD_DIM]
            s = lax.dot_general(qs_scr[:, :HEAD_DIM], k, (((1,), (1,)), ((), ())),
                                preferred_element_type=F32)
            m_old = m_scr[...]
            m_new = jnp.maximum(m_old, jnp.max(s, axis=-1, keepdims=True))
            alpha = jnp.exp2(m_old - m_new)
            p = jnp.exp2(s - m_new)
            l_scr[...] = alpha * l_scr[...] + jnp.sum(p, axis=-1, keepdims=True)
            acc_scr[:, :HEAD_DIM] = alpha * acc_scr[:, :HEAD_DIM] + jnp.dot(
                p.astype(BF16), v, preferred_element_type=F32)
            m_scr[...] = m_new

        online_chunk(0, ctx_len)

        def online_body(c, carry):
            online_chunk(lat_start(c), tkc)
            return carry

        lax.fori_loop(0, n, online_body, 0)
        write(acc_scr[:, :HEAD_DIM] / l_scr[...])


def _attn_call(qn, ke, ve, q_gain, k_gain, ctx_len):
    bsz, l, qw = qn.shape
    tq = 256
    assert ctx_len % tq == 0 and l % tq == 0
    t_lat = l - ctx_len
    tkc = _tile(t_lat // 2, 2048, 256)
    gw = GROUP * HEAD_DIM
    ew = 2 * HEAD_DIM
    kern = functools.partial(_attn_kernel, tq=tq, tkc=tkc, ctx_len=ctx_len, n_lat_chunks=t_lat // tkc)
    return pl.pallas_call(
        kern,
        grid=(bsz, N_KV_HEADS, l // tq),
        in_specs=[pl.BlockSpec((1, tq, gw), lambda b, g, i: (b, i, g)),
                  pl.BlockSpec((1, l, ew), lambda b, g, i: (b, 0, g)),
                  pl.BlockSpec((1, l, ew), lambda b, g, i: (b, 0, g)),
                  pl.BlockSpec((1, HEAD_DIM), lambda b, g, i: (0, 0)),
                  pl.BlockSpec((1, HEAD_DIM), lambda b, g, i: (0, 0))],
        out_specs=pl.BlockSpec((1, tq, gw), lambda b, g, i: (b, i, g)),
        out_shape=jax.ShapeDtypeStruct((bsz, l, qw), BF16),
        scratch_shapes=[pltpu.VMEM((GROUP * tq, ew), BF16),
                        pltpu.VMEM((GROUP * tq, ew), F32),
                        pltpu.VMEM((GROUP * tq, tkc), F32),
                        pltpu.VMEM((GROUP * tq, tkc), F32),
                        pltpu.VMEM((GROUP * tq, 1), F32),
                        pltpu.VMEM((GROUP * tq, 1), F32)],
        compiler_params=_cparams(("parallel", "parallel", "arbitrary"), 56),
        name="attn",
    )(qn, ke, ve, q_gain.reshape(1, HEAD_DIM), k_gain.reshape(1, HEAD_DIM))


def _segment_bounds(row, ctx_len, seq_len):
    is_ctx = row < ctx_len
    first = jnp.where(is_ctx, 0, ctx_len)
    last = jnp.where(is_ctx, ctx_len - 1, seq_len - 1)
    return first, last


def _shift_down(x, prev, k, local):
    y = pltpu.roll(x, k, 0)
    for r in range(k):
        y = jnp.where(local == r, prev[HALO - k + r:HALO - k + r + 1, :], y)
    return y


def _shift_up(x, nxt, k, local, tt):
    y = pltpu.roll(x, tt - k, 0)
    for r in range(k):
        y = jnp.where(local == tt - k + r, nxt[r:r + 1, :], y)
    return y


def _halo_shifts(x, prev, nxt):
    tt = x.shape[0]
    sub = lax.broadcasted_iota(jnp.int32, (SUBLANES, x.shape[1]), 0)

    def down(k):
        y = pltpu.roll(x, k, 0)
        head = y[:SUBLANES]
        for r in range(k):
            head = jnp.where(sub == r, prev[HALO - k + r:HALO - k + r + 1, :], head)
        return jnp.concatenate([head, y[SUBLANES:]], axis=0)

    def up(k):
        y = pltpu.roll(x, tt - k, 0)
        tail = y[tt - SUBLANES:]
        for r in range(k):
            tail = jnp.where(sub == SUBLANES - k + r, nxt[r:r + 1, :], tail)
        return jnp.concatenate([y[:tt - SUBLANES], tail], axis=0)

    return down, up


def _scan_tile(a, d, carry, reverse):
    n_groups = a.shape[0] // SUBLANES
    sub = lax.broadcasted_iota(jnp.int32, (SUBLANES, a.shape[1]), 0)
    groups = []
    for v in range(n_groups):
        av = a[v * SUBLANES:(v + 1) * SUBLANES]
        dv = d[v * SUBLANES:(v + 1) * SUBLANES]
        for k in (1, 2, 4):
            keep = sub < SUBLANES - k if reverse else sub >= k
            shift = SUBLANES - k if reverse else k
            a_n = jnp.where(keep, pltpu.roll(av, shift, 0), 1.0)
            d_n = jnp.where(keep, pltpu.roll(dv, shift, 0), 0.0)
            dv = av * d_n + dv
            av = av * a_n
        groups.append((av, dv))
    hs = [None] * n_groups
    for v in (reversed(range(n_groups)) if reverse else range(n_groups)):
        av, dv = groups[v]
        hv = dv + av * carry
        carry = hv[0:1] if reverse else hv[SUBLANES - 1:SUBLANES]
        hs[v] = hv
    return jnp.concatenate(hs, axis=0), carry


def _lru_kernel(*refs, reverse, tt, ctx_len, n_sub):
    if reverse:
        (x_ref, xp_ref, xn_ref, cw_ref, cb_ref, wa_ref, ba_ref, wx_ref, bx_ref, lam_ref,
         hf_ref, g_ref, o_ref, carry_scr) = refs
    else:
        (x_ref, xp_ref, xn_ref, cw_ref, cb_ref, wa_ref, ba_ref, wx_ref, bx_ref, lam_ref,
         o_ref, carry_scr) = refs
    s = pl.program_id(2)
    nt = pl.num_programs(2)
    ti = jnp.where(s == 0, 0, nt - s) if reverse else s

    @pl.when(s == 0)
    def _():
        carry_scr[...] = jnp.zeros(carry_scr.shape, F32)

    prev_ok = jnp.logical_and(ti != 0, ti * tt != ctx_len)
    next_ok = jnp.logical_and(ti != nt - 1, (ti + 1) * tt != ctx_len)
    x = x_ref[0].astype(F32)
    xp = jnp.where(prev_ok, xp_ref[0].astype(F32), 0.0)
    xn = jnp.where(next_ok, xn_ref[0].astype(F32), 0.0)
    down, up = _halo_shifts(x, xp, xn)
    cw = cw_ref[...]
    u = cb_ref[...] + cw[2:3, :] * x + cw[0:1, :] * down(2) + cw[1:2, :] * down(1) + cw[3:4, :] * up(1)
    nlam = -lam_ref[...]
    softplus = jnp.maximum(nlam, 0.0) + jnp.log1p(jnp.exp(-jnp.abs(nlam)))

    for j in range(n_sub):
        ls = slice(j * LRU_BLOCK, (j + 1) * LRU_BLOCK)
        uj = u[:, ls]
        ub = uj.astype(BF16)
        r = jax.nn.sigmoid(jnp.dot(ub, wa_ref[j], preferred_element_type=F32) + ba_ref[:, ls])
        i = jax.nn.sigmoid(jnp.dot(ub, wx_ref[j], preferred_element_type=F32) + bx_ref[:, ls])
        a = jnp.exp((-LRU_C) * r * softplus[:, ls])
        d = jnp.sqrt(1.0 - a * a) * (i * uj)
        h, carry = _scan_tile(a, d, carry_scr[:, ls], reverse)
        carry_scr[:, ls] = carry
        if reverse:
            o_ref[0, :, ls] = ((hf_ref[0, :, ls] + h) * _gelu(g_ref[0, :, ls].astype(F32))).astype(BF16)
        else:
            o_ref[0, :, ls] = h


def _lru_call(p, conv_w, conv_b, wa, ba, wx, bx, lam, ctx_len, reverse, h_fwd=None):
    bsz, l, _ = p.shape
    c = conv_w.shape[1]
    tt = 256
    assert ctx_len % tt == 0 and l % tt == 0
    nt = l // tt
    n_sub = 8
    cw = n_sub * LRU_BLOCK
    nb = c // cw
    x_col0 = (N_HEADS + 2 * N_KV_HEADS) * HEAD_DIM // cw
    g_col0 = x_col0 + nb
    hb = tt // HALO
    n_hblk = l // HALO

    def tile_of(s):
        return jnp.where(s == 0, 0, nt - s) if reverse else s

    x_map = lambda b, cb, s: (b, tile_of(s), x_col0 + cb)
    prev_map = lambda b, cb, s: (b, jnp.maximum(tile_of(s) * hb - 1, 0), x_col0 + cb)
    next_map = lambda b, cb, s: (b, jnp.minimum((tile_of(s) + 1) * hb, n_hblk - 1), x_col0 + cb)
    vec_map = lambda b, cb, s: (0, cb)
    w_spec = pl.BlockSpec((n_sub, LRU_BLOCK, LRU_BLOCK), lambda b, cb, s: (cb, 0, 0))
    in_specs = [pl.BlockSpec((1, tt, cw), x_map),
                pl.BlockSpec((1, HALO, cw), prev_map),
                pl.BlockSpec((1, HALO, cw), next_map),
                pl.BlockSpec((LRU_CONV, cw), vec_map),
                pl.BlockSpec((1, cw), vec_map),
                w_spec,
                pl.BlockSpec((1, cw), vec_map),
                w_spec,
                pl.BlockSpec((1, cw), vec_map),
                pl.BlockSpec((1, cw), vec_map)]
    args = [p, p, p, conv_w, conv_b.reshape(1, c), wa.astype(BF16), ba.reshape(1, c),
            wx.astype(BF16), bx.reshape(1, c), lam.reshape(1, c)]
    out_map = lambda b, cb, s: (b, tile_of(s), cb)
    if reverse:
        in_specs += [pl.BlockSpec((1, tt, cw), out_map),
                     pl.BlockSpec((1, tt, cw), lambda b, cb, s: (b, tile_of(s), g_col0 + cb))]
        args += [h_fwd, p]
        out_dtype = BF16
    else:
        out_dtype = F32
    kern = functools.partial(_lru_kernel, reverse=reverse, tt=tt, ctx_len=ctx_len, n_sub=n_sub)
    return pl.pallas_call(
        kern,
        grid=(bsz, nb, nt),
        in_specs=in_specs,
        out_specs=pl.BlockSpec((1, tt, cw), out_map),
        out_shape=jax.ShapeDtypeStruct((bsz, l, c), out_dtype),
        scratch_shapes=[pltpu.VMEM((1, cw), F32)],
        compiler_params=_cparams(("parallel", "parallel", "arbitrary"), 32),
        name="lru_rev" if reverse else "lru_fwd",
    )(*args)


def _sconv_kernel(bg_ref, cg_ref, u_ref, cgp_ref, up_ref, cgn_ref, un_ref, w_ref, b_ref, o_ref,
                  *, tt, ctx_len, seq_len):
    ti = pl.program_id(1)
    z = cg_ref[0].astype(F32) * u_ref[0].astype(F32)
    zp = cgp_ref[0].astype(F32) * up_ref[0].astype(F32)
    zn = cgn_ref[0].astype(F32) * un_ref[0].astype(F32)
    local = lax.broadcasted_iota(jnp.int32, (tt, 1), 0)
    row = ti * tt + local
    first, last = _segment_bounds(row, ctx_len, seq_len)
    w = w_ref[...]
    y = b_ref[...] + w[1:2, :] * z
    y = y + w[0:1, :] * jnp.where(row - 1 >= first, _shift_down(z, zp, 1, local), 0.0)
    y = y + w[2:3, :] * jnp.where(row + 1 <= last, _shift_up(z, zn, 1, local, tt), 0.0)
    o_ref[0] = (bg_ref[0].astype(F32) * y).astype(BF16)


def _sconv_call(p, w, b, ctx_len):
    bsz, l, _ = p.shape
    c = w.shape[1]
    tt = _tile(l, 768, 16)
    tc = 512
    ncb = c // tc
    b_col0 = (N_HEADS * HEAD_DIM + 2 * N_KV_HEADS * HEAD_DIM + 2 * c) // tc
    c_col0 = b_col0 + ncb
    u_col0 = c_col0 + ncb
    hb = tt // HALO
    n_hblk = l // HALO
    prev = lambda i: jnp.maximum(i * hb - 1, 0)
    nxt = lambda i: jnp.minimum((i + 1) * hb, n_hblk - 1)
    kern = functools.partial(_sconv_kernel, tt=tt, ctx_len=ctx_len, seq_len=l)
    return pl.pallas_call(
        kern,
        grid=(bsz, l // tt, ncb),
        in_specs=[pl.BlockSpec((1, tt, tc), lambda b_, i, j: (b_, i, b_col0 + j)),
                  pl.BlockSpec((1, tt, tc), lambda b_, i, j: (b_, i, c_col0 + j)),
                  pl.BlockSpec((1, tt, tc), lambda b_, i, j: (b_, i, u_col0 + j)),
                  pl.BlockSpec((1, HALO, tc), lambda b_, i, j: (b_, prev(i), c_col0 + j)),
                  pl.BlockSpec((1, HALO, tc), lambda b_, i, j: (b_, prev(i), u_col0 + j)),
                  pl.BlockSpec((1, HALO, tc), lambda b_, i, j: (b_, nxt(i), c_col0 + j)),
                  pl.BlockSpec((1, HALO, tc), lambda b_, i, j: (b_, nxt(i), u_col0 + j)),
                  pl.BlockSpec((SC_CONV, tc), lambda b_, i, j: (0, j)),
                  pl.BlockSpec((1, tc), lambda b_, i, j: (0, j))],
        out_specs=pl.BlockSpec((1, tt, tc), lambda b_, i, j: (b_, i, j)),
        out_shape=jax.ShapeDtypeStruct((bsz, l, c), BF16),
        compiler_params=_cparams(("parallel", "parallel", "parallel"), 32),
        name="sconv",
    )(p, p, p, p, p, p, p, w, b.reshape(1, c))


def _merge_kernel(xa_ref, xl_ref, xs_ref, ga_ref, gl_ref, gs_ref, wa_ref, wl_ref, ws_ref, o_ref):
    y = jax.nn.sigmoid(ga_ref[0].astype(F32)) * jnp.dot(xa_ref[0], wa_ref[...], preferred_element_type=F32)
    y = y + jax.nn.sigmoid(gl_ref[0].astype(F32)) * jnp.dot(xl_ref[0], wl_ref[...], preferred_element_type=F32)
    y = y + jax.nn.sigmoid(gs_ref[0].astype(F32)) * jnp.dot(xs_ref[0], ws_ref[...], preferred_element_type=F32)
    o_ref[0] = y.astype(BF16)


def _merge_call(x_att, x_lru, x_sc, p, layer, w_att, w_lru, w_sc):
    bsz, l, d = x_att.shape
    tm = _tile(l, 768, 16)
    tn = 512
    g_col0 = (p.shape[2] - 3 * d) // tn
    nj = d // tn
    xspec = pl.BlockSpec((1, tm, d), lambda b, i, j: (b, i, 0))
    wspec = pl.BlockSpec((None, d, tn), lambda b, i, j: (layer, 0, j))
    gspec = lambda k: pl.BlockSpec((1, tm, tn), lambda b, i, j: (b, i, g_col0 + k * nj + j))
    return pl.pallas_call(
        _merge_kernel,
        grid=(bsz, l // tm, nj),
        in_specs=[xspec, xspec, xspec, gspec(0), gspec(1), gspec(2), wspec, wspec, wspec],
        out_specs=pl.BlockSpec((1, tm, tn), lambda b, i, j: (b, i, j)),
        out_shape=jax.ShapeDtypeStruct((bsz, l, d), BF16),
        compiler_params=_cparams(("parallel", "parallel", "arbitrary"), 48),
        name="merge",
    )(x_att, x_lru, x_sc, p, p, p, w_att, w_lru, w_sc)


def _outproj_kernel(y_ref, w_ref, x_ref, gate_ref, shift_ref, scale_ref, g_ref, xo_ref, ht_ref,
                    *, tm, ctx_len):
    b = pl.program_id(0)
    i = pl.program_id(1)
    row = i * tm + lax.broadcasted_iota(jnp.int32, (tm, 1), 0)
    is_ctx = row < ctx_len
    acc = jnp.dot(y_ref[0], w_ref[...], preferred_element_type=F32)
    xn = x_ref[0] + _row_select(gate_ref, b, is_ctx) * acc
    xo_ref[0] = xn
    var = jnp.mean(xn * xn, axis=-1, keepdims=True)
    h = xn * lax.rsqrt(var + EPS) * g_ref[...]
    h = h * (1.0 + _row_select(scale_ref, b, is_ctx)) + _row_select(shift_ref, b, is_ctx)
    ht_ref[0] = h.T.astype(BF16)


def _outproj_call(y, w_out, x, mod, layer, gain, ctx_len):
    bsz, l, d = x.shape
    tm = _tile(l, 384, LANES)
    kern = functools.partial(_outproj_kernel, tm=tm, ctx_len=ctx_len)
    mspec = lambda k: pl.BlockSpec((None, 8, d), lambda b, i: (layer, 0, k))
    return pl.pallas_call(
        kern,
        grid=(bsz, l // tm),
        in_specs=[pl.BlockSpec((1, tm, d), lambda b, i: (b, i, 0)),
                  pl.BlockSpec((None, d, d), lambda b, i: (layer, 0, 0)),
                  pl.BlockSpec((1, tm, d), lambda b, i: (b, i, 0)),
                  mspec(2), mspec(3), mspec(4),
                  pl.BlockSpec((1, d), lambda b, i: (0, 0))],
        out_specs=[pl.BlockSpec((1, tm, d), lambda b, i: (b, i, 0)),
                   pl.BlockSpec((1, d, tm), lambda b, i: (b, 0, i))],
        out_shape=[jax.ShapeDtypeStruct((bsz, l, d), F32),
                   jax.ShapeDtypeStruct((bsz, d, l), BF16)],
        compiler_params=_cparams(("parallel", "parallel"), 48),
        name="outproj",
    )(y, w_out, x, mod, mod, mod, gain.reshape(1, d))


_CAND_ROWS = tuple(PEER_TOPK // (i + 1) for i in range(PEER_TOPK))


def _top16(s, v_scr):
    n = s.shape[0]
    key = lax.broadcasted_iota(jnp.int32, s.shape, 0)

    def body(r, carry):
        work, rank = carry
        m = jnp.max(work, axis=0, keepdims=True)
        v_scr[pl.ds(r, 1), :] = m
        first = jnp.min(jnp.where(work == m, key, n), axis=0, keepdims=True)
        sel = key == first
        return jnp.where(sel, NEG_INF, work), jnp.where(sel, jnp.asarray(r, F32), rank)

    _, rank = lax.fori_loop(0, PEER_TOPK, body, (s, jnp.full(s.shape, float(PEER_TOPK), F32)))
    return rank


MARK = -(2.0 ** 127)


def _top16_distinct(ss, v_scrs):
    def body(r, works):
        mark = jnp.asarray(r, F32) * (MARK / 32.0) + MARK
        out = []
        for work, v_scr in zip(works, v_scrs):
            m = jnp.max(work, axis=0, keepdims=True)
            v_scr[pl.ds(r, 1), :] = m
            out.append(jnp.where(work == m, mark, work))
        return tuple(out)

    works = lax.fori_loop(0, PEER_TOPK, body, tuple(ss))
    return [jnp.where(w <= MARK, (MARK - w) * (-32.0 / MARK), float(PEER_TOPK)) for w in works]


def _peer_topk_kernel(ht_ref, wq_ref, k1_ref, k2_ref, cnt_ref, e1_ref, rk_ref, e2_ref,
                      q_scr, v_scr, *, tm, n_par):
    half = PEER_NKEYS
    q_scr[...] = jnp.dot(wq_ref[...], ht_ref[0], preferred_element_type=F32).astype(BF16)
    sub = lax.broadcasted_iota(jnp.int32, (SUBLANES, LANES), 0)
    big = PEER_TOPK * PEER_TOPK

    def candidates(v1, v2):
        pieces, poss = [], []
        for i in range(SUBLANES):
            for j0 in range(0, _CAND_ROWS[i], SUBLANES):
                c = v1[i:i + 1, :] + v2[j0:j0 + SUBLANES, :]
                valid = sub + j0 < _CAND_ROWS[i]
                pieces.append(jnp.where(valid, c, NEG_INF))
                poss.append(jnp.where(valid, i * PEER_TOPK + j0 + sub, big))
        pieces.append(v1[SUBLANES:, :] + v2[0:1, :])
        poss.append((sub + SUBLANES) * PEER_TOPK)
        return pieces, poss

    def pick_exact(pieces, poss):
        def pick(_, carry2):
            cs, sels = carry2
            m = functools.reduce(jnp.maximum, cs)
            m = jnp.max(m, axis=0, keepdims=True)
            cand_pos = functools.reduce(jnp.minimum, [jnp.where(c == m, p_, big) for c, p_ in zip(cs, poss)])
            first = jnp.min(cand_pos, axis=0, keepdims=True)
            hit = [p_ == first for p_ in poss]
            cs = tuple(jnp.where(hh, NEG_INF, c) for hh, c in zip(hit, cs))
            sels = tuple(jnp.where(hh, 1.0, s_) for hh, s_ in zip(hit, sels))
            return cs, sels

        zeros = tuple(jnp.zeros((SUBLANES, LANES), F32) for _ in pieces)
        return lax.fori_loop(0, PEER_TOPK, pick, (tuple(pieces), zeros))[1]

    def pick_distinct(groups):
        npc = len(groups[0])

        def pick(_, cs):
            out = []
            for g in range(len(groups)):
                grp = cs[g * npc:(g + 1) * npc]
                m = jnp.max(functools.reduce(jnp.maximum, grp), axis=0, keepdims=True)
                out.extend(jnp.where(c == m, MARK, c) for c in grp)
            return tuple(out)

        marked = lax.fori_loop(0, PEER_TOPK, pick, tuple(c for grp in groups for c in grp))
        return [tuple(jnp.where(c == MARK, 1.0, 0.0) for c in marked[g * npc:(g + 1) * npc])
                for g in range(len(groups))]

    def compute(h, lss, exact):
        base = pl.multiple_of(h * 2 * half, 2 * half)
        n_g = len(lss)
        s1s = [jnp.dot(k1_ref[h], q_scr[pl.ds(base, half), ls], preferred_element_type=F32) for ls in lss]
        s2s = [jnp.dot(k2_ref[h], q_scr[pl.ds(base + half, half), ls], preferred_element_type=F32)
               for ls in lss]
        v1_refs = [v_scr.at[2 * g] for g in range(n_g)]
        v2_refs = [v_scr.at[2 * g + 1] for g in range(n_g)]
        if exact:
            rank1s = [_top16(s, r) for s, r in zip(s1s, v1_refs)]
            rank2s = [_top16(s, r) for s, r in zip(s2s, v2_refs)]
        else:
            ranks = _top16_distinct(tuple(s1s + s2s), tuple(v1_refs + v2_refs))
            rank1s, rank2s = ranks[:n_g], ranks[n_g:]
        v1s = [r[...] for r in v1_refs]
        v2s = [r[...] for r in v2_refs]
        cands = [candidates(v1, v2) for v1, v2 in zip(v1s, v2s)]
        if exact:
            all_sels = [pick_exact(pieces, poss) for pieces, poss in cands]
        else:
            all_sels = pick_distinct([pieces for pieces, _ in cands])
        most = None
        for g in range(n_g):
            most_g = finish(h, lss[g], s1s[g], s2s[g], rank1s[g], rank2s[g], v1s[g], v2s[g],
                            cands[g][0], all_sels[g])
            most = most_g if most is None else jnp.maximum(most, most_g)
        return most

    def finish(h, ls, s1, s2, rank1, rank2, v1, v2, orig, sels):
        top = v1[0:1, :] + v2[0:1, :]
        zsum = functools.reduce(
            lambda a_, b_: a_ + b_,
            [jnp.where(s_ > 0.0, jnp.exp(o - top), 0.0) for s_, o in zip(sels, orig)])
        zinv = 1.0 / jnp.sum(zsum, axis=0, keepdims=True)

        counts = []
        pi = 0
        for i in range(SUBLANES):
            c = None
            for j0 in range(0, _CAND_ROWS[i], SUBLANES):
                part = jnp.sum(sels[pi], axis=0, keepdims=True)
                c = part if c is None else c + part
                pi += 1
            counts.append(c)
        tail = sels[pi]
        for i in range(SUBLANES, PEER_TOPK):
            counts.append(tail[i - SUBLANES:i - SUBLANES + 1, :])
        cnt = jnp.zeros(rank1.shape, F32)
        for i in range(PEER_TOPK):
            cnt = jnp.where(rank1 == float(i), counts[i], cnt)

        cnt_ref[0, h, :, ls] = cnt
        e1_ref[0, h, :, ls] = jnp.exp(s1 - v1[0:1, :]) * zinv
        rk_ref[0, h, :, ls] = rank2.astype(BF16)
        e2_ref[0, h, :, ls] = jnp.exp(s2 - v2[0:1, :]).astype(BF16)
        ranked = jnp.where(rank1 < PEER_TOPK, 1.0, 0.0)
        ranked = jnp.maximum(jnp.sum(ranked, axis=0, keepdims=True),
                             jnp.sum(jnp.where(rank2 < PEER_TOPK, 1.0, 0.0), axis=0, keepdims=True))
        picked = jnp.sum(functools.reduce(lambda a_, b_: a_ + b_, sels), axis=0, keepdims=True)
        return jnp.max(jnp.maximum(ranked, picked))

    n_trips = tm // (n_par * LANES)

    def head_lane_groups(idx, carry):
        h = idx // n_trips
        first = (idx % n_trips) * n_par
        lss = [pl.ds(pl.multiple_of((first + g) * LANES, LANES), LANES) for g in range(n_par)]
        most = compute(h, lss, exact=False)

        @pl.when(most > PEER_TOPK)
        def _():
            compute(h, lss, exact=True)

        return carry

    lax.fori_loop(0, PEER_HEADS * n_trips, head_lane_groups, 0)


def _peer_topk_call(ht, layer, wq_t, k1, k2):
    bsz, d, l = ht.shape
    qd = wq_t.shape[1]
    n_par = 2
    tm = _tile(l, 768, n_par * LANES)
    kern = functools.partial(_peer_topk_kernel, tm=tm, n_par=n_par)
    ospec = pl.BlockSpec((1, PEER_HEADS, PEER_NKEYS, tm), lambda b, i: (b, 0, 0, i))
    oshape = jax.ShapeDtypeStruct((bsz, PEER_HEADS, PEER_NKEYS, l), F32)
    kspec = pl.BlockSpec((None, PEER_HEADS, PEER_NKEYS, PEER_NKEYS), lambda b, i: (layer, 0, 0, 0))
    return pl.pallas_call(
        kern,
        grid=(bsz, l // tm),
        in_specs=[pl.BlockSpec((1, d, tm), lambda b, i: (b, 0, i)),
                  pl.BlockSpec((None, qd, d), lambda b, i: (layer, 0, 0)),
                  kspec, kspec],
        out_specs=[ospec, ospec, ospec, ospec],
        out_shape=[oshape, oshape, jax.ShapeDtypeStruct(oshape.shape, BF16),
                   jax.ShapeDtypeStruct(oshape.shape, BF16)],
        scratch_shapes=[pltpu.VMEM((qd, tm), BF16),
                        pltpu.VMEM((2 * n_par, PEER_TOPK, LANES), F32)],
        compiler_params=_cparams(("parallel", "parallel"), 56),
        name="peer_topk",
    )(ht, wq_t, k1, k2)


def _peer_dense_kernel(ht_ref, u_ref, vt_ref, cnt_ref, e1_ref, rk_ref, e2_ref, o_ref, wz_scr, *, n_sub):
    e = pl.program_id(2)

    @pl.when(e == 0)
    def _():
        o_ref[...] = jnp.zeros(o_ref.shape, F32)

    ht = ht_ref[0]
    per = 4
    n_split = n_sub // per
    for part in range(n_split):
        s = jnp.dot(u_ref[part * per * PEER_NKEYS:(part + 1) * per * PEER_NKEYS, :], ht,
                    preferred_element_type=F32)
        for a in range(part * per, (part + 1) * per):
            rows = slice(a * PEER_NKEYS, (a + 1) * PEER_NKEYS)
            local = slice((a - part * per) * PEER_NKEYS, (a - part * per + 1) * PEER_NKEYS)
            w = None
            for h in range(PEER_HEADS):
                cnt = cnt_ref[0, h, a:a + 1, :].astype(BF16)
                e1 = e1_ref[0, h, a:a + 1, :].astype(BF16)
                term = jnp.where(rk_ref[0, h] < cnt, e2_ref[0, h] * e1, 0.0)
                w = term if w is None else w + term
            wz_scr[rows, :] = w * _gelu(s[local, :]).astype(BF16)
    o_ref[0] += jnp.dot(vt_ref[...], wz_scr[...], preferred_element_type=F32)


def _peer_dense_call(ht, layer, u_bf16, vt_bf16, cnt, e1n, rank2, e2):
    bsz, d, l = ht.shape
    n_exp = u_bf16.shape[1]
    tm = _tile(l, 768, LANES)
    n_sub = SUBLANES
    te = n_sub * PEER_NKEYS
    kern = functools.partial(_peer_dense_kernel, n_sub=n_sub)
    aspec = pl.BlockSpec((1, PEER_HEADS, n_sub, tm), lambda b, i, e: (b, 0, e, i))
    fspec = pl.BlockSpec((1, PEER_HEADS, PEER_NKEYS, tm), lambda b, i, e: (b, 0, 0, i))
    return pl.pallas_call(
        kern,
        grid=(bsz, l // tm, n_exp // te),
        in_specs=[pl.BlockSpec((1, d, tm), lambda b, i, e: (b, 0, i)),
                  pl.BlockSpec((None, te, d), lambda b, i, e: (layer, e, 0)),
                  pl.BlockSpec((None, d, te), lambda b, i, e: (layer, 0, e)),
                  aspec, aspec, fspec, fspec],
        out_specs=pl.BlockSpec((1, d, tm), lambda b, i, e: (b, 0, i)),
        out_shape=jax.ShapeDtypeStruct((bsz, d, l), F32),
        scratch_shapes=[pltpu.VMEM((te, tm), BF16)],
        compiler_params=_cparams(("parallel", "parallel", "arbitrary"), 58),
        name="peer_dense",
    )(ht, u_bf16, vt_bf16, cnt, e1n, rank2, e2)


def _resid_kernel(x_ref, yt_ref, gate_ref, o_ref, *, tm, ctx_len, first_tile):
    b = pl.program_id(0)
    i = pl.program_id(1) + first_tile
    row = i * tm + lax.broadcasted_iota(jnp.int32, (tm, 1), 0)
    o_ref[0] = x_ref[0] + _row_select(gate_ref, b, row < ctx_len) * yt_ref[0].T


def _resid_call(x, yt, mod, layer, ctx_len, latent_only):
    bsz, l, d = x.shape
    tm = _tile(math.gcd(l, ctx_len), 384, LANES)
    first_tile = ctx_len // tm if latent_only else 0
    n_tiles = l // tm - first_tile
    kern = functools.partial(_resid_kernel, tm=tm, ctx_len=ctx_len, first_tile=first_tile)
    return pl.pallas_call(
        kern,
        grid=(bsz, n_tiles),
        in_specs=[pl.BlockSpec((1, tm, d), lambda b, i: (b, i + first_tile, 0)),
                  pl.BlockSpec((1, d, tm), lambda b, i: (b, 0, i + first_tile)),
                  pl.BlockSpec((None, 8, d), lambda b, i: (layer, 0, 5))],
        out_specs=pl.BlockSpec((1, tm, d), lambda b, i: (b, i, 0)),
        out_shape=jax.ShapeDtypeStruct((bsz, n_tiles * tm, d), F32),
        compiler_params=_cparams(("parallel", "parallel"), 40),
        name="resid",
    )(x, yt, mod)


def _rope_tables(ctx_len, t_lat):
    rows = t_lat // GRID_W
    row = jnp.repeat(jnp.arange(rows, dtype=F32), GRID_W)
    col = jnp.tile(jnp.arange(GRID_W, dtype=F32), rows)
    inv = ROPE_THETA ** (-jnp.arange(ROPE_PAIRS, dtype=F32) / ROPE_PAIRS)
    ang = jnp.concatenate([row[:, None] * inv] * 2 + [col[:, None] * inv] * 2, axis=1)
    ang = jnp.concatenate([jnp.zeros((ctx_len, HEAD_DIM), F32), ang], axis=0)
    sign = jnp.where((jnp.arange(HEAD_DIM) & ROPE_PAIRS) == 0, -1.0, 1.0).astype(F32)
    return jnp.cos(ang), jnp.sin(ang) * sign


def kernel(x, c, ctx, c_ctx, w_mod, b_mod, norm_mix, norm_ffn, w_in, q_norm, k_norm, lru_conv_w, lru_conv_b, lru_wa, lru_ba, lru_wx, lru_bx, lru_lambda, sc_conv_w, sc_conv_b, w_o_attn, w_o_lru, w_o_sc, w_out, peer_wq, peer_k1, peer_k2, peer_u, peer_v):
    bsz, t_lat, d = x.shape
    ctx_len = ctx.shape[1]
    depth = w_mod.shape[0]
    assert bsz == 2, "modulation rows are laid out as [latent 0, latent 1, context]"

    xs = jnp.concatenate([ctx, x], axis=1)
    s8 = jnp.concatenate([c, c_ctx[None, :], jnp.zeros((8 - bsz - 1, d), F32)], axis=0)
    mod = _mod_call(s8, w_mod, b_mod)
    cos, sin_signed = _rope_tables(ctx_len, t_lat)

    w_in_b = w_in.astype(BF16)
    w_att_b, w_lru_b, w_sc_b = w_o_attn.astype(BF16), w_o_lru.astype(BF16), w_o_sc.astype(BF16)
    w_out_b = w_out.astype(BF16)
    wq_t = jnp.swapaxes(peer_wq, 1, 2).astype(BF16)
    k1_b, k2_b = peer_k1.astype(BF16), peer_k2.astype(BF16)
    u_b = peer_u.astype(BF16)
    vt_b = jnp.swapaxes(peer_v, 1, 2).astype(BF16)

    for l in range(depth):
        p = _inproj_call(xs, mod, l, norm_mix[l], w_in_b, ctx_len)
        qn, ke, ve = _qkprep_call(p, cos, sin_signed, q_norm[l], k_norm[l])
        x_att = _attn_call(qn, ke, ve, q_norm[l], k_norm[l], ctx_len)
        lru_args = (lru_conv_w[l], lru_conv_b[l])
        h_fwd = _lru_call(p, *lru_args, lru_wa[l, 0], lru_ba[l, 0], lru_wx[l, 0], lru_bx[l, 0],
                          lru_lambda[l, 0], ctx_len, reverse=False)
        x_lru = _lru_call(p, *lru_args, lru_wa[l, 1], lru_ba[l, 1], lru_wx[l, 1], lru_bx[l, 1],
                          lru_lambda[l, 1], ctx_len, reverse=True, h_fwd=h_fwd)
        x_sc = _sconv_call(p, sc_conv_w[l], sc_conv_b[l], ctx_len)
        y = _merge_call(x_att, x_lru, x_sc, p, l, w_att_b, w_lru_b, w_sc_b)
        xs, ht = _outproj_call(y, w_out_b, xs, mod, l, norm_ffn[l], ctx_len)
        cnt, e1n, rank2, e2 = _peer_topk_call(ht, l, wq_t, k1_b, k2_b)
        yt = _peer_dense_call(ht, l, u_b, vt_b, cnt, e1n, rank2, e2)
        xs = _resid_call(xs, yt, mod, l, ctx_len, latent_only=(l == depth - 1))
    return xs
```

```python
import functools
import math

import numpy as np
import jax
import jax.numpy as jnp
from jax import lax
from jax.experimental import pallas as pl
from jax.experimental.pallas import tpu as pltpu

F32 = jnp.float32
BF16 = jnp.bfloat16

GRID_W = 64
EPS = 1e-6
N_MOD = 6

N_HEADS = 16
N_KV_HEADS = 4
HEAD_DIM = 128
GROUP = N_HEADS // N_KV_HEADS
ROPE_PAIRS = HEAD_DIM // 4
ROPE_THETA = 10000.0

LRU_BLOCKS = 16
LRU_BLOCK = 128
LRU_CONV = 4
LRU_C = 8.0
SC_CONV = 3

PEER_HEADS = 8
PEER_NKEYS = 128
PEER_TOPK = 16

LANES = 128
SUBLANES = 8
HALO = SUBLANES
NEG_INF = float("-inf")
LOG2E = 1.4426950408889634
GELU_C = math.sqrt(2.0 / math.pi)
Q_SCALE = HEAD_DIM ** -0.5 * LOG2E
SHIFT_MARGIN = 1.02
MIN_DENOM = 2.0 ** -100


def _tile(n, target, mult):
    best = None
    for t in range(mult, min(n, target) + 1, mult):
        if n % t == 0:
            best = t
    assert best is not None, (n, target, mult)
    return best


def _cparams(sem, vmem_mib):
    return pltpu.CompilerParams(dimension_semantics=sem, vmem_limit_bytes=vmem_mib << 20)


def _gelu(x):
    return 0.5 * x * (1.0 + jnp.tanh(GELU_C * (x + 0.044715 * (x * x * x))))


def _row_select(mod_ref, b, is_ctx):
    return jnp.where(is_ctx, mod_ref[2:3, :], mod_ref[pl.ds(b, 1), :])


def _mod_kernel(s_ref, w_ref, b_ref, o_ref):
    s = s_ref[...]
    s = s * jax.nn.sigmoid(s)
    o_ref[0] = jnp.dot(s, w_ref[0], preferred_element_type=F32,
                       precision=lax.Precision.HIGHEST) + b_ref[0]


def _mod_call(s8, w_mod, b_mod):
    depth, d, n = w_mod.shape
    tn = _tile(n, 1024, LANES)
    return pl.pallas_call(
        _mod_kernel,
        grid=(depth, n // tn),
        in_specs=[pl.BlockSpec((8, d), lambda l, j: (0, 0)),
                  pl.BlockSpec((1, d, tn), lambda l, j: (l, 0, j)),
                  pl.BlockSpec((1, 1, tn), lambda l, j: (l, 0, j))],
        out_specs=pl.BlockSpec((1, 8, tn), lambda l, j: (l, 0, j)),
        out_shape=jax.ShapeDtypeStruct((depth, 8, n), F32),
        compiler_params=_cparams(("parallel", "parallel"), 32),
        name="mod",
    )(s8, w_mod, b_mod.reshape(depth, 1, n))


def _inproj_kernel(x_ref, shift_ref, scale_ref, g_ref, w_ref, o_ref, h_scr, *, tm, rows, ctx_len):
    b = pl.program_id(0)
    i = pl.program_id(1)

    @pl.when(pl.program_id(2) == 0)
    def _():
        def norm_rows(c, carry):
            r0 = pl.multiple_of(c * rows, rows)
            x = x_ref[0, pl.ds(r0, rows), :]
            var = jnp.mean(x * x, axis=-1, keepdims=True)
            y = x * lax.rsqrt(var + EPS) * g_ref[...]
            row = i * tm + r0 + lax.broadcasted_iota(jnp.int32, (rows, 1), 0)
            is_ctx = row < ctx_len
            sh = _row_select(shift_ref, b, is_ctx)
            sc = _row_select(scale_ref, b, is_ctx)
            h_scr[pl.ds(r0, rows), :] = (y * (1.0 + sc) + sh).astype(BF16)
            return carry

        lax.fori_loop(0, tm // rows, norm_rows, 0)

    o_ref[0] = jnp.dot(h_scr[...], w_ref[...], preferred_element_type=F32).astype(BF16)


def _inproj_call(x, mod, layer, gain, w_bf16, ctx_len):
    bsz, l, d = x.shape
    n = w_bf16.shape[2]
    tm = _tile(l, 1056, 16)
    rows = _tile(tm, 352, 16)
    tn = 1024
    kern = functools.partial(_inproj_kernel, tm=tm, rows=rows, ctx_len=ctx_len)
    return pl.pallas_call(
        kern,
        grid=(bsz, l // tm, n // tn),
        in_specs=[pl.BlockSpec((1, tm, d), lambda b, i, j: (b, i, 0)),
                  pl.BlockSpec((None, 8, d), lambda b, i, j: (layer, 0, 0)),
                  pl.BlockSpec((None, 8, d), lambda b, i, j: (layer, 0, 1)),
                  pl.BlockSpec((1, d), lambda b, i, j: (0, 0)),
                  pl.BlockSpec((None, d, tn), lambda b, i, j: (layer, 0, j))],
        out_specs=pl.BlockSpec((1, tm, tn), lambda b, i, j: (b, i, j)),
        out_shape=jax.ShapeDtypeStruct((bsz, l, n), BF16),
        scratch_shapes=[pltpu.VMEM((tm, d), BF16)],
        compiler_params=_cparams(("parallel", "parallel", "arbitrary"), 48),
        name="inproj",
    )(x, mod, mod, gain.reshape(1, d), w_bf16)


def _norm_rope(t, gain, cos, sin_signed, lane_lo):
    var = jnp.mean(t * t, axis=-1, keepdims=True)
    y = t * lax.rsqrt(var + EPS) * gain
    swapped = jnp.where(lane_lo, pltpu.roll(y, HEAD_DIM - ROPE_PAIRS, 1), pltpu.roll(y, ROPE_PAIRS, 1))
    return y * cos + swapped * sin_signed


def _qkprep_kernel(q_ref, k_ref, v_ref, cos_ref, sin_ref, qg_ref, kg_ref, qo_ref, ko_ref, vo_ref):
    cos = cos_ref[...]
    sin = sin_ref[...]
    lane = lax.broadcasted_iota(jnp.int32, cos.shape, 1)
    lane_lo = (lane & ROPE_PAIRS) == 0
    qg = qg_ref[...]
    kg = kg_ref[...]
    for h in range(N_HEADS):
        sl = slice(h * HEAD_DIM, (h + 1) * HEAD_DIM)
        t = q_ref[0, :, sl].astype(F32)
        qo_ref[0, :, sl] = (_norm_rope(t, qg, cos, sin, lane_lo) * Q_SCALE).astype(BF16)
    k_tail = jnp.where(lane == 0, 1.0, 0.0).astype(BF16)
    v_tail = jnp.ones(cos.shape, BF16)
    for h in range(N_KV_HEADS):
        sl = slice(h * HEAD_DIM, (h + 1) * HEAD_DIM)
        lo = slice(2 * h * HEAD_DIM, (2 * h + 1) * HEAD_DIM)
        hi = slice((2 * h + 1) * HEAD_DIM, (2 * h + 2) * HEAD_DIM)
        t = k_ref[0, :, sl].astype(F32)
        ko_ref[0, :, lo] = _norm_rope(t, kg, cos, sin, lane_lo).astype(BF16)
        ko_ref[0, :, hi] = k_tail
        vo_ref[0, :, lo] = v_ref[0, :, sl]
        vo_ref[0, :, hi] = v_tail


def _qkprep_call(p, cos, sin_signed, q_gain, k_gain):
    bsz, l, _ = p.shape
    qw = N_HEADS * HEAD_DIM
    kw = N_KV_HEADS * HEAD_DIM
    tm = _tile(l, 768, 16)
    return pl.pallas_call(
        _qkprep_kernel,
        grid=(bsz, l // tm),
        in_specs=[pl.BlockSpec((1, tm, qw), lambda b, i: (b, i, 0)),
                  pl.BlockSpec((1, tm, kw), lambda b, i: (b, i, qw // kw)),
                  pl.BlockSpec((1, tm, kw), lambda b, i: (b, i, qw // kw + 1)),
                  pl.BlockSpec((tm, HEAD_DIM), lambda b, i: (i, 0)),
                  pl.BlockSpec((tm, HEAD_DIM), lambda b, i: (i, 0)),
                  pl.BlockSpec((1, HEAD_DIM), lambda b, i: (0, 0)),
                  pl.BlockSpec((1, HEAD_DIM), lambda b, i: (0, 0))],
        out_specs=[pl.BlockSpec((1, tm, qw), lambda b, i: (b, i, 0)),
                   pl.BlockSpec((1, tm, 2 * kw), lambda b, i: (b, i, 0)),
                   pl.BlockSpec((1, tm, 2 * kw), lambda b, i: (b, i, 0))],
        out_shape=[jax.ShapeDtypeStruct((bsz, l, qw), BF16),
                   jax.ShapeDtypeStruct((bsz, l, 2 * kw), BF16),
                   jax.ShapeDtypeStruct((bsz, l, 2 * kw), BF16)],
        compiler_params=_cparams(("parallel", "parallel"), 40),
        name="qkprep",
    )(p, p, p, cos, sin_signed, q_gain.reshape(1, HEAD_DIM), k_gain.reshape(1, HEAD_DIM))


def _attn_kernel(q_ref, k_ref, v_ref, qg_ref, kg_ref,
                 x_ref, xp_ref, xn_ref, cw_ref, cb_ref, wa_ref, ba_ref, wx_ref, bx_ref, lam_ref,
                 o_ref, h_ref, qs_scr, acc_scr, sa_scr, sb_scr, m_scr, l_scr, carry_scr,
                 *, tq, tkc, ctx_len, n_lat_chunks, reverse, n_sub):
    qi = pl.program_id(2)
    q = q_ref[0]
    gq = jnp.max(jnp.abs(qg_ref[...]), axis=-1, keepdims=True)
    gk = jnp.max(jnp.abs(kg_ref[...]), axis=-1, keepdims=True)
    bound = (SHIFT_MARGIN * HEAD_DIM * Q_SCALE) * gq * gk
    lane = lax.broadcasted_iota(jnp.int32, (1, HEAD_DIM), 1)
    tail = jnp.where(lane == 0, -bound, 0.0).astype(BF16)
    for h in range(GROUP):
        qs_scr[h * tq:(h + 1) * tq, :HEAD_DIM] = q[:, h * HEAD_DIM:(h + 1) * HEAD_DIM]
        qs_scr[h * tq:(h + 1) * tq, HEAD_DIM:] = jnp.broadcast_to(tail, (tq, HEAD_DIM))
    n = jnp.where(qi < ctx_len // tq, 0, n_lat_chunks)

    def lat_start(c):
        return pl.multiple_of(ctx_len + c * tkc, math.gcd(ctx_len, tkc))

    def write(out):
        for h in range(GROUP):
            o_ref[0, :, h * HEAD_DIM:(h + 1) * HEAD_DIM] = out[h * tq:(h + 1) * tq, :].astype(BF16)

    def scores(start, size):
        return lax.dot_general(qs_scr[...], k_ref[0, pl.ds(start, size), :], (((1,), (1,)), ((), ())),
                               preferred_element_type=F32)

    def accumulate(s, start, size):
        p = jnp.exp2(s).astype(BF16)
        acc_scr[...] += jnp.dot(p, v_ref[0, pl.ds(start, size), :], preferred_element_type=F32)

    acc_scr[...] = jnp.zeros(acc_scr.shape, F32)
    sa_scr[...] = scores(lat_start(0), tkc)
    accumulate(scores(0, ctx_len), 0, ctx_len)

    def chunk_pair(j, look_ahead):
        c0 = 2 * j
        sb_scr[...] = scores(lat_start(c0 + 1), tkc)
        accumulate(sa_scr[...], lat_start(c0), tkc)
        if look_ahead:
            sa_scr[...] = scores(lat_start(c0 + 2), tkc)
        accumulate(sb_scr[...], lat_start(c0 + 1), tkc)

    def shifted_body(j, carry):
        chunk_pair(j, look_ahead=True)
        return carry

    lax.fori_loop(0, n // 2 - 1, shifted_body, 0)

    def lru_tile():
        _lru_tile(x_ref, xp_ref, xn_ref, cw_ref, cb_ref, wa_ref, ba_ref, wx_ref, bx_ref, lam_ref,
                  h_ref, carry_scr, reverse=reverse, tt=tq, ctx_len=ctx_len, n_sub=n_sub)

    @pl.when(n > 0)
    def _():
        lru_tile()
        chunk_pair(n // 2 - 1, look_ahead=False)

    @pl.when(n == 0)
    def _():
        lru_tile()

    acc = acc_scr[...]
    den = acc[:, HEAD_DIM:]
    write(acc[:, :HEAD_DIM] / den)

    @pl.when(jnp.logical_not(jnp.min(den) >= MIN_DENOM))
    def _():
        m_scr[...] = jnp.full(m_scr.shape, NEG_INF, F32)
        l_scr[...] = jnp.zeros(l_scr.shape, F32)
        acc_scr[...] = jnp.zeros(acc_scr.shape, F32)

        def online_chunk(start, size):
            k = k_ref[0, pl.ds(start, size), :HEAD_DIM]
            v = v_ref[0, pl.ds(start, size), :HEAD_DIM]
            s = lax.dot_general(qs_scr[:, :HEAD_DIM], k, (((1,), (1,)), ((), ())),
                                preferred_element_type=F32)
            m_old = m_scr[...]
            m_new = jnp.maximum(m_old, jnp.max(s, axis=-1, keepdims=True))
            alpha = jnp.exp2(m_old - m_new)
            p = jnp.exp2(s - m_new)
            l_scr[...] = alpha * l_scr[...] + jnp.sum(p, axis=-1, keepdims=True)
            acc_scr[:, :HEAD_DIM] = alpha * acc_scr[:, :HEAD_DIM] + jnp.dot(
                p.astype(BF16), v, preferred_element_type=F32)
            m_scr[...] = m_new

        online_chunk(0, ctx_len)

        def online_body(c, carry):
            online_chunk(lat_start(c), tkc)
            return carry

        lax.fori_loop(0, n, online_body, 0)
        write(acc_scr[:, :HEAD_DIM] / l_scr[...])


def _attn_lru_call(qn, ke, ve, q_gain, k_gain, p, conv_w, conv_b, wa, ba, wx, bx, lam, ctx_len, reverse):
    bsz, l, qw = qn.shape
    c = conv_w.shape[1]
    tq = 256
    assert ctx_len % tq == 0 and l % tq == 0
    nt = l // tq
    t_lat = l - ctx_len
    tkc = _tile(t_lat // 2, 2048, 256)
    gw = GROUP * HEAD_DIM
    ew = 2 * HEAD_DIM
    n_half = N_KV_HEADS // 2
    head0 = n_half if reverse else 0
    cw = c // n_half
    n_sub = cw // LRU_BLOCK
    x_col0 = (N_HEADS + 2 * N_KV_HEADS) * HEAD_DIM // cw
    hb = tq // HALO
    n_hblk = l // HALO

    def tile_of(s):
        return jnp.where(s == 0, 0, nt - s) if reverse else s

    vec_map = lambda b, g, s: (0, g)
    w_spec = pl.BlockSpec((n_sub, LRU_BLOCK, LRU_BLOCK), lambda b, g, s: (g, 0, 0))
    kern = functools.partial(_attn_kernel, tq=tq, tkc=tkc, ctx_len=ctx_len, n_lat_chunks=t_lat // tkc,
                             reverse=reverse, n_sub=n_sub)
    return pl.pallas_call(
        kern,
        grid=(bsz, n_half, nt),
        in_specs=[pl.BlockSpec((1, tq, gw), lambda b, g, s: (b, s, head0 + g)),
                  pl.BlockSpec((1, l, ew), lambda b, g, s: (b, 0, head0 + g)),
                  pl.BlockSpec((1, l, ew), lambda b, g, s: (b, 0, head0 + g)),
                  pl.BlockSpec((1, HEAD_DIM), lambda b, g, s: (0, 0)),
                  pl.BlockSpec((1, HEAD_DIM), lambda b, g, s: (0, 0)),
                  pl.BlockSpec((1, tq, cw), lambda b, g, s: (b, tile_of(s), x_col0 + g)),
                  pl.BlockSpec((1, HALO, cw),
                               lambda b, g, s: (b, jnp.maximum(tile_of(s) * hb - 1, 0), x_col0 + g)),
                  pl.BlockSpec((1, HALO, cw),
                               lambda b, g, s: (b, jnp.minimum((tile_of(s) + 1) * hb, n_hblk - 1), x_col0 + g)),
                  pl.BlockSpec((LRU_CONV, cw), vec_map),
                  pl.BlockSpec((1, cw), vec_map),
                  w_spec,
                  pl.BlockSpec((1, cw), vec_map),
                  w_spec,
                  pl.BlockSpec((1, cw), vec_map),
                  pl.BlockSpec((1, cw), vec_map)],
        out_specs=[pl.BlockSpec((1, tq, gw), lambda b, g, s: (b, s, g)),
                   pl.BlockSpec((1, tq, cw), lambda b, g, s: (b, tile_of(s), g))],
        out_shape=[jax.ShapeDtypeStruct((bsz, l, qw // 2), BF16),
                   jax.ShapeDtypeStruct((bsz, l, c), BF16)],
        scratch_shapes=[pltpu.VMEM((GROUP * tq, ew), BF16),
                        pltpu.VMEM((GROUP * tq, ew), F32),
                        pltpu.VMEM((GROUP * tq, tkc), F32),
                        pltpu.VMEM((GROUP * tq, tkc), F32),
                        pltpu.VMEM((GROUP * tq, 1), F32),
                        pltpu.VMEM((GROUP * tq, 1), F32),
                        pltpu.VMEM((1, cw), F32)],
        compiler_params=_cparams(("parallel", "parallel", "arbitrary"), 58),
        name="attn_lru_rev" if reverse else "attn_lru_fwd",
    )(qn, ke, ve, q_gain.reshape(1, HEAD_DIM), k_gain.reshape(1, HEAD_DIM),
      p, p, p, conv_w, conv_b.reshape(1, c), wa.astype(BF16), ba.reshape(1, c),
      wx.astype(BF16), bx.reshape(1, c), lam.reshape(1, c))


def _segment_bounds(row, ctx_len, seq_len):
    is_ctx = row < ctx_len
    first = jnp.where(is_ctx, 0, ctx_len)
    last = jnp.where(is_ctx, ctx_len - 1, seq_len - 1)
    return first, last


def _shift_down(x, prev, k, local):
    y = pltpu.roll(x, k, 0)
    for r in range(k):
        y = jnp.where(local == r, prev[HALO - k + r:HALO - k + r + 1, :], y)
    return y


def _shift_up(x, nxt, k, local, tt):
    y = pltpu.roll(x, tt - k, 0)
    for r in range(k):
        y = jnp.where(local == tt - k + r, nxt[r:r + 1, :], y)
    return y


def _halo_shifts(x, prev, nxt):
    tt = x.shape[0]
    sub = lax.broadcasted_iota(jnp.int32, (SUBLANES, x.shape[1]), 0)

    def down(k):
        y = pltpu.roll(x, k, 0)
        head = y[:SUBLANES]
        for r in range(k):
            head = jnp.where(sub == r, prev[HALO - k + r:HALO - k + r + 1, :], head)
        return jnp.concatenate([head, y[SUBLANES:]], axis=0)

    def up(k):
        y = pltpu.roll(x, tt - k, 0)
        tail = y[tt - SUBLANES:]
        for r in range(k):
            tail = jnp.where(sub == SUBLANES - k + r, nxt[r:r + 1, :], tail)
        return jnp.concatenate([y[:tt - SUBLANES], tail], axis=0)

    return down, up


def _scan_tile(a, d, carry, reverse):
    n_groups = a.shape[0] // SUBLANES
    sub = lax.broadcasted_iota(jnp.int32, (SUBLANES, a.shape[1]), 0)
    groups = []
    for v in range(n_groups):
        av = a[v * SUBLANES:(v + 1) * SUBLANES]
        dv = d[v * SUBLANES:(v + 1) * SUBLANES]
        for k in (1, 2, 4):
            keep = sub < SUBLANES - k if reverse else sub >= k
            shift = SUBLANES - k if reverse else k
            a_n = jnp.where(keep, pltpu.roll(av, shift, 0), 1.0)
            d_n = jnp.where(keep, pltpu.roll(dv, shift, 0), 0.0)
            dv = av * d_n + dv
            av = av * a_n
        groups.append((av, dv))
    hs = [None] * n_groups
    for v in (reversed(range(n_groups)) if reverse else range(n_groups)):
        av, dv = groups[v]
        hv = dv + av * carry
        carry = hv[0:1] if reverse else hv[SUBLANES - 1:SUBLANES]
        hs[v] = hv
    return jnp.concatenate(hs, axis=0), carry


def _lru_tile(x_ref, xp_ref, xn_ref, cw_ref, cb_ref, wa_ref, ba_ref, wx_ref, bx_ref, lam_ref,
              o_ref, carry_scr, *, reverse, tt, ctx_len, n_sub):
    s = pl.program_id(2)
    nt = pl.num_programs(2)
    ti = jnp.where(s == 0, 0, nt - s) if reverse else s

    @pl.when(s == 0)
    def _():
        carry_scr[...] = jnp.zeros(carry_scr.shape, F32)

    prev_ok = jnp.logical_and(ti != 0, ti * tt != ctx_len)
    next_ok = jnp.logical_and(ti != nt - 1, (ti + 1) * tt != ctx_len)
    x = x_ref[0].astype(F32)
    xp = jnp.where(prev_ok, xp_ref[0].astype(F32), 0.0)
    xn = jnp.where(next_ok, xn_ref[0].astype(F32), 0.0)
    down, up = _halo_shifts(x, xp, xn)
    cw = cw_ref[...]
    u = cb_ref[...] + cw[2:3, :] * x + cw[0:1, :] * down(2) + cw[1:2, :] * down(1) + cw[3:4, :] * up(1)
    nlam = -lam_ref[...]
    softplus = jnp.maximum(nlam, 0.0) + jnp.log1p(jnp.exp(-jnp.abs(nlam)))

    for j in range(n_sub):
        ls = slice(j * LRU_BLOCK, (j + 1) * LRU_BLOCK)
        uj = u[:, ls]
        ub = uj.astype(BF16)
        r = jax.nn.sigmoid(jnp.dot(ub, wa_ref[j], preferred_element_type=F32) + ba_ref[:, ls])
        i = jax.nn.sigmoid(jnp.dot(ub, wx_ref[j], preferred_element_type=F32) + bx_ref[:, ls])
        a = jnp.exp((-LRU_C) * r * softplus[:, ls])
        d = jnp.sqrt(1.0 - a * a) * (i * uj)
        h, carry = _scan_tile(a, d, carry_scr[:, ls], reverse)
        carry_scr[:, ls] = carry
        o_ref[0, :, ls] = h.astype(BF16)


def _sconv_kernel(bg_ref, cg_ref, u_ref, cgp_ref, up_ref, cgn_ref, un_ref, w_ref, b_ref,
                  hf_ref, hr_ref, lg_ref, o_ref, lru_ref, *, tt, ctx_len, seq_len):
    h = hf_ref[0].astype(F32) + hr_ref[0].astype(F32)
    lru_ref[0] = (h * _gelu(lg_ref[0].astype(F32))).astype(BF16)
    ti = pl.program_id(1)
    z = cg_ref[0].astype(F32) * u_ref[0].astype(F32)
    zp = cgp_ref[0].astype(F32) * up_ref[0].astype(F32)
    zn = cgn_ref[0].astype(F32) * un_ref[0].astype(F32)
    local = lax.broadcasted_iota(jnp.int32, (tt, 1), 0)
    row = ti * tt + local
    first, last = _segment_bounds(row, ctx_len, seq_len)
    w = w_ref[...]
    y = b_ref[...] + w[1:2, :] * z
    y = y + w[0:1, :] * jnp.where(row - 1 >= first, _shift_down(z, zp, 1, local), 0.0)
    y = y + w[2:3, :] * jnp.where(row + 1 <= last, _shift_up(z, zn, 1, local, tt), 0.0)
    o_ref[0] = (bg_ref[0].astype(F32) * y).astype(BF16)


def _sconv_call(p, w, b, h_fwd, h_rev, ctx_len):
    bsz, l, _ = p.shape
    c = w.shape[1]
    tt = _tile(l, 768, 16)
    tc = 512
    ncb = c // tc
    lg_col0 = (N_HEADS * HEAD_DIM + 2 * N_KV_HEADS * HEAD_DIM + c) // tc
    b_col0 = lg_col0 + ncb
    tile = pl.BlockSpec((1, tt, tc), lambda b_, i, j: (b_, i, j))
    c_col0 = b_col0 + ncb
    u_col0 = c_col0 + ncb
    hb = tt // HALO
    n_hblk = l // HALO
    prev = lambda i: jnp.maximum(i * hb - 1, 0)
    nxt = lambda i: jnp.minimum((i + 1) * hb, n_hblk - 1)
    kern = functools.partial(_sconv_kernel, tt=tt, ctx_len=ctx_len, seq_len=l)
    return pl.pallas_call(
        kern,
        grid=(bsz, l // tt, ncb),
        in_specs=[pl.BlockSpec((1, tt, tc), lambda b_, i, j: (b_, i, b_col0 + j)),
                  pl.BlockSpec((1, tt, tc), lambda b_, i, j: (b_, i, c_col0 + j)),
                  pl.BlockSpec((1, tt, tc), lambda b_, i, j: (b_, i, u_col0 + j)),
                  pl.BlockSpec((1, HALO, tc), lambda b_, i, j: (b_, prev(i), c_col0 + j)),
                  pl.BlockSpec((1, HALO, tc), lambda b_, i, j: (b_, prev(i), u_col0 + j)),
                  pl.BlockSpec((1, HALO, tc), lambda b_, i, j: (b_, nxt(i), c_col0 + j)),
                  pl.BlockSpec((1, HALO, tc), lambda b_, i, j: (b_, nxt(i), u_col0 + j)),
                  pl.BlockSpec((SC_CONV, tc), lambda b_, i, j: (0, j)),
                  pl.BlockSpec((1, tc), lambda b_, i, j: (0, j)),
                  tile, tile,
                  pl.BlockSpec((1, tt, tc), lambda b_, i, j: (b_, i, lg_col0 + j))],
        out_specs=[tile, tile],
        out_shape=[jax.ShapeDtypeStruct((bsz, l, c), BF16), jax.ShapeDtypeStruct((bsz, l, c), BF16)],
        compiler_params=_cparams(("parallel", "parallel", "parallel"), 32),
        name="sconv",
    )(p, p, p, p, p, p, p, w, b.reshape(1, c), h_fwd, h_rev, p)


def _merge_kernel(xa0_ref, xa1_ref, xl_ref, xs_ref, ga_ref, gl_ref, gs_ref, wa0_ref, wa1_ref, wl_ref, ws_ref,
                  o_ref):
    att = (jnp.dot(xa0_ref[0], wa0_ref[...], preferred_element_type=F32)
           + jnp.dot(xa1_ref[0], wa1_ref[...], preferred_element_type=F32))
    y = jax.nn.sigmoid(ga_ref[0].astype(F32)) * att
    y = y + jax.nn.sigmoid(gl_ref[0].astype(F32)) * jnp.dot(xl_ref[0], wl_ref[...], preferred_element_type=F32)
    y = y + jax.nn.sigmoid(gs_ref[0].astype(F32)) * jnp.dot(xs_ref[0], ws_ref[...], preferred_element_type=F32)
    o_ref[0] = y.astype(BF16)


def _merge_call(x_att0, x_att1, x_lru, x_sc, p, layer, w_att, w_lru, w_sc):
    bsz, l, d = x_lru.shape
    half = x_att0.shape[2]
    tm = _tile(l, 768, 16)
    tn = 512
    g_col0 = (p.shape[2] - 3 * d) // tn
    nj = d // tn
    xspec = pl.BlockSpec((1, tm, d), lambda b, i, j: (b, i, 0))
    xhalf = pl.BlockSpec((1, tm, half), lambda b, i, j: (b, i, 0))
    wspec = pl.BlockSpec((None, d, tn), lambda b, i, j: (layer, 0, j))
    whalf = lambda k: pl.BlockSpec((None, half, tn), lambda b, i, j: (layer, k, j))
    gspec = lambda k: pl.BlockSpec((1, tm, tn), lambda b, i, j: (b, i, g_col0 + k * nj + j))
    return pl.pallas_call(
        _merge_kernel,
        grid=(bsz, l // tm, nj),
        in_specs=[xhalf, xhalf, xspec, xspec, gspec(0), gspec(1), gspec(2), whalf(0), whalf(1), wspec, wspec],
        out_specs=pl.BlockSpec((1, tm, tn), lambda b, i, j: (b, i, j)),
        out_shape=jax.ShapeDtypeStruct((bsz, l, d), BF16),
        compiler_params=_cparams(("parallel", "parallel", "arbitrary"), 48),
        name="merge",
    )(x_att0, x_att1, x_lru, x_sc, p, p, p, w_att, w_att, w_lru, w_sc)


def _outproj_kernel(y_ref, w_ref, x_ref, gate_ref, shift_ref, scale_ref, g_ref, xo_ref, ht_ref,
                    *, tm, ctx_len):
    b = pl.program_id(0)
    i = pl.program_id(1)
    row = i * tm + lax.broadcasted_iota(jnp.int32, (tm, 1), 0)
    is_ctx = row < ctx_len
    acc = jnp.dot(y_ref[0], w_ref[...], preferred_element_type=F32)
    xn = x_ref[0] + _row_select(gate_ref, b, is_ctx) * acc
    xo_ref[0] = xn
    var = jnp.mean(xn * xn, axis=-1, keepdims=True)
    h = xn * lax.rsqrt(var + EPS) * g_ref[...]
    h = h * (1.0 + _row_select(scale_ref, b, is_ctx)) + _row_select(shift_ref, b, is_ctx)
    ht_ref[0] = h.T.astype(BF16)


def _outproj_call(y, w_out, x, mod, layer, gain, ctx_len):
    bsz, l, d = x.shape
    tm = _tile(l, 384, LANES)
    kern = functools.partial(_outproj_kernel, tm=tm, ctx_len=ctx_len)
    mspec = lambda k: pl.BlockSpec((None, 8, d), lambda b, i: (layer, 0, k))
    return pl.pallas_call(
        kern,
        grid=(bsz, l // tm),
        in_specs=[pl.BlockSpec((1, tm, d), lambda b, i: (b, i, 0)),
                  pl.BlockSpec((None, d, d), lambda b, i: (layer, 0, 0)),
                  pl.BlockSpec((1, tm, d), lambda b, i: (b, i, 0)),
                  mspec(2), mspec(3), mspec(4),
                  pl.BlockSpec((1, d), lambda b, i: (0, 0))],
        out_specs=[pl.BlockSpec((1, tm, d), lambda b, i: (b, i, 0)),
                   pl.BlockSpec((1, d, tm), lambda b, i: (b, 0, i))],
        out_shape=[jax.ShapeDtypeStruct((bsz, l, d), F32),
                   jax.ShapeDtypeStruct((bsz, d, l), BF16)],
        compiler_params=_cparams(("parallel", "parallel"), 48),
        name="outproj",
    )(y, w_out, x, mod, mod, mod, gain.reshape(1, d))


_CAND_ROWS = tuple(PEER_TOPK // (i + 1) for i in range(PEER_TOPK))


def _top16(s, v_scr):
    n = s.shape[0]
    key = lax.broadcasted_iota(jnp.int32, s.shape, 0)

    def body(r, carry):
        work, rank = carry
        m = jnp.max(work, axis=0, keepdims=True)
        v_scr[pl.ds(r, 1), :] = m
        first = jnp.min(jnp.where(work == m, key, n), axis=0, keepdims=True)
        sel = key == first
        return jnp.where(sel, NEG_INF, work), jnp.where(sel, jnp.asarray(r, F32), rank)

    _, rank = lax.fori_loop(0, PEER_TOPK, body, (s, jnp.full(s.shape, float(PEER_TOPK), F32)))
    return rank


MARK = -(2.0 ** 127)


def _top16_distinct(ss, v_scrs):
    def body(r, works):
        mark = jnp.asarray(r, F32) * (MARK / 32.0) + MARK
        out = []
        for work, v_scr in zip(works, v_scrs):
            m = jnp.max(work, axis=0, keepdims=True)
            v_scr[pl.ds(r, 1), :] = m
            out.append(jnp.where(work == m, mark, work))
        return tuple(out)

    works = lax.fori_loop(0, PEER_TOPK, body, tuple(ss))
    return [jnp.where(w <= MARK, (MARK - w) * (-32.0 / MARK), float(PEER_TOPK)) for w in works]


def _peer_topk_kernel(ht_ref, wq_ref, k1_ref, k2_ref, cnt_ref, e1_ref, rk_ref, e2_ref,
                      q_scr, v_scr, *, tm, n_par):
    half = PEER_NKEYS
    q_scr[...] = jnp.dot(wq_ref[...], ht_ref[0], preferred_element_type=F32).astype(BF16)
    sub = lax.broadcasted_iota(jnp.int32, (SUBLANES, LANES), 0)
    big = PEER_TOPK * PEER_TOPK

    def candidates(v1, v2):
        pieces, poss = [], []
        for i in range(SUBLANES):
            for j0 in range(0, _CAND_ROWS[i], SUBLANES):
                c = v1[i:i + 1, :] + v2[j0:j0 + SUBLANES, :]
                valid = sub + j0 < _CAND_ROWS[i]
                pieces.append(jnp.where(valid, c, NEG_INF))
                poss.append(jnp.where(valid, i * PEER_TOPK + j0 + sub, big))
        pieces.append(v1[SUBLANES:, :] + v2[0:1, :])
        poss.append((sub + SUBLANES) * PEER_TOPK)
        return pieces, poss

    def pick_exact(pieces, poss):
        def pick(_, carry2):
            cs, sels = carry2
            m = functools.reduce(jnp.maximum, cs)
            m = jnp.max(m, axis=0, keepdims=True)
            cand_pos = functools.reduce(jnp.minimum, [jnp.where(c == m, p_, big) for c, p_ in zip(cs, poss)])
            first = jnp.min(cand_pos, axis=0, keepdims=True)
            hit = [p_ == first for p_ in poss]
            cs = tuple(jnp.where(hh, NEG_INF, c) for hh, c in zip(hit, cs))
            sels = tuple(jnp.where(hh, 1.0, s_) for hh, s_ in zip(hit, sels))
            return cs, sels

        zeros = tuple(jnp.zeros((SUBLANES, LANES), F32) for _ in pieces)
        return lax.fori_loop(0, PEER_TOPK, pick, (tuple(pieces), zeros))[1]

    def pick_distinct(groups):
        npc = len(groups[0])

        def pick(_, cs):
            out = []
            for g in range(len(groups)):
                grp = cs[g * npc:(g + 1) * npc]
                m = jnp.max(functools.reduce(jnp.maximum, grp), axis=0, keepdims=True)
                out.extend(jnp.where(c == m, MARK, c) for c in grp)
            return tuple(out)

        marked = lax.fori_loop(0, PEER_TOPK, pick, tuple(c for grp in groups for c in grp))
        return [tuple(jnp.where(c == MARK, 1.0, 0.0) for c in marked[g * npc:(g + 1) * npc])
                for g in range(len(groups))]

    def compute(h, lss, exact):
        base = pl.multiple_of(h * 2 * half, 2 * half)
        n_g = len(lss)
        s1s = [jnp.dot(k1_ref[h], q_scr[pl.ds(base, half), ls], preferred_element_type=F32) for ls in lss]
        s2s = [jnp.dot(k2_ref[h], q_scr[pl.ds(base + half, half), ls], preferred_element_type=F32)
               for ls in lss]
        v1_refs = [v_scr.at[2 * g] for g in range(n_g)]
        v2_refs = [v_scr.at[2 * g + 1] for g in range(n_g)]
        if exact:
            rank1s = [_top16(s, r) for s, r in zip(s1s, v1_refs)]
            rank2s = [_top16(s, r) for s, r in zip(s2s, v2_refs)]
        else:
            ranks = _top16_distinct(tuple(s1s + s2s), tuple(v1_refs + v2_refs))
            rank1s, rank2s = ranks[:n_g], ranks[n_g:]
        v1s = [r[...] for r in v1_refs]
        v2s = [r[...] for r in v2_refs]
        cands = [candidates(v1, v2) for v1, v2 in zip(v1s, v2s)]
        if exact:
            all_sels = [pick_exact(pieces, poss) for pieces, poss in cands]
        else:
            all_sels = pick_distinct([pieces for pieces, _ in cands])
        most = None
        for g in range(n_g):
            most_g = finish(h, lss[g], s1s[g], s2s[g], rank1s[g], rank2s[g], v1s[g], v2s[g],
                            cands[g][0], all_sels[g])
            most = most_g if most is None else jnp.maximum(most, most_g)
        return most

    def finish(h, ls, s1, s2, rank1, rank2, v1, v2, orig, sels):
        top = v1[0:1, :] + v2[0:1, :]
        zsum = functools.reduce(
            lambda a_, b_: a_ + b_,
            [jnp.where(s_ > 0.0, jnp.exp(o - top), 0.0) for s_, o in zip(sels, orig)])
        zinv = 1.0 / jnp.sum(zsum, axis=0, keepdims=True)

        counts = []
        pi = 0
        for i in range(SUBLANES):
            c = None
            for j0 in range(0, _CAND_ROWS[i], SUBLANES):
                part = jnp.sum(sels[pi], axis=0, keepdims=True)
                c = part if c is None else c + part
                pi += 1
            counts.append(c)
        tail = sels[pi]
        for i in range(SUBLANES, PEER_TOPK):
            counts.append(tail[i - SUBLANES:i - SUBLANES + 1, :])
        cnt = jnp.zeros(rank1.shape, F32)
        for i in range(PEER_TOPK):
            cnt = jnp.where(rank1 == float(i), counts[i], cnt)

        cnt_ref[0, h, :, ls] = cnt
        e1_ref[0, h, :, ls] = jnp.exp(s1 - v1[0:1, :]) * zinv
        rk_ref[0, h, :, ls] = rank2.astype(BF16)
        e2_ref[0, h, :, ls] = jnp.exp(s2 - v2[0:1, :]).astype(BF16)
        ranked = jnp.where(rank1 < PEER_TOPK, 1.0, 0.0)
        ranked = jnp.maximum(jnp.sum(ranked, axis=0, keepdims=True),
                             jnp.sum(jnp.where(rank2 < PEER_TOPK, 1.0, 0.0), axis=0, keepdims=True))
        picked = jnp.sum(functools.reduce(lambda a_, b_: a_ + b_, sels), axis=0, keepdims=True)
        return jnp.max(jnp.maximum(ranked, picked))

    n_trips = tm // (n_par * LANES)

    def head_lane_groups(idx, carry):
        h = idx // n_trips
        first = (idx % n_trips) * n_par
        lss = [pl.ds(pl.multiple_of((first + g) * LANES, LANES), LANES) for g in range(n_par)]
        most = compute(h, lss, exact=False)

        @pl.when(most > PEER_TOPK)
        def _():
            compute(h, lss, exact=True)

        return carry

    lax.fori_loop(0, PEER_HEADS * n_trips, head_lane_groups, 0)


def _peer_topk_call(ht, layer, wq_t, k1, k2):
    bsz, d, l = ht.shape
    qd = wq_t.shape[1]
    n_par = 2
    tm = _tile(l, 768, n_par * LANES)
    kern = functools.partial(_peer_topk_kernel, tm=tm, n_par=n_par)
    ospec = pl.BlockSpec((1, PEER_HEADS, PEER_NKEYS, tm), lambda b, i: (b, 0, 0, i))
    oshape = jax.ShapeDtypeStruct((bsz, PEER_HEADS, PEER_NKEYS, l), F32)
    kspec = pl.BlockSpec((None, PEER_HEADS, PEER_NKEYS, PEER_NKEYS), lambda b, i: (layer, 0, 0, 0))
    return pl.pallas_call(
        kern,
        grid=(bsz, l // tm),
        in_specs=[pl.BlockSpec((1, d, tm), lambda b, i: (b, 0, i)),
                  pl.BlockSpec((None, qd, d), lambda b, i: (layer, 0, 0)),
                  kspec, kspec],
        out_specs=[ospec, ospec, ospec, ospec],
        out_shape=[oshape, oshape, jax.ShapeDtypeStruct(oshape.shape, BF16),
                   jax.ShapeDtypeStruct(oshape.shape, BF16)],
        scratch_shapes=[pltpu.VMEM((qd, tm), BF16),
                        pltpu.VMEM((2 * n_par, PEER_TOPK, LANES), F32)],
        compiler_params=_cparams(("parallel", "parallel"), 56),
        name="peer_topk",
    )(ht, wq_t, k1, k2)


def _peer_dense_kernel(ht_ref, u_ref, vt_ref, cnt_ref, e1_ref, rk_ref, e2_ref, o_ref, wz_scr, *, n_sub):
    e = pl.program_id(2)

    @pl.when(e == 0)
    def _():
        o_ref[...] = jnp.zeros(o_ref.shape, F32)

    ht = ht_ref[0]
    per = 4
    n_split = n_sub // per
    for part in range(n_split):
        s = jnp.dot(u_ref[part * per * PEER_NKEYS:(part + 1) * per * PEER_NKEYS, :], ht,
                    preferred_element_type=F32)
        for a in range(part * per, (part + 1) * per):
            rows = slice(a * PEER_NKEYS, (a + 1) * PEER_NKEYS)
            local = slice((a - part * per) * PEER_NKEYS, (a - part * per + 1) * PEER_NKEYS)
            w = None
            for h in range(PEER_HEADS):
                cnt = cnt_ref[0, h, a:a + 1, :].astype(BF16)
                e1 = e1_ref[0, h, a:a + 1, :].astype(BF16)
                term = jnp.where(rk_ref[0, h] < cnt, e2_ref[0, h] * e1, 0.0)
                w = term if w is None else w + term
            wz_scr[rows, :] = w * _gelu(s[local, :]).astype(BF16)
    o_ref[0] += jnp.dot(vt_ref[...], wz_scr[...], preferred_element_type=F32)


def _peer_dense_call(ht, layer, u_bf16, vt_bf16, cnt, e1n, rank2, e2):
    bsz, d, l = ht.shape
    n_exp = u_bf16.shape[1]
    tm = _tile(l, 768, LANES)
    n_sub = SUBLANES
    te = n_sub * PEER_NKEYS
    kern = functools.partial(_peer_dense_kernel, n_sub=n_sub)
    aspec = pl.BlockSpec((1, PEER_HEADS, n_sub, tm), lambda b, i, e: (b, 0, e, i))
    fspec = pl.BlockSpec((1, PEER_HEADS, PEER_NKEYS, tm), lambda b, i, e: (b, 0, 0, i))
    return pl.pallas_call(
        kern,
        grid=(bsz, l // tm, n_exp // te),
        in_specs=[pl.BlockSpec((1, d, tm), lambda b, i, e: (b, 0, i)),
                  pl.BlockSpec((None, te, d), lambda b, i, e: (layer, e, 0)),
                  pl.BlockSpec((None, d, te), lambda b, i, e: (layer, 0, e)),
                  aspec, aspec, fspec, fspec],
        out_specs=pl.BlockSpec((1, d, tm), lambda b, i, e: (b, 0, i)),
        out_shape=jax.ShapeDtypeStruct((bsz, d, l), F32),
        scratch_shapes=[pltpu.VMEM((te, tm), BF16)],
        compiler_params=_cparams(("parallel", "parallel", "arbitrary"), 58),
        name="peer_dense",
    )(ht, u_bf16, vt_bf16, cnt, e1n, rank2, e2)


def _resid_kernel(x_ref, yt_ref, gate_ref, o_ref, *, tm, ctx_len, first_tile):
    b = pl.program_id(0)
    i = pl.program_id(1) + first_tile
    row = i * tm + lax.broadcasted_iota(jnp.int32, (tm, 1), 0)
    o_ref[0] = x_ref[0] + _row_select(gate_ref, b, row < ctx_len) * yt_ref[0].T


def _resid_call(x, yt, mod, layer, ctx_len, latent_only):
    bsz, l, d = x.shape
    tm = _tile(math.gcd(l, ctx_len), 384, LANES)
    first_tile = ctx_len // tm if latent_only else 0
    n_tiles = l // tm - first_tile
    kern = functools.partial(_resid_kernel, tm=tm, ctx_len=ctx_len, first_tile=first_tile)
    return pl.pallas_call(
        kern,
        grid=(bsz, n_tiles),
        in_specs=[pl.BlockSpec((1, tm, d), lambda b, i: (b, i + first_tile, 0)),
                  pl.BlockSpec((1, d, tm), lambda b, i: (b, 0, i + first_tile)),
                  pl.BlockSpec((None, 8, d), lambda b, i: (layer, 0, 5))],
        out_specs=pl.BlockSpec((1, tm, d), lambda b, i: (b, i, 0)),
        out_shape=jax.ShapeDtypeStruct((bsz, n_tiles * tm, d), F32),
        compiler_params=_cparams(("parallel", "parallel"), 40),
        name="resid",
    )(x, yt, mod)


def _rope_tables(ctx_len, t_lat):
    rows = t_lat // GRID_W
    row = jnp.repeat(jnp.arange(rows, dtype=F32), GRID_W)
    col = jnp.tile(jnp.arange(GRID_W, dtype=F32), rows)
    inv = ROPE_THETA ** (-jnp.arange(ROPE_PAIRS, dtype=F32) / ROPE_PAIRS)
    ang = jnp.concatenate([row[:, None] * inv] * 2 + [col[:, None] * inv] * 2, axis=1)
    ang = jnp.concatenate([jnp.zeros((ctx_len, HEAD_DIM), F32), ang], axis=0)
    sign = jnp.where((jnp.arange(HEAD_DIM) & ROPE_PAIRS) == 0, -1.0, 1.0).astype(F32)
    return jnp.cos(ang), jnp.sin(ang) * sign


def kernel(x, c, ctx, c_ctx, w_mod, b_mod, norm_mix, norm_ffn, w_in, q_norm, k_norm, lru_conv_w, lru_conv_b, lru_wa, lru_ba, lru_wx, lru_bx, lru_lambda, sc_conv_w, sc_conv_b, w_o_attn, w_o_lru, w_o_sc, w_out, peer_wq, peer_k1, peer_k2, peer_u, peer_v):
    bsz, t_lat, d = x.shape
    ctx_len = ctx.shape[1]
    depth = w_mod.shape[0]
    assert bsz == 2, "modulation rows are laid out as [latent 0, latent 1, context]"

    xs = jnp.concatenate([ctx, x], axis=1)
    s8 = jnp.concatenate([c, c_ctx[None, :], jnp.zeros((8 - bsz - 1, d), F32)], axis=0)
    mod = _mod_call(s8, w_mod, b_mod)
    cos, sin_signed = _rope_tables(ctx_len, t_lat)

    w_in_b = w_in.astype(BF16)
    w_att_b, w_lru_b, w_sc_b = w_o_attn.astype(BF16), w_o_lru.astype(BF16), w_o_sc.astype(BF16)
    w_out_b = w_out.astype(BF16)
    wq_t = jnp.swapaxes(peer_wq, 1, 2).astype(BF16)
    k1_b, k2_b = peer_k1.astype(BF16), peer_k2.astype(BF16)
    u_b = peer_u.astype(BF16)
    vt_b = jnp.swapaxes(peer_v, 1, 2).astype(BF16)

    for l in range(depth):
        p = _inproj_call(xs, mod, l, norm_mix[l], w_in_b, ctx_len)
        qn, ke, ve = _qkprep_call(p, cos, sin_signed, q_norm[l], k_norm[l])
        att_args = (qn, ke, ve, q_norm[l], k_norm[l], p, lru_conv_w[l], lru_conv_b[l])
        x_att0, h_fwd = _attn_lru_call(*att_args, lru_wa[l, 0], lru_ba[l, 0], lru_wx[l, 0], lru_bx[l, 0],
                                       lru_lambda[l, 0], ctx_len, reverse=False)
        x_att1, h_rev = _attn_lru_call(*att_args, lru_wa[l, 1], lru_ba[l, 1], lru_wx[l, 1], lru_bx[l, 1],
                                       lru_lambda[l, 1], ctx_len, reverse=True)
        x_sc, x_lru = _sconv_call(p, sc_conv_w[l], sc_conv_b[l], h_fwd, h_rev, ctx_len)
        y = _merge_call(x_att0, x_att1, x_lru, x_sc, p, l, w_att_b, w_lru_b, w_sc_b)
        xs, ht = _outproj_call(y, w_out_b, xs, mod, l, norm_ffn[l], ctx_len)
        cnt, e1n, rank2, e2 = _peer_topk_call(ht, l, wq_t, k1_b, k2_b)
        yt = _peer_dense_call(ht, l, u_b, vt_b, cnt, e1n, rank2, e2)
        xs = _resid_call(xs, yt, mod, l, ctx_len, latent_only=(l == depth - 1))
    return xs
```

```python
import functools
import math

import jax
import jax.numpy as jnp
from jax import lax
from jax.experimental import pallas as pl
from jax.experimental.pallas import tpu as pltpu

F32 = jnp.float32
BF16 = jnp.bfloat16

GRID_W = 64
EPS = 1e-6

N_HEADS = 16
N_KV_HEADS = 4
HEAD_DIM = 128
GROUP = N_HEADS // N_KV_HEADS
ROPE_PAIRS = HEAD_DIM // 4
ROPE_THETA = 10000.0

LRU_BLOCK = 128
LRU_CONV = 4
LRU_C = 8.0
SC_CONV = 3

PEER_HEADS = 8
PEER_NKEYS = 128
PEER_TOPK = 16

LANES = 128
SUBLANES = 8
HALO = SUBLANES
NEG_INF = float("-inf")
LOG2E = 1.4426950408889634
GELU_C = math.sqrt(2.0 / math.pi)
Q_SCALE = HEAD_DIM ** -0.5 * LOG2E
SHIFT_MARGIN = 1.02
MIN_DENOM = 2.0 ** -100
ATTN_CHUNKS = 4


def _tile(n, target, mult):
    best = None
    for t in range(mult, min(n, target) + 1, mult):
        if n % t == 0:
            best = t
    assert best is not None, (n, target, mult)
    return best


def _cparams(sem, vmem_mib):
    return pltpu.CompilerParams(dimension_semantics=sem, vmem_limit_bytes=vmem_mib << 20)


def _gelu(x):
    return 0.5 * x * (1.0 + jnp.tanh(GELU_C * (x + 0.044715 * (x * x * x))))


def _row_select(mod_ref, b, is_ctx):
    return jnp.where(is_ctx, mod_ref[2:3, :], mod_ref[pl.ds(b, 1), :])


def _mod_kernel(s_ref, w_ref, b_ref, o_ref):
    s = s_ref[...]
    s = s * jax.nn.sigmoid(s)
    o_ref[0] = jnp.dot(s, w_ref[0], preferred_element_type=F32,
                       precision=lax.Precision.HIGHEST) + b_ref[0]


def _mod_call(s8, w_mod, b_mod):
    depth, d, n = w_mod.shape
    tn = _tile(n, 1024, LANES)
    return pl.pallas_call(
        _mod_kernel,
        grid=(depth, n // tn),
        in_specs=[pl.BlockSpec((8, d), lambda l, j: (0, 0)),
                  pl.BlockSpec((1, d, tn), lambda l, j: (l, 0, j)),
                  pl.BlockSpec((1, 1, tn), lambda l, j: (l, 0, j))],
        out_specs=pl.BlockSpec((1, 8, tn), lambda l, j: (l, 0, j)),
        out_shape=jax.ShapeDtypeStruct((depth, 8, n), F32),
        compiler_params=_cparams(("parallel", "parallel"), 32),
        name="mod",
    )(s8, w_mod, b_mod.reshape(depth, 1, n))


def _inproj_kernel(x_ref, shift_ref, scale_ref, g_ref, w_ref, o_ref, h_scr, *, tm, rows, ctx_len):
    b = pl.program_id(0)
    i = pl.program_id(1)

    @pl.when(pl.program_id(2) == 0)
    def _():
        def norm_rows(c, carry):
            r0 = pl.multiple_of(c * rows, rows)
            x = x_ref[0, pl.ds(r0, rows), :]
            var = jnp.mean(x * x, axis=-1, keepdims=True)
            y = x * lax.rsqrt(var + EPS) * g_ref[...]
            row = i * tm + r0 + lax.broadcasted_iota(jnp.int32, (rows, 1), 0)
            is_ctx = row < ctx_len
            sh = _row_select(shift_ref, b, is_ctx)
            sc = _row_select(scale_ref, b, is_ctx)
            h_scr[pl.ds(r0, rows), :] = (y * (1.0 + sc) + sh).astype(BF16)
            return carry

        lax.fori_loop(0, tm // rows, norm_rows, 0)

    o_ref[0] = jnp.dot(h_scr[...], w_ref[...], preferred_element_type=F32).astype(BF16)


def _inproj_call(x, mod, layer, gain, w_bf16, ctx_len):
    bsz, l, d = x.shape
    n = w_bf16.shape[2]
    tm = _tile(l, 1056, 16)
    rows = _tile(tm, 352, 16)
    tn = 1024
    kern = functools.partial(_inproj_kernel, tm=tm, rows=rows, ctx_len=ctx_len)
    return pl.pallas_call(
        kern,
        grid=(bsz, l // tm, n // tn),
        in_specs=[pl.BlockSpec((1, tm, d), lambda b, i, j: (b, i, 0)),
                  pl.BlockSpec((None, 8, d), lambda b, i, j: (layer, 0, 0)),
                  pl.BlockSpec((None, 8, d), lambda b, i, j: (layer, 0, 1)),
                  pl.BlockSpec((1, d), lambda b, i, j: (0, 0)),
                  pl.BlockSpec((None, d, tn), lambda b, i, j: (layer, 0, j))],
        out_specs=pl.BlockSpec((1, tm, tn), lambda b, i, j: (b, i, j)),
        out_shape=jax.ShapeDtypeStruct((bsz, l, n), BF16),
        scratch_shapes=[pltpu.VMEM((tm, d), BF16)],
        compiler_params=_cparams(("parallel", "parallel", "arbitrary"), 48),
        name="inproj",
    )(x, mod, mod, gain.reshape(1, d), w_bf16)


def _norm_rope(t, gain, cos, sin_signed, lane_lo):
    var = jnp.mean(t * t, axis=-1, keepdims=True)
    y = t * lax.rsqrt(var + EPS) * gain
    swapped = jnp.where(lane_lo, pltpu.roll(y, HEAD_DIM - ROPE_PAIRS, 1), pltpu.roll(y, ROPE_PAIRS, 1))
    return y * cos + swapped * sin_signed


def _qkprep_kernel(q_ref, k_ref, v_ref, cos_ref, sin_ref, qg_ref, kg_ref, qo_ref, ko_ref, vo_ref):
    cos = cos_ref[...]
    sin = sin_ref[...]
    lane = lax.broadcasted_iota(jnp.int32, cos.shape, 1)
    lane_lo = (lane & ROPE_PAIRS) == 0
    qg = qg_ref[...]
    kg = kg_ref[...]
    for h in range(N_HEADS):
        sl = slice(h * HEAD_DIM, (h + 1) * HEAD_DIM)
        t = q_ref[0, :, sl].astype(F32)
        qo_ref[0, :, sl] = (_norm_rope(t, qg, cos, sin, lane_lo) * Q_SCALE).astype(BF16)
    k_tail = jnp.where(lane == 0, 1.0, 0.0).astype(BF16)
    v_tail = jnp.ones(cos.shape, BF16)
    for h in range(N_KV_HEADS):
        sl = slice(h * HEAD_DIM, (h + 1) * HEAD_DIM)
        lo = slice(2 * h * HEAD_DIM, (2 * h + 1) * HEAD_DIM)
        hi = slice((2 * h + 1) * HEAD_DIM, (2 * h + 2) * HEAD_DIM)
        t = k_ref[0, :, sl].astype(F32)
        ko_ref[0, :, lo] = _norm_rope(t, kg, cos, sin, lane_lo).astype(BF16)
        ko_ref[0, :, hi] = k_tail
        vo_ref[0, :, lo] = v_ref[0, :, sl]
        vo_ref[0, :, hi] = v_tail


def _qkprep_call(p, cos, sin_signed, q_gain, k_gain):
    bsz, l, _ = p.shape
    qw = N_HEADS * HEAD_DIM
    kw = N_KV_HEADS * HEAD_DIM
    tm = _tile(l, 768, 16)
    return pl.pallas_call(
        _qkprep_kernel,
        grid=(bsz, l // tm),
        in_specs=[pl.BlockSpec((1, tm, qw), lambda b, i: (b, i, 0)),
                  pl.BlockSpec((1, tm, kw), lambda b, i: (b, i, qw // kw)),
                  pl.BlockSpec((1, tm, kw), lambda b, i: (b, i, qw // kw + 1)),
                  pl.BlockSpec((tm, HEAD_DIM), lambda b, i: (i, 0)),
                  pl.BlockSpec((tm, HEAD_DIM), lambda b, i: (i, 0)),
                  pl.BlockSpec((1, HEAD_DIM), lambda b, i: (0, 0)),
                  pl.BlockSpec((1, HEAD_DIM), lambda b, i: (0, 0))],
        out_specs=[pl.BlockSpec((1, tm, qw), lambda b, i: (b, i, 0)),
                   pl.BlockSpec((1, tm, 2 * kw), lambda b, i: (b, i, 0)),
                   pl.BlockSpec((1, tm, 2 * kw), lambda b, i: (b, i, 0))],
        out_shape=[jax.ShapeDtypeStruct((bsz, l, qw), BF16),
                   jax.ShapeDtypeStruct((bsz, l, 2 * kw), BF16),
                   jax.ShapeDtypeStruct((bsz, l, 2 * kw), BF16)],
        compiler_params=_cparams(("parallel", "parallel"), 40),
        name="qkprep",
    )(p, p, p, cos, sin_signed, q_gain.reshape(1, HEAD_DIM), k_gain.reshape(1, HEAD_DIM))


def _attn_kernel(q_ref, k_ref, v_ref, qg_ref, kg_ref, o_ref, qs_scr, acc_scr, sa_scr, sb_scr, m_scr, l_scr,
                 *, tq, tkc, ctx_len, n_lat_chunks):
    qi = pl.program_id(2)
    q = q_ref[0]
    gq = jnp.max(jnp.abs(qg_ref[...]), axis=-1, keepdims=True)
    gk = jnp.max(jnp.abs(kg_ref[...]), axis=-1, keepdims=True)
    bound = (SHIFT_MARGIN * HEAD_DIM * Q_SCALE) * gq * gk
    lane = lax.broadcasted_iota(jnp.int32, (1, HEAD_DIM), 1)
    tail = jnp.where(lane == 0, -bound, 0.0).astype(BF16)
    for h in range(GROUP):
        qs_scr[h * tq:(h + 1) * tq, :HEAD_DIM] = q[:, h * HEAD_DIM:(h + 1) * HEAD_DIM]
        qs_scr[h * tq:(h + 1) * tq, HEAD_DIM:] = jnp.broadcast_to(tail, (tq, HEAD_DIM))
    n = jnp.where(qi < ctx_len // tq, 0, n_lat_chunks)

    def lat_start(c):
        return pl.multiple_of(ctx_len + c * tkc, math.gcd(ctx_len, tkc))

    def write(out):
        for h in range(GROUP):
            o_ref[0, :, h * HEAD_DIM:(h + 1) * HEAD_DIM] = out[h * tq:(h + 1) * tq, :].astype(BF16)

    def scores(start, size):
        return lax.dot_general(qs_scr[...], k_ref[0, pl.ds(start, size), :], (((1,), (1,)), ((), ())),
                               preferred_element_type=F32)

    def accumulate(s, start, size):
        p = jnp.exp2(s).astype(BF16)
        acc_scr[...] += jnp.dot(p, v_ref[0, pl.ds(start, size), :], preferred_element_type=F32)

    acc_scr[...] = jnp.zeros(acc_scr.shape, F32)

    @pl.when(n == 0)
    def _():
        accumulate(scores(0, ctx_len), 0, ctx_len)

    @pl.when(n > 0)
    def _():
        bounds = [(0, ctx_len + tkc)] + [(ctx_len + c * tkc, tkc) for c in range(1, n_lat_chunks)]
        bufs = (sa_scr, sb_scr)
        bufs[0][:, :bounds[0][1]] = scores(*bounds[0])
        for c, (start, size) in enumerate(bounds):
            if c + 1 < len(bounds):
                nxt_start, nxt_size = bounds[c + 1]
                bufs[(c + 1) % 2][:, :nxt_size] = scores(nxt_start, nxt_size)
            accumulate(bufs[c % 2][:, :size], start, size)

    acc = acc_scr[...]
    den = acc[:, HEAD_DIM:]
    write(acc[:, :HEAD_DIM] / den)

    @pl.when(jnp.logical_not(jnp.min(den) >= MIN_DENOM))
    def _():
        m_scr[...] = jnp.full(m_scr.shape, NEG_INF, F32)
        l_scr[...] = jnp.zeros(l_scr.shape, F32)
        acc_scr[...] = jnp.zeros(acc_scr.shape, F32)

        def online_chunk(start, size):
            k = k_ref[0, pl.ds(start, size), :HEAD_DIM]
            v = v_ref[0, pl.ds(start, size), :HEAD_DIM]
            s = lax.dot_general(qs_scr[:, :HEAD_DIM], k, (((1,), (1,)), ((), ())),
                                preferred_element_type=F32)
            m_old = m_scr[...]
            m_new = jnp.maximum(m_old, jnp.max(s, axis=-1, keepdims=True))
            alpha = jnp.exp2(m_old - m_new)
            p = jnp.exp2(s - m_new)
            l_scr[...] = alpha * l_scr[...] + jnp.sum(p, axis=-1, keepdims=True)
            acc_scr[:, :HEAD_DIM] = alpha * acc_scr[:, :HEAD_DIM] + jnp.dot(
                p.astype(BF16), v, preferred_element_type=F32)
            m_scr[...] = m_new

        online_chunk(0, ctx_len)

        def online_body(c, carry):
            online_chunk(lat_start(c), tkc)
            return carry

        lax.fori_loop(0, n, online_body, 0)
        write(acc_scr[:, :HEAD_DIM] / l_scr[...])


def _attn_call(qn, ke, ve, q_gain, k_gain, ctx_len):
    bsz, l, qw = qn.shape
    tq = 256
    assert ctx_len % tq == 0 and l % tq == 0
    t_lat = l - ctx_len
    assert t_lat % (ATTN_CHUNKS * LANES) == 0
    tkc = t_lat // ATTN_CHUNKS
    gw = GROUP * HEAD_DIM
    ew = 2 * HEAD_DIM
    kern = functools.partial(_attn_kernel, tq=tq, tkc=tkc, ctx_len=ctx_len, n_lat_chunks=t_lat // tkc)
    return pl.pallas_call(
        kern,
        grid=(bsz, N_KV_HEADS, l // tq),
        in_specs=[pl.BlockSpec((1, tq, gw), lambda b, g, i: (b, i, g)),
                  pl.BlockSpec((1, l, ew), lambda b, g, i: (b, 0, g)),
                  pl.BlockSpec((1, l, ew), lambda b, g, i: (b, 0, g)),
                  pl.BlockSpec((1, HEAD_DIM), lambda b, g, i: (0, 0)),
                  pl.BlockSpec((1, HEAD_DIM), lambda b, g, i: (0, 0))],
        out_specs=pl.BlockSpec((1, tq, gw), lambda b, g, i: (b, i, g)),
        out_shape=jax.ShapeDtypeStruct((bsz, l, qw), BF16),
        scratch_shapes=[pltpu.VMEM((GROUP * tq, ew), BF16),
                        pltpu.VMEM((GROUP * tq, ew), F32),
                        pltpu.VMEM((GROUP * tq, ctx_len + tkc), F32),
                        pltpu.VMEM((GROUP * tq, ctx_len + tkc), F32),
                        pltpu.VMEM((GROUP * tq, 1), F32),
                        pltpu.VMEM((GROUP * tq, 1), F32)],
        compiler_params=_cparams(("parallel", "parallel", "arbitrary"), 56),
        name="attn",
    )(qn, ke, ve, q_gain.reshape(1, HEAD_DIM), k_gain.reshape(1, HEAD_DIM))


def _segment_bounds(row, ctx_len, seq_len):
    is_ctx = row < ctx_len
    first = jnp.where(is_ctx, 0, ctx_len)
    last = jnp.where(is_ctx, ctx_len - 1, seq_len - 1)
    return first, last


def _shift_down(x, prev, k, local):
    y = pltpu.roll(x, k, 0)
    for r in range(k):
        y = jnp.where(local == r, prev[HALO - k + r:HALO - k + r + 1, :], y)
    return y


def _shift_up(x, nxt, k, local, tt):
    y = pltpu.roll(x, tt - k, 0)
    for r in range(k):
        y = jnp.where(local == tt - k + r, nxt[r:r + 1, :], y)
    return y


def _halo_shifts(x, prev, nxt):
    tt = x.shape[0]
    sub = lax.broadcasted_iota(jnp.int32, (SUBLANES, x.shape[1]), 0)

    def down(k):
        y = pltpu.roll(x, k, 0)
        head = y[:SUBLANES]
        for r in range(k):
            head = jnp.where(sub == r, prev[HALO - k + r:HALO - k + r + 1, :], head)
        return jnp.concatenate([head, y[SUBLANES:]], axis=0)

    def up(k):
        y = pltpu.roll(x, tt - k, 0)
        tail = y[tt - SUBLANES:]
        for r in range(k):
            tail = jnp.where(sub == SUBLANES - k + r, nxt[r:r + 1, :], tail)
        return jnp.concatenate([y[:tt - SUBLANES], tail], axis=0)

    return down, up


def _scan_tile(a, d, carry, reverse):
    n_groups = a.shape[0] // SUBLANES
    sub = lax.broadcasted_iota(jnp.int32, (SUBLANES, a.shape[1]), 0)
    groups = []
    for v in range(n_groups):
        av = a[v * SUBLANES:(v + 1) * SUBLANES]
        dv = d[v * SUBLANES:(v + 1) * SUBLANES]
        for k in (1, 2, 4):
            keep = sub < SUBLANES - k if reverse else sub >= k
            shift = SUBLANES - k if reverse else k
            a_n = jnp.where(keep, pltpu.roll(av, shift, 0), 1.0)
            d_n = jnp.where(keep, pltpu.roll(dv, shift, 0), 0.0)
            dv = av * d_n + dv
            av = av * a_n
        groups.append((av, dv))
    hs = [None] * n_groups
    for v in (reversed(range(n_groups)) if reverse else range(n_groups)):
        av, dv = groups[v]
        hv = dv + av * carry
        carry = hv[0:1] if reverse else hv[SUBLANES - 1:SUBLANES]
        hs[v] = hv
    return jnp.concatenate(hs, axis=0), carry


def _lru_kernel(*refs, reverse, tt, ctx_len, n_sub):
    if reverse:
        (x_ref, xp_ref, xn_ref, cw_ref, cb_ref, wa_ref, ba_ref, wx_ref, bx_ref, lam_ref,
         hf_ref, g_ref, o_ref, carry_scr) = refs
    else:
        (x_ref, xp_ref, xn_ref, cw_ref, cb_ref, wa_ref, ba_ref, wx_ref, bx_ref, lam_ref,
         o_ref, carry_scr) = refs
    s = pl.program_id(2)
    nt = pl.num_programs(2)
    ti = jnp.where(s == 0, 0, nt - s) if reverse else s

    @pl.when(s == 0)
    def _():
        carry_scr[...] = jnp.zeros(carry_scr.shape, F32)

    prev_ok = jnp.logical_and(ti != 0, ti * tt != ctx_len)
    next_ok = jnp.logical_and(ti != nt - 1, (ti + 1) * tt != ctx_len)
    x = x_ref[0].astype(F32)
    xp = jnp.where(prev_ok, xp_ref[0].astype(F32), 0.0)
    xn = jnp.where(next_ok, xn_ref[0].astype(F32), 0.0)
    down, up = _halo_shifts(x, xp, xn)
    cw = cw_ref[...]
    u = cb_ref[...] + cw[2:3, :] * x + cw[0:1, :] * down(2) + cw[1:2, :] * down(1) + cw[3:4, :] * up(1)
    nlam = -lam_ref[...]
    softplus = jnp.maximum(nlam, 0.0) + jnp.log1p(jnp.exp(-jnp.abs(nlam)))

    for j in range(n_sub):
        ls = slice(j * LRU_BLOCK, (j + 1) * LRU_BLOCK)
        uj = u[:, ls]
        ub = uj.astype(BF16)
        r = jax.nn.sigmoid(jnp.dot(ub, wa_ref[j], preferred_element_type=F32) + ba_ref[:, ls])
        i = jax.nn.sigmoid(jnp.dot(ub, wx_ref[j], preferred_element_type=F32) + bx_ref[:, ls])
        a = jnp.exp((-LRU_C) * r * softplus[:, ls])
        d = jnp.sqrt(1.0 - a * a) * (i * uj)
        h, carry = _scan_tile(a, d, carry_scr[:, ls], reverse)
        carry_scr[:, ls] = carry
        if reverse:
            o_ref[0, :, ls] = ((hf_ref[0, :, ls] + h) * _gelu(g_ref[0, :, ls].astype(F32))).astype(BF16)
        else:
            o_ref[0, :, ls] = h


def _lru_call(p, conv_w, conv_b, wa, ba, wx, bx, lam, ctx_len, reverse, h_fwd=None):
    bsz, l, _ = p.shape
    c = conv_w.shape[1]
    tt = 256
    assert ctx_len % tt == 0 and l % tt == 0
    nt = l // tt
    n_sub = 8
    cw = n_sub * LRU_BLOCK
    nb = c // cw
    x_col0 = (N_HEADS + 2 * N_KV_HEADS) * HEAD_DIM // cw
    g_col0 = x_col0 + nb
    hb = tt // HALO
    n_hblk = l // HALO

    def tile_of(s):
        return jnp.where(s == 0, 0, nt - s) if reverse else s

    x_map = lambda b, cb, s: (b, tile_of(s), x_col0 + cb)
    prev_map = lambda b, cb, s: (b, jnp.maximum(tile_of(s) * hb - 1, 0), x_col0 + cb)
    next_map = lambda b, cb, s: (b, jnp.minimum((tile_of(s) + 1) * hb, n_hblk - 1), x_col0 + cb)
    vec_map = lambda b, cb, s: (0, cb)
    w_spec = pl.BlockSpec((n_sub, LRU_BLOCK, LRU_BLOCK), lambda b, cb, s: (cb, 0, 0))
    in_specs = [pl.BlockSpec((1, tt, cw), x_map),
                pl.BlockSpec((1, HALO, cw), prev_map),
                pl.BlockSpec((1, HALO, cw), next_map),
                pl.BlockSpec((LRU_CONV, cw), vec_map),
                pl.BlockSpec((1, cw), vec_map),
                w_spec,
                pl.BlockSpec((1, cw), vec_map),
                w_spec,
                pl.BlockSpec((1, cw), vec_map),
                pl.BlockSpec((1, cw), vec_map)]
    args = [p, p, p, conv_w, conv_b.reshape(1, c), wa.astype(BF16), ba.reshape(1, c),
            wx.astype(BF16), bx.reshape(1, c), lam.reshape(1, c)]
    out_map = lambda b, cb, s: (b, tile_of(s), cb)
    if reverse:
        in_specs += [pl.BlockSpec((1, tt, cw), out_map),
                     pl.BlockSpec((1, tt, cw), lambda b, cb, s: (b, tile_of(s), g_col0 + cb))]
        args += [h_fwd, p]
        out_dtype = BF16
    else:
        out_dtype = F32
    kern = functools.partial(_lru_kernel, reverse=reverse, tt=tt, ctx_len=ctx_len, n_sub=n_sub)
    return pl.pallas_call(
        kern,
        grid=(bsz, nb, nt),
        in_specs=in_specs,
        out_specs=pl.BlockSpec((1, tt, cw), out_map),
        out_shape=jax.ShapeDtypeStruct((bsz, l, c), out_dtype),
        scratch_shapes=[pltpu.VMEM((1, cw), F32)],
        compiler_params=_cparams(("parallel", "parallel", "arbitrary"), 32),
        name="lru_rev" if reverse else "lru_fwd",
    )(*args)


def _sconv_kernel(bg_ref, cg_ref, u_ref, cgp_ref, up_ref, cgn_ref, un_ref, w_ref, b_ref, o_ref,
                  *, tt, ctx_len, seq_len):
    ti = pl.program_id(1)
    z = cg_ref[0].astype(F32) * u_ref[0].astype(F32)
    zp = cgp_ref[0].astype(F32) * up_ref[0].astype(F32)
    zn = cgn_ref[0].astype(F32) * un_ref[0].astype(F32)
    local = lax.broadcasted_iota(jnp.int32, (tt, 1), 0)
    row = ti * tt + local
    first, last = _segment_bounds(row, ctx_len, seq_len)
    w = w_ref[...]
    y = b_ref[...] + w[1:2, :] * z
    y = y + w[0:1, :] * jnp.where(row - 1 >= first, _shift_down(z, zp, 1, local), 0.0)
    y = y + w[2:3, :] * jnp.where(row + 1 <= last, _shift_up(z, zn, 1, local, tt), 0.0)
    o_ref[0] = (bg_ref[0].astype(F32) * y).astype(BF16)


def _sconv_call(p, w, b, ctx_len):
    bsz, l, _ = p.shape
    c = w.shape[1]
    tt = _tile(l, 768, 16)
    tc = 512
    ncb = c // tc
    b_col0 = (N_HEADS * HEAD_DIM + 2 * N_KV_HEADS * HEAD_DIM + 2 * c) // tc
    c_col0 = b_col0 + ncb
    u_col0 = c_col0 + ncb
    hb = tt // HALO
    n_hblk = l // HALO
    prev = lambda i: jnp.maximum(i * hb - 1, 0)
    nxt = lambda i: jnp.minimum((i + 1) * hb, n_hblk - 1)
    kern = functools.partial(_sconv_kernel, tt=tt, ctx_len=ctx_len, seq_len=l)
    return pl.pallas_call(
        kern,
        grid=(bsz, l // tt, ncb),
        in_specs=[pl.BlockSpec((1, tt, tc), lambda b_, i, j: (b_, i, b_col0 + j)),
                  pl.BlockSpec((1, tt, tc), lambda b_, i, j: (b_, i, c_col0 + j)),
                  pl.BlockSpec((1, tt, tc), lambda b_, i, j: (b_, i, u_col0 + j)),
                  pl.BlockSpec((1, HALO, tc), lambda b_, i, j: (b_, prev(i), c_col0 + j)),
                  pl.BlockSpec((1, HALO, tc), lambda b_, i, j: (b_, prev(i), u_col0 + j)),
                  pl.BlockSpec((1, HALO, tc), lambda b_, i, j: (b_, nxt(i), c_col0 + j)),
                  pl.BlockSpec((1, HALO, tc), lambda b_, i, j: (b_, nxt(i), u_col0 + j)),
                  pl.BlockSpec((SC_CONV, tc), lambda b_, i, j: (0, j)),
                  pl.BlockSpec((1, tc), lambda b_, i, j: (0, j))],
        out_specs=pl.BlockSpec((1, tt, tc), lambda b_, i, j: (b_, i, j)),
        out_shape=jax.ShapeDtypeStruct((bsz, l, c), BF16),
        compiler_params=_cparams(("parallel", "parallel", "parallel"), 32),
        name="sconv",
    )(p, p, p, p, p, p, p, w, b.reshape(1, c))


def _merge_kernel(xa_ref, xl_ref, xs_ref, ga_ref, gl_ref, gs_ref, wa_ref, wl_ref, ws_ref, o_ref):
    y = jax.nn.sigmoid(ga_ref[0].astype(F32)) * jnp.dot(xa_ref[0], wa_ref[...], preferred_element_type=F32)
    y = y + jax.nn.sigmoid(gl_ref[0].astype(F32)) * jnp.dot(xl_ref[0], wl_ref[...], preferred_element_type=F32)
    y = y + jax.nn.sigmoid(gs_ref[0].astype(F32)) * jnp.dot(xs_ref[0], ws_ref[...], preferred_element_type=F32)
    o_ref[0] = y.astype(BF16)


def _merge_call(x_att, x_lru, x_sc, p, layer, w_att, w_lru, w_sc):
    bsz, l, d = x_att.shape
    tm = _tile(l, 768, 16)
    tn = 512
    g_col0 = (p.shape[2] - 3 * d) // tn
    nj = d // tn
    xspec = pl.BlockSpec((1, tm, d), lambda b, i, j: (b, i, 0))
    wspec = pl.BlockSpec((None, d, tn), lambda b, i, j: (layer, 0, j))
    gspec = lambda k: pl.BlockSpec((1, tm, tn), lambda b, i, j: (b, i, g_col0 + k * nj + j))
    return pl.pallas_call(
        _merge_kernel,
        grid=(bsz, l // tm, nj),
        in_specs=[xspec, xspec, xspec, gspec(0), gspec(1), gspec(2), wspec, wspec, wspec],
        out_specs=pl.BlockSpec((1, tm, tn), lambda b, i, j: (b, i, j)),
        out_shape=jax.ShapeDtypeStruct((bsz, l, d), BF16),
        compiler_params=_cparams(("parallel", "parallel", "arbitrary"), 48),
        name="merge",
    )(x_att, x_lru, x_sc, p, p, p, w_att, w_lru, w_sc)


def _outproj_kernel(y_ref, w_ref, x_ref, gate_ref, shift_ref, scale_ref, g_ref, xo_ref, ht_ref,
                    *, tm, ctx_len):
    b = pl.program_id(0)
    i = pl.program_id(1)
    row = i * tm + lax.broadcasted_iota(jnp.int32, (tm, 1), 0)
    is_ctx = row < ctx_len
    acc = jnp.dot(y_ref[0], w_ref[...], preferred_element_type=F32)
    xn = x_ref[0] + _row_select(gate_ref, b, is_ctx) * acc
    xo_ref[0] = xn
    var = jnp.mean(xn * xn, axis=-1, keepdims=True)
    h = xn * lax.rsqrt(var + EPS) * g_ref[...]
    h = h * (1.0 + _row_select(scale_ref, b, is_ctx)) + _row_select(shift_ref, b, is_ctx)
    ht_ref[0] = h.T.astype(BF16)


def _outproj_call(y, w_out, x, mod, layer, gain, ctx_len):
    bsz, l, d = x.shape
    tm = _tile(l, 384, LANES)
    kern = functools.partial(_outproj_kernel, tm=tm, ctx_len=ctx_len)
    mspec = lambda k: pl.BlockSpec((None, 8, d), lambda b, i: (layer, 0, k))
    return pl.pallas_call(
        kern,
        grid=(bsz, l // tm),
        in_specs=[pl.BlockSpec((1, tm, d), lambda b, i: (b, i, 0)),
                  pl.BlockSpec((None, d, d), lambda b, i: (layer, 0, 0)),
                  pl.BlockSpec((1, tm, d), lambda b, i: (b, i, 0)),
                  mspec(2), mspec(3), mspec(4),
                  pl.BlockSpec((1, d), lambda b, i: (0, 0))],
        out_specs=[pl.BlockSpec((1, tm, d), lambda b, i: (b, i, 0)),
                   pl.BlockSpec((1, d, tm), lambda b, i: (b, 0, i))],
        out_shape=[jax.ShapeDtypeStruct((bsz, l, d), F32),
                   jax.ShapeDtypeStruct((bsz, d, l), BF16)],
        compiler_params=_cparams(("parallel", "parallel"), 48),
        name="outproj",
    )(y, w_out, x, mod, mod, mod, gain.reshape(1, d))


_CAND_ROWS = tuple(PEER_TOPK // (i + 1) for i in range(PEER_TOPK))


def _top16(s, v_scr):
    n = s.shape[0]
    key = lax.broadcasted_iota(jnp.int32, s.shape, 0)

    def body(r, carry):
        work, rank = carry
        m = jnp.max(work, axis=0, keepdims=True)
        v_scr[pl.ds(r, 1), :] = m
        first = jnp.min(jnp.where(work == m, key, n), axis=0, keepdims=True)
        sel = key == first
        return jnp.where(sel, NEG_INF, work), jnp.where(sel, jnp.asarray(r, F32), rank)

    _, rank = lax.fori_loop(0, PEER_TOPK, body, (s, jnp.full(s.shape, float(PEER_TOPK), F32)))
    return rank


MARK = -(2.0 ** 127)


def _top16_distinct(ss, v_scrs):
    def body(r, works):
        mark = jnp.asarray(r, F32) * (MARK / 32.0) + MARK
        out = []
        for work, v_scr in zip(works, v_scrs):
            m = jnp.max(work, axis=0, keepdims=True)
            v_scr[pl.ds(r, 1), :] = m
            out.append(jnp.where(work == m, mark, work))
        return tuple(out)

    works = lax.fori_loop(0, PEER_TOPK, body, tuple(ss))
    return [jnp.where(w <= MARK, (MARK - w) * (-32.0 / MARK), float(PEER_TOPK)) for w in works]


def _peer_topk_kernel(ht_ref, wq_ref, k1_ref, k2_ref, cnt_ref, e1_ref, rk_ref, e2_ref,
                      q_scr, v_scr, *, tm, n_par):
    half = PEER_NKEYS
    q_scr[...] = jnp.dot(wq_ref[...], ht_ref[0], preferred_element_type=F32).astype(BF16)
    sub = lax.broadcasted_iota(jnp.int32, (SUBLANES, LANES), 0)
    big = PEER_TOPK * PEER_TOPK

    def candidates(v1, v2):
        pieces, poss = [], []
        for i in range(SUBLANES):
            for j0 in range(0, _CAND_ROWS[i], SUBLANES):
                c = v1[i:i + 1, :] + v2[j0:j0 + SUBLANES, :]
                valid = sub + j0 < _CAND_ROWS[i]
                pieces.append(jnp.where(valid, c, NEG_INF))
                poss.append(jnp.where(valid, i * PEER_TOPK + j0 + sub, big))
        pieces.append(v1[SUBLANES:, :] + v2[0:1, :])
        poss.append((sub + SUBLANES) * PEER_TOPK)
        return pieces, poss

    def pick_exact(pieces, poss):
        def pick(_, carry2):
            cs, sels = carry2
            m = functools.reduce(jnp.maximum, cs)
            m = jnp.max(m, axis=0, keepdims=True)
            cand_pos = functools.reduce(jnp.minimum, [jnp.where(c == m, p_, big) for c, p_ in zip(cs, poss)])
            first = jnp.min(cand_pos, axis=0, keepdims=True)
            hit = [p_ == first for p_ in poss]
            cs = tuple(jnp.where(hh, NEG_INF, c) for hh, c in zip(hit, cs))
            sels = tuple(jnp.where(hh, 1.0, s_) for hh, s_ in zip(hit, sels))
            return cs, sels

        zeros = tuple(jnp.zeros((SUBLANES, LANES), F32) for _ in pieces)
        return lax.fori_loop(0, PEER_TOPK, pick, (tuple(pieces), zeros))[1]

    def pick_distinct(groups):
        npc = len(groups[0])

        def pick(_, cs):
            out = []
            for g in range(len(groups)):
                grp = cs[g * npc:(g + 1) * npc]
                m = jnp.max(functools.reduce(jnp.maximum, grp), axis=0, keepdims=True)
                out.extend(jnp.where(c == m, MARK, c) for c in grp)
            return tuple(out)

        marked = lax.fori_loop(0, PEER_TOPK, pick, tuple(c for grp in groups for c in grp))
        return [tuple(jnp.where(c == MARK, 1.0, 0.0) for c in marked[g * npc:(g + 1) * npc])
                for g in range(len(groups))]

    def compute(h, lss, exact):
        base = pl.multiple_of(h * 2 * half, 2 * half)
        n_g = len(lss)
        s1s = [jnp.dot(k1_ref[h], q_scr[pl.ds(base, half), ls], preferred_element_type=F32) for ls in lss]
        s2s = [jnp.dot(k2_ref[h], q_scr[pl.ds(base + half, half), ls], preferred_element_type=F32)
               for ls in lss]
        v1_refs = [v_scr.at[2 * g] for g in range(n_g)]
        v2_refs = [v_scr.at[2 * g + 1] for g in range(n_g)]
        if exact:
            rank1s = [_top16(s, r) for s, r in zip(s1s, v1_refs)]
            rank2s = [_top16(s, r) for s, r in zip(s2s, v2_refs)]
        else:
            ranks = _top16_distinct(tuple(s1s + s2s), tuple(v1_refs + v2_refs))
            rank1s, rank2s = ranks[:n_g], ranks[n_g:]
        v1s = [r[...] for r in v1_refs]
        v2s = [r[...] for r in v2_refs]
        cands = [candidates(v1, v2) for v1, v2 in zip(v1s, v2s)]
        if exact:
            all_sels = [pick_exact(pieces, poss) for pieces, poss in cands]
        else:
            all_sels = pick_distinct([pieces for pieces, _ in cands])
        most = None
        for g in range(n_g):
            most_g = finish(h, lss[g], s1s[g], s2s[g], rank1s[g], rank2s[g], v1s[g], v2s[g],
                            cands[g][0], all_sels[g])
            most = most_g if most is None else jnp.maximum(most, most_g)
        return most

    def finish(h, ls, s1, s2, rank1, rank2, v1, v2, orig, sels):
        top = v1[0:1, :] + v2[0:1, :]
        zsum = functools.reduce(
            lambda a_, b_: a_ + b_,
            [jnp.where(s_ > 0.0, jnp.exp(o - top), 0.0) for s_, o in zip(sels, orig)])
        zinv = 1.0 / jnp.sum(zsum, axis=0, keepdims=True)

        counts = []
        pi = 0
        for i in range(SUBLANES):
            c = None
            for j0 in range(0, _CAND_ROWS[i], SUBLANES):
                part = jnp.sum(sels[pi], axis=0, keepdims=True)
                c = part if c is None else c + part
                pi += 1
            counts.append(c)
        tail = sels[pi]
        for i in range(SUBLANES, PEER_TOPK):
            counts.append(tail[i - SUBLANES:i - SUBLANES + 1, :])
        cnt = jnp.zeros(rank1.shape, F32)
        for i in range(PEER_TOPK):
            cnt = jnp.where(rank1 == float(i), counts[i], cnt)

        cnt_ref[0, h, :, ls] = cnt
        e1_ref[0, h, :, ls] = jnp.exp(s1 - v1[0:1, :]) * zinv
        rk_ref[0, h, :, ls] = rank2.astype(BF16)
        e2_ref[0, h, :, ls] = jnp.exp(s2 - v2[0:1, :]).astype(BF16)
        ranked = jnp.where(rank1 < PEER_TOPK, 1.0, 0.0)
        ranked = jnp.maximum(jnp.sum(ranked, axis=0, keepdims=True),
                             jnp.sum(jnp.where(rank2 < PEER_TOPK, 1.0, 0.0), axis=0, keepdims=True))
        picked = jnp.sum(functools.reduce(lambda a_, b_: a_ + b_, sels), axis=0, keepdims=True)
        return jnp.max(jnp.maximum(ranked, picked))

    n_trips = tm // (n_par * LANES)

    def head_lane_groups(idx, carry):
        h = idx // n_trips
        first = (idx % n_trips) * n_par
        lss = [pl.ds(pl.multiple_of((first + g) * LANES, LANES), LANES) for g in range(n_par)]
        most = compute(h, lss, exact=False)

        @pl.when(most > PEER_TOPK)
        def _():
            compute(h, lss, exact=True)

        return carry

    lax.fori_loop(0, PEER_HEADS * n_trips, head_lane_groups, 0)


def _peer_topk_call(ht, layer, wq_t, k1, k2):
    bsz, d, l = ht.shape
    qd = wq_t.shape[1]
    n_par = 2
    tm = _tile(l, 768, n_par * LANES)
    kern = functools.partial(_peer_topk_kernel, tm=tm, n_par=n_par)
    ospec = pl.BlockSpec((1, PEER_HEADS, PEER_NKEYS, tm), lambda b, i: (b, 0, 0, i))
    oshape = jax.ShapeDtypeStruct((bsz, PEER_HEADS, PEER_NKEYS, l), F32)
    kspec = pl.BlockSpec((None, PEER_HEADS, PEER_NKEYS, PEER_NKEYS), lambda b, i: (layer, 0, 0, 0))
    return pl.pallas_call(
        kern,
        grid=(bsz, l // tm),
        in_specs=[pl.BlockSpec((1, d, tm), lambda b, i: (b, 0, i)),
                  pl.BlockSpec((None, qd, d), lambda b, i: (layer, 0, 0)),
                  kspec, kspec],
        out_specs=[ospec, ospec, ospec, ospec],
        out_shape=[oshape, oshape, jax.ShapeDtypeStruct(oshape.shape, BF16),
                   jax.ShapeDtypeStruct(oshape.shape, BF16)],
        scratch_shapes=[pltpu.VMEM((qd, tm), BF16),
                        pltpu.VMEM((2 * n_par, PEER_TOPK, LANES), F32)],
        compiler_params=_cparams(("parallel", "parallel"), 56),
        name="peer_topk",
    )(ht, wq_t, k1, k2)


def _peer_dense_kernel(ht_ref, u_ref, vt_ref, cnt_ref, e1_ref, rk_ref, e2_ref, o_ref, wz_scr, *, n_sub):
    e = pl.program_id(2)

    @pl.when(e == 0)
    def _():
        o_ref[...] = jnp.zeros(o_ref.shape, F32)

    ht = ht_ref[0]
    per = 4
    n_split = n_sub // per
    for part in range(n_split):
        s = jnp.dot(u_ref[part * per * PEER_NKEYS:(part + 1) * per * PEER_NKEYS, :], ht,
                    preferred_element_type=F32)
        for a in range(part * per, (part + 1) * per):
            rows = slice(a * PEER_NKEYS, (a + 1) * PEER_NKEYS)
            local = slice((a - part * per) * PEER_NKEYS, (a - part * per + 1) * PEER_NKEYS)
            w = None
            for h in range(PEER_HEADS):
                cnt = cnt_ref[0, h, a:a + 1, :].astype(BF16)
                e1 = e1_ref[0, h, a:a + 1, :].astype(BF16)
                term = jnp.where(rk_ref[0, h] < cnt, e2_ref[0, h] * e1, 0.0)
                w = term if w is None else w + term
            wz_scr[rows, :] = w * _gelu(s[local, :]).astype(BF16)
    o_ref[0] += jnp.dot(vt_ref[...], wz_scr[...], preferred_element_type=F32)


def _peer_dense_call(ht, layer, u_bf16, vt_bf16, cnt, e1n, rank2, e2):
    bsz, d, l = ht.shape
    n_exp = u_bf16.shape[1]
    tm = _tile(l, 768, LANES)
    n_sub = SUBLANES
    te = n_sub * PEER_NKEYS
    kern = functools.partial(_peer_dense_kernel, n_sub=n_sub)
    aspec = pl.BlockSpec((1, PEER_HEADS, n_sub, tm), lambda b, i, e: (b, 0, e, i))
    fspec = pl.BlockSpec((1, PEER_HEADS, PEER_NKEYS, tm), lambda b, i, e: (b, 0, 0, i))
    return pl.pallas_call(
        kern,
        grid=(bsz, l // tm, n_exp // te),
        in_specs=[pl.BlockSpec((1, d, tm), lambda b, i, e: (b, 0, i)),
                  pl.BlockSpec((None, te, d), lambda b, i, e: (layer, e, 0)),
                  pl.BlockSpec((None, d, te), lambda b, i, e: (layer, 0, e)),
                  aspec, aspec, fspec, fspec],
        out_specs=pl.BlockSpec((1, d, tm), lambda b, i, e: (b, 0, i)),
        out_shape=jax.ShapeDtypeStruct((bsz, d, l), F32),
        scratch_shapes=[pltpu.VMEM((te, tm), BF16)],
        compiler_params=_cparams(("parallel", "parallel", "arbitrary"), 58),
        name="peer_dense",
    )(ht, u_bf16, vt_bf16, cnt, e1n, rank2, e2)


def _resid_kernel(x_ref, yt_ref, gate_ref, o_ref, *, tm, ctx_len, first_tile):
    b = pl.program_id(0)
    i = pl.program_id(1) + first_tile
    row = i * tm + lax.broadcasted_iota(jnp.int32, (tm, 1), 0)
    o_ref[0] = x_ref[0] + _row_select(gate_ref, b, row < ctx_len) * yt_ref[0].T


def _resid_call(x, yt, mod, layer, ctx_len, latent_only):
    bsz, l, d = x.shape
    tm = _tile(math.gcd(l, ctx_len), 384, LANES)
    first_tile = ctx_len // tm if latent_only else 0
    n_tiles = l // tm - first_tile
    kern = functools.partial(_resid_kernel, tm=tm, ctx_len=ctx_len, first_tile=first_tile)
    return pl.pallas_call(
        kern,
        grid=(bsz, n_tiles),
        in_specs=[pl.BlockSpec((1, tm, d), lambda b, i: (b, i + first_tile, 0)),
                  pl.BlockSpec((1, d, tm), lambda b, i: (b, 0, i + first_tile)),
                  pl.BlockSpec((None, 8, d), lambda b, i: (layer, 0, 5))],
        out_specs=pl.BlockSpec((1, tm, d), lambda b, i: (b, i, 0)),
        out_shape=jax.ShapeDtypeStruct((bsz, n_tiles * tm, d), F32),
        compiler_params=_cparams(("parallel", "parallel"), 40),
        name="resid",
    )(x, yt, mod)


def _rope_tables(ctx_len, t_lat):
    rows = t_lat // GRID_W
    row = jnp.repeat(jnp.arange(rows, dtype=F32), GRID_W)
    col = jnp.tile(jnp.arange(GRID_W, dtype=F32), rows)
    inv = ROPE_THETA ** (-jnp.arange(ROPE_PAIRS, dtype=F32) / ROPE_PAIRS)
    ang = jnp.concatenate([row[:, None] * inv] * 2 + [col[:, None] * inv] * 2, axis=1)
    ang = jnp.concatenate([jnp.zeros((ctx_len, HEAD_DIM), F32), ang], axis=0)
    sign = jnp.where((jnp.arange(HEAD_DIM) & ROPE_PAIRS) == 0, -1.0, 1.0).astype(F32)
    return jnp.cos(ang), jnp.sin(ang) * sign


def kernel(x, c, ctx, c_ctx, w_mod, b_mod, norm_mix, norm_ffn, w_in, q_norm, k_norm, lru_conv_w, lru_conv_b, lru_wa, lru_ba, lru_wx, lru_bx, lru_lambda, sc_conv_w, sc_conv_b, w_o_attn, w_o_lru, w_o_sc, w_out, peer_wq, peer_k1, peer_k2, peer_u, peer_v):
    bsz, t_lat, d = x.shape
    ctx_len = ctx.shape[1]
    depth = w_mod.shape[0]
    assert bsz == 2, "modulation rows are laid out as [latent 0, latent 1, context]"

    xs = jnp.concatenate([ctx, x], axis=1)
    s8 = jnp.concatenate([c, c_ctx[None, :], jnp.zeros((8 - bsz - 1, d), F32)], axis=0)
    mod = _mod_call(s8, w_mod, b_mod)
    cos, sin_signed = _rope_tables(ctx_len, t_lat)

    w_in_b = w_in.astype(BF16)
    w_att_b, w_lru_b, w_sc_b = w_o_attn.astype(BF16), w_o_lru.astype(BF16), w_o_sc.astype(BF16)
    w_out_b = w_out.astype(BF16)
    wq_t = jnp.swapaxes(peer_wq, 1, 2).astype(BF16)
    k1_b, k2_b = peer_k1.astype(BF16), peer_k2.astype(BF16)
    u_b = peer_u.astype(BF16)
    vt_b = jnp.swapaxes(peer_v, 1, 2).astype(BF16)

    for l in range(depth):
        p = _inproj_call(xs, mod, l, norm_mix[l], w_in_b, ctx_len)
        qn, ke, ve = _qkprep_call(p, cos, sin_signed, q_norm[l], k_norm[l])
        x_att = _attn_call(qn, ke, ve, q_norm[l], k_norm[l], ctx_len)
        lru_args = (lru_conv_w[l], lru_conv_b[l])
        h_fwd = _lru_call(p, *lru_args, lru_wa[l, 0], lru_ba[l, 0], lru_wx[l, 0], lru_bx[l, 0],
                          lru_lambda[l, 0], ctx_len, reverse=False)
        x_lru = _lru_call(p, *lru_args, lru_wa[l, 1], lru_ba[l, 1], lru_wx[l, 1], lru_bx[l, 1],
                          lru_lambda[l, 1], ctx_len, reverse=True, h_fwd=h_fwd)
        x_sc = _sconv_call(p, sc_conv_w[l], sc_conv_b[l], ctx_len)
        y = _merge_call(x_att, x_lru, x_sc, p, l, w_att_b, w_lru_b, w_sc_b)
        xs, ht = _outproj_call(y, w_out_b, xs, mod, l, norm_ffn[l], ctx_len)
        cnt, e1n, rank2, e2 = _peer_topk_call(ht, l, wq_t, k1_b, k2_b)
        yt = _peer_dense_call(ht, l, u_b, vt_b, cnt, e1n, rank2, e2)
        xs = _resid_call(xs, yt, mod, l, ctx_len, latent_only=(l == depth - 1))
    return xs
```

```python
import functools
import math

import jax
import jax.numpy as jnp
from jax import lax
from jax.experimental import pallas as pl
from jax.experimental.pallas import tpu as pltpu

F32 = jnp.float32
BF16 = jnp.bfloat16

GRID_W = 64
EPS = 1e-6

N_HEADS = 16
N_KV_HEADS = 4
HEAD_DIM = 128
GROUP = N_HEADS // N_KV_HEADS
ROPE_PAIRS = HEAD_DIM // 4
ROPE_THETA = 10000.0

LRU_BLOCK = 128
LRU_CONV = 4
LRU_C = 8.0
SC_CONV = 3

PEER_HEADS = 8
PEER_NKEYS = 128
PEER_TOPK = 16

LANES = 128
SUBLANES = 8
HALO = SUBLANES
NEG_INF = float("-inf")
LOG2E = 1.4426950408889634
GELU_C = math.sqrt(2.0 / math.pi)
Q_SCALE = HEAD_DIM ** -0.5 * LOG2E
SHIFT_MARGIN = 1.02
MIN_DENOM = 2.0 ** -100
ATTN_CHUNKS = 4


def _tile(n, target, mult):
    best = None
    for t in range(mult, min(n, target) + 1, mult):
        if n % t == 0:
            best = t
    assert best is not None, (n, target, mult)
    return best


def _cparams(sem, vmem_mib):
    return pltpu.CompilerParams(dimension_semantics=sem, vmem_limit_bytes=vmem_mib << 20)


def _gelu(x):
    return 0.5 * x * (1.0 + jnp.tanh(GELU_C * (x + 0.044715 * (x * x * x))))


def _row_select(mod_ref, b, is_ctx):
    return jnp.where(is_ctx, mod_ref[2:3, :], mod_ref[pl.ds(b, 1), :])


def _mod_kernel(s_ref, w_ref, b_ref, o_ref):
    s = s_ref[...]
    s = s * jax.nn.sigmoid(s)
    o_ref[0] = jnp.dot(s, w_ref[0], preferred_element_type=F32,
                       precision=lax.Precision.HIGHEST) + b_ref[0]


def _mod_call(s8, w_mod, b_mod):
    depth, d, n = w_mod.shape
    tn = _tile(n, 1024, LANES)
    return pl.pallas_call(
        _mod_kernel,
        grid=(depth, n // tn),
        in_specs=[pl.BlockSpec((8, d), lambda l, j: (0, 0)),
                  pl.BlockSpec((1, d, tn), lambda l, j: (l, 0, j)),
                  pl.BlockSpec((1, 1, tn), lambda l, j: (l, 0, j))],
        out_specs=pl.BlockSpec((1, 8, tn), lambda l, j: (l, 0, j)),
        out_shape=jax.ShapeDtypeStruct((depth, 8, n), F32),
        compiler_params=_cparams(("parallel", "parallel"), 32),
        name="mod",
    )(s8, w_mod, b_mod.reshape(depth, 1, n))


def _inproj_kernel(x_ref, shift_ref, scale_ref, g_ref, w_ref, o_ref, h_scr, *, tm, rows, ctx_len):
    b = pl.program_id(0)
    i = pl.program_id(1)

    @pl.when(pl.program_id(2) == 0)
    def _():
        def norm_rows(c, carry):
            r0 = pl.multiple_of(c * rows, rows)
            x = x_ref[0, pl.ds(r0, rows), :]
            var = jnp.mean(x * x, axis=-1, keepdims=True)
            y = x * lax.rsqrt(var + EPS) * g_ref[...]
            row = i * tm + r0 + lax.broadcasted_iota(jnp.int32, (rows, 1), 0)
            is_ctx = row < ctx_len
            sh = _row_select(shift_ref, b, is_ctx)
            sc = _row_select(scale_ref, b, is_ctx)
            h_scr[pl.ds(r0, rows), :] = (y * (1.0 + sc) + sh).astype(BF16)
            return carry

        lax.fori_loop(0, tm // rows, norm_rows, 0)

    o_ref[0] = jnp.dot(h_scr[...], w_ref[...], preferred_element_type=F32).astype(BF16)


def _inproj_call(x, mod, layer, gain, w_bf16, ctx_len):
    bsz, l, d = x.shape
    n = w_bf16.shape[2]
    tm = _tile(l, 1056, 16)
    rows = _tile(tm, 352, 16)
    tn = 1024
    kern = functools.partial(_inproj_kernel, tm=tm, rows=rows, ctx_len=ctx_len)
    return pl.pallas_call(
        kern,
        grid=(bsz, l // tm, n // tn),
        in_specs=[pl.BlockSpec((1, tm, d), lambda b, i, j: (b, i, 0)),
                  pl.BlockSpec((None, 8, d), lambda b, i, j: (layer, 0, 0)),
                  pl.BlockSpec((None, 8, d), lambda b, i, j: (layer, 0, 1)),
                  pl.BlockSpec((1, d), lambda b, i, j: (0, 0)),
                  pl.BlockSpec((None, d, tn), lambda b, i, j: (layer, 0, j))],
        out_specs=pl.BlockSpec((1, tm, tn), lambda b, i, j: (b, i, j)),
        out_shape=jax.ShapeDtypeStruct((bsz, l, n), BF16),
        scratch_shapes=[pltpu.VMEM((tm, d), BF16)],
        compiler_params=_cparams(("parallel", "parallel", "arbitrary"), 48),
        name="inproj",
    )(x, mod, mod, gain.reshape(1, d), w_bf16)


def _norm_rope(t, gain, cos, sin_signed, lane_lo):
    var = jnp.mean(t * t, axis=-1, keepdims=True)
    y = t * lax.rsqrt(var + EPS) * gain
    swapped = jnp.where(lane_lo, pltpu.roll(y, HEAD_DIM - ROPE_PAIRS, 1), pltpu.roll(y, ROPE_PAIRS, 1))
    return y * cos + swapped * sin_signed


def _qkprep_kernel(q_ref, k_ref, v_ref, cos_ref, sin_ref, qg_ref, kg_ref, qo_ref, ko_ref, vo_ref):
    cos = cos_ref[...]
    sin = sin_ref[...]
    lane = lax.broadcasted_iota(jnp.int32, cos.shape, 1)
    lane_lo = (lane & ROPE_PAIRS) == 0
    qg = qg_ref[...]
    kg = kg_ref[...]
    for h in range(N_HEADS):
        sl = slice(h * HEAD_DIM, (h + 1) * HEAD_DIM)
        t = q_ref[0, :, sl].astype(F32)
        qo_ref[0, :, sl] = (_norm_rope(t, qg, cos, sin, lane_lo) * Q_SCALE).astype(BF16)
    k_tail = jnp.where(lane == 0, 1.0, 0.0).astype(BF16)
    v_tail = jnp.ones(cos.shape, BF16)
    for h in range(N_KV_HEADS):
        sl = slice(h * HEAD_DIM, (h + 1) * HEAD_DIM)
        lo = slice(2 * h * HEAD_DIM, (2 * h + 1) * HEAD_DIM)
        hi = slice((2 * h + 1) * HEAD_DIM, (2 * h + 2) * HEAD_DIM)
        t = k_ref[0, :, sl].astype(F32)
        ko_ref[0, :, lo] = _norm_rope(t, kg, cos, sin, lane_lo).astype(BF16)
        ko_ref[0, :, hi] = k_tail
        vo_ref[0, :, lo] = v_ref[0, :, sl]
        vo_ref[0, :, hi] = v_tail


def _qkprep_call(p, cos, sin_signed, q_gain, k_gain):
    bsz, l, _ = p.shape
    qw = N_HEADS * HEAD_DIM
    kw = N_KV_HEADS * HEAD_DIM
    tm = _tile(l, 768, 16)
    return pl.pallas_call(
        _qkprep_kernel,
        grid=(bsz, l // tm),
        in_specs=[pl.BlockSpec((1, tm, qw), lambda b, i: (b, i, 0)),
                  pl.BlockSpec((1, tm, kw), lambda b, i: (b, i, qw // kw)),
                  pl.BlockSpec((1, tm, kw), lambda b, i: (b, i, qw // kw + 1)),
                  pl.BlockSpec((tm, HEAD_DIM), lambda b, i: (i, 0)),
                  pl.BlockSpec((tm, HEAD_DIM), lambda b, i: (i, 0)),
                  pl.BlockSpec((1, HEAD_DIM), lambda b, i: (0, 0)),
                  pl.BlockSpec((1, HEAD_DIM), lambda b, i: (0, 0))],
        out_specs=[pl.BlockSpec((1, tm, qw), lambda b, i: (b, i, 0)),
                   pl.BlockSpec((1, tm, 2 * kw), lambda b, i: (b, i, 0)),
                   pl.BlockSpec((1, tm, 2 * kw), lambda b, i: (b, i, 0))],
        out_shape=[jax.ShapeDtypeStruct((bsz, l, qw), BF16),
                   jax.ShapeDtypeStruct((bsz, l, 2 * kw), BF16),
                   jax.ShapeDtypeStruct((bsz, l, 2 * kw), BF16)],
        compiler_params=_cparams(("parallel", "parallel"), 40),
        name="qkprep",
    )(p, p, p, cos, sin_signed, q_gain.reshape(1, HEAD_DIM), k_gain.reshape(1, HEAD_DIM))


def _attn_kernel(q_ref, k_ref, v_ref, qg_ref, kg_ref, o_ref, qs_scr, acc_scr, sa_scr, sb_scr, m_scr, l_scr,
                 *, tq, tkc, ctx_len, n_lat_chunks):
    qi = pl.program_id(2)
    q = q_ref[0]
    gq = jnp.max(jnp.abs(qg_ref[...]), axis=-1, keepdims=True)
    gk = jnp.max(jnp.abs(kg_ref[...]), axis=-1, keepdims=True)
    bound = (SHIFT_MARGIN * HEAD_DIM * Q_SCALE) * gq * gk
    lane = lax.broadcasted_iota(jnp.int32, (1, HEAD_DIM), 1)
    tail = jnp.where(lane == 0, -bound, 0.0).astype(BF16)
    for h in range(GROUP):
        qs_scr[h * tq:(h + 1) * tq, :HEAD_DIM] = q[:, h * HEAD_DIM:(h + 1) * HEAD_DIM]
        qs_scr[h * tq:(h + 1) * tq, HEAD_DIM:] = jnp.broadcast_to(tail, (tq, HEAD_DIM))
    n = jnp.where(qi < ctx_len // tq, 0, n_lat_chunks)

    def lat_start(c):
        return pl.multiple_of(ctx_len + c * tkc, math.gcd(ctx_len, tkc))

    def write(out):
        for h in range(GROUP):
            o_ref[0, :, h * HEAD_DIM:(h + 1) * HEAD_DIM] = out[h * tq:(h + 1) * tq, :].astype(BF16)

    def scores(start, size):
        return lax.dot_general(qs_scr[...], k_ref[0, pl.ds(start, size), :], (((1,), (1,)), ((), ())),
                               preferred_element_type=F32)

    def accumulate(s, start, size):
        p = jnp.exp2(s).astype(BF16)
        acc_scr[...] += jnp.dot(p, v_ref[0, pl.ds(start, size), :], preferred_element_type=F32)

    acc_scr[...] = jnp.zeros(acc_scr.shape, F32)

    @pl.when(n == 0)
    def _():
        accumulate(scores(0, ctx_len), 0, ctx_len)

    @pl.when(n > 0)
    def _():
        bounds = [(0, ctx_len + tkc)] + [(ctx_len + c * tkc, tkc) for c in range(1, n_lat_chunks)]
        bufs = (sa_scr, sb_scr)
        bufs[0][:, :bounds[0][1]] = scores(*bounds[0])
        for c, (start, size) in enumerate(bounds):
            if c + 1 < len(bounds):
                nxt_start, nxt_size = bounds[c + 1]
                bufs[(c + 1) % 2][:, :nxt_size] = scores(nxt_start, nxt_size)
            accumulate(bufs[c % 2][:, :size], start, size)

    acc = acc_scr[...]
    den = acc[:, HEAD_DIM:]
    write(acc[:, :HEAD_DIM] / den)

    @pl.when(jnp.logical_not(jnp.min(den) >= MIN_DENOM))
    def _():
        m_scr[...] = jnp.full(m_scr.shape, NEG_INF, F32)
        l_scr[...] = jnp.zeros(l_scr.shape, F32)
        acc_scr[...] = jnp.zeros(acc_scr.shape, F32)

        def online_chunk(start, size):
            k = k_ref[0, pl.ds(start, size), :HEAD_DIM]
            v = v_ref[0, pl.ds(start, size), :HEAD_DIM]
            s = lax.dot_general(qs_scr[:, :HEAD_DIM], k, (((1,), (1,)), ((), ())),
                                preferred_element_type=F32)
            m_old = m_scr[...]
            m_new = jnp.maximum(m_old, jnp.max(s, axis=-1, keepdims=True))
            alpha = jnp.exp2(m_old - m_new)
            p = jnp.exp2(s - m_new)
            l_scr[...] = alpha * l_scr[...] + jnp.sum(p, axis=-1, keepdims=True)
            acc_scr[:, :HEAD_DIM] = alpha * acc_scr[:, :HEAD_DIM] + jnp.dot(
                p.astype(BF16), v, preferred_element_type=F32)
            m_scr[...] = m_new

        online_chunk(0, ctx_len)

        def online_body(c, carry):
            online_chunk(lat_start(c), tkc)
            return carry

        lax.fori_loop(0, n, online_body, 0)
        write(acc_scr[:, :HEAD_DIM] / l_scr[...])


def _attn_call(qn, ke, ve, q_gain, k_gain, ctx_len):
    bsz, l, qw = qn.shape
    tq = 256
    assert ctx_len % tq == 0 and l % tq == 0
    t_lat = l - ctx_len
    assert t_lat % (ATTN_CHUNKS * LANES) == 0
    tkc = t_lat // ATTN_CHUNKS
    gw = GROUP * HEAD_DIM
    ew = 2 * HEAD_DIM
    kern = functools.partial(_attn_kernel, tq=tq, tkc=tkc, ctx_len=ctx_len, n_lat_chunks=t_lat // tkc)
    return pl.pallas_call(
        kern,
        grid=(bsz, N_KV_HEADS, l // tq),
        in_specs=[pl.BlockSpec((1, tq, gw), lambda b, g, i: (b, i, g)),
                  pl.BlockSpec((1, l, ew), lambda b, g, i: (b, 0, g)),
                  pl.BlockSpec((1, l, ew), lambda b, g, i: (b, 0, g)),
                  pl.BlockSpec((1, HEAD_DIM), lambda b, g, i: (0, 0)),
                  pl.BlockSpec((1, HEAD_DIM), lambda b, g, i: (0, 0))],
        out_specs=pl.BlockSpec((1, tq, gw), lambda b, g, i: (b, i, g)),
        out_shape=jax.ShapeDtypeStruct((bsz, l, qw), BF16),
        scratch_shapes=[pltpu.VMEM((GROUP * tq, ew), BF16),
                        pltpu.VMEM((GROUP * tq, ew), F32),
                        pltpu.VMEM((GROUP * tq, ctx_len + tkc), F32),
                        pltpu.VMEM((GROUP * tq, ctx_len + tkc), F32),
                        pltpu.VMEM((GROUP * tq, 1), F32),
                        pltpu.VMEM((GROUP * tq, 1), F32)],
        compiler_params=_cparams(("parallel", "parallel", "arbitrary"), 56),
        name="attn",
    )(qn, ke, ve, q_gain.reshape(1, HEAD_DIM), k_gain.reshape(1, HEAD_DIM))


def _segment_bounds(row, ctx_len, seq_len):
    is_ctx = row < ctx_len
    first = jnp.where(is_ctx, 0, ctx_len)
    last = jnp.where(is_ctx, ctx_len - 1, seq_len - 1)
    return first, last


def _shift_down(x, prev, k, local):
    y = pltpu.roll(x, k, 0)
    for r in range(k):
        y = jnp.where(local == r, prev[HALO - k + r:HALO - k + r + 1, :], y)
    return y


def _shift_up(x, nxt, k, local, tt):
    y = pltpu.roll(x, tt - k, 0)
    for r in range(k):
        y = jnp.where(local == tt - k + r, nxt[r:r + 1, :], y)
    return y


def _halo_shifts(x, prev, nxt):
    tt = x.shape[0]
    sub = lax.broadcasted_iota(jnp.int32, (SUBLANES, x.shape[1]), 0)

    def down(k):
        y = pltpu.roll(x, k, 0)
        head = y[:SUBLANES]
        for r in range(k):
            head = jnp.where(sub == r, prev[HALO - k + r:HALO - k + r + 1, :], head)
        return jnp.concatenate([head, y[SUBLANES:]], axis=0)

    def up(k):
        y = pltpu.roll(x, tt - k, 0)
        tail = y[tt - SUBLANES:]
        for r in range(k):
            tail = jnp.where(sub == SUBLANES - k + r, nxt[r:r + 1, :], tail)
        return jnp.concatenate([y[:tt - SUBLANES], tail], axis=0)

    return down, up


def _scan_tile(a, d, carry, reverse):
    n_groups = a.shape[0] // SUBLANES
    sub = lax.broadcasted_iota(jnp.int32, (SUBLANES, a.shape[1]), 0)
    groups = []
    for v in range(n_groups):
        av = a[v * SUBLANES:(v + 1) * SUBLANES]
        dv = d[v * SUBLANES:(v + 1) * SUBLANES]
        for k in (1, 2, 4):
            keep = sub < SUBLANES - k if reverse else sub >= k
            shift = SUBLANES - k if reverse else k
            a_n = jnp.where(keep, pltpu.roll(av, shift, 0), 1.0)
            d_n = jnp.where(keep, pltpu.roll(dv, shift, 0), 0.0)
            dv = av * d_n + dv
            av = av * a_n
        groups.append((av, dv))
    hs = [None] * n_groups
    for v in (reversed(range(n_groups)) if reverse else range(n_groups)):
        av, dv = groups[v]
        hv = dv + av * carry
        carry = hv[0:1] if reverse else hv[SUBLANES - 1:SUBLANES]
        hs[v] = hv
    return jnp.concatenate(hs, axis=0), carry


def _lru_kernel(*refs, reverse, tt, ctx_len, n_sub):
    if reverse:
        (x_ref, xp_ref, xn_ref, cw_ref, cb_ref, wa_ref, ba_ref, wx_ref, bx_ref, lam_ref,
         hf_ref, g_ref, o_ref, carry_scr) = refs
    else:
        (x_ref, xp_ref, xn_ref, cw_ref, cb_ref, wa_ref, ba_ref, wx_ref, bx_ref, lam_ref,
         o_ref, carry_scr) = refs
    s = pl.program_id(2)
    nt = pl.num_programs(2)
    ti = jnp.where(s == 0, 0, nt - s) if reverse else s

    @pl.when(s == 0)
    def _():
        carry_scr[...] = jnp.zeros(carry_scr.shape, F32)

    prev_ok = jnp.logical_and(ti != 0, ti * tt != ctx_len)
    next_ok = jnp.logical_and(ti != nt - 1, (ti + 1) * tt != ctx_len)
    x = x_ref[0].astype(F32)
    xp = jnp.where(prev_ok, xp_ref[0].astype(F32), 0.0)
    xn = jnp.where(next_ok, xn_ref[0].astype(F32), 0.0)
    down, up = _halo_shifts(x, xp, xn)
    cw = cw_ref[...]
    u = cb_ref[...] + cw[2:3, :] * x + cw[0:1, :] * down(2) + cw[1:2, :] * down(1) + cw[3:4, :] * up(1)
    nlam = -lam_ref[...]
    softplus = jnp.maximum(nlam, 0.0) + jnp.log1p(jnp.exp(-jnp.abs(nlam)))

    for j in range(n_sub):
        ls = slice(j * LRU_BLOCK, (j + 1) * LRU_BLOCK)
        uj = u[:, ls]
        ub = uj.astype(BF16)
        r = jax.nn.sigmoid(jnp.dot(ub, wa_ref[j], preferred_element_type=F32) + ba_ref[:, ls])
        i = jax.nn.sigmoid(jnp.dot(ub, wx_ref[j], preferred_element_type=F32) + bx_ref[:, ls])
        a = jnp.exp((-LRU_C) * r * softplus[:, ls])
        d = jnp.sqrt(1.0 - a * a) * (i * uj)
        h, carry = _scan_tile(a, d, carry_scr[:, ls], reverse)
        carry_scr[:, ls] = carry
        if reverse:
            o_ref[0, :, ls] = ((hf_ref[0, :, ls] + h) * _gelu(g_ref[0, :, ls].astype(F32))).astype(BF16)
        else:
            o_ref[0, :, ls] = h


def _lru_call(p, conv_w, conv_b, wa, ba, wx, bx, lam, ctx_len, reverse, h_fwd=None):
    bsz, l, _ = p.shape
    c = conv_w.shape[1]
    tt = 256
    assert ctx_len % tt == 0 and l % tt == 0
    nt = l // tt
    n_sub = 8
    cw = n_sub * LRU_BLOCK
    nb = c // cw
    x_col0 = (N_HEADS + 2 * N_KV_HEADS) * HEAD_DIM // cw
    g_col0 = x_col0 + nb
    hb = tt // HALO
    n_hblk = l // HALO

    def tile_of(s):
        return jnp.where(s == 0, 0, nt - s) if reverse else s

    x_map = lambda b, cb, s: (b, tile_of(s), x_col0 + cb)
    prev_map = lambda b, cb, s: (b, jnp.maximum(tile_of(s) * hb - 1, 0), x_col0 + cb)
    next_map = lambda b, cb, s: (b, jnp.minimum((tile_of(s) + 1) * hb, n_hblk - 1), x_col0 + cb)
    vec_map = lambda b, cb, s: (0, cb)
    w_spec = pl.BlockSpec((n_sub, LRU_BLOCK, LRU_BLOCK), lambda b, cb, s: (cb, 0, 0))
    in_specs = [pl.BlockSpec((1, tt, cw), x_map),
                pl.BlockSpec((1, HALO, cw), prev_map),
                pl.BlockSpec((1, HALO, cw), next_map),
                pl.BlockSpec((LRU_CONV, cw), vec_map),
                pl.BlockSpec((1, cw), vec_map),
                w_spec,
                pl.BlockSpec((1, cw), vec_map),
                w_spec,
                pl.BlockSpec((1, cw), vec_map),
                pl.BlockSpec((1, cw), vec_map)]
    args = [p, p, p, conv_w, conv_b.reshape(1, c), wa.astype(BF16), ba.reshape(1, c),
            wx.astype(BF16), bx.reshape(1, c), lam.reshape(1, c)]
    out_map = lambda b, cb, s: (b, tile_of(s), cb)
    if reverse:
        in_specs += [pl.BlockSpec((1, tt, cw), out_map),
                     pl.BlockSpec((1, tt, cw), lambda b, cb, s: (b, tile_of(s), g_col0 + cb))]
        args += [h_fwd, p]
        out_dtype = BF16
    else:
        out_dtype = F32
    kern = functools.partial(_lru_kernel, reverse=reverse, tt=tt, ctx_len=ctx_len, n_sub=n_sub)
    return pl.pallas_call(
        kern,
        grid=(bsz, nb, nt),
        in_specs=in_specs,
        out_specs=pl.BlockSpec((1, tt, cw), out_map),
        out_shape=jax.ShapeDtypeStruct((bsz, l, c), out_dtype),
        scratch_shapes=[pltpu.VMEM((1, cw), F32)],
        compiler_params=_cparams(("parallel", "parallel", "arbitrary"), 32),
        name="lru_rev" if reverse else "lru_fwd",
    )(*args)


def _sconv_kernel(bg_ref, cg_ref, u_ref, cgp_ref, up_ref, cgn_ref, un_ref, w_ref, b_ref, o_ref,
                  *, tt, ctx_len, seq_len):
    ti = pl.program_id(1)
    z = cg_ref[0].astype(F32) * u_ref[0].astype(F32)
    zp = cgp_ref[0].astype(F32) * up_ref[0].astype(F32)
    zn = cgn_ref[0].astype(F32) * un_ref[0].astype(F32)
    local = lax.broadcasted_iota(jnp.int32, (tt, 1), 0)
    row = ti * tt + local
    first, last = _segment_bounds(row, ctx_len, seq_len)
    w = w_ref[...]
    y = b_ref[...] + w[1:2, :] * z
    y = y + w[0:1, :] * jnp.where(row - 1 >= first, _shift_down(z, zp, 1, local), 0.0)
    y = y + w[2:3, :] * jnp.where(row + 1 <= last, _shift_up(z, zn, 1, local, tt), 0.0)
    o_ref[0] = (bg_ref[0].astype(F32) * y).astype(BF16)


def _sconv_call(p, w, b, ctx_len):
    bsz, l, _ = p.shape
    c = w.shape[1]
    tt = _tile(l, 768, 16)
    tc = 512
    ncb = c // tc
    b_col0 = (N_HEADS * HEAD_DIM + 2 * N_KV_HEADS * HEAD_DIM + 2 * c) // tc
    c_col0 = b_col0 + ncb
    u_col0 = c_col0 + ncb
    hb = tt // HALO
    n_hblk = l // HALO
    prev = lambda i: jnp.maximum(i * hb - 1, 0)
    nxt = lambda i: jnp.minimum((i + 1) * hb, n_hblk - 1)
    kern = functools.partial(_sconv_kernel, tt=tt, ctx_len=ctx_len, seq_len=l)
    return pl.pallas_call(
        kern,
        grid=(bsz, l // tt, ncb),
        in_specs=[pl.BlockSpec((1, tt, tc), lambda b_, i, j: (b_, i, b_col0 + j)),
                  pl.BlockSpec((1, tt, tc), lambda b_, i, j: (b_, i, c_col0 + j)),
                  pl.BlockSpec((1, tt, tc), lambda b_, i, j: (b_, i, u_col0 + j)),
                  pl.BlockSpec((1, HALO, tc), lambda b_, i, j: (b_, prev(i), c_col0 + j)),
                  pl.BlockSpec((1, HALO, tc), lambda b_, i, j: (b_, prev(i), u_col0 + j)),
                  pl.BlockSpec((1, HALO, tc), lambda b_, i, j: (b_, nxt(i), c_col0 + j)),
                  pl.BlockSpec((1, HALO, tc), lambda b_, i, j: (b_, nxt(i), u_col0 + j)),
                  pl.BlockSpec((SC_CONV, tc), lambda b_, i, j: (0, j)),
                  pl.BlockSpec((1, tc), lambda b_, i, j: (0, j))],
        out_specs=pl.BlockSpec((1, tt, tc), lambda b_, i, j: (b_, i, j)),
        out_shape=jax.ShapeDtypeStruct((bsz, l, c), BF16),
        compiler_params=_cparams(("parallel", "parallel", "parallel"), 32),
        name="sconv",
    )(p, p, p, p, p, p, p, w, b.reshape(1, c))


def _merge_kernel(xa_ref, xl_ref, xs_ref, ga_ref, gl_ref, gs_ref, wa_ref, wl_ref, ws_ref, o_ref):
    y = jax.nn.sigmoid(ga_ref[0].astype(F32)) * jnp.dot(xa_ref[0], wa_ref[...], preferred_element_type=F32)
    y = y + jax.nn.sigmoid(gl_ref[0].astype(F32)) * jnp.dot(xl_ref[0], wl_ref[...], preferred_element_type=F32)
    y = y + jax.nn.sigmoid(gs_ref[0].astype(F32)) * jnp.dot(xs_ref[0], ws_ref[...], preferred_element_type=F32)
    o_ref[0] = y.astype(BF16)


def _merge_call(x_att, x_lru, x_sc, p, layer, w_att, w_lru, w_sc):
    bsz, l, d = x_att.shape
    tm = _tile(l, 768, 16)
    tn = 512
    g_col0 = (p.shape[2] - 3 * d) // tn
    nj = d // tn
    xspec = pl.BlockSpec((1, tm, d), lambda b, i, j: (b, i, 0))
    wspec = pl.BlockSpec((None, d, tn), lambda b, i, j: (layer, 0, j))
    gspec = lambda k: pl.BlockSpec((1, tm, tn), lambda b, i, j: (b, i, g_col0 + k * nj + j))
    return pl.pallas_call(
        _merge_kernel,
        grid=(bsz, l // tm, nj),
        in_specs=[xspec, xspec, xspec, gspec(0), gspec(1), gspec(2), wspec, wspec, wspec],
        out_specs=pl.BlockSpec((1, tm, tn), lambda b, i, j: (b, i, j)),
        out_shape=jax.ShapeDtypeStruct((bsz, l, d), BF16),
        compiler_params=_cparams(("parallel", "parallel", "arbitrary"), 48),
        name="merge",
    )(x_att, x_lru, x_sc, p, p, p, w_att, w_lru, w_sc)


def _outproj_kernel(y_ref, w_ref, x_ref, gate_ref, shift_ref, scale_ref, g_ref, xo_ref, ht_ref,
                    *, tm, ctx_len):
    b = pl.program_id(0)
    i = pl.program_id(1)
    row = i * tm + lax.broadcasted_iota(jnp.int32, (tm, 1), 0)
    is_ctx = row < ctx_len
    acc = jnp.dot(y_ref[0], w_ref[...], preferred_element_type=F32)
    xn = x_ref[0] + _row_select(gate_ref, b, is_ctx) * acc
    xo_ref[0] = xn
    var = jnp.mean(xn * xn, axis=-1, keepdims=True)
    h = xn * lax.rsqrt(var + EPS) * g_ref[...]
    h = h * (1.0 + _row_select(scale_ref, b, is_ctx)) + _row_select(shift_ref, b, is_ctx)
    ht_ref[0] = h.T.astype(BF16)


def _outproj_call(y, w_out, x, mod, layer, gain, ctx_len):
    bsz, l, d = x.shape
    tm = _tile(l, 384, LANES)
    kern = functools.partial(_outproj_kernel, tm=tm, ctx_len=ctx_len)
    mspec = lambda k: pl.BlockSpec((None, 8, d), lambda b, i: (layer, 0, k))
    return pl.pallas_call(
        kern,
        grid=(bsz, l // tm),
        in_specs=[pl.BlockSpec((1, tm, d), lambda b, i: (b, i, 0)),
                  pl.BlockSpec((None, d, d), lambda b, i: (layer, 0, 0)),
                  pl.BlockSpec((1, tm, d), lambda b, i: (b, i, 0)),
                  mspec(2), mspec(3), mspec(4),
                  pl.BlockSpec((1, d), lambda b, i: (0, 0))],
        out_specs=[pl.BlockSpec((1, tm, d), lambda b, i: (b, i, 0)),
                   pl.BlockSpec((1, d, tm), lambda b, i: (b, 0, i))],
        out_shape=[jax.ShapeDtypeStruct((bsz, l, d), F32),
                   jax.ShapeDtypeStruct((bsz, d, l), BF16)],
        compiler_params=_cparams(("parallel", "parallel"), 48),
        name="outproj",
    )(y, w_out, x, mod, mod, mod, gain.reshape(1, d))


_CAND_ROWS = tuple(PEER_TOPK // (i + 1) for i in range(PEER_TOPK))


def _top16(s, v_scr):
    n = s.shape[0]
    key = lax.broadcasted_iota(jnp.int32, s.shape, 0)

    def body(r, carry):
        work, rank = carry
        m = jnp.max(work, axis=0, keepdims=True)
        v_scr[pl.ds(r, 1), :] = m
        first = jnp.min(jnp.where(work == m, key, n), axis=0, keepdims=True)
        sel = key == first
        return jnp.where(sel, NEG_INF, work), jnp.where(sel, jnp.asarray(r, F32), rank)

    _, rank = lax.fori_loop(0, PEER_TOPK, body, (s, jnp.full(s.shape, float(PEER_TOPK), F32)))
    return rank


MARK = -(2.0 ** 127)


def _top16_distinct(ss, v_scrs):
    def body(r, works):
        mark = jnp.asarray(r, F32) * (MARK / 32.0) + MARK
        out = []
        for work, v_scr in zip(works, v_scrs):
            m = jnp.max(work, axis=0, keepdims=True)
            v_scr[pl.ds(r, 1), :] = m
            out.append(jnp.where(work == m, mark, work))
        return tuple(out)

    works = lax.fori_loop(0, PEER_TOPK, body, tuple(ss), unroll=True)
    return [jnp.where(w <= MARK, (MARK - w) * (-32.0 / MARK), float(PEER_TOPK)) for w in works]


def _peer_topk_kernel(ht_ref, wq_ref, k1_ref, k2_ref, cnt_ref, e1_ref, rk_ref, e2_ref,
                      q_scr, v_scr, *, tm, n_par):
    half = PEER_NKEYS
    q_scr[...] = jnp.dot(wq_ref[...], ht_ref[0], preferred_element_type=F32).astype(BF16)
    sub = lax.broadcasted_iota(jnp.int32, (SUBLANES, LANES), 0)
    big = PEER_TOPK * PEER_TOPK

    def candidates(v1, v2):
        pieces, poss = [], []
        for i in range(SUBLANES):
            for j0 in range(0, _CAND_ROWS[i], SUBLANES):
                c = v1[i:i + 1, :] + v2[j0:j0 + SUBLANES, :]
                valid = sub + j0 < _CAND_ROWS[i]
                pieces.append(jnp.where(valid, c, NEG_INF))
                poss.append(jnp.where(valid, i * PEER_TOPK + j0 + sub, big))
        pieces.append(v1[SUBLANES:, :] + v2[0:1, :])
        poss.append((sub + SUBLANES) * PEER_TOPK)
        return pieces, poss

    def pick_exact(pieces, poss):
        def pick(_, carry2):
            cs, sels = carry2
            m = functools.reduce(jnp.maximum, cs)
            m = jnp.max(m, axis=0, keepdims=True)
            cand_pos = functools.reduce(jnp.minimum, [jnp.where(c == m, p_, big) for c, p_ in zip(cs, poss)])
            first = jnp.min(cand_pos, axis=0, keepdims=True)
            hit = [p_ == first for p_ in poss]
            cs = tuple(jnp.where(hh, NEG_INF, c) for hh, c in zip(hit, cs))
            sels = tuple(jnp.where(hh, 1.0, s_) for hh, s_ in zip(hit, sels))
            return cs, sels

        zeros = tuple(jnp.zeros((SUBLANES, LANES), F32) for _ in pieces)
        return lax.fori_loop(0, PEER_TOPK, pick, (tuple(pieces), zeros))[1]

    def pick_distinct(groups):
        npc = len(groups[0])

        def pick(_, cs):
            out = []
            for g in range(len(groups)):
                grp = cs[g * npc:(g + 1) * npc]
                m = jnp.max(functools.reduce(jnp.maximum, grp), axis=0, keepdims=True)
                out.extend(jnp.where(c == m, MARK, c) for c in grp)
            return tuple(out)

        marked = lax.fori_loop(0, PEER_TOPK, pick, tuple(c for grp in groups for c in grp), unroll=True)
        return [tuple(jnp.where(c == MARK, 1.0, 0.0) for c in marked[g * npc:(g + 1) * npc])
                for g in range(len(groups))]

    def compute(h, lss, exact):
        base = pl.multiple_of(h * 2 * half, 2 * half)
        n_g = len(lss)
        s1s = [jnp.dot(k1_ref[h], q_scr[pl.ds(base, half), ls], preferred_element_type=F32) for ls in lss]
        s2s = [jnp.dot(k2_ref[h], q_scr[pl.ds(base + half, half), ls], preferred_element_type=F32)
               for ls in lss]
        v1_refs = [v_scr.at[2 * g] for g in range(n_g)]
        v2_refs = [v_scr.at[2 * g + 1] for g in range(n_g)]
        if exact:
            rank1s = [_top16(s, r) for s, r in zip(s1s, v1_refs)]
            rank2s = [_top16(s, r) for s, r in zip(s2s, v2_refs)]
        else:
            ranks = _top16_distinct(tuple(s1s + s2s), tuple(v1_refs + v2_refs))
            rank1s, rank2s = ranks[:n_g], ranks[n_g:]
        v1s = [r[...] for r in v1_refs]
        v2s = [r[...] for r in v2_refs]
        cands = [candidates(v1, v2) for v1, v2 in zip(v1s, v2s)]
        if exact:
            all_sels = [pick_exact(pieces, poss) for pieces, poss in cands]
        else:
            all_sels = pick_distinct([pieces for pieces, _ in cands])
        most = None
        for g in range(n_g):
            most_g = finish(h, lss[g], s1s[g], s2s[g], rank1s[g], rank2s[g], v1s[g], v2s[g],
                            cands[g][0], all_sels[g])
            most = most_g if most is None else jnp.maximum(most, most_g)
        return most

    def finish(h, ls, s1, s2, rank1, rank2, v1, v2, orig, sels):
        top = v1[0:1, :] + v2[0:1, :]
        zsum = functools.reduce(
            lambda a_, b_: a_ + b_,
            [jnp.where(s_ > 0.0, jnp.exp(o - top), 0.0) for s_, o in zip(sels, orig)])
        zinv = 1.0 / jnp.sum(zsum, axis=0, keepdims=True)

        counts = []
        pi = 0
        for i in range(SUBLANES):
            c = None
            for j0 in range(0, _CAND_ROWS[i], SUBLANES):
                part = jnp.sum(sels[pi], axis=0, keepdims=True)
                c = part if c is None else c + part
                pi += 1
            counts.append(c)
        tail = sels[pi]
        for i in range(SUBLANES, PEER_TOPK):
            counts.append(tail[i - SUBLANES:i - SUBLANES + 1, :])
        cnt = jnp.zeros(rank1.shape, F32)
        for i in range(PEER_TOPK):
            cnt = jnp.where(rank1 == float(i), counts[i], cnt)

        cnt_ref[0, h, :, ls] = cnt
        e1_ref[0, h, :, ls] = jnp.exp(s1 - v1[0:1, :]) * zinv
        rk_ref[0, h, :, ls] = rank2.astype(BF16)
        e2_ref[0, h, :, ls] = jnp.exp(s2 - v2[0:1, :]).astype(BF16)
        ranked = jnp.where(rank1 < PEER_TOPK, 1.0, 0.0)
        ranked = jnp.maximum(jnp.sum(ranked, axis=0, keepdims=True),
                             jnp.sum(jnp.where(rank2 < PEER_TOPK, 1.0, 0.0), axis=0, keepdims=True))
        picked = jnp.sum(functools.reduce(lambda a_, b_: a_ + b_, sels), axis=0, keepdims=True)
        return jnp.max(jnp.maximum(ranked, picked))

    n_trips = tm // (n_par * LANES)

    def head_lane_groups(idx, carry):
        h = idx // n_trips
        first = (idx % n_trips) * n_par
        lss = [pl.ds(pl.multiple_of((first + g) * LANES, LANES), LANES) for g in range(n_par)]
        most = compute(h, lss, exact=False)

        @pl.when(most > PEER_TOPK)
        def _():
            compute(h, lss, exact=True)

        return carry

    lax.fori_loop(0, PEER_HEADS * n_trips, head_lane_groups, 0)


def _peer_topk_call(ht, layer, wq_t, k1, k2):
    bsz, d, l = ht.shape
    qd = wq_t.shape[1]
    n_par = 2
    tm = _tile(l, 768, n_par * LANES)
    kern = functools.partial(_peer_topk_kernel, tm=tm, n_par=n_par)
    ospec = pl.BlockSpec((1, PEER_HEADS, PEER_NKEYS, tm), lambda b, i: (b, 0, 0, i))
    oshape = jax.ShapeDtypeStruct((bsz, PEER_HEADS, PEER_NKEYS, l), F32)
    kspec = pl.BlockSpec((None, PEER_HEADS, PEER_NKEYS, PEER_NKEYS), lambda b, i: (layer, 0, 0, 0))
    return pl.pallas_call(
        kern,
        grid=(bsz, l // tm),
        in_specs=[pl.BlockSpec((1, d, tm), lambda b, i: (b, 0, i)),
                  pl.BlockSpec((None, qd, d), lambda b, i: (layer, 0, 0)),
                  kspec, kspec],
        out_specs=[ospec, ospec, ospec, ospec],
        out_shape=[oshape, oshape, jax.ShapeDtypeStruct(oshape.shape, BF16),
                   jax.ShapeDtypeStruct(oshape.shape, BF16)],
        scratch_shapes=[pltpu.VMEM((qd, tm), BF16),
                        pltpu.VMEM((2 * n_par, PEER_TOPK, LANES), F32)],
        compiler_params=_cparams(("parallel", "parallel"), 56),
        name="peer_topk",
    )(ht, wq_t, k1, k2)


def _peer_dense_kernel(ht_ref, u_ref, vt_ref, cnt_ref, e1_ref, rk_ref, e2_ref, o_ref, wz_scr, *, n_sub):
    e = pl.program_id(2)

    @pl.when(e == 0)
    def _():
        o_ref[...] = jnp.zeros(o_ref.shape, F32)

    ht = ht_ref[0]
    per = 4
    n_split = n_sub // per
    for part in range(n_split):
        s = jnp.dot(u_ref[part * per * PEER_NKEYS:(part + 1) * per * PEER_NKEYS, :], ht,
                    preferred_element_type=F32)
        for a in range(part * per, (part + 1) * per):
            rows = slice(a * PEER_NKEYS, (a + 1) * PEER_NKEYS)
            local = slice((a - part * per) * PEER_NKEYS, (a - part * per + 1) * PEER_NKEYS)
            w = None
            for h in range(PEER_HEADS):
                cnt = cnt_ref[0, h, a:a + 1, :].astype(BF16)
                e1 = e1_ref[0, h, a:a + 1, :].astype(BF16)
                term = jnp.where(rk_ref[0, h] < cnt, e2_ref[0, h] * e1, 0.0)
                w = term if w is None else w + term
            wz_scr[rows, :] = w * _gelu(s[local, :]).astype(BF16)
    o_ref[0] += jnp.dot(vt_ref[...], wz_scr[...], preferred_element_type=F32)


def _peer_dense_call(ht, layer, u_bf16, vt_bf16, cnt, e1n, rank2, e2):
    bsz, d, l = ht.shape
    n_exp = u_bf16.shape[1]
    tm = _tile(l, 768, LANES)
    n_sub = SUBLANES
    te = n_sub * PEER_NKEYS
    kern = functools.partial(_peer_dense_kernel, n_sub=n_sub)
    aspec = pl.BlockSpec((1, PEER_HEADS, n_sub, tm), lambda b, i, e: (b, 0, e, i))
    fspec = pl.BlockSpec((1, PEER_HEADS, PEER_NKEYS, tm), lambda b, i, e: (b, 0, 0, i))
    return pl.pallas_call(
        kern,
        grid=(bsz, l // tm, n_exp // te),
        in_specs=[pl.BlockSpec((1, d, tm), lambda b, i, e: (b, 0, i)),
                  pl.BlockSpec((None, te, d), lambda b, i, e: (layer, e, 0)),
                  pl.BlockSpec((None, d, te), lambda b, i, e: (layer, 0, e)),
                  aspec, aspec, fspec, fspec],
        out_specs=pl.BlockSpec((1, d, tm), lambda b, i, e: (b, 0, i)),
        out_shape=jax.ShapeDtypeStruct((bsz, d, l), F32),
        scratch_shapes=[pltpu.VMEM((te, tm), BF16)],
        compiler_params=_cparams(("parallel", "parallel", "arbitrary"), 58),
        name="peer_dense",
    )(ht, u_bf16, vt_bf16, cnt, e1n, rank2, e2)


def _resid_kernel(x_ref, yt_ref, gate_ref, o_ref, *, tm, ctx_len, first_tile):
    b = pl.program_id(0)
    i = pl.program_id(1) + first_tile
    row = i * tm + lax.broadcasted_iota(jnp.int32, (tm, 1), 0)
    o_ref[0] = x_ref[0] + _row_select(gate_ref, b, row < ctx_len) * yt_ref[0].T


def _resid_call(x, yt, mod, layer, ctx_len, latent_only):
    bsz, l, d = x.shape
    tm = _tile(math.gcd(l, ctx_len), 384, LANES)
    first_tile = ctx_len // tm if latent_only else 0
    n_tiles = l // tm - first_tile
    kern = functools.partial(_resid_kernel, tm=tm, ctx_len=ctx_len, first_tile=first_tile)
    return pl.pallas_call(
        kern,
        grid=(bsz, n_tiles),
        in_specs=[pl.BlockSpec((1, tm, d), lambda b, i: (b, i + first_tile, 0)),
                  pl.BlockSpec((1, d, tm), lambda b, i: (b, 0, i + first_tile)),
                  pl.BlockSpec((None, 8, d), lambda b, i: (layer, 0, 5))],
        out_specs=pl.BlockSpec((1, tm, d), lambda b, i: (b, i, 0)),
        out_shape=jax.ShapeDtypeStruct((bsz, n_tiles * tm, d), F32),
        compiler_params=_cparams(("parallel", "parallel"), 40),
        name="resid",
    )(x, yt, mod)


def _rope_tables(ctx_len, t_lat):
    rows = t_lat // GRID_W
    row = jnp.repeat(jnp.arange(rows, dtype=F32), GRID_W)
    col = jnp.tile(jnp.arange(GRID_W, dtype=F32), rows)
    inv = ROPE_THETA ** (-jnp.arange(ROPE_PAIRS, dtype=F32) / ROPE_PAIRS)
    ang = jnp.concatenate([row[:, None] * inv] * 2 + [col[:, None] * inv] * 2, axis=1)
    ang = jnp.concatenate([jnp.zeros((ctx_len, HEAD_DIM), F32), ang], axis=0)
    sign = jnp.where((jnp.arange(HEAD_DIM) & ROPE_PAIRS) == 0, -1.0, 1.0).astype(F32)
    return jnp.cos(ang), jnp.sin(ang) * sign


def kernel(x, c, ctx, c_ctx, w_mod, b_mod, norm_mix, norm_ffn, w_in, q_norm, k_norm, lru_conv_w, lru_conv_b, lru_wa, lru_ba, lru_wx, lru_bx, lru_lambda, sc_conv_w, sc_conv_b, w_o_attn, w_o_lru, w_o_sc, w_out, peer_wq, peer_k1, peer_k2, peer_u, peer_v):
    bsz, t_lat, d = x.shape
    ctx_len = ctx.shape[1]
    depth = w_mod.shape[0]
    assert bsz == 2, "modulation rows are laid out as [latent 0, latent 1, context]"

    xs = jnp.concatenate([ctx, x], axis=1)
    s8 = jnp.concatenate([c, c_ctx[None, :], jnp.zeros((8 - bsz - 1, d), F32)], axis=0)
    mod = _mod_call(s8, w_mod, b_mod)
    cos, sin_signed = _rope_tables(ctx_len, t_lat)

    w_in_b = w_in.astype(BF16)
    w_att_b, w_lru_b, w_sc_b = w_o_attn.astype(BF16), w_o_lru.astype(BF16), w_o_sc.astype(BF16)
    w_out_b = w_out.astype(BF16)
    wq_t = jnp.swapaxes(peer_wq, 1, 2).astype(BF16)
    k1_b, k2_b = peer_k1.astype(BF16), peer_k2.astype(BF16)
    u_b = peer_u.astype(BF16)
    vt_b = jnp.swapaxes(peer_v, 1, 2).astype(BF16)

    for l in range(depth):
        p = _inproj_call(xs, mod, l, norm_mix[l], w_in_b, ctx_len)
        qn, ke, ve = _qkprep_call(p, cos, sin_signed, q_norm[l], k_norm[l])
        x_att = _attn_call(qn, ke, ve, q_norm[l], k_norm[l], ctx_len)
        lru_args = (lru_conv_w[l], lru_conv_b[l])
        h_fwd = _lru_call(p, *lru_args, lru_wa[l, 0], lru_ba[l, 0], lru_wx[l, 0], lru_bx[l, 0],
                          lru_lambda[l, 0], ctx_len, reverse=False)
        x_lru = _lru_call(p, *lru_args, lru_wa[l, 1], lru_ba[l, 1], lru_wx[l, 1], lru_bx[l, 1],
                          lru_lambda[l, 1], ctx_len, reverse=True, h_fwd=h_fwd)
        x_sc = _sconv_call(p, sc_conv_w[l], sc_conv_b[l], ctx_len)
        y = _merge_call(x_att, x_lru, x_sc, p, l, w_att_b, w_lru_b, w_sc_b)
        xs, ht = _outproj_call(y, w_out_b, xs, mod, l, norm_ffn[l], ctx_len)
        cnt, e1n, rank2, e2 = _peer_topk_call(ht, l, wq_t, k1_b, k2_b)
        yt = _peer_dense_call(ht, l, u_b, vt_b, cnt, e1n, rank2, e2)
        xs = _resid_call(xs, yt, mod, l, ctx_len, latent_only=(l == depth - 1))
    return xs
```

```python
import functools
import math

import jax
import jax.numpy as jnp
from jax import lax
from jax.experimental import pallas as pl
from jax.experimental.pallas import tpu as pltpu

F32 = jnp.float32
BF16 = jnp.bfloat16

GRID_W = 64
EPS = 1e-6

N_HEADS = 16
N_KV_HEADS = 4
HEAD_DIM = 128
GROUP = N_HEADS // N_KV_HEADS
ROPE_PAIRS = HEAD_DIM // 4
ROPE_THETA = 10000.0

LRU_BLOCK = 128
LRU_CONV = 4
LRU_C = 8.0
SC_CONV = 3

PEER_HEADS = 8
PEER_NKEYS = 128
PEER_TOPK = 16

LANES = 128
SUBLANES = 8
HALO = SUBLANES
NEG_INF = float("-inf")
LOG2E = 1.4426950408889634
GELU_C = math.sqrt(2.0 / math.pi)
Q_SCALE = HEAD_DIM ** -0.5 * LOG2E
SHIFT_MARGIN = 1.02
MIN_DENOM = 2.0 ** -100
ATTN_CHUNKS = 4


def _tile(n, target, mult):
    best = None
    for t in range(mult, min(n, target) + 1, mult):
        if n % t == 0:
            best = t
    assert best is not None, (n, target, mult)
    return best


def _cparams(sem, vmem_mib):
    return pltpu.CompilerParams(dimension_semantics=sem, vmem_limit_bytes=vmem_mib << 20)


def _gelu(x):
    return 0.5 * x * (1.0 + jnp.tanh(GELU_C * (x + 0.044715 * (x * x * x))))


def _row_select(mod_ref, b, is_ctx):
    return jnp.where(is_ctx, mod_ref[2:3, :], mod_ref[pl.ds(b, 1), :])


def _mod_kernel(s_ref, w_ref, b_ref, o_ref):
    s = s_ref[...]
    s = s * jax.nn.sigmoid(s)
    o_ref[0] = jnp.dot(s, w_ref[0], preferred_element_type=F32,
                       precision=lax.Precision.HIGHEST) + b_ref[0]


def _mod_call(s8, w_mod, b_mod):
    depth, d, n = w_mod.shape
    tn = _tile(n, 1024, LANES)
    return pl.pallas_call(
        _mod_kernel,
        grid=(depth, n // tn),
        in_specs=[pl.BlockSpec((8, d), lambda l, j: (0, 0)),
                  pl.BlockSpec((1, d, tn), lambda l, j: (l, 0, j)),
                  pl.BlockSpec((1, 1, tn), lambda l, j: (l, 0, j))],
        out_specs=pl.BlockSpec((1, 8, tn), lambda l, j: (l, 0, j)),
        out_shape=jax.ShapeDtypeStruct((depth, 8, n), F32),
        compiler_params=_cparams(("parallel", "parallel"), 32),
        name="mod",
    )(s8, w_mod, b_mod.reshape(depth, 1, n))


def _inproj_kernel(x_ref, shift_ref, scale_ref, g_ref, w_ref, o_ref, h_scr, *, tm, rows, ctx_len):
    b = pl.program_id(0)
    i = pl.program_id(1)

    @pl.when(pl.program_id(2) == 0)
    def _():
        def norm_rows(c, carry):
            r0 = pl.multiple_of(c * rows, rows)
            x = x_ref[0, pl.ds(r0, rows), :]
            var = jnp.mean(x * x, axis=-1, keepdims=True)
            y = x * lax.rsqrt(var + EPS) * g_ref[...]
            row = i * tm + r0 + lax.broadcasted_iota(jnp.int32, (rows, 1), 0)
            is_ctx = row < ctx_len
            sh = _row_select(shift_ref, b, is_ctx)
            sc = _row_select(scale_ref, b, is_ctx)
            h_scr[pl.ds(r0, rows), :] = (y * (1.0 + sc) + sh).astype(BF16)
            return carry

        lax.fori_loop(0, tm // rows, norm_rows, 0)

    o_ref[0] = jnp.dot(h_scr[...], w_ref[...], preferred_element_type=F32).astype(BF16)


def _inproj_call(x, mod, layer, gain, w_bf16, ctx_len):
    bsz, l, d = x.shape
    n = w_bf16.shape[2]
    tm = _tile(l, 1056, 16)
    rows = _tile(tm, 352, 16)
    tn = 1024
    kern = functools.partial(_inproj_kernel, tm=tm, rows=rows, ctx_len=ctx_len)
    return pl.pallas_call(
        kern,
        grid=(bsz, l // tm, n // tn),
        in_specs=[pl.BlockSpec((1, tm, d), lambda b, i, j: (b, i, 0)),
                  pl.BlockSpec((None, 8, d), lambda b, i, j: (layer, 0, 0)),
                  pl.BlockSpec((None, 8, d), lambda b, i, j: (layer, 0, 1)),
                  pl.BlockSpec((1, d), lambda b, i, j: (0, 0)),
                  pl.BlockSpec((None, d, tn), lambda b, i, j: (layer, 0, j))],
        out_specs=pl.BlockSpec((1, tm, tn), lambda b, i, j: (b, i, j)),
        out_shape=jax.ShapeDtypeStruct((bsz, l, n), BF16),
        scratch_shapes=[pltpu.VMEM((tm, d), BF16)],
        compiler_params=_cparams(("parallel", "parallel", "arbitrary"), 48),
        name="inproj",
    )(x, mod, mod, gain.reshape(1, d), w_bf16)


def _norm_rope(t, gain, cos, sin_signed, lane_lo):
    var = jnp.mean(t * t, axis=-1, keepdims=True)
    y = t * lax.rsqrt(var + EPS) * gain
    swapped = jnp.where(lane_lo, pltpu.roll(y, HEAD_DIM - ROPE_PAIRS, 1), pltpu.roll(y, ROPE_PAIRS, 1))
    return y * cos + swapped * sin_signed


def _qkprep_kernel(q_ref, k_ref, v_ref, cos_ref, sin_ref, qg_ref, kg_ref, qo_ref, ko_ref, vo_ref):
    cos = cos_ref[...]
    sin = sin_ref[...]
    lane = lax.broadcasted_iota(jnp.int32, cos.shape, 1)
    lane_lo = (lane & ROPE_PAIRS) == 0
    qg = qg_ref[...]
    kg = kg_ref[...]
    for h in range(N_HEADS):
        sl = slice(h * HEAD_DIM, (h + 1) * HEAD_DIM)
        t = q_ref[0, :, sl].astype(F32)
        qo_ref[0, :, sl] = (_norm_rope(t, qg, cos, sin, lane_lo) * Q_SCALE).astype(BF16)
    k_tail = jnp.where(lane == 0, 1.0, 0.0).astype(BF16)
    v_tail = jnp.ones(cos.shape, BF16)
    for h in range(N_KV_HEADS):
        sl = slice(h * HEAD_DIM, (h + 1) * HEAD_DIM)
        lo = slice(2 * h * HEAD_DIM, (2 * h + 1) * HEAD_DIM)
        hi = slice((2 * h + 1) * HEAD_DIM, (2 * h + 2) * HEAD_DIM)
        t = k_ref[0, :, sl].astype(F32)
        ko_ref[0, :, lo] = _norm_rope(t, kg, cos, sin, lane_lo).astype(BF16)
        ko_ref[0, :, hi] = k_tail
        vo_ref[0, :, lo] = v_ref[0, :, sl]
        vo_ref[0, :, hi] = v_tail


def _qkprep_call(p, cos, sin_signed, q_gain, k_gain):
    bsz, l, _ = p.shape
    qw = N_HEADS * HEAD_DIM
    kw = N_KV_HEADS * HEAD_DIM
    tm = _tile(l, 768, 16)
    return pl.pallas_call(
        _qkprep_kernel,
        grid=(bsz, l // tm),
        in_specs=[pl.BlockSpec((1, tm, qw), lambda b, i: (b, i, 0)),
                  pl.BlockSpec((1, tm, kw), lambda b, i: (b, i, qw // kw)),
                  pl.BlockSpec((1, tm, kw), lambda b, i: (b, i, qw // kw + 1)),
                  pl.BlockSpec((tm, HEAD_DIM), lambda b, i: (i, 0)),
                  pl.BlockSpec((tm, HEAD_DIM), lambda b, i: (i, 0)),
                  pl.BlockSpec((1, HEAD_DIM), lambda b, i: (0, 0)),
                  pl.BlockSpec((1, HEAD_DIM), lambda b, i: (0, 0))],
        out_specs=[pl.BlockSpec((1, tm, qw), lambda b, i: (b, i, 0)),
                   pl.BlockSpec((1, tm, 2 * kw), lambda b, i: (b, i, 0)),
                   pl.BlockSpec((1, tm, 2 * kw), lambda b, i: (b, i, 0))],
        out_shape=[jax.ShapeDtypeStruct((bsz, l, qw), BF16),
                   jax.ShapeDtypeStruct((bsz, l, 2 * kw), BF16),
                   jax.ShapeDtypeStruct((bsz, l, 2 * kw), BF16)],
        compiler_params=_cparams(("parallel", "parallel"), 40),
        name="qkprep",
    )(p, p, p, cos, sin_signed, q_gain.reshape(1, HEAD_DIM), k_gain.reshape(1, HEAD_DIM))


def _attn_kernel(q_ref, k_ref, v_ref, qg_ref, kg_ref, o_ref, qs_scr, acc_scr, sa_scr, sb_scr, m_scr, l_scr,
                 *, tq, tkc, ctx_len, n_lat_chunks):
    qi = pl.program_id(2)
    q = q_ref[0]
    gq = jnp.max(jnp.abs(qg_ref[...]), axis=-1, keepdims=True)
    gk = jnp.max(jnp.abs(kg_ref[...]), axis=-1, keepdims=True)
    bound = (SHIFT_MARGIN * HEAD_DIM * Q_SCALE) * gq * gk
    lane = lax.broadcasted_iota(jnp.int32, (1, HEAD_DIM), 1)
    tail = jnp.where(lane == 0, -bound, 0.0).astype(BF16)
    for h in range(GROUP):
        qs_scr[h * tq:(h + 1) * tq, :HEAD_DIM] = q[:, h * HEAD_DIM:(h + 1) * HEAD_DIM]
        qs_scr[h * tq:(h + 1) * tq, HEAD_DIM:] = jnp.broadcast_to(tail, (tq, HEAD_DIM))
    n = jnp.where(qi < ctx_len // tq, 0, n_lat_chunks)

    def lat_start(c):
        return pl.multiple_of(ctx_len + c * tkc, math.gcd(ctx_len, tkc))

    def write(out):
        for h in range(GROUP):
            o_ref[0, :, h * HEAD_DIM:(h + 1) * HEAD_DIM] = out[h * tq:(h + 1) * tq, :].astype(BF16)

    def scores(start, size):
        return lax.dot_general(qs_scr[...], k_ref[0, pl.ds(start, size), :], (((1,), (1,)), ((), ())),
                               preferred_element_type=F32)

    def accumulate(s, start, size):
        p = jnp.exp2(s).astype(BF16)
        acc_scr[...] += jnp.dot(p, v_ref[0, pl.ds(start, size), :], preferred_element_type=F32)

    acc_scr[...] = jnp.zeros(acc_scr.shape, F32)

    @pl.when(n == 0)
    def _():
        accumulate(scores(0, ctx_len), 0, ctx_len)

    @pl.when(n > 0)
    def _():
        bounds = [(0, ctx_len + tkc)] + [(ctx_len + c * tkc, tkc) for c in range(1, n_lat_chunks)]
        bufs = (sa_scr, sb_scr)
        bufs[0][:, :bounds[0][1]] = scores(*bounds[0])
        for c, (start, size) in enumerate(bounds):
            if c + 1 < len(bounds):
                nxt_start, nxt_size = bounds[c + 1]
                bufs[(c + 1) % 2][:, :nxt_size] = scores(nxt_start, nxt_size)
            accumulate(bufs[c % 2][:, :size], start, size)

    acc = acc_scr[...]
    den = acc[:, HEAD_DIM:]
    write(acc[:, :HEAD_DIM] / den)

    @pl.when(jnp.logical_not(jnp.min(den) >= MIN_DENOM))
    def _():
        m_scr[...] = jnp.full(m_scr.shape, NEG_INF, F32)
        l_scr[...] = jnp.zeros(l_scr.shape, F32)
        acc_scr[...] = jnp.zeros(acc_scr.shape, F32)

        def online_chunk(start, size):
            k = k_ref[0, pl.ds(start, size), :HEAD_DIM]
            v = v_ref[0, pl.ds(start, size), :HEAD_DIM]
            s = lax.dot_general(qs_scr[:, :HEAD_DIM], k, (((1,), (1,)), ((), ())),
                                preferred_element_type=F32)
            m_old = m_scr[...]
            m_new = jnp.maximum(m_old, jnp.max(s, axis=-1, keepdims=True))
            alpha = jnp.exp2(m_old - m_new)
            p = jnp.exp2(s - m_new)
            l_scr[...] = alpha * l_scr[...] + jnp.sum(p, axis=-1, keepdims=True)
            acc_scr[:, :HEAD_DIM] = alpha * acc_scr[:, :HEAD_DIM] + jnp.dot(
                p.astype(BF16), v, preferred_element_type=F32)
            m_scr[...] = m_new

        online_chunk(0, ctx_len)

        def online_body(c, carry):
            online_chunk(lat_start(c), tkc)
            return carry

        lax.fori_loop(0, n, online_body, 0)
        write(acc_scr[:, :HEAD_DIM] / l_scr[...])


def _attn_call(qn, ke, ve, q_gain, k_gain, ctx_len):
    bsz, l, qw = qn.shape
    tq = 256
    assert ctx_len % tq == 0 and l % tq == 0
    t_lat = l - ctx_len
    assert t_lat % (ATTN_CHUNKS * LANES) == 0
    tkc = t_lat // ATTN_CHUNKS
    gw = GROUP * HEAD_DIM
    ew = 2 * HEAD_DIM
    kern = functools.partial(_attn_kernel, tq=tq, tkc=tkc, ctx_len=ctx_len, n_lat_chunks=t_lat // tkc)
    return pl.pallas_call(
        kern,
        grid=(bsz, N_KV_HEADS, l // tq),
        in_specs=[pl.BlockSpec((1, tq, gw), lambda b, g, i: (b, i, g)),
                  pl.BlockSpec((1, l, ew), lambda b, g, i: (b, 0, g)),
                  pl.BlockSpec((1, l, ew), lambda b, g, i: (b, 0, g)),
                  pl.BlockSpec((1, HEAD_DIM), lambda b, g, i: (0, 0)),
                  pl.BlockSpec((1, HEAD_DIM), lambda b, g, i: (0, 0))],
        out_specs=pl.BlockSpec((1, tq, gw), lambda b, g, i: (b, i, g)),
        out_shape=jax.ShapeDtypeStruct((bsz, l, qw), BF16),
        scratch_shapes=[pltpu.VMEM((GROUP * tq, ew), BF16),
                        pltpu.VMEM((GROUP * tq, ew), F32),
                        pltpu.VMEM((GROUP * tq, ctx_len + tkc), F32),
                        pltpu.VMEM((GROUP * tq, ctx_len + tkc), F32),
                        pltpu.VMEM((GROUP * tq, 1), F32),
                        pltpu.VMEM((GROUP * tq, 1), F32)],
        compiler_params=_cparams(("parallel", "parallel", "arbitrary"), 56),
        name="attn",
    )(qn, ke, ve, q_gain.reshape(1, HEAD_DIM), k_gain.reshape(1, HEAD_DIM))


def _segment_bounds(row, ctx_len, seq_len):
    is_ctx = row < ctx_len
    first = jnp.where(is_ctx, 0, ctx_len)
    last = jnp.where(is_ctx, ctx_len - 1, seq_len - 1)
    return first, last


def _shift_down(x, prev, k, local):
    y = pltpu.roll(x, k, 0)
    for r in range(k):
        y = jnp.where(local == r, prev[HALO - k + r:HALO - k + r + 1, :], y)
    return y


def _shift_up(x, nxt, k, local, tt):
    y = pltpu.roll(x, tt - k, 0)
    for r in range(k):
        y = jnp.where(local == tt - k + r, nxt[r:r + 1, :], y)
    return y


def _halo_shifts(x, prev, nxt):
    tt = x.shape[0]
    sub = lax.broadcasted_iota(jnp.int32, (SUBLANES, x.shape[1]), 0)

    def down(k):
        y = pltpu.roll(x, k, 0)
        head = y[:SUBLANES]
        for r in range(k):
            head = jnp.where(sub == r, prev[HALO - k + r:HALO - k + r + 1, :], head)
        return jnp.concatenate([head, y[SUBLANES:]], axis=0)

    def up(k):
        y = pltpu.roll(x, tt - k, 0)
        tail = y[tt - SUBLANES:]
        for r in range(k):
            tail = jnp.where(sub == SUBLANES - k + r, nxt[r:r + 1, :], tail)
        return jnp.concatenate([y[:tt - SUBLANES], tail], axis=0)

    return down, up


def _scan_tile(a, d, carry, reverse):
    n_groups = a.shape[0] // SUBLANES
    sub = lax.broadcasted_iota(jnp.int32, (SUBLANES, a.shape[1]), 0)
    groups = []
    for v in range(n_groups):
        av = a[v * SUBLANES:(v + 1) * SUBLANES]
        dv = d[v * SUBLANES:(v + 1) * SUBLANES]
        for k in (1, 2, 4):
            keep = sub < SUBLANES - k if reverse else sub >= k
            shift = SUBLANES - k if reverse else k
            a_n = jnp.where(keep, pltpu.roll(av, shift, 0), 1.0)
            d_n = jnp.where(keep, pltpu.roll(dv, shift, 0), 0.0)
            dv = av * d_n + dv
            av = av * a_n
        groups.append((av, dv))
    hs = [None] * n_groups
    for v in (reversed(range(n_groups)) if reverse else range(n_groups)):
        av, dv = groups[v]
        hv = dv + av * carry
        carry = hv[0:1] if reverse else hv[SUBLANES - 1:SUBLANES]
        hs[v] = hv
    return jnp.concatenate(hs, axis=0), carry


def _lru_kernel(*refs, reverse, tt, ctx_len, n_sub):
    if reverse:
        (x_ref, xp_ref, xn_ref, cw_ref, cb_ref, wa_ref, ba_ref, wx_ref, bx_ref, lam_ref,
         hf_ref, g_ref, o_ref, carry_scr) = refs
    else:
        (x_ref, xp_ref, xn_ref, cw_ref, cb_ref, wa_ref, ba_ref, wx_ref, bx_ref, lam_ref,
         o_ref, carry_scr) = refs
    s = pl.program_id(2)
    nt = pl.num_programs(2)
    ti = jnp.where(s == 0, 0, nt - s) if reverse else s

    @pl.when(s == 0)
    def _():
        carry_scr[...] = jnp.zeros(carry_scr.shape, F32)

    prev_ok = jnp.logical_and(ti != 0, ti * tt != ctx_len)
    next_ok = jnp.logical_and(ti != nt - 1, (ti + 1) * tt != ctx_len)
    x = x_ref[0].astype(F32)
    xp = jnp.where(prev_ok, xp_ref[0].astype(F32), 0.0)
    xn = jnp.where(next_ok, xn_ref[0].astype(F32), 0.0)
    down, up = _halo_shifts(x, xp, xn)
    cw = cw_ref[...]
    u = cb_ref[...] + cw[2:3, :] * x + cw[0:1, :] * down(2) + cw[1:2, :] * down(1) + cw[3:4, :] * up(1)
    nlam = -lam_ref[...]
    softplus = jnp.maximum(nlam, 0.0) + jnp.log1p(jnp.exp(-jnp.abs(nlam)))

    for j in range(n_sub):
        ls = slice(j * LRU_BLOCK, (j + 1) * LRU_BLOCK)
        uj = u[:, ls]
        ub = uj.astype(BF16)
        r = jax.nn.sigmoid(jnp.dot(ub, wa_ref[j], preferred_element_type=F32) + ba_ref[:, ls])
        i = jax.nn.sigmoid(jnp.dot(ub, wx_ref[j], preferred_element_type=F32) + bx_ref[:, ls])
        a = jnp.exp((-LRU_C) * r * softplus[:, ls])
        d = jnp.sqrt(1.0 - a * a) * (i * uj)
        h, carry = _scan_tile(a, d, carry_scr[:, ls], reverse)
        carry_scr[:, ls] = carry
        if reverse:
            o_ref[0, :, ls] = ((hf_ref[0, :, ls] + h) * _gelu(g_ref[0, :, ls].astype(F32))).astype(BF16)
        else:
            o_ref[0, :, ls] = h


def _lru_call(p, conv_w, conv_b, wa, ba, wx, bx, lam, ctx_len, reverse, h_fwd=None):
    bsz, l, _ = p.shape
    c = conv_w.shape[1]
    tt = 256
    assert ctx_len % tt == 0 and l % tt == 0
    nt = l // tt
    n_sub = 8
    cw = n_sub * LRU_BLOCK
    nb = c // cw
    x_col0 = (N_HEADS + 2 * N_KV_HEADS) * HEAD_DIM // cw
    g_col0 = x_col0 + nb
    hb = tt // HALO
    n_hblk = l // HALO

    def tile_of(s):
        return jnp.where(s == 0, 0, nt - s) if reverse else s

    x_map = lambda b, cb, s: (b, tile_of(s), x_col0 + cb)
    prev_map = lambda b, cb, s: (b, jnp.maximum(tile_of(s) * hb - 1, 0), x_col0 + cb)
    next_map = lambda b, cb, s: (b, jnp.minimum((tile_of(s) + 1) * hb, n_hblk - 1), x_col0 + cb)
    vec_map = lambda b, cb, s: (0, cb)
    w_spec = pl.BlockSpec((n_sub, LRU_BLOCK, LRU_BLOCK), lambda b, cb, s: (cb, 0, 0))
    in_specs = [pl.BlockSpec((1, tt, cw), x_map),
                pl.BlockSpec((1, HALO, cw), prev_map),
                pl.BlockSpec((1, HALO, cw), next_map),
                pl.BlockSpec((LRU_CONV, cw), vec_map),
                pl.BlockSpec((1, cw), vec_map),
                w_spec,
                pl.BlockSpec((1, cw), vec_map),
                w_spec,
                pl.BlockSpec((1, cw), vec_map),
                pl.BlockSpec((1, cw), vec_map)]
    args = [p, p, p, conv_w, conv_b.reshape(1, c), wa.astype(BF16), ba.reshape(1, c),
            wx.astype(BF16), bx.reshape(1, c), lam.reshape(1, c)]
    out_map = lambda b, cb, s: (b, tile_of(s), cb)
    if reverse:
        in_specs += [pl.BlockSpec((1, tt, cw), out_map),
                     pl.BlockSpec((1, tt, cw), lambda b, cb, s: (b, tile_of(s), g_col0 + cb))]
        args += [h_fwd, p]
        out_dtype = BF16
    else:
        out_dtype = F32
    kern = functools.partial(_lru_kernel, reverse=reverse, tt=tt, ctx_len=ctx_len, n_sub=n_sub)
    return pl.pallas_call(
        kern,
        grid=(bsz, nb, nt),
        in_specs=in_specs,
        out_specs=pl.BlockSpec((1, tt, cw), out_map),
        out_shape=jax.ShapeDtypeStruct((bsz, l, c), out_dtype),
        scratch_shapes=[pltpu.VMEM((1, cw), F32)],
        compiler_params=_cparams(("parallel", "parallel", "arbitrary"), 32),
        name="lru_rev" if reverse else "lru_fwd",
    )(*args)


def _sconv_kernel(bg_ref, cg_ref, u_ref, cgp_ref, up_ref, cgn_ref, un_ref, w_ref, b_ref, o_ref,
                  *, tt, ctx_len, seq_len):
    ti = pl.program_id(1)
    z = cg_ref[0].astype(F32) * u_ref[0].astype(F32)
    zp = cgp_ref[0].astype(F32) * up_ref[0].astype(F32)
    zn = cgn_ref[0].astype(F32) * un_ref[0].astype(F32)
    local = lax.broadcasted_iota(jnp.int32, (tt, 1), 0)
    row = ti * tt + local
    first, last = _segment_bounds(row, ctx_len, seq_len)
    w = w_ref[...]
    y = b_ref[...] + w[1:2, :] * z
    y = y + w[0:1, :] * jnp.where(row - 1 >= first, _shift_down(z, zp, 1, local), 0.0)
    y = y + w[2:3, :] * jnp.where(row + 1 <= last, _shift_up(z, zn, 1, local, tt), 0.0)
    o_ref[0] = (bg_ref[0].astype(F32) * y).astype(BF16)


def _sconv_call(p, w, b, ctx_len):
    bsz, l, _ = p.shape
    c = w.shape[1]
    tt = _tile(l, 768, 16)
    tc = 512
    ncb = c // tc
    b_col0 = (N_HEADS * HEAD_DIM + 2 * N_KV_HEADS * HEAD_DIM + 2 * c) // tc
    c_col0 = b_col0 + ncb
    u_col0 = c_col0 + ncb
    hb = tt // HALO
    n_hblk = l // HALO
    prev = lambda i: jnp.maximum(i * hb - 1, 0)
    nxt = lambda i: jnp.minimum((i + 1) * hb, n_hblk - 1)
    kern = functools.partial(_sconv_kernel, tt=tt, ctx_len=ctx_len, seq_len=l)
    return pl.pallas_call(
        kern,
        grid=(bsz, l // tt, ncb),
        in_specs=[pl.BlockSpec((1, tt, tc), lambda b_, i, j: (b_, i, b_col0 + j)),
                  pl.BlockSpec((1, tt, tc), lambda b_, i, j: (b_, i, c_col0 + j)),
                  pl.BlockSpec((1, tt, tc), lambda b_, i, j: (b_, i, u_col0 + j)),
                  pl.BlockSpec((1, HALO, tc), lambda b_, i, j: (b_, prev(i), c_col0 + j)),
                  pl.BlockSpec((1, HALO, tc), lambda b_, i, j: (b_, prev(i), u_col0 + j)),
                  pl.BlockSpec((1, HALO, tc), lambda b_, i, j: (b_, nxt(i), c_col0 + j)),
                  pl.BlockSpec((1, HALO, tc), lambda b_, i, j: (b_, nxt(i), u_col0 + j)),
                  pl.BlockSpec((SC_CONV, tc), lambda b_, i, j: (0, j)),
                  pl.BlockSpec((1, tc), lambda b_, i, j: (0, j))],
        out_specs=pl.BlockSpec((1, tt, tc), lambda b_, i, j: (b_, i, j)),
        out_shape=jax.ShapeDtypeStruct((bsz, l, c), BF16),
        compiler_params=_cparams(("parallel", "parallel", "parallel"), 32),
        name="sconv",
    )(p, p, p, p, p, p, p, w, b.reshape(1, c))


def _merge_kernel(xa_ref, xl_ref, xs_ref, ga_ref, gl_ref, gs_ref, wa_ref, wl_ref, ws_ref, o_ref):
    y = jax.nn.sigmoid(ga_ref[0].astype(F32)) * jnp.dot(xa_ref[0], wa_ref[...], preferred_element_type=F32)
    y = y + jax.nn.sigmoid(gl_ref[0].astype(F32)) * jnp.dot(xl_ref[0], wl_ref[...], preferred_element_type=F32)
    y = y + jax.nn.sigmoid(gs_ref[0].astype(F32)) * jnp.dot(xs_ref[0], ws_ref[...], preferred_element_type=F32)
    o_ref[0] = y.astype(BF16)


def _merge_call(x_att, x_lru, x_sc, p, layer, w_att, w_lru, w_sc):
    bsz, l, d = x_att.shape
    tm = _tile(l, 768, 16)
    tn = 512
    g_col0 = (p.shape[2] - 3 * d) // tn
    nj = d // tn
    xspec = pl.BlockSpec((1, tm, d), lambda b, i, j: (b, i, 0))
    wspec = pl.BlockSpec((None, d, tn), lambda b, i, j: (layer, 0, j))
    gspec = lambda k: pl.BlockSpec((1, tm, tn), lambda b, i, j: (b, i, g_col0 + k * nj + j))
    return pl.pallas_call(
        _merge_kernel,
        grid=(bsz, l // tm, nj),
        in_specs=[xspec, xspec, xspec, gspec(0), gspec(1), gspec(2), wspec, wspec, wspec],
        out_specs=pl.BlockSpec((1, tm, tn), lambda b, i, j: (b, i, j)),
        out_shape=jax.ShapeDtypeStruct((bsz, l, d), BF16),
        compiler_params=_cparams(("parallel", "parallel", "arbitrary"), 48),
        name="merge",
    )(x_att, x_lru, x_sc, p, p, p, w_att, w_lru, w_sc)


def _outproj_kernel(y_ref, w_ref, x_ref, gate_ref, shift_ref, scale_ref, g_ref, xo_ref, ht_ref,
                    *, tm, ctx_len):
    b = pl.program_id(0)
    i = pl.program_id(1)
    row = i * tm + lax.broadcasted_iota(jnp.int32, (tm, 1), 0)
    is_ctx = row < ctx_len
    acc = jnp.dot(y_ref[0], w_ref[...], preferred_element_type=F32)
    xn = x_ref[0] + _row_select(gate_ref, b, is_ctx) * acc
    xo_ref[0] = xn
    var = jnp.mean(xn * xn, axis=-1, keepdims=True)
    h = xn * lax.rsqrt(var + EPS) * g_ref[...]
    h = h * (1.0 + _row_select(scale_ref, b, is_ctx)) + _row_select(shift_ref, b, is_ctx)
    ht_ref[0] = h.T.astype(BF16)


def _outproj_call(y, w_out, x, mod, layer, gain, ctx_len):
    bsz, l, d = x.shape
    tm = _tile(l, 384, LANES)
    kern = functools.partial(_outproj_kernel, tm=tm, ctx_len=ctx_len)
    mspec = lambda k: pl.BlockSpec((None, 8, d), lambda b, i: (layer, 0, k))
    return pl.pallas_call(
        kern,
        grid=(bsz, l // tm),
        in_specs=[pl.BlockSpec((1, tm, d), lambda b, i: (b, i, 0)),
                  pl.BlockSpec((None, d, d), lambda b, i: (layer, 0, 0)),
                  pl.BlockSpec((1, tm, d), lambda b, i: (b, i, 0)),
                  mspec(2), mspec(3), mspec(4),
                  pl.BlockSpec((1, d), lambda b, i: (0, 0))],
        out_specs=[pl.BlockSpec((1, tm, d), lambda b, i: (b, i, 0)),
                   pl.BlockSpec((1, d, tm), lambda b, i: (b, 0, i))],
        out_shape=[jax.ShapeDtypeStruct((bsz, l, d), F32),
                   jax.ShapeDtypeStruct((bsz, d, l), BF16)],
        compiler_params=_cparams(("parallel", "parallel"), 48),
        name="outproj",
    )(y, w_out, x, mod, mod, mod, gain.reshape(1, d))


_CAND_ROWS = tuple(PEER_TOPK // (i + 1) for i in range(PEER_TOPK))


def _top16(s, v_scr):
    n = s.shape[0]
    key = lax.broadcasted_iota(jnp.int32, s.shape, 0)

    def body(r, carry):
        work, rank = carry
        m = jnp.max(work, axis=0, keepdims=True)
        v_scr[pl.ds(r, 1), :] = m
        first = jnp.min(jnp.where(work == m, key, n), axis=0, keepdims=True)
        sel = key == first
        return jnp.where(sel, NEG_INF, work), jnp.where(sel, jnp.asarray(r, F32), rank)

    _, rank = lax.fori_loop(0, PEER_TOPK, body, (s, jnp.full(s.shape, float(PEER_TOPK), F32)))
    return rank


MARK = -(2.0 ** 127)


def _top16_distinct(ss, v_scrs):
    def body(r, works):
        mark = jnp.asarray(r, F32) * (MARK / 32.0) + MARK
        out = []
        for work, v_scr in zip(works, v_scrs):
            m = jnp.max(work, axis=0, keepdims=True)
            v_scr[pl.ds(r, 1), :] = m
            out.append(jnp.where(work == m, mark, work))
        return tuple(out)

    works = lax.fori_loop(0, PEER_TOPK, body, tuple(ss), unroll=True)
    return [jnp.where(w <= MARK, (MARK - w) * (-32.0 / MARK), float(PEER_TOPK)) for w in works]


def _peer_topk_kernel(ht_ref, wq_ref, k1_ref, k2_ref, cnt_ref, e1_ref, rk_ref, e2_ref,
                      q_scr, v_scr, *, tm, n_par):
    half = PEER_NKEYS
    q_scr[...] = jnp.dot(wq_ref[...], ht_ref[0], preferred_element_type=F32).astype(BF16)
    sub = lax.broadcasted_iota(jnp.int32, (SUBLANES, LANES), 0)
    big = PEER_TOPK * PEER_TOPK

    def candidates(v1, v2):
        pieces, poss = [], []
        for i in range(SUBLANES):
            for j0 in range(0, _CAND_ROWS[i], SUBLANES):
                c = v1[i:i + 1, :] + v2[j0:j0 + SUBLANES, :]
                valid = sub + j0 < _CAND_ROWS[i]
                pieces.append(jnp.where(valid, c, NEG_INF))
                poss.append(jnp.where(valid, i * PEER_TOPK + j0 + sub, big))
        pieces.append(v1[SUBLANES:, :] + v2[0:1, :])
        poss.append((sub + SUBLANES) * PEER_TOPK)
        return pieces, poss

    def pick_exact(pieces, poss):
        def pick(_, carry2):
            cs, sels = carry2
            m = functools.reduce(jnp.maximum, cs)
            m = jnp.max(m, axis=0, keepdims=True)
            cand_pos = functools.reduce(jnp.minimum, [jnp.where(c == m, p_, big) for c, p_ in zip(cs, poss)])
            first = jnp.min(cand_pos, axis=0, keepdims=True)
            hit = [p_ == first for p_ in poss]
            cs = tuple(jnp.where(hh, NEG_INF, c) for hh, c in zip(hit, cs))
            sels = tuple(jnp.where(hh, 1.0, s_) for hh, s_ in zip(hit, sels))
            return cs, sels

        zeros = tuple(jnp.zeros((SUBLANES, LANES), F32) for _ in pieces)
        return lax.fori_loop(0, PEER_TOPK, pick, (tuple(pieces), zeros))[1]

    def pick_distinct(groups):
        npc = len(groups[0])

        def pick(_, cs):
            out = []
            for g in range(len(groups)):
                grp = cs[g * npc:(g + 1) * npc]
                m = jnp.max(functools.reduce(jnp.maximum, grp), axis=0, keepdims=True)
                out.extend(jnp.where(c == m, MARK, c) for c in grp)
            return tuple(out)

        marked = lax.fori_loop(0, PEER_TOPK, pick, tuple(c for grp in groups for c in grp), unroll=True)
        return [tuple(jnp.where(c == MARK, 1.0, 0.0) for c in marked[g * npc:(g + 1) * npc])
                for g in range(len(groups))]

    def compute(h, lss, exact):
        base = pl.multiple_of(h * 2 * half, 2 * half)
        n_g = len(lss)
        s1s = [jnp.dot(k1_ref[h], q_scr[pl.ds(base, half), ls], preferred_element_type=F32) for ls in lss]
        s2s = [jnp.dot(k2_ref[h], q_scr[pl.ds(base + half, half), ls], preferred_element_type=F32)
               for ls in lss]
        v1_refs = [v_scr.at[2 * g] for g in range(n_g)]
        v2_refs = [v_scr.at[2 * g + 1] for g in range(n_g)]
        if exact:
            rank1s = [_top16(s, r) for s, r in zip(s1s, v1_refs)]
            rank2s = [_top16(s, r) for s, r in zip(s2s, v2_refs)]
        else:
            ranks = _top16_distinct(tuple(s1s + s2s), tuple(v1_refs + v2_refs))
            rank1s, rank2s = ranks[:n_g], ranks[n_g:]
        v1s = [r[...] for r in v1_refs]
        v2s = [r[...] for r in v2_refs]
        cands = [candidates(v1, v2) for v1, v2 in zip(v1s, v2s)]
        if exact:
            all_sels = [pick_exact(pieces, poss) for pieces, poss in cands]
        else:
            all_sels = pick_distinct([pieces for pieces, _ in cands])
        most = None
        for g in range(n_g):
            most_g = finish(h, lss[g], s1s[g], s2s[g], rank1s[g], rank2s[g], v1s[g], v2s[g],
                            cands[g][0], all_sels[g])
            most = most_g if most is None else jnp.maximum(most, most_g)
        return most

    def finish(h, ls, s1, s2, rank1, rank2, v1, v2, orig, sels):
        top = v1[0:1, :] + v2[0:1, :]
        zsum = functools.reduce(
            lambda a_, b_: a_ + b_,
            [jnp.where(s_ > 0.0, jnp.exp(o - top), 0.0) for s_, o in zip(sels, orig)])
        zinv = 1.0 / jnp.sum(zsum, axis=0, keepdims=True)

        counts = []
        pi = 0
        for i in range(SUBLANES):
            c = None
            for j0 in range(0, _CAND_ROWS[i], SUBLANES):
                part = jnp.sum(sels[pi], axis=0, keepdims=True)
                c = part if c is None else c + part
                pi += 1
            counts.append(c)
        tail = sels[pi]
        for i in range(SUBLANES, PEER_TOPK):
            counts.append(tail[i - SUBLANES:i - SUBLANES + 1, :])
        cnt = jnp.zeros(rank1.shape, F32)
        for i in range(PEER_TOPK):
            cnt = jnp.where(rank1 == float(i), counts[i], cnt)

        cnt_ref[0, h, :, ls] = cnt
        e1_ref[0, h, :, ls] = jnp.exp(s1 - v1[0:1, :]) * zinv
        rk_ref[0, h, :, ls] = rank2.astype(BF16)
        e2_ref[0, h, :, ls] = jnp.exp(s2 - v2[0:1, :]).astype(BF16)
        ranked = jnp.where(rank1 < PEER_TOPK, 1.0, 0.0)
        ranked = jnp.maximum(jnp.sum(ranked, axis=0, keepdims=True),
                             jnp.sum(jnp.where(rank2 < PEER_TOPK, 1.0, 0.0), axis=0, keepdims=True))
        picked = jnp.sum(functools.reduce(lambda a_, b_: a_ + b_, sels), axis=0, keepdims=True)
        return jnp.max(jnp.maximum(ranked, picked))

    n_trips = tm // (n_par * LANES)

    def head_lane_groups(idx, carry):
        h = idx // n_trips
        first = (idx % n_trips) * n_par
        lss = [pl.ds(pl.multiple_of((first + g) * LANES, LANES), LANES) for g in range(n_par)]
        most = compute(h, lss, exact=False)

        @pl.when(most > PEER_TOPK)
        def _():
            compute(h, lss, exact=True)

        return carry

    lax.fori_loop(0, PEER_HEADS * n_trips, head_lane_groups, 0)


def _peer_topk_call(ht, layer, wq_t, k1, k2):
    bsz, d, l = ht.shape
    qd = wq_t.shape[1]
    n_par = 6
    tm = _tile(l, 768, n_par * LANES)
    kern = functools.partial(_peer_topk_kernel, tm=tm, n_par=n_par)
    ospec = pl.BlockSpec((1, PEER_HEADS, PEER_NKEYS, tm), lambda b, i: (b, 0, 0, i))
    oshape = jax.ShapeDtypeStruct((bsz, PEER_HEADS, PEER_NKEYS, l), F32)
    kspec = pl.BlockSpec((None, PEER_HEADS, PEER_NKEYS, PEER_NKEYS), lambda b, i: (layer, 0, 0, 0))
    return pl.pallas_call(
        kern,
        grid=(bsz, l // tm),
        in_specs=[pl.BlockSpec((1, d, tm), lambda b, i: (b, 0, i)),
                  pl.BlockSpec((None, qd, d), lambda b, i: (layer, 0, 0)),
                  kspec, kspec],
        out_specs=[ospec, ospec, ospec, ospec],
        out_shape=[oshape, oshape, jax.ShapeDtypeStruct(oshape.shape, BF16),
                   jax.ShapeDtypeStruct(oshape.shape, BF16)],
        scratch_shapes=[pltpu.VMEM((qd, tm), BF16),
                        pltpu.VMEM((2 * n_par, PEER_TOPK, LANES), F32)],
        compiler_params=_cparams(("parallel", "parallel"), 56),
        name="peer_topk",
    )(ht, wq_t, k1, k2)


def _peer_dense_kernel(ht_ref, u_ref, vt_ref, cnt_ref, e1_ref, rk_ref, e2_ref, o_ref, wz_scr, *, n_sub):
    e = pl.program_id(2)

    @pl.when(e == 0)
    def _():
        o_ref[...] = jnp.zeros(o_ref.shape, F32)

    ht = ht_ref[0]
    per = 4
    n_split = n_sub // per
    for part in range(n_split):
        s = jnp.dot(u_ref[part * per * PEER_NKEYS:(part + 1) * per * PEER_NKEYS, :], ht,
                    preferred_element_type=F32)
        for a in range(part * per, (part + 1) * per):
            rows = slice(a * PEER_NKEYS, (a + 1) * PEER_NKEYS)
            local = slice((a - part * per) * PEER_NKEYS, (a - part * per + 1) * PEER_NKEYS)
            w = None
            for h in range(PEER_HEADS):
                cnt = cnt_ref[0, h, a:a + 1, :].astype(BF16)
                e1 = e1_ref[0, h, a:a + 1, :].astype(BF16)
                term = jnp.where(rk_ref[0, h] < cnt, e2_ref[0, h] * e1, 0.0)
                w = term if w is None else w + term
            wz_scr[rows, :] = w * _gelu(s[local, :]).astype(BF16)
    o_ref[0] += jnp.dot(vt_ref[...], wz_scr[...], preferred_element_type=F32)


def _peer_dense_call(ht, layer, u_bf16, vt_bf16, cnt, e1n, rank2, e2):
    bsz, d, l = ht.shape
    n_exp = u_bf16.shape[1]
    tm = _tile(l, 768, LANES)
    n_sub = SUBLANES
    te = n_sub * PEER_NKEYS
    kern = functools.partial(_peer_dense_kernel, n_sub=n_sub)
    aspec = pl.BlockSpec((1, PEER_HEADS, n_sub, tm), lambda b, i, e: (b, 0, e, i))
    fspec = pl.BlockSpec((1, PEER_HEADS, PEER_NKEYS, tm), lambda b, i, e: (b, 0, 0, i))
    return pl.pallas_call(
        kern,
        grid=(bsz, l // tm, n_exp // te),
        in_specs=[pl.BlockSpec((1, d, tm), lambda b, i, e: (b, 0, i)),
                  pl.BlockSpec((None, te, d), lambda b, i, e: (layer, e, 0)),
                  pl.BlockSpec((None, d, te), lambda b, i, e: (layer, 0, e)),
                  aspec, aspec, fspec, fspec],
        out_specs=pl.BlockSpec((1, d, tm), lambda b, i, e: (b, 0, i)),
        out_shape=jax.ShapeDtypeStruct((bsz, d, l), F32),
        scratch_shapes=[pltpu.VMEM((te, tm), BF16)],
        compiler_params=_cparams(("parallel", "parallel", "arbitrary"), 58),
        name="peer_dense",
    )(ht, u_bf16, vt_bf16, cnt, e1n, rank2, e2)


def _resid_kernel(x_ref, yt_ref, gate_ref, o_ref, *, tm, ctx_len, first_tile):
    b = pl.program_id(0)
    i = pl.program_id(1) + first_tile
    row = i * tm + lax.broadcasted_iota(jnp.int32, (tm, 1), 0)
    o_ref[0] = x_ref[0] + _row_select(gate_ref, b, row < ctx_len) * yt_ref[0].T


def _resid_call(x, yt, mod, layer, ctx_len, latent_only):
    bsz, l, d = x.shape
    tm = _tile(math.gcd(l, ctx_len), 384, LANES)
    first_tile = ctx_len // tm if latent_only else 0
    n_tiles = l // tm - first_tile
    kern = functools.partial(_resid_kernel, tm=tm, ctx_len=ctx_len, first_tile=first_tile)
    return pl.pallas_call(
        kern,
        grid=(bsz, n_tiles),
        in_specs=[pl.BlockSpec((1, tm, d), lambda b, i: (b, i + first_tile, 0)),
                  pl.BlockSpec((1, d, tm), lambda b, i: (b, 0, i + first_tile)),
                  pl.BlockSpec((None, 8, d), lambda b, i: (layer, 0, 5))],
        out_specs=pl.BlockSpec((1, tm, d), lambda b, i: (b, i, 0)),
        out_shape=jax.ShapeDtypeStruct((bsz, n_tiles * tm, d), F32),
        compiler_params=_cparams(("parallel", "parallel"), 40),
        name="resid",
    )(x, yt, mod)


def _rope_tables(ctx_len, t_lat):
    rows = t_lat // GRID_W
    row = jnp.repeat(jnp.arange(rows, dtype=F32), GRID_W)
    col = jnp.tile(jnp.arange(GRID_W, dtype=F32), rows)
    inv = ROPE_THETA ** (-jnp.arange(ROPE_PAIRS, dtype=F32) / ROPE_PAIRS)
    ang = jnp.concatenate([row[:, None] * inv] * 2 + [col[:, None] * inv] * 2, axis=1)
    ang = jnp.concatenate([jnp.zeros((ctx_len, HEAD_DIM), F32), ang], axis=0)
    sign = jnp.where((jnp.arange(HEAD_DIM) & ROPE_PAIRS) == 0, -1.0, 1.0).astype(F32)
    return jnp.cos(ang), jnp.sin(ang) * sign


def kernel(x, c, ctx, c_ctx, w_mod, b_mod, norm_mix, norm_ffn, w_in, q_norm, k_norm, lru_conv_w, lru_conv_b, lru_wa, lru_ba, lru_wx, lru_bx, lru_lambda, sc_conv_w, sc_conv_b, w_o_attn, w_o_lru, w_o_sc, w_out, peer_wq, peer_k1, peer_k2, peer_u, peer_v):
    bsz, t_lat, d = x.shape
    ctx_len = ctx.shape[1]
    depth = w_mod.shape[0]
    assert bsz == 2, "modulation rows are laid out as [latent 0, latent 1, context]"

    xs = jnp.concatenate([ctx, x], axis=1)
    s8 = jnp.concatenate([c, c_ctx[None, :], jnp.zeros((8 - bsz - 1, d), F32)], axis=0)
    mod = _mod_call(s8, w_mod, b_mod)
    cos, sin_signed = _rope_tables(ctx_len, t_lat)

    w_in_b = w_in.astype(BF16)
    w_att_b, w_lru_b, w_sc_b = w_o_attn.astype(BF16), w_o_lru.astype(BF16), w_o_sc.astype(BF16)
    w_out_b = w_out.astype(BF16)
    wq_t = jnp.swapaxes(peer_wq, 1, 2).astype(BF16)
    k1_b, k2_b = peer_k1.astype(BF16), peer_k2.astype(BF16)
    u_b = peer_u.astype(BF16)
    vt_b = jnp.swapaxes(peer_v, 1, 2).astype(BF16)

    for l in range(depth):
        p = _inproj_call(xs, mod, l, norm_mix[l], w_in_b, ctx_len)
        qn, ke, ve = _qkprep_call(p, cos, sin_signed, q_norm[l], k_norm[l])
        x_att = _attn_call(qn, ke, ve, q_norm[l], k_norm[l], ctx_len)
        lru_args = (lru_conv_w[l], lru_conv_b[l])
        h_fwd = _lru_call(p, *lru_args, lru_wa[l, 0], lru_ba[l, 0], lru_wx[l, 0], lru_bx[l, 0],
                          lru_lambda[l, 0], ctx_len, reverse=False)
        x_lru = _lru_call(p, *lru_args, lru_wa[l, 1], lru_ba[l, 1], lru_wx[l, 1], lru_bx[l, 1],
                          lru_lambda[l, 1], ctx_len, reverse=True, h_fwd=h_fwd)
        x_sc = _sconv_call(p, sc_conv_w[l], sc_conv_b[l], ctx_len)
        y = _merge_call(x_att, x_lru, x_sc, p, l, w_att_b, w_lru_b, w_sc_b)
        xs, ht = _outproj_call(y, w_out_b, xs, mod, l, norm_ffn[l], ctx_len)
        cnt, e1n, rank2, e2 = _peer_topk_call(ht, l, wq_t, k1_b, k2_b)
        yt = _peer_dense_call(ht, l, u_b, vt_b, cnt, e1n, rank2, e2)
        xs = _resid_call(xs, yt, mod, l, ctx_len, latent_only=(l == depth - 1))
    return xs
```

```python
import functools
import math

import jax
import jax.numpy as jnp
from jax import lax
from jax.experimental import pallas as pl
from jax.experimental.pallas import tpu as pltpu

F32 = jnp.float32
BF16 = jnp.bfloat16

GRID_W = 64
EPS = 1e-6

N_HEADS = 16
N_KV_HEADS = 4
HEAD_DIM = 128
GROUP = N_HEADS // N_KV_HEADS
ROPE_PAIRS = HEAD_DIM // 4
ROPE_THETA = 10000.0

LRU_BLOCK = 128
LRU_CONV = 4
LRU_C = 8.0
SC_CONV = 3

PEER_HEADS = 8
PEER_NKEYS = 128
PEER_TOPK = 16

LANES = 128
SUBLANES = 8
HALO = SUBLANES
NEG_INF = float("-inf")
LOG2E = 1.4426950408889634
GELU_C = math.sqrt(2.0 / math.pi)
Q_SCALE = HEAD_DIM ** -0.5 * LOG2E
SHIFT_MARGIN = 1.02
MIN_DENOM = 2.0 ** -100
ATTN_CHUNKS = 4


def _tile(n, target, mult):
    best = None
    for t in range(mult, min(n, target) + 1, mult):
        if n % t == 0:
            best = t
    assert best is not None, (n, target, mult)
    return best


def _cparams(sem, vmem_mib):
    return pltpu.CompilerParams(dimension_semantics=sem, vmem_limit_bytes=vmem_mib << 20)


def _gelu(x):
    return 0.5 * x * (1.0 + jnp.tanh(GELU_C * (x + 0.044715 * (x * x * x))))


def _row_select(mod_ref, b, is_ctx):
    return jnp.where(is_ctx, mod_ref[2:3, :], mod_ref[pl.ds(b, 1), :])


def _mod_kernel(s_ref, w_ref, b_ref, o_ref):
    s = s_ref[...]
    s = s * jax.nn.sigmoid(s)
    o_ref[0] = jnp.dot(s, w_ref[0], preferred_element_type=F32,
                       precision=lax.Precision.HIGHEST) + b_ref[0]


def _mod_call(s8, w_mod, b_mod):
    depth, d, n = w_mod.shape
    tn = _tile(n, 1024, LANES)
    return pl.pallas_call(
        _mod_kernel,
        grid=(depth, n // tn),
        in_specs=[pl.BlockSpec((8, d), lambda l, j: (0, 0)),
                  pl.BlockSpec((1, d, tn), lambda l, j: (l, 0, j)),
                  pl.BlockSpec((1, 1, tn), lambda l, j: (l, 0, j))],
        out_specs=pl.BlockSpec((1, 8, tn), lambda l, j: (l, 0, j)),
        out_shape=jax.ShapeDtypeStruct((depth, 8, n), F32),
        compiler_params=_cparams(("parallel", "parallel"), 32),
        name="mod",
    )(s8, w_mod, b_mod.reshape(depth, 1, n))


def _inproj_kernel(x_ref, shift_ref, scale_ref, g_ref, w_ref, o_ref, h_scr, *, tm, rows, ctx_len):
    b = pl.program_id(0)
    i = pl.program_id(1)

    @pl.when(pl.program_id(2) == 0)
    def _():
        def norm_rows(c, carry):
            r0 = pl.multiple_of(c * rows, rows)
            x = x_ref[0, pl.ds(r0, rows), :]
            var = jnp.mean(x * x, axis=-1, keepdims=True)
            y = x * lax.rsqrt(var + EPS) * g_ref[...]
            row = i * tm + r0 + lax.broadcasted_iota(jnp.int32, (rows, 1), 0)
            is_ctx = row < ctx_len
            sh = _row_select(shift_ref, b, is_ctx)
            sc = _row_select(scale_ref, b, is_ctx)
            h_scr[pl.ds(r0, rows), :] = (y * (1.0 + sc) + sh).astype(BF16)
            return carry

        lax.fori_loop(0, tm // rows, norm_rows, 0)

    o_ref[0] = jnp.dot(h_scr[...], w_ref[...], preferred_element_type=F32).astype(BF16)


def _inproj_call(x, mod, layer, gain, w_bf16, ctx_len):
    bsz, l, d = x.shape
    n = w_bf16.shape[2]
    tm = _tile(l, 1056, 16)
    rows = _tile(tm, 352, 16)
    tn = 1024
    kern = functools.partial(_inproj_kernel, tm=tm, rows=rows, ctx_len=ctx_len)
    return pl.pallas_call(
        kern,
        grid=(bsz, l // tm, n // tn),
        in_specs=[pl.BlockSpec((1, tm, d), lambda b, i, j: (b, i, 0)),
                  pl.BlockSpec((None, 8, d), lambda b, i, j: (layer, 0, 0)),
                  pl.BlockSpec((None, 8, d), lambda b, i, j: (layer, 0, 1)),
                  pl.BlockSpec((1, d), lambda b, i, j: (0, 0)),
                  pl.BlockSpec((None, d, tn), lambda b, i, j: (layer, 0, j))],
        out_specs=pl.BlockSpec((1, tm, tn), lambda b, i, j: (b, i, j)),
        out_shape=jax.ShapeDtypeStruct((bsz, l, n), BF16),
        scratch_shapes=[pltpu.VMEM((tm, d), BF16)],
        compiler_params=_cparams(("parallel", "parallel", "arbitrary"), 48),
        name="inproj",
    )(x, mod, mod, gain.reshape(1, d), w_bf16)


def _norm_rope(t, gain, cos, sin_signed, lane_lo):
    var = jnp.mean(t * t, axis=-1, keepdims=True)
    y = t * lax.rsqrt(var + EPS) * gain
    swapped = jnp.where(lane_lo, pltpu.roll(y, HEAD_DIM - ROPE_PAIRS, 1), pltpu.roll(y, ROPE_PAIRS, 1))
    return y * cos + swapped * sin_signed


def _qkprep_kernel(q_ref, k_ref, v_ref, cos_ref, sin_ref, qg_ref, kg_ref, qo_ref, ko_ref, vo_ref):
    cos = cos_ref[...]
    sin = sin_ref[...]
    lane = lax.broadcasted_iota(jnp.int32, cos.shape, 1)
    lane_lo = (lane & ROPE_PAIRS) == 0
    qg = qg_ref[...]
    kg = kg_ref[...]
    for h in range(N_HEADS):
        sl = slice(h * HEAD_DIM, (h + 1) * HEAD_DIM)
        t = q_ref[0, :, sl].astype(F32)
        qo_ref[0, :, sl] = (_norm_rope(t, qg, cos, sin, lane_lo) * Q_SCALE).astype(BF16)
    k_tail = jnp.where(lane == 0, 1.0, 0.0).astype(BF16)
    v_tail = jnp.ones(cos.shape, BF16)
    for h in range(N_KV_HEADS):
        sl = slice(h * HEAD_DIM, (h + 1) * HEAD_DIM)
        lo = slice(2 * h * HEAD_DIM, (2 * h + 1) * HEAD_DIM)
        hi = slice((2 * h + 1) * HEAD_DIM, (2 * h + 2) * HEAD_DIM)
        t = k_ref[0, :, sl].astype(F32)
        ko_ref[0, :, lo] = _norm_rope(t, kg, cos, sin, lane_lo).astype(BF16)
        ko_ref[0, :, hi] = k_tail
        vo_ref[0, :, lo] = v_ref[0, :, sl]
        vo_ref[0, :, hi] = v_tail


def _qkprep_call(p, cos, sin_signed, q_gain, k_gain):
    bsz, l, _ = p.shape
    qw = N_HEADS * HEAD_DIM
    kw = N_KV_HEADS * HEAD_DIM
    tm = _tile(l, 768, 16)
    return pl.pallas_call(
        _qkprep_kernel,
        grid=(bsz, l // tm),
        in_specs=[pl.BlockSpec((1, tm, qw), lambda b, i: (b, i, 0)),
                  pl.BlockSpec((1, tm, kw), lambda b, i: (b, i, qw // kw)),
                  pl.BlockSpec((1, tm, kw), lambda b, i: (b, i, qw // kw + 1)),
                  pl.BlockSpec((tm, HEAD_DIM), lambda b, i: (i, 0)),
                  pl.BlockSpec((tm, HEAD_DIM), lambda b, i: (i, 0)),
                  pl.BlockSpec((1, HEAD_DIM), lambda b, i: (0, 0)),
                  pl.BlockSpec((1, HEAD_DIM), lambda b, i: (0, 0))],
        out_specs=[pl.BlockSpec((1, tm, qw), lambda b, i: (b, i, 0)),
                   pl.BlockSpec((1, tm, 2 * kw), lambda b, i: (b, i, 0)),
                   pl.BlockSpec((1, tm, 2 * kw), lambda b, i: (b, i, 0))],
        out_shape=[jax.ShapeDtypeStruct((bsz, l, qw), BF16),
                   jax.ShapeDtypeStruct((bsz, l, 2 * kw), BF16),
                   jax.ShapeDtypeStruct((bsz, l, 2 * kw), BF16)],
        compiler_params=_cparams(("parallel", "parallel"), 40),
        name="qkprep",
    )(p, p, p, cos, sin_signed, q_gain.reshape(1, HEAD_DIM), k_gain.reshape(1, HEAD_DIM))


def _attn_kernel(q_ref, k_ref, v_ref, qg_ref, kg_ref, o_ref, qs_scr, acc_scr, sa_scr, sb_scr, m_scr, l_scr,
                 *, tq, tkc, ctx_len, n_lat_chunks):
    qi = pl.program_id(2)
    q = q_ref[0]
    gq = jnp.max(jnp.abs(qg_ref[...]), axis=-1, keepdims=True)
    gk = jnp.max(jnp.abs(kg_ref[...]), axis=-1, keepdims=True)
    bound = (SHIFT_MARGIN * HEAD_DIM * Q_SCALE) * gq * gk
    lane = lax.broadcasted_iota(jnp.int32, (1, HEAD_DIM), 1)
    tail = jnp.where(lane == 0, -bound, 0.0).astype(BF16)
    for h in range(GROUP):
        qs_scr[h * tq:(h + 1) * tq, :HEAD_DIM] = q[:, h * HEAD_DIM:(h + 1) * HEAD_DIM]
        qs_scr[h * tq:(h + 1) * tq, HEAD_DIM:] = jnp.broadcast_to(tail, (tq, HEAD_DIM))
    n = jnp.where(qi < ctx_len // tq, 0, n_lat_chunks)

    def lat_start(c):
        return pl.multiple_of(ctx_len + c * tkc, math.gcd(ctx_len, tkc))

    def write(out):
        for h in range(GROUP):
            o_ref[0, :, h * HEAD_DIM:(h + 1) * HEAD_DIM] = out[h * tq:(h + 1) * tq, :].astype(BF16)

    def scores(start, size):
        return lax.dot_general(qs_scr[...], k_ref[0, pl.ds(start, size), :], (((1,), (1,)), ((), ())),
                               preferred_element_type=F32)

    def accumulate(s, start, size):
        p = jnp.exp2(s).astype(BF16)
        acc_scr[...] += jnp.dot(p, v_ref[0, pl.ds(start, size), :], preferred_element_type=F32)

    acc_scr[...] = jnp.zeros(acc_scr.shape, F32)

    @pl.when(n == 0)
    def _():
        accumulate(scores(0, ctx_len), 0, ctx_len)

    @pl.when(n > 0)
    def _():
        bounds = [(0, ctx_len + tkc)] + [(ctx_len + c * tkc, tkc) for c in range(1, n_lat_chunks)]
        bufs = (sa_scr, sb_scr)
        bufs[0][:, :bounds[0][1]] = scores(*bounds[0])
        for c, (start, size) in enumerate(bounds):
            if c + 1 < len(bounds):
                nxt_start, nxt_size = bounds[c + 1]
                bufs[(c + 1) % 2][:, :nxt_size] = scores(nxt_start, nxt_size)
            accumulate(bufs[c % 2][:, :size], start, size)

    acc = acc_scr[...]
    den = acc[:, HEAD_DIM:]
    write(acc[:, :HEAD_DIM] / den)

    @pl.when(jnp.logical_not(jnp.min(den) >= MIN_DENOM))
    def _():
        m_scr[...] = jnp.full(m_scr.shape, NEG_INF, F32)
        l_scr[...] = jnp.zeros(l_scr.shape, F32)
        acc_scr[...] = jnp.zeros(acc_scr.shape, F32)

        def online_chunk(start, size):
            k = k_ref[0, pl.ds(start, size), :HEAD_DIM]
            v = v_ref[0, pl.ds(start, size), :HEAD_DIM]
            s = lax.dot_general(qs_scr[:, :HEAD_DIM], k, (((1,), (1,)), ((), ())),
                                preferred_element_type=F32)
            m_old = m_scr[...]
            m_new = jnp.maximum(m_old, jnp.max(s, axis=-1, keepdims=True))
            alpha = jnp.exp2(m_old - m_new)
            p = jnp.exp2(s - m_new)
            l_scr[...] = alpha * l_scr[...] + jnp.sum(p, axis=-1, keepdims=True)
            acc_scr[:, :HEAD_DIM] = alpha * acc_scr[:, :HEAD_DIM] + jnp.dot(
                p.astype(BF16), v, preferred_element_type=F32)
            m_scr[...] = m_new

        online_chunk(0, ctx_len)

        def online_body(c, carry):
            online_chunk(lat_start(c), tkc)
            return carry

        lax.fori_loop(0, n, online_body, 0)
        write(acc_scr[:, :HEAD_DIM] / l_scr[...])


def _attn_call(qn, ke, ve, q_gain, k_gain, ctx_len):
    bsz, l, qw = qn.shape
    tq = 256
    assert ctx_len % tq == 0 and l % tq == 0
    t_lat = l - ctx_len
    assert t_lat % (ATTN_CHUNKS * LANES) == 0
    tkc = t_lat // ATTN_CHUNKS
    gw = GROUP * HEAD_DIM
    ew = 2 * HEAD_DIM
    kern = functools.partial(_attn_kernel, tq=tq, tkc=tkc, ctx_len=ctx_len, n_lat_chunks=t_lat // tkc)
    return pl.pallas_call(
        kern,
        grid=(bsz, N_KV_HEADS, l // tq),
        in_specs=[pl.BlockSpec((1, tq, gw), lambda b, g, i: (b, i, g)),
                  pl.BlockSpec((1, l, ew), lambda b, g, i: (b, 0, g)),
                  pl.BlockSpec((1, l, ew), lambda b, g, i: (b, 0, g)),
                  pl.BlockSpec((1, HEAD_DIM), lambda b, g, i: (0, 0)),
                  pl.BlockSpec((1, HEAD_DIM), lambda b, g, i: (0, 0))],
        out_specs=pl.BlockSpec((1, tq, gw), lambda b, g, i: (b, i, g)),
        out_shape=jax.ShapeDtypeStruct((bsz, l, qw), BF16),
        scratch_shapes=[pltpu.VMEM((GROUP * tq, ew), BF16),
                        pltpu.VMEM((GROUP * tq, ew), F32),
                        pltpu.VMEM((GROUP * tq, ctx_len + tkc), F32),
                        pltpu.VMEM((GROUP * tq, ctx_len + tkc), F32),
                        pltpu.VMEM((GROUP * tq, 1), F32),
                        pltpu.VMEM((GROUP * tq, 1), F32)],
        compiler_params=_cparams(("parallel", "parallel", "arbitrary"), 56),
        name="attn",
    )(qn, ke, ve, q_gain.reshape(1, HEAD_DIM), k_gain.reshape(1, HEAD_DIM))


def _segment_bounds(row, ctx_len, seq_len):
    is_ctx = row < ctx_len
    first = jnp.where(is_ctx, 0, ctx_len)
    last = jnp.where(is_ctx, ctx_len - 1, seq_len - 1)
    return first, last


def _shift_down(x, prev, k, local):
    y = pltpu.roll(x, k, 0)
    for r in range(k):
        y = jnp.where(local == r, prev[HALO - k + r:HALO - k + r + 1, :], y)
    return y


def _shift_up(x, nxt, k, local, tt):
    y = pltpu.roll(x, tt - k, 0)
    for r in range(k):
        y = jnp.where(local == tt - k + r, nxt[r:r + 1, :], y)
    return y


def _halo_shifts(x, prev, nxt):
    tt = x.shape[0]
    sub = lax.broadcasted_iota(jnp.int32, (SUBLANES, x.shape[1]), 0)

    def down(k):
        y = pltpu.roll(x, k, 0)
        head = y[:SUBLANES]
        for r in range(k):
            head = jnp.where(sub == r, prev[HALO - k + r:HALO - k + r + 1, :], head)
        return jnp.concatenate([head, y[SUBLANES:]], axis=0)

    def up(k):
        y = pltpu.roll(x, tt - k, 0)
        tail = y[tt - SUBLANES:]
        for r in range(k):
            tail = jnp.where(sub == SUBLANES - k + r, nxt[r:r + 1, :], tail)
        return jnp.concatenate([y[:tt - SUBLANES], tail], axis=0)

    return down, up


def _scan_tile(a, d, carry, reverse):
    n_groups = a.shape[0] // SUBLANES
    sub = lax.broadcasted_iota(jnp.int32, (SUBLANES, a.shape[1]), 0)
    groups = []
    for v in range(n_groups):
        av = a[v * SUBLANES:(v + 1) * SUBLANES]
        dv = d[v * SUBLANES:(v + 1) * SUBLANES]
        for k in (1, 2, 4):
            keep = sub < SUBLANES - k if reverse else sub >= k
            shift = SUBLANES - k if reverse else k
            a_n = jnp.where(keep, pltpu.roll(av, shift, 0), 1.0)
            d_n = jnp.where(keep, pltpu.roll(dv, shift, 0), 0.0)
            dv = av * d_n + dv
            av = av * a_n
        groups.append((av, dv))
    hs = [None] * n_groups
    for v in (reversed(range(n_groups)) if reverse else range(n_groups)):
        av, dv = groups[v]
        hv = dv + av * carry
        carry = hv[0:1] if reverse else hv[SUBLANES - 1:SUBLANES]
        hs[v] = hv
    return jnp.concatenate(hs, axis=0), carry


def _lru_kernel(*refs, reverse, tt, ctx_len, n_sub):
    if reverse:
        (x_ref, xp_ref, xn_ref, cw_ref, cb_ref, wa_ref, ba_ref, wx_ref, bx_ref, lam_ref,
         hf_ref, g_ref, o_ref, carry_scr) = refs
    else:
        (x_ref, xp_ref, xn_ref, cw_ref, cb_ref, wa_ref, ba_ref, wx_ref, bx_ref, lam_ref,
         o_ref, carry_scr) = refs
    s = pl.program_id(2)
    nt = pl.num_programs(2)
    ti = jnp.where(s == 0, 0, nt - s) if reverse else s

    @pl.when(s == 0)
    def _():
        carry_scr[...] = jnp.zeros(carry_scr.shape, F32)

    prev_ok = jnp.logical_and(ti != 0, ti * tt != ctx_len)
    next_ok = jnp.logical_and(ti != nt - 1, (ti + 1) * tt != ctx_len)
    x = x_ref[0].astype(F32)
    xp = jnp.where(prev_ok, xp_ref[0].astype(F32), 0.0)
    xn = jnp.where(next_ok, xn_ref[0].astype(F32), 0.0)
    down, up = _halo_shifts(x, xp, xn)
    cw = cw_ref[...]
    u = cb_ref[...] + cw[2:3, :] * x + cw[0:1, :] * down(2) + cw[1:2, :] * down(1) + cw[3:4, :] * up(1)
    nlam = -lam_ref[...]
    softplus = jnp.maximum(nlam, 0.0) + jnp.log1p(jnp.exp(-jnp.abs(nlam)))

    for j in range(n_sub):
        ls = slice(j * LRU_BLOCK, (j + 1) * LRU_BLOCK)
        uj = u[:, ls]
        ub = uj.astype(BF16)
        r = jax.nn.sigmoid(jnp.dot(ub, wa_ref[j], preferred_element_type=F32) + ba_ref[:, ls])
        i = jax.nn.sigmoid(jnp.dot(ub, wx_ref[j], preferred_element_type=F32) + bx_ref[:, ls])
        a = jnp.exp((-LRU_C) * r * softplus[:, ls])
        d = jnp.sqrt(1.0 - a * a) * (i * uj)
        h, carry = _scan_tile(a, d, carry_scr[:, ls], reverse)
        carry_scr[:, ls] = carry
        if reverse:
            o_ref[0, :, ls] = ((hf_ref[0, :, ls] + h) * _gelu(g_ref[0, :, ls].astype(F32))).astype(BF16)
        else:
            o_ref[0, :, ls] = h


def _lru_call(p, conv_w, conv_b, wa, ba, wx, bx, lam, ctx_len, reverse, h_fwd=None):
    bsz, l, _ = p.shape
    c = conv_w.shape[1]
    tt = 256
    assert ctx_len % tt == 0 and l % tt == 0
    nt = l // tt
    n_sub = 8
    cw = n_sub * LRU_BLOCK
    nb = c // cw
    x_col0 = (N_HEADS + 2 * N_KV_HEADS) * HEAD_DIM // cw
    g_col0 = x_col0 + nb
    hb = tt // HALO
    n_hblk = l // HALO

    def tile_of(s):
        return jnp.where(s == 0, 0, nt - s) if reverse else s

    x_map = lambda b, cb, s: (b, tile_of(s), x_col0 + cb)
    prev_map = lambda b, cb, s: (b, jnp.maximum(tile_of(s) * hb - 1, 0), x_col0 + cb)
    next_map = lambda b, cb, s: (b, jnp.minimum((tile_of(s) + 1) * hb, n_hblk - 1), x_col0 + cb)
    vec_map = lambda b, cb, s: (0, cb)
    w_spec = pl.BlockSpec((n_sub, LRU_BLOCK, LRU_BLOCK), lambda b, cb, s: (cb, 0, 0))
    in_specs = [pl.BlockSpec((1, tt, cw), x_map),
                pl.BlockSpec((1, HALO, cw), prev_map),
                pl.BlockSpec((1, HALO, cw), next_map),
                pl.BlockSpec((LRU_CONV, cw), vec_map),
                pl.BlockSpec((1, cw), vec_map),
                w_spec,
                pl.BlockSpec((1, cw), vec_map),
                w_spec,
                pl.BlockSpec((1, cw), vec_map),
                pl.BlockSpec((1, cw), vec_map)]
    args = [p, p, p, conv_w, conv_b.reshape(1, c), wa.astype(BF16), ba.reshape(1, c),
            wx.astype(BF16), bx.reshape(1, c), lam.reshape(1, c)]
    out_map = lambda b, cb, s: (b, tile_of(s), cb)
    if reverse:
        in_specs += [pl.BlockSpec((1, tt, cw), out_map),
                     pl.BlockSpec((1, tt, cw), lambda b, cb, s: (b, tile_of(s), g_col0 + cb))]
        args += [h_fwd, p]
        out_dtype = BF16
    else:
        out_dtype = F32
    kern = functools.partial(_lru_kernel, reverse=reverse, tt=tt, ctx_len=ctx_len, n_sub=n_sub)
    return pl.pallas_call(
        kern,
        grid=(bsz, nb, nt),
        in_specs=in_specs,
        out_specs=pl.BlockSpec((1, tt, cw), out_map),
        out_shape=jax.ShapeDtypeStruct((bsz, l, c), out_dtype),
        scratch_shapes=[pltpu.VMEM((1, cw), F32)],
        compiler_params=_cparams(("parallel", "parallel", "arbitrary"), 32),
        name="lru_rev" if reverse else "lru_fwd",
    )(*args)


def _sconv_kernel(bg_ref, cg_ref, u_ref, cgp_ref, up_ref, cgn_ref, un_ref, w_ref, b_ref, o_ref,
                  *, tt, ctx_len, seq_len):
    ti = pl.program_id(1)
    z = cg_ref[0].astype(F32) * u_ref[0].astype(F32)
    zp = cgp_ref[0].astype(F32) * up_ref[0].astype(F32)
    zn = cgn_ref[0].astype(F32) * un_ref[0].astype(F32)
    local = lax.broadcasted_iota(jnp.int32, (tt, 1), 0)
    row = ti * tt + local
    first, last = _segment_bounds(row, ctx_len, seq_len)
    w = w_ref[...]
    y = b_ref[...] + w[1:2, :] * z
    y = y + w[0:1, :] * jnp.where(row - 1 >= first, _shift_down(z, zp, 1, local), 0.0)
    y = y + w[2:3, :] * jnp.where(row + 1 <= last, _shift_up(z, zn, 1, local, tt), 0.0)
    o_ref[0] = (bg_ref[0].astype(F32) * y).astype(BF16)


def _sconv_call(p, w, b, ctx_len):
    bsz, l, _ = p.shape
    c = w.shape[1]
    tt = _tile(l, 768, 16)
    tc = 512
    ncb = c // tc
    b_col0 = (N_HEADS * HEAD_DIM + 2 * N_KV_HEADS * HEAD_DIM + 2 * c) // tc
    c_col0 = b_col0 + ncb
    u_col0 = c_col0 + ncb
    hb = tt // HALO
    n_hblk = l // HALO
    prev = lambda i: jnp.maximum(i * hb - 1, 0)
    nxt = lambda i: jnp.minimum((i + 1) * hb, n_hblk - 1)
    kern = functools.partial(_sconv_kernel, tt=tt, ctx_len=ctx_len, seq_len=l)
    return pl.pallas_call(
        kern,
        grid=(bsz, l // tt, ncb),
        in_specs=[pl.BlockSpec((1, tt, tc), lambda b_, i, j: (b_, i, b_col0 + j)),
                  pl.BlockSpec((1, tt, tc), lambda b_, i, j: (b_, i, c_col0 + j)),
                  pl.BlockSpec((1, tt, tc), lambda b_, i, j: (b_, i, u_col0 + j)),
                  pl.BlockSpec((1, HALO, tc), lambda b_, i, j: (b_, prev(i), c_col0 + j)),
                  pl.BlockSpec((1, HALO, tc), lambda b_, i, j: (b_, prev(i), u_col0 + j)),
                  pl.BlockSpec((1, HALO, tc), lambda b_, i, j: (b_, nxt(i), c_col0 + j)),
                  pl.BlockSpec((1, HALO, tc), lambda b_, i, j: (b_, nxt(i), u_col0 + j)),
                  pl.BlockSpec((SC_CONV, tc), lambda b_, i, j: (0, j)),
                  pl.BlockSpec((1, tc), lambda b_, i, j: (0, j))],
        out_specs=pl.BlockSpec((1, tt, tc), lambda b_, i, j: (b_, i, j)),
        out_shape=jax.ShapeDtypeStruct((bsz, l, c), BF16),
        compiler_params=_cparams(("parallel", "parallel", "parallel"), 32),
        name="sconv",
    )(p, p, p, p, p, p, p, w, b.reshape(1, c))


def _merge_kernel(xa_ref, xl_ref, xs_ref, ga_ref, gl_ref, gs_ref, wa_ref, wl_ref, ws_ref, o_ref):
    y = jax.nn.sigmoid(ga_ref[0].astype(F32)) * jnp.dot(xa_ref[0], wa_ref[...], preferred_element_type=F32)
    y = y + jax.nn.sigmoid(gl_ref[0].astype(F32)) * jnp.dot(xl_ref[0], wl_ref[...], preferred_element_type=F32)
    y = y + jax.nn.sigmoid(gs_ref[0].astype(F32)) * jnp.dot(xs_ref[0], ws_ref[...], preferred_element_type=F32)
    o_ref[0] = y.astype(BF16)


def _merge_call(x_att, x_lru, x_sc, p, layer, w_att, w_lru, w_sc):
    bsz, l, d = x_att.shape
    tm = _tile(l, 768, 16)
    tn = 512
    g_col0 = (p.shape[2] - 3 * d) // tn
    nj = d // tn
    xspec = pl.BlockSpec((1, tm, d), lambda b, i, j: (b, i, 0))
    wspec = pl.BlockSpec((None, d, tn), lambda b, i, j: (layer, 0, j))
    gspec = lambda k: pl.BlockSpec((1, tm, tn), lambda b, i, j: (b, i, g_col0 + k * nj + j))
    return pl.pallas_call(
        _merge_kernel,
        grid=(bsz, l // tm, nj),
        in_specs=[xspec, xspec, xspec, gspec(0), gspec(1), gspec(2), wspec, wspec, wspec],
        out_specs=pl.BlockSpec((1, tm, tn), lambda b, i, j: (b, i, j)),
        out_shape=jax.ShapeDtypeStruct((bsz, l, d), BF16),
        compiler_params=_cparams(("parallel", "parallel", "arbitrary"), 48),
        name="merge",
    )(x_att, x_lru, x_sc, p, p, p, w_att, w_lru, w_sc)


def _outproj_kernel(y_ref, w_ref, x_ref, gate_ref, shift_ref, scale_ref, g_ref, xo_ref, ht_ref,
                    *, tm, ctx_len):
    b = pl.program_id(0)
    i = pl.program_id(1)
    row = i * tm + lax.broadcasted_iota(jnp.int32, (tm, 1), 0)
    is_ctx = row < ctx_len
    acc = jnp.dot(y_ref[0], w_ref[...], preferred_element_type=F32)
    xn = x_ref[0] + _row_select(gate_ref, b, is_ctx) * acc
    xo_ref[0] = xn
    var = jnp.mean(xn * xn, axis=-1, keepdims=True)
    h = xn * lax.rsqrt(var + EPS) * g_ref[...]
    h = h * (1.0 + _row_select(scale_ref, b, is_ctx)) + _row_select(shift_ref, b, is_ctx)
    ht_ref[0] = h.T.astype(BF16)


def _outproj_call(y, w_out, x, mod, layer, gain, ctx_len):
    bsz, l, d = x.shape
    tm = _tile(l, 384, LANES)
    kern = functools.partial(_outproj_kernel, tm=tm, ctx_len=ctx_len)
    mspec = lambda k: pl.BlockSpec((None, 8, d), lambda b, i: (layer, 0, k))
    return pl.pallas_call(
        kern,
        grid=(bsz, l // tm),
        in_specs=[pl.BlockSpec((1, tm, d), lambda b, i: (b, i, 0)),
                  pl.BlockSpec((None, d, d), lambda b, i: (layer, 0, 0)),
                  pl.BlockSpec((1, tm, d), lambda b, i: (b, i, 0)),
                  mspec(2), mspec(3), mspec(4),
                  pl.BlockSpec((1, d), lambda b, i: (0, 0))],
        out_specs=[pl.BlockSpec((1, tm, d), lambda b, i: (b, i, 0)),
                   pl.BlockSpec((1, d, tm), lambda b, i: (b, 0, i))],
        out_shape=[jax.ShapeDtypeStruct((bsz, l, d), F32),
                   jax.ShapeDtypeStruct((bsz, d, l), BF16)],
        compiler_params=_cparams(("parallel", "parallel"), 48),
        name="outproj",
    )(y, w_out, x, mod, mod, mod, gain.reshape(1, d))


_CAND_ROWS = tuple(PEER_TOPK // (i + 1) for i in range(PEER_TOPK))


def _top16(s, v_scr):
    n = s.shape[0]
    key = lax.broadcasted_iota(jnp.int32, s.shape, 0)

    def body(r, carry):
        work, rank = carry
        m = jnp.max(work, axis=0, keepdims=True)
        v_scr[pl.ds(r, 1), :] = m
        first = jnp.min(jnp.where(work == m, key, n), axis=0, keepdims=True)
        sel = key == first
        return jnp.where(sel, NEG_INF, work), jnp.where(sel, jnp.asarray(r, F32), rank)

    _, rank = lax.fori_loop(0, PEER_TOPK, body, (s, jnp.full(s.shape, float(PEER_TOPK), F32)))
    return rank


MARK = -(2.0 ** 127)


def _top16_distinct(ss, v_scrs):
    def body(r, works):
        mark = jnp.asarray(r, F32) * (MARK / 32.0) + MARK
        out = []
        for work, v_scr in zip(works, v_scrs):
            m = jnp.max(work, axis=0, keepdims=True)
            v_scr[pl.ds(r, 1), :] = m
            out.append(jnp.where(work == m, mark, work))
        return tuple(out)

    works = lax.fori_loop(0, PEER_TOPK, body, tuple(ss), unroll=True)
    return [jnp.where(w <= MARK, (MARK - w) * (-32.0 / MARK), float(PEER_TOPK)) for w in works]


def _peer_topk_kernel(ht_ref, wq_ref, k1_ref, k2_ref, cnt_ref, e1_ref, rk_ref, e2_ref,
                      q_scr, v_scr, *, tm, n_par, h_par):
    half = PEER_NKEYS
    q_scr[...] = jnp.dot(wq_ref[...], ht_ref[0], preferred_element_type=F32).astype(BF16)
    sub = lax.broadcasted_iota(jnp.int32, (SUBLANES, LANES), 0)
    big = PEER_TOPK * PEER_TOPK

    def candidates(v1, v2):
        pieces, poss = [], []
        for i in range(SUBLANES):
            for j0 in range(0, _CAND_ROWS[i], SUBLANES):
                c = v1[i:i + 1, :] + v2[j0:j0 + SUBLANES, :]
                valid = sub + j0 < _CAND_ROWS[i]
                pieces.append(jnp.where(valid, c, NEG_INF))
                poss.append(jnp.where(valid, i * PEER_TOPK + j0 + sub, big))
        pieces.append(v1[SUBLANES:, :] + v2[0:1, :])
        poss.append((sub + SUBLANES) * PEER_TOPK)
        return pieces, poss

    def pick_exact(pieces, poss):
        def pick(_, carry2):
            cs, sels = carry2
            m = functools.reduce(jnp.maximum, cs)
            m = jnp.max(m, axis=0, keepdims=True)
            cand_pos = functools.reduce(jnp.minimum, [jnp.where(c == m, p_, big) for c, p_ in zip(cs, poss)])
            first = jnp.min(cand_pos, axis=0, keepdims=True)
            hit = [p_ == first for p_ in poss]
            cs = tuple(jnp.where(hh, NEG_INF, c) for hh, c in zip(hit, cs))
            sels = tuple(jnp.where(hh, 1.0, s_) for hh, s_ in zip(hit, sels))
            return cs, sels

        zeros = tuple(jnp.zeros((SUBLANES, LANES), F32) for _ in pieces)
        return lax.fori_loop(0, PEER_TOPK, pick, (tuple(pieces), zeros))[1]

    def pick_distinct(groups):
        npc = len(groups[0])

        def pick(_, cs):
            out = []
            for g in range(len(groups)):
                grp = cs[g * npc:(g + 1) * npc]
                m = jnp.max(functools.reduce(jnp.maximum, grp), axis=0, keepdims=True)
                out.extend(jnp.where(c == m, MARK, c) for c in grp)
            return tuple(out)

        marked = lax.fori_loop(0, PEER_TOPK, pick, tuple(c for grp in groups for c in grp), unroll=True)
        return [tuple(jnp.where(c == MARK, 1.0, 0.0) for c in marked[g * npc:(g + 1) * npc])
                for g in range(len(groups))]

    def compute(units, exact):
        n_g = len(units)
        bases = [pl.multiple_of(h * 2 * half, 2 * half) for h, _ in units]
        s1s = [jnp.dot(k1_ref[h], q_scr[pl.ds(b0, half), ls], preferred_element_type=F32)
               for (h, ls), b0 in zip(units, bases)]
        s2s = [jnp.dot(k2_ref[h], q_scr[pl.ds(b0 + half, half), ls], preferred_element_type=F32)
               for (h, ls), b0 in zip(units, bases)]
        v1_refs = [v_scr.at[2 * g] for g in range(n_g)]
        v2_refs = [v_scr.at[2 * g + 1] for g in range(n_g)]
        if exact:
            rank1s = [_top16(s, r) for s, r in zip(s1s, v1_refs)]
            rank2s = [_top16(s, r) for s, r in zip(s2s, v2_refs)]
        else:
            ranks = _top16_distinct(tuple(s1s + s2s), tuple(v1_refs + v2_refs))
            rank1s, rank2s = ranks[:n_g], ranks[n_g:]
        v1s = [r[...] for r in v1_refs]
        v2s = [r[...] for r in v2_refs]
        cands = [candidates(v1, v2) for v1, v2 in zip(v1s, v2s)]
        if exact:
            all_sels = [pick_exact(pieces, poss) for pieces, poss in cands]
        else:
            all_sels = pick_distinct([pieces for pieces, _ in cands])
        most = None
        for g in range(n_g):
            most_g = finish(units[g][0], units[g][1], s1s[g], s2s[g], rank1s[g], rank2s[g], v1s[g], v2s[g],
                            cands[g][0], all_sels[g])
            most = most_g if most is None else jnp.maximum(most, most_g)
        return most

    def finish(h, ls, s1, s2, rank1, rank2, v1, v2, orig, sels):
        top = v1[0:1, :] + v2[0:1, :]
        zsum = functools.reduce(
            lambda a_, b_: a_ + b_,
            [jnp.where(s_ > 0.0, jnp.exp(o - top), 0.0) for s_, o in zip(sels, orig)])
        zinv = 1.0 / jnp.sum(zsum, axis=0, keepdims=True)

        counts = []
        pi = 0
        for i in range(SUBLANES):
            c = None
            for j0 in range(0, _CAND_ROWS[i], SUBLANES):
                part = jnp.sum(sels[pi], axis=0, keepdims=True)
                c = part if c is None else c + part
                pi += 1
            counts.append(c)
        tail = sels[pi]
        for i in range(SUBLANES, PEER_TOPK):
            counts.append(tail[i - SUBLANES:i - SUBLANES + 1, :])
        cnt = jnp.zeros(rank1.shape, F32)
        for i in range(PEER_TOPK):
            cnt = jnp.where(rank1 == float(i), counts[i], cnt)

        cnt_ref[0, h, :, ls] = cnt
        e1_ref[0, h, :, ls] = jnp.exp(s1 - v1[0:1, :]) * zinv
        rk_ref[0, h, :, ls] = rank2.astype(BF16)
        e2_ref[0, h, :, ls] = jnp.exp(s2 - v2[0:1, :]).astype(BF16)
        ranked = jnp.where(rank1 < PEER_TOPK, 1.0, 0.0)
        ranked = jnp.maximum(jnp.sum(ranked, axis=0, keepdims=True),
                             jnp.sum(jnp.where(rank2 < PEER_TOPK, 1.0, 0.0), axis=0, keepdims=True))
        picked = jnp.sum(functools.reduce(lambda a_, b_: a_ + b_, sels), axis=0, keepdims=True)
        return jnp.max(jnp.maximum(ranked, picked))

    n_trips = tm // (n_par * LANES)

    def head_lane_groups(idx, carry):
        h0 = (idx // n_trips) * h_par
        first = (idx % n_trips) * n_par
        lss = [pl.ds(pl.multiple_of((first + g) * LANES, LANES), LANES) for g in range(n_par)]
        units = [(h0 + k, ls) for k in range(h_par) for ls in lss]
        most = compute(units, exact=False)

        @pl.when(most > PEER_TOPK)
        def _():
            compute(units, exact=True)

        return carry

    lax.fori_loop(0, (PEER_HEADS // h_par) * n_trips, head_lane_groups, 0)


def _peer_topk_call(ht, layer, wq_t, k1, k2):
    bsz, d, l = ht.shape
    qd = wq_t.shape[1]
    n_par = 6
    tm = _tile(l, 768, n_par * LANES)
    h_par = 2
    kern = functools.partial(_peer_topk_kernel, tm=tm, n_par=n_par, h_par=h_par)
    ospec = pl.BlockSpec((1, PEER_HEADS, PEER_NKEYS, tm), lambda b, i: (b, 0, 0, i))
    oshape = jax.ShapeDtypeStruct((bsz, PEER_HEADS, PEER_NKEYS, l), F32)
    kspec = pl.BlockSpec((None, PEER_HEADS, PEER_NKEYS, PEER_NKEYS), lambda b, i: (layer, 0, 0, 0))
    return pl.pallas_call(
        kern,
        grid=(bsz, l // tm),
        in_specs=[pl.BlockSpec((1, d, tm), lambda b, i: (b, 0, i)),
                  pl.BlockSpec((None, qd, d), lambda b, i: (layer, 0, 0)),
                  kspec, kspec],
        out_specs=[ospec, ospec, ospec, ospec],
        out_shape=[oshape, oshape, jax.ShapeDtypeStruct(oshape.shape, BF16),
                   jax.ShapeDtypeStruct(oshape.shape, BF16)],
        scratch_shapes=[pltpu.VMEM((qd, tm), BF16),
                        pltpu.VMEM((2 * n_par * h_par, PEER_TOPK, LANES), F32)],
        compiler_params=_cparams(("parallel", "parallel"), 56),
        name="peer_topk",
    )(ht, wq_t, k1, k2)


def _peer_dense_kernel(ht_ref, u_ref, vt_ref, cnt_ref, e1_ref, rk_ref, e2_ref, o_ref, wz_scr, *, n_sub):
    e = pl.program_id(2)

    @pl.when(e == 0)
    def _():
        o_ref[...] = jnp.zeros(o_ref.shape, F32)

    ht = ht_ref[0]
    per = 4
    n_split = n_sub // per
    for part in range(n_split):
        s = jnp.dot(u_ref[part * per * PEER_NKEYS:(part + 1) * per * PEER_NKEYS, :], ht,
                    preferred_element_type=F32)
        for a in range(part * per, (part + 1) * per):
            rows = slice(a * PEER_NKEYS, (a + 1) * PEER_NKEYS)
            local = slice((a - part * per) * PEER_NKEYS, (a - part * per + 1) * PEER_NKEYS)
            w = None
            for h in range(PEER_HEADS):
                cnt = cnt_ref[0, h, a:a + 1, :].astype(BF16)
                e1 = e1_ref[0, h, a:a + 1, :].astype(BF16)
                term = jnp.where(rk_ref[0, h] < cnt, e2_ref[0, h] * e1, 0.0)
                w = term if w is None else w + term
            wz_scr[rows, :] = w * _gelu(s[local, :]).astype(BF16)
    o_ref[0] += jnp.dot(vt_ref[...], wz_scr[...], preferred_element_type=F32)


def _peer_dense_call(ht, layer, u_bf16, vt_bf16, cnt, e1n, rank2, e2):
    bsz, d, l = ht.shape
    n_exp = u_bf16.shape[1]
    tm = _tile(l, 768, LANES)
    n_sub = SUBLANES
    te = n_sub * PEER_NKEYS
    kern = functools.partial(_peer_dense_kernel, n_sub=n_sub)
    aspec = pl.BlockSpec((1, PEER_HEADS, n_sub, tm), lambda b, i, e: (b, 0, e, i))
    fspec = pl.BlockSpec((1, PEER_HEADS, PEER_NKEYS, tm), lambda b, i, e: (b, 0, 0, i))
    return pl.pallas_call(
        kern,
        grid=(bsz, l // tm, n_exp // te),
        in_specs=[pl.BlockSpec((1, d, tm), lambda b, i, e: (b, 0, i)),
                  pl.BlockSpec((None, te, d), lambda b, i, e: (layer, e, 0)),
                  pl.BlockSpec((None, d, te), lambda b, i, e: (layer, 0, e)),
                  aspec, aspec, fspec, fspec],
        out_specs=pl.BlockSpec((1, d, tm), lambda b, i, e: (b, 0, i)),
        out_shape=jax.ShapeDtypeStruct((bsz, d, l), F32),
        scratch_shapes=[pltpu.VMEM((te, tm), BF16)],
        compiler_params=_cparams(("parallel", "parallel", "arbitrary"), 58),
        name="peer_dense",
    )(ht, u_bf16, vt_bf16, cnt, e1n, rank2, e2)


def _resid_kernel(x_ref, yt_ref, gate_ref, o_ref, *, tm, ctx_len, first_tile):
    b = pl.program_id(0)
    i = pl.program_id(1) + first_tile
    row = i * tm + lax.broadcasted_iota(jnp.int32, (tm, 1), 0)
    o_ref[0] = x_ref[0] + _row_select(gate_ref, b, row < ctx_len) * yt_ref[0].T


def _resid_call(x, yt, mod, layer, ctx_len, latent_only):
    bsz, l, d = x.shape
    tm = _tile(math.gcd(l, ctx_len), 384, LANES)
    first_tile = ctx_len // tm if latent_only else 0
    n_tiles = l // tm - first_tile
    kern = functools.partial(_resid_kernel, tm=tm, ctx_len=ctx_len, first_tile=first_tile)
    return pl.pallas_call(
        kern,
        grid=(bsz, n_tiles),
        in_specs=[pl.BlockSpec((1, tm, d), lambda b, i: (b, i + first_tile, 0)),
                  pl.BlockSpec((1, d, tm), lambda b, i: (b, 0, i + first_tile)),
                  pl.BlockSpec((None, 8, d), lambda b, i: (layer, 0, 5))],
        out_specs=pl.BlockSpec((1, tm, d), lambda b, i: (b, i, 0)),
        out_shape=jax.ShapeDtypeStruct((bsz, n_tiles * tm, d), F32),
        compiler_params=_cparams(("parallel", "parallel"), 40),
        name="resid",
    )(x, yt, mod)


def _rope_tables(ctx_len, t_lat):
    rows = t_lat // GRID_W
    row = jnp.repeat(jnp.arange(rows, dtype=F32), GRID_W)
    col = jnp.tile(jnp.arange(GRID_W, dtype=F32), rows)
    inv = ROPE_THETA ** (-jnp.arange(ROPE_PAIRS, dtype=F32) / ROPE_PAIRS)
    ang = jnp.concatenate([row[:, None] * inv] * 2 + [col[:, None] * inv] * 2, axis=1)
    ang = jnp.concatenate([jnp.zeros((ctx_len, HEAD_DIM), F32), ang], axis=0)
    sign = jnp.where((jnp.arange(HEAD_DIM) & ROPE_PAIRS) == 0, -1.0, 1.0).astype(F32)
    return jnp.cos(ang), jnp.sin(ang) * sign


def kernel(x, c, ctx, c_ctx, w_mod, b_mod, norm_mix, norm_ffn, w_in, q_norm, k_norm, lru_conv_w, lru_conv_b, lru_wa, lru_ba, lru_wx, lru_bx, lru_lambda, sc_conv_w, sc_conv_b, w_o_attn, w_o_lru, w_o_sc, w_out, peer_wq, peer_k1, peer_k2, peer_u, peer_v):
    bsz, t_lat, d = x.shape
    ctx_len = ctx.shape[1]
    depth = w_mod.shape[0]
    assert bsz == 2, "modulation rows are laid out as [latent 0, latent 1, context]"

    xs = jnp.concatenate([ctx, x], axis=1)
    s8 = jnp.concatenate([c, c_ctx[None, :], jnp.zeros((8 - bsz - 1, d), F32)], axis=0)
    mod = _mod_call(s8, w_mod, b_mod)
    cos, sin_signed = _rope_tables(ctx_len, t_lat)

    w_in_b = w_in.astype(BF16)
    w_att_b, w_lru_b, w_sc_b = w_o_attn.astype(BF16), w_o_lru.astype(BF16), w_o_sc.astype(BF16)
    w_out_b = w_out.astype(BF16)
    wq_t = jnp.swapaxes(peer_wq, 1, 2).astype(BF16)
    k1_b, k2_b = peer_k1.astype(BF16), peer_k2.astype(BF16)
    u_b = peer_u.astype(BF16)
    vt_b = jnp.swapaxes(peer_v, 1, 2).astype(BF16)

    for l in range(depth):
        p = _inproj_call(xs, mod, l, norm_mix[l], w_in_b, ctx_len)
        qn, ke, ve = _qkprep_call(p, cos, sin_signed, q_norm[l], k_norm[l])
        x_att = _attn_call(qn, ke, ve, q_norm[l], k_norm[l], ctx_len)
        lru_args = (lru_conv_w[l], lru_conv_b[l])
        h_fwd = _lru_call(p, *lru_args, lru_wa[l, 0], lru_ba[l, 0], lru_wx[l, 0], lru_bx[l, 0],
                          lru_lambda[l, 0], ctx_len, reverse=False)
        x_lru = _lru_call(p, *lru_args, lru_wa[l, 1], lru_ba[l, 1], lru_wx[l, 1], lru_bx[l, 1],
                          lru_lambda[l, 1], ctx_len, reverse=True, h_fwd=h_fwd)
        x_sc = _sconv_call(p, sc_conv_w[l], sc_conv_b[l], ctx_len)
        y = _merge_call(x_att, x_lru, x_sc, p, l, w_att_b, w_lru_b, w_sc_b)
        xs, ht = _outproj_call(y, w_out_b, xs, mod, l, norm_ffn[l], ctx_len)
        cnt, e1n, rank2, e2 = _peer_topk_call(ht, l, wq_t, k1_b, k2_b)
        yt = _peer_dense_call(ht, l, u_b, vt_b, cnt, e1n, rank2, e2)
        xs = _resid_call(xs, yt, mod, l, ctx_len, latent_only=(l == depth - 1))
    return xs
```

```python
import functools
import math

import jax
import jax.numpy as jnp
from jax import lax
from jax.experimental import pallas as pl
from jax.experimental.pallas import tpu as pltpu

F32 = jnp.float32
BF16 = jnp.bfloat16

GRID_W = 64
EPS = 1e-6

N_HEADS = 16
N_KV_HEADS = 4
HEAD_DIM = 128
GROUP = N_HEADS // N_KV_HEADS
ROPE_PAIRS = HEAD_DIM // 4
ROPE_THETA = 10000.0

LRU_BLOCK = 128
LRU_CONV = 4
LRU_C = 8.0
SC_CONV = 3

PEER_HEADS = 8
PEER_NKEYS = 128
PEER_TOPK = 16

LANES = 128
SUBLANES = 8
HALO = SUBLANES
NEG_INF = float("-inf")
LOG2E = 1.4426950408889634
GELU_C = math.sqrt(2.0 / math.pi)
Q_SCALE = HEAD_DIM ** -0.5 * LOG2E
SHIFT_MARGIN = 1.02
MIN_DENOM = 2.0 ** -100
ATTN_CHUNKS = 4


def _tile(n, target, mult):
    best = None
    for t in range(mult, min(n, target) + 1, mult):
        if n % t == 0:
            best = t
    assert best is not None, (n, target, mult)
    return best


def _cparams(sem, vmem_mib):
    return pltpu.CompilerParams(dimension_semantics=sem, vmem_limit_bytes=vmem_mib << 20)


def _gelu(x):
    return 0.5 * x * (1.0 + jnp.tanh(GELU_C * (x + 0.044715 * (x * x * x))))


def _row_select(mod_ref, b, is_ctx):
    return jnp.where(is_ctx, mod_ref[2:3, :], mod_ref[pl.ds(b, 1), :])


def _mod_kernel(s_ref, w_ref, b_ref, o_ref):
    s = s_ref[...]
    s = s * jax.nn.sigmoid(s)
    o_ref[0] = jnp.dot(s, w_ref[0], preferred_element_type=F32,
                       precision=lax.Precision.HIGHEST) + b_ref[0]


def _mod_call(s8, w_mod, b_mod):
    depth, d, n = w_mod.shape
    tn = _tile(n, 1024, LANES)
    return pl.pallas_call(
        _mod_kernel,
        grid=(depth, n // tn),
        in_specs=[pl.BlockSpec((8, d), lambda l, j: (0, 0)),
                  pl.BlockSpec((1, d, tn), lambda l, j: (l, 0, j)),
                  pl.BlockSpec((1, 1, tn), lambda l, j: (l, 0, j))],
        out_specs=pl.BlockSpec((1, 8, tn), lambda l, j: (l, 0, j)),
        out_shape=jax.ShapeDtypeStruct((depth, 8, n), F32),
        compiler_params=_cparams(("parallel", "parallel"), 32),
        name="mod",
    )(s8, w_mod, b_mod.reshape(depth, 1, n))


def _inproj_kernel(x_ref, shift_ref, scale_ref, g_ref, w_ref, o_ref, h_scr, *, tm, rows, ctx_len):
    b = pl.program_id(0)
    i = pl.program_id(1)

    @pl.when(pl.program_id(2) == 0)
    def _():
        def norm_rows(c, carry):
            r0 = pl.multiple_of(c * rows, rows)
            x = x_ref[0, pl.ds(r0, rows), :]
            var = jnp.mean(x * x, axis=-1, keepdims=True)
            y = x * lax.rsqrt(var + EPS) * g_ref[...]
            row = i * tm + r0 + lax.broadcasted_iota(jnp.int32, (rows, 1), 0)
            is_ctx = row < ctx_len
            sh = _row_select(shift_ref, b, is_ctx)
            sc = _row_select(scale_ref, b, is_ctx)
            h_scr[pl.ds(r0, rows), :] = (y * (1.0 + sc) + sh).astype(BF16)
            return carry

        lax.fori_loop(0, tm // rows, norm_rows, 0)

    o_ref[0] = jnp.dot(h_scr[...], w_ref[...], preferred_element_type=F32).astype(BF16)


def _inproj_call(x, mod, layer, gain, w_bf16, ctx_len):
    bsz, l, d = x.shape
    n = w_bf16.shape[2]
    tm = _tile(l, 1056, 16)
    rows = _tile(tm, 352, 16)
    tn = 1024
    kern = functools.partial(_inproj_kernel, tm=tm, rows=rows, ctx_len=ctx_len)
    return pl.pallas_call(
        kern,
        grid=(bsz, l // tm, n // tn),
        in_specs=[pl.BlockSpec((1, tm, d), lambda b, i, j: (b, i, 0)),
                  pl.BlockSpec((None, 8, d), lambda b, i, j: (layer, 0, 0)),
                  pl.BlockSpec((None, 8, d), lambda b, i, j: (layer, 0, 1)),
                  pl.BlockSpec((1, d), lambda b, i, j: (0, 0)),
                  pl.BlockSpec((None, d, tn), lambda b, i, j: (layer, 0, j))],
        out_specs=pl.BlockSpec((1, tm, tn), lambda b, i, j: (b, i, j)),
        out_shape=jax.ShapeDtypeStruct((bsz, l, n), BF16),
        scratch_shapes=[pltpu.VMEM((tm, d), BF16)],
        compiler_params=_cparams(("parallel", "parallel", "arbitrary"), 48),
        name="inproj",
    )(x, mod, mod, gain.reshape(1, d), w_bf16)


def _norm_rope(t, gain, cos, sin_signed, lane_lo):
    var = jnp.mean(t * t, axis=-1, keepdims=True)
    y = t * lax.rsqrt(var + EPS) * gain
    swapped = jnp.where(lane_lo, pltpu.roll(y, HEAD_DIM - ROPE_PAIRS, 1), pltpu.roll(y, ROPE_PAIRS, 1))
    return y * cos + swapped * sin_signed


def _qkprep_kernel(q_ref, k_ref, v_ref, cos_ref, sin_ref, qg_ref, kg_ref, qo_ref, ko_ref, vo_ref):
    cos = cos_ref[...]
    sin = sin_ref[...]
    lane = lax.broadcasted_iota(jnp.int32, cos.shape, 1)
    lane_lo = (lane & ROPE_PAIRS) == 0
    qg = qg_ref[...]
    kg = kg_ref[...]
    for h in range(N_HEADS):
        sl = slice(h * HEAD_DIM, (h + 1) * HEAD_DIM)
        t = q_ref[0, :, sl].astype(F32)
        qo_ref[0, :, sl] = (_norm_rope(t, qg, cos, sin, lane_lo) * Q_SCALE).astype(BF16)
    k_tail = jnp.where(lane == 0, 1.0, 0.0).astype(BF16)
    v_tail = jnp.ones(cos.shape, BF16)
    for h in range(N_KV_HEADS):
        sl = slice(h * HEAD_DIM, (h + 1) * HEAD_DIM)
        lo = slice(2 * h * HEAD_DIM, (2 * h + 1) * HEAD_DIM)
        hi = slice((2 * h + 1) * HEAD_DIM, (2 * h + 2) * HEAD_DIM)
        t = k_ref[0, :, sl].astype(F32)
        ko_ref[0, :, lo] = _norm_rope(t, kg, cos, sin, lane_lo).astype(BF16)
        ko_ref[0, :, hi] = k_tail
        vo_ref[0, :, lo] = v_ref[0, :, sl]
        vo_ref[0, :, hi] = v_tail


def _qkprep_call(p, cos, sin_signed, q_gain, k_gain):
    bsz, l, _ = p.shape
    qw = N_HEADS * HEAD_DIM
    kw = N_KV_HEADS * HEAD_DIM
    tm = _tile(l, 768, 16)
    return pl.pallas_call(
        _qkprep_kernel,
        grid=(bsz, l // tm),
        in_specs=[pl.BlockSpec((1, tm, qw), lambda b, i: (b, i, 0)),
                  pl.BlockSpec((1, tm, kw), lambda b, i: (b, i, qw // kw)),
                  pl.BlockSpec((1, tm, kw), lambda b, i: (b, i, qw // kw + 1)),
                  pl.BlockSpec((tm, HEAD_DIM), lambda b, i: (i, 0)),
                  pl.BlockSpec((tm, HEAD_DIM), lambda b, i: (i, 0)),
                  pl.BlockSpec((1, HEAD_DIM), lambda b, i: (0, 0)),
                  pl.BlockSpec((1, HEAD_DIM), lambda b, i: (0, 0))],
        out_specs=[pl.BlockSpec((1, tm, qw), lambda b, i: (b, i, 0)),
                   pl.BlockSpec((1, tm, 2 * kw), lambda b, i: (b, i, 0)),
                   pl.BlockSpec((1, tm, 2 * kw), lambda b, i: (b, i, 0))],
        out_shape=[jax.ShapeDtypeStruct((bsz, l, qw), BF16),
                   jax.ShapeDtypeStruct((bsz, l, 2 * kw), BF16),
                   jax.ShapeDtypeStruct((bsz, l, 2 * kw), BF16)],
        compiler_params=_cparams(("parallel", "parallel"), 40),
        name="qkprep",
    )(p, p, p, cos, sin_signed, q_gain.reshape(1, HEAD_DIM), k_gain.reshape(1, HEAD_DIM))


def _attn_kernel(q_ref, k_ref, v_ref, qg_ref, kg_ref, o_ref, qs_scr, acc_scr, sa_scr, sb_scr, m_scr, l_scr,
                 *, tq, tkc, ctx_len, n_lat_chunks):
    qi = pl.program_id(2)
    q = q_ref[0]
    gq = jnp.max(jnp.abs(qg_ref[...]), axis=-1, keepdims=True)
    gk = jnp.max(jnp.abs(kg_ref[...]), axis=-1, keepdims=True)
    bound = (SHIFT_MARGIN * HEAD_DIM * Q_SCALE) * gq * gk
    lane = lax.broadcasted_iota(jnp.int32, (1, HEAD_DIM), 1)
    tail = jnp.where(lane == 0, -bound, 0.0).astype(BF16)
    for h in range(GROUP):
        qs_scr[h * tq:(h + 1) * tq, :HEAD_DIM] = q[:, h * HEAD_DIM:(h + 1) * HEAD_DIM]
        qs_scr[h * tq:(h + 1) * tq, HEAD_DIM:] = jnp.broadcast_to(tail, (tq, HEAD_DIM))
    n = jnp.where(qi < ctx_len // tq, 0, n_lat_chunks)

    def lat_start(c):
        return pl.multiple_of(ctx_len + c * tkc, math.gcd(ctx_len, tkc))

    def write(out):
        for h in range(GROUP):
            o_ref[0, :, h * HEAD_DIM:(h + 1) * HEAD_DIM] = out[h * tq:(h + 1) * tq, :].astype(BF16)

    def scores(start, size):
        return lax.dot_general(qs_scr[...], k_ref[0, pl.ds(start, size), :], (((1,), (1,)), ((), ())),
                               preferred_element_type=F32)

    def accumulate(s, start, size, first=False):
        p = jnp.exp2(s).astype(BF16)
        pv = jnp.dot(p, v_ref[0, pl.ds(start, size), :], preferred_element_type=F32)
        acc_scr[...] = pv if first else acc_scr[...] + pv

    @pl.when(n == 0)
    def _():
        accumulate(scores(0, ctx_len), 0, ctx_len, first=True)

    @pl.when(n > 0)
    def _():
        bounds = [(0, ctx_len + tkc)] + [(ctx_len + c * tkc, tkc) for c in range(1, n_lat_chunks)]
        bufs = (sa_scr, sb_scr)
        bufs[0][:, :bounds[0][1]] = scores(*bounds[0])
        for c, (start, size) in enumerate(bounds):
            if c + 1 < len(bounds):
                nxt_start, nxt_size = bounds[c + 1]
                bufs[(c + 1) % 2][:, :nxt_size] = scores(nxt_start, nxt_size)
            accumulate(bufs[c % 2][:, :size], start, size, first=(c == 0))

    acc = acc_scr[...]
    den = acc[:, HEAD_DIM:]
    write(acc[:, :HEAD_DIM] / den)

    @pl.when(jnp.logical_not(jnp.min(den) >= MIN_DENOM))
    def _():
        m_scr[...] = jnp.full(m_scr.shape, NEG_INF, F32)
        l_scr[...] = jnp.zeros(l_scr.shape, F32)
        acc_scr[...] = jnp.zeros(acc_scr.shape, F32)

        def online_chunk(start, size):
            k = k_ref[0, pl.ds(start, size), :HEAD_DIM]
            v = v_ref[0, pl.ds(start, size), :HEAD_DIM]
            s = lax.dot_general(qs_scr[:, :HEAD_DIM], k, (((1,), (1,)), ((), ())),
                                preferred_element_type=F32)
            m_old = m_scr[...]
            m_new = jnp.maximum(m_old, jnp.max(s, axis=-1, keepdims=True))
            alpha = jnp.exp2(m_old - m_new)
            p = jnp.exp2(s - m_new)
            l_scr[...] = alpha * l_scr[...] + jnp.sum(p, axis=-1, keepdims=True)
            acc_scr[:, :HEAD_DIM] = alpha * acc_scr[:, :HEAD_DIM] + jnp.dot(
                p.astype(BF16), v, preferred_element_type=F32)
            m_scr[...] = m_new

        online_chunk(0, ctx_len)

        def online_body(c, carry):
            online_chunk(lat_start(c), tkc)
            return carry

        lax.fori_loop(0, n, online_body, 0)
        write(acc_scr[:, :HEAD_DIM] / l_scr[...])


def _attn_call(qn, ke, ve, q_gain, k_gain, ctx_len):
    bsz, l, qw = qn.shape
    tq = 256
    assert ctx_len % tq == 0 and l % tq == 0
    t_lat = l - ctx_len
    assert t_lat % (ATTN_CHUNKS * LANES) == 0
    tkc = t_lat // ATTN_CHUNKS
    gw = GROUP * HEAD_DIM
    ew = 2 * HEAD_DIM
    kern = functools.partial(_attn_kernel, tq=tq, tkc=tkc, ctx_len=ctx_len, n_lat_chunks=t_lat // tkc)
    return pl.pallas_call(
        kern,
        grid=(bsz, N_KV_HEADS, l // tq),
        in_specs=[pl.BlockSpec((1, tq, gw), lambda b, g, i: (b, i, g)),
                  pl.BlockSpec((1, l, ew), lambda b, g, i: (b, 0, g)),
                  pl.BlockSpec((1, l, ew), lambda b, g, i: (b, 0, g)),
                  pl.BlockSpec((1, HEAD_DIM), lambda b, g, i: (0, 0)),
                  pl.BlockSpec((1, HEAD_DIM), lambda b, g, i: (0, 0))],
        out_specs=pl.BlockSpec((1, tq, gw), lambda b, g, i: (b, i, g)),
        out_shape=jax.ShapeDtypeStruct((bsz, l, qw), BF16),
        scratch_shapes=[pltpu.VMEM((GROUP * tq, ew), BF16),
                        pltpu.VMEM((GROUP * tq, ew), F32),
                        pltpu.VMEM((GROUP * tq, ctx_len + tkc), F32),
                        pltpu.VMEM((GROUP * tq, ctx_len + tkc), F32),
                        pltpu.VMEM((GROUP * tq, 1), F32),
                        pltpu.VMEM((GROUP * tq, 1), F32)],
        compiler_params=_cparams(("parallel", "parallel", "arbitrary"), 56),
        name="attn",
    )(qn, ke, ve, q_gain.reshape(1, HEAD_DIM), k_gain.reshape(1, HEAD_DIM))


def _segment_bounds(row, ctx_len, seq_len):
    is_ctx = row < ctx_len
    first = jnp.where(is_ctx, 0, ctx_len)
    last = jnp.where(is_ctx, ctx_len - 1, seq_len - 1)
    return first, last


def _shift_down(x, prev, k, local):
    y = pltpu.roll(x, k, 0)
    for r in range(k):
        y = jnp.where(local == r, prev[HALO - k + r:HALO - k + r + 1, :], y)
    return y


def _shift_up(x, nxt, k, local, tt):
    y = pltpu.roll(x, tt - k, 0)
    for r in range(k):
        y = jnp.where(local == tt - k + r, nxt[r:r + 1, :], y)
    return y


def _halo_shifts(x, prev, nxt):
    tt = x.shape[0]
    sub = lax.broadcasted_iota(jnp.int32, (SUBLANES, x.shape[1]), 0)

    def down(k):
        y = pltpu.roll(x, k, 0)
        head = y[:SUBLANES]
        for r in range(k):
            head = jnp.where(sub == r, prev[HALO - k + r:HALO - k + r + 1, :], head)
        return jnp.concatenate([head, y[SUBLANES:]], axis=0)

    def up(k):
        y = pltpu.roll(x, tt - k, 0)
        tail = y[tt - SUBLANES:]
        for r in range(k):
            tail = jnp.where(sub == SUBLANES - k + r, nxt[r:r + 1, :], tail)
        return jnp.concatenate([y[:tt - SUBLANES], tail], axis=0)

    return down, up


def _scan_tile(a, d, carry, reverse):
    n_groups = a.shape[0] // SUBLANES
    sub = lax.broadcasted_iota(jnp.int32, (SUBLANES, a.shape[1]), 0)
    groups = []
    for v in range(n_groups):
        av = a[v * SUBLANES:(v + 1) * SUBLANES]
        dv = d[v * SUBLANES:(v + 1) * SUBLANES]
        for k in (1, 2, 4):
            keep = sub < SUBLANES - k if reverse else sub >= k
            shift = SUBLANES - k if reverse else k
            a_n = jnp.where(keep, pltpu.roll(av, shift, 0), 1.0)
            d_n = jnp.where(keep, pltpu.roll(dv, shift, 0), 0.0)
            dv = av * d_n + dv
            av = av * a_n
        groups.append((av, dv))
    hs = [None] * n_groups
    for v in (reversed(range(n_groups)) if reverse else range(n_groups)):
        av, dv = groups[v]
        hv = dv + av * carry
        carry = hv[0:1] if reverse else hv[SUBLANES - 1:SUBLANES]
        hs[v] = hv
    return jnp.concatenate(hs, axis=0), carry


def _lru_kernel(*refs, reverse, tt, ctx_len, n_sub):
    if reverse:
        (x_ref, xp_ref, xn_ref, cw_ref, cb_ref, wa_ref, ba_ref, wx_ref, bx_ref, lam_ref,
         hf_ref, g_ref, o_ref, carry_scr) = refs
    else:
        (x_ref, xp_ref, xn_ref, cw_ref, cb_ref, wa_ref, ba_ref, wx_ref, bx_ref, lam_ref,
         o_ref, carry_scr) = refs
    s = pl.program_id(2)
    nt = pl.num_programs(2)
    ti = jnp.where(s == 0, 0, nt - s) if reverse else s

    @pl.when(s == 0)
    def _():
        carry_scr[...] = jnp.zeros(carry_scr.shape, F32)

    prev_ok = jnp.logical_and(ti != 0, ti * tt != ctx_len)
    next_ok = jnp.logical_and(ti != nt - 1, (ti + 1) * tt != ctx_len)
    x = x_ref[0].astype(F32)
    xp = jnp.where(prev_ok, xp_ref[0].astype(F32), 0.0)
    xn = jnp.where(next_ok, xn_ref[0].astype(F32), 0.0)
    down, up = _halo_shifts(x, xp, xn)
    cw = cw_ref[...]
    u = cb_ref[...] + cw[2:3, :] * x + cw[0:1, :] * down(2) + cw[1:2, :] * down(1) + cw[3:4, :] * up(1)
    nlam = -lam_ref[...]
    softplus = jnp.maximum(nlam, 0.0) + jnp.log1p(jnp.exp(-jnp.abs(nlam)))

    for j in range(n_sub):
        ls = slice(j * LRU_BLOCK, (j + 1) * LRU_BLOCK)
        uj = u[:, ls]
        ub = uj.astype(BF16)
        r = jax.nn.sigmoid(jnp.dot(ub, wa_ref[j], preferred_element_type=F32) + ba_ref[:, ls])
        i = jax.nn.sigmoid(jnp.dot(ub, wx_ref[j], preferred_element_type=F32) + bx_ref[:, ls])
        a = jnp.exp((-LRU_C) * r * softplus[:, ls])
        d = jnp.sqrt(1.0 - a * a) * (i * uj)
        h, carry = _scan_tile(a, d, carry_scr[:, ls], reverse)
        carry_scr[:, ls] = carry
        if reverse:
            o_ref[0, :, ls] = ((hf_ref[0, :, ls] + h) * _gelu(g_ref[0, :, ls].astype(F32))).astype(BF16)
        else:
            o_ref[0, :, ls] = h


def _lru_call(p, conv_w, conv_b, wa, ba, wx, bx, lam, ctx_len, reverse, h_fwd=None):
    bsz, l, _ = p.shape
    c = conv_w.shape[1]
    tt = 256
    assert ctx_len % tt == 0 and l % tt == 0
    nt = l // tt
    n_sub = 8
    cw = n_sub * LRU_BLOCK
    nb = c // cw
    x_col0 = (N_HEADS + 2 * N_KV_HEADS) * HEAD_DIM // cw
    g_col0 = x_col0 + nb
    hb = tt // HALO
    n_hblk = l // HALO

    def tile_of(s):
        return jnp.where(s == 0, 0, nt - s) if reverse else s

    x_map = lambda b, cb, s: (b, tile_of(s), x_col0 + cb)
    prev_map = lambda b, cb, s: (b, jnp.maximum(tile_of(s) * hb - 1, 0), x_col0 + cb)
    next_map = lambda b, cb, s: (b, jnp.minimum((tile_of(s) + 1) * hb, n_hblk - 1), x_col0 + cb)
    vec_map = lambda b, cb, s: (0, cb)
    w_spec = pl.BlockSpec((n_sub, LRU_BLOCK, LRU_BLOCK), lambda b, cb, s: (cb, 0, 0))
    in_specs = [pl.BlockSpec((1, tt, cw), x_map),
                pl.BlockSpec((1, HALO, cw), prev_map),
                pl.BlockSpec((1, HALO, cw), next_map),
                pl.BlockSpec((LRU_CONV, cw), vec_map),
                pl.BlockSpec((1, cw), vec_map),
                w_spec,
                pl.BlockSpec((1, cw), vec_map),
                w_spec,
                pl.BlockSpec((1, cw), vec_map),
                pl.BlockSpec((1, cw), vec_map)]
    args = [p, p, p, conv_w, conv_b.reshape(1, c), wa.astype(BF16), ba.reshape(1, c),
            wx.astype(BF16), bx.reshape(1, c), lam.reshape(1, c)]
    out_map = lambda b, cb, s: (b, tile_of(s), cb)
    if reverse:
        in_specs += [pl.BlockSpec((1, tt, cw), out_map),
                     pl.BlockSpec((1, tt, cw), lambda b, cb, s: (b, tile_of(s), g_col0 + cb))]
        args += [h_fwd, p]
        out_dtype = BF16
    else:
        out_dtype = F32
    kern = functools.partial(_lru_kernel, reverse=reverse, tt=tt, ctx_len=ctx_len, n_sub=n_sub)
    return pl.pallas_call(
        kern,
        grid=(bsz, nb, nt),
        in_specs=in_specs,
        out_specs=pl.BlockSpec((1, tt, cw), out_map),
        out_shape=jax.ShapeDtypeStruct((bsz, l, c), out_dtype),
        scratch_shapes=[pltpu.VMEM((1, cw), F32)],
        compiler_params=_cparams(("parallel", "parallel", "arbitrary"), 32),
        name="lru_rev" if reverse else "lru_fwd",
    )(*args)


def _sconv_kernel(bg_ref, cg_ref, u_ref, cgp_ref, up_ref, cgn_ref, un_ref, w_ref, b_ref, o_ref,
                  *, tt, ctx_len, seq_len):
    ti = pl.program_id(1)
    z = cg_ref[0].astype(F32) * u_ref[0].astype(F32)
    zp = cgp_ref[0].astype(F32) * up_ref[0].astype(F32)
    zn = cgn_ref[0].astype(F32) * un_ref[0].astype(F32)
    local = lax.broadcasted_iota(jnp.int32, (tt, 1), 0)
    row = ti * tt + local
    first, last = _segment_bounds(row, ctx_len, seq_len)
    w = w_ref[...]
    y = b_ref[...] + w[1:2, :] * z
    y = y + w[0:1, :] * jnp.where(row - 1 >= first, _shift_down(z, zp, 1, local), 0.0)
    y = y + w[2:3, :] * jnp.where(row + 1 <= last, _shift_up(z, zn, 1, local, tt), 0.0)
    o_ref[0] = (bg_ref[0].astype(F32) * y).astype(BF16)


def _sconv_call(p, w, b, ctx_len):
    bsz, l, _ = p.shape
    c = w.shape[1]
    tt = _tile(l, 768, 16)
    tc = 512
    ncb = c // tc
    b_col0 = (N_HEADS * HEAD_DIM + 2 * N_KV_HEADS * HEAD_DIM + 2 * c) // tc
    c_col0 = b_col0 + ncb
    u_col0 = c_col0 + ncb
    hb = tt // HALO
    n_hblk = l // HALO
    prev = lambda i: jnp.maximum(i * hb - 1, 0)
    nxt = lambda i: jnp.minimum((i + 1) * hb, n_hblk - 1)
    kern = functools.partial(_sconv_kernel, tt=tt, ctx_len=ctx_len, seq_len=l)
    return pl.pallas_call(
        kern,
        grid=(bsz, l // tt, ncb),
        in_specs=[pl.BlockSpec((1, tt, tc), lambda b_, i, j: (b_, i, b_col0 + j)),
                  pl.BlockSpec((1, tt, tc), lambda b_, i, j: (b_, i, c_col0 + j)),
                  pl.BlockSpec((1, tt, tc), lambda b_, i, j: (b_, i, u_col0 + j)),
                  pl.BlockSpec((1, HALO, tc), lambda b_, i, j: (b_, prev(i), c_col0 + j)),
                  pl.BlockSpec((1, HALO, tc), lambda b_, i, j: (b_, prev(i), u_col0 + j)),
                  pl.BlockSpec((1, HALO, tc), lambda b_, i, j: (b_, nxt(i), c_col0 + j)),
                  pl.BlockSpec((1, HALO, tc), lambda b_, i, j: (b_, nxt(i), u_col0 + j)),
                  pl.BlockSpec((SC_CONV, tc), lambda b_, i, j: (0, j)),
                  pl.BlockSpec((1, tc), lambda b_, i, j: (0, j))],
        out_specs=pl.BlockSpec((1, tt, tc), lambda b_, i, j: (b_, i, j)),
        out_shape=jax.ShapeDtypeStruct((bsz, l, c), BF16),
        compiler_params=_cparams(("parallel", "parallel", "parallel"), 32),
        name="sconv",
    )(p, p, p, p, p, p, p, w, b.reshape(1, c))


def _merge_kernel(xa_ref, xl_ref, xs_ref, ga_ref, gl_ref, gs_ref, wa_ref, wl_ref, ws_ref, o_ref):
    y = jax.nn.sigmoid(ga_ref[0].astype(F32)) * jnp.dot(xa_ref[0], wa_ref[...], preferred_element_type=F32)
    y = y + jax.nn.sigmoid(gl_ref[0].astype(F32)) * jnp.dot(xl_ref[0], wl_ref[...], preferred_element_type=F32)
    y = y + jax.nn.sigmoid(gs_ref[0].astype(F32)) * jnp.dot(xs_ref[0], ws_ref[...], preferred_element_type=F32)
    o_ref[0] = y.astype(BF16)


def _merge_call(x_att, x_lru, x_sc, p, layer, w_att, w_lru, w_sc):
    bsz, l, d = x_att.shape
    tm = _tile(l, 768, 16)
    tn = 512
    g_col0 = (p.shape[2] - 3 * d) // tn
    nj = d // tn
    xspec = pl.BlockSpec((1, tm, d), lambda b, i, j: (b, i, 0))
    wspec = pl.BlockSpec((None, d, tn), lambda b, i, j: (layer, 0, j))
    gspec = lambda k: pl.BlockSpec((1, tm, tn), lambda b, i, j: (b, i, g_col0 + k * nj + j))
    return pl.pallas_call(
        _merge_kernel,
        grid=(bsz, l // tm, nj),
        in_specs=[xspec, xspec, xspec, gspec(0), gspec(1), gspec(2), wspec, wspec, wspec],
        out_specs=pl.BlockSpec((1, tm, tn), lambda b, i, j: (b, i, j)),
        out_shape=jax.ShapeDtypeStruct((bsz, l, d), BF16),
        compiler_params=_cparams(("parallel", "parallel", "arbitrary"), 48),
        name="merge",
    )(x_att, x_lru, x_sc, p, p, p, w_att, w_lru, w_sc)


def _outproj_kernel(y_ref, w_ref, x_ref, gate_ref, shift_ref, scale_ref, g_ref, xo_ref, ht_ref,
                    *, tm, ctx_len):
    b = pl.program_id(0)
    i = pl.program_id(1)
    row = i * tm + lax.broadcasted_iota(jnp.int32, (tm, 1), 0)
    is_ctx = row < ctx_len
    acc = jnp.dot(y_ref[0], w_ref[...], preferred_element_type=F32)
    xn = x_ref[0] + _row_select(gate_ref, b, is_ctx) * acc
    xo_ref[0] = xn
    var = jnp.mean(xn * xn, axis=-1, keepdims=True)
    h = xn * lax.rsqrt(var + EPS) * g_ref[...]
    h = h * (1.0 + _row_select(scale_ref, b, is_ctx)) + _row_select(shift_ref, b, is_ctx)
    ht_ref[0] = h.T.astype(BF16)


def _outproj_call(y, w_out, x, mod, layer, gain, ctx_len):
    bsz, l, d = x.shape
    tm = _tile(l, 384, LANES)
    kern = functools.partial(_outproj_kernel, tm=tm, ctx_len=ctx_len)
    mspec = lambda k: pl.BlockSpec((None, 8, d), lambda b, i: (layer, 0, k))
    return pl.pallas_call(
        kern,
        grid=(bsz, l // tm),
        in_specs=[pl.BlockSpec((1, tm, d), lambda b, i: (b, i, 0)),
                  pl.BlockSpec((None, d, d), lambda b, i: (layer, 0, 0)),
                  pl.BlockSpec((1, tm, d), lambda b, i: (b, i, 0)),
                  mspec(2), mspec(3), mspec(4),
                  pl.BlockSpec((1, d), lambda b, i: (0, 0))],
        out_specs=[pl.BlockSpec((1, tm, d), lambda b, i: (b, i, 0)),
                   pl.BlockSpec((1, d, tm), lambda b, i: (b, 0, i))],
        out_shape=[jax.ShapeDtypeStruct((bsz, l, d), F32),
                   jax.ShapeDtypeStruct((bsz, d, l), BF16)],
        compiler_params=_cparams(("parallel", "parallel"), 48),
        name="outproj",
    )(y, w_out, x, mod, mod, mod, gain.reshape(1, d))


_CAND_ROWS = tuple(PEER_TOPK // (i + 1) for i in range(PEER_TOPK))


def _top16(s, v_scr):
    n = s.shape[0]
    key = lax.broadcasted_iota(jnp.int32, s.shape, 0)

    def body(r, carry):
        work, rank = carry
        m = jnp.max(work, axis=0, keepdims=True)
        v_scr[pl.ds(r, 1), :] = m
        first = jnp.min(jnp.where(work == m, key, n), axis=0, keepdims=True)
        sel = key == first
        return jnp.where(sel, NEG_INF, work), jnp.where(sel, jnp.asarray(r, F32), rank)

    _, rank = lax.fori_loop(0, PEER_TOPK, body, (s, jnp.full(s.shape, float(PEER_TOPK), F32)))
    return rank


MARK = -(2.0 ** 127)


def _top16_distinct(ss, v_scrs):
    def body(r, works):
        mark = jnp.asarray(r, F32) * (MARK / 32.0) + MARK
        out = []
        for work, v_scr in zip(works, v_scrs):
            m = jnp.max(work, axis=0, keepdims=True)
            v_scr[pl.ds(r, 1), :] = m
            out.append(jnp.where(work == m, mark, work))
        return tuple(out)

    works = lax.fori_loop(0, PEER_TOPK, body, tuple(ss), unroll=True)
    return [jnp.where(w <= MARK, (MARK - w) * (-32.0 / MARK), float(PEER_TOPK)) for w in works]


def _peer_topk_kernel(ht_ref, wq_ref, k1_ref, k2_ref, cnt_ref, e1_ref, rk_ref, e2_ref,
                      q_scr, v_scr, *, tm, n_par):
    half = PEER_NKEYS
    q_scr[...] = jnp.dot(wq_ref[...], ht_ref[0], preferred_element_type=F32).astype(BF16)
    sub = lax.broadcasted_iota(jnp.int32, (SUBLANES, LANES), 0)
    big = PEER_TOPK * PEER_TOPK

    def candidates(v1, v2):
        pieces, poss = [], []
        for i in range(SUBLANES):
            for j0 in range(0, _CAND_ROWS[i], SUBLANES):
                c = v1[i:i + 1, :] + v2[j0:j0 + SUBLANES, :]
                valid = sub + j0 < _CAND_ROWS[i]
                pieces.append(jnp.where(valid, c, NEG_INF))
                poss.append(jnp.where(valid, i * PEER_TOPK + j0 + sub, big))
        pieces.append(v1[SUBLANES:, :] + v2[0:1, :])
        poss.append((sub + SUBLANES) * PEER_TOPK)
        return pieces, poss

    def pick_exact(pieces, poss):
        def pick(_, carry2):
            cs, sels = carry2
            m = functools.reduce(jnp.maximum, cs)
            m = jnp.max(m, axis=0, keepdims=True)
            cand_pos = functools.reduce(jnp.minimum, [jnp.where(c == m, p_, big) for c, p_ in zip(cs, poss)])
            first = jnp.min(cand_pos, axis=0, keepdims=True)
            hit = [p_ == first for p_ in poss]
            cs = tuple(jnp.where(hh, NEG_INF, c) for hh, c in zip(hit, cs))
            sels = tuple(jnp.where(hh, 1.0, s_) for hh, s_ in zip(hit, sels))
            return cs, sels

        zeros = tuple(jnp.zeros((SUBLANES, LANES), F32) for _ in pieces)
        return lax.fori_loop(0, PEER_TOPK, pick, (tuple(pieces), zeros))[1]

    def pick_distinct(groups):
        npc = len(groups[0])

        def pick(_, cs):
            out = []
            for g in range(len(groups)):
                grp = cs[g * npc:(g + 1) * npc]
                m = jnp.max(functools.reduce(jnp.maximum, grp), axis=0, keepdims=True)
                out.extend(jnp.where(c == m, MARK, c) for c in grp)
            return tuple(out)

        marked = lax.fori_loop(0, PEER_TOPK, pick, tuple(c for grp in groups for c in grp), unroll=True)
        return [tuple(jnp.where(c == MARK, 1.0, 0.0) for c in marked[g * npc:(g + 1) * npc])
                for g in range(len(groups))]

    def compute(h, lss, exact):
        base = pl.multiple_of(h * 2 * half, 2 * half)
        n_g = len(lss)
        s1s = [jnp.dot(k1_ref[h], q_scr[pl.ds(base, half), ls], preferred_element_type=F32) for ls in lss]
        s2s = [jnp.dot(k2_ref[h], q_scr[pl.ds(base + half, half), ls], preferred_element_type=F32)
               for ls in lss]
        v1_refs = [v_scr.at[2 * g] for g in range(n_g)]
        v2_refs = [v_scr.at[2 * g + 1] for g in range(n_g)]
        if exact:
            rank1s = [_top16(s, r) for s, r in zip(s1s, v1_refs)]
            rank2s = [_top16(s, r) for s, r in zip(s2s, v2_refs)]
        else:
            ranks = _top16_distinct(tuple(s1s + s2s), tuple(v1_refs + v2_refs))
            rank1s, rank2s = ranks[:n_g], ranks[n_g:]
        v1s = [r[...] for r in v1_refs]
        v2s = [r[...] for r in v2_refs]
        cands = [candidates(v1, v2) for v1, v2 in zip(v1s, v2s)]
        if exact:
            all_sels = [pick_exact(pieces, poss) for pieces, poss in cands]
        else:
            all_sels = pick_distinct([pieces for pieces, _ in cands])
        most = None
        for g in range(n_g):
            most_g = finish(h, lss[g], s1s[g], s2s[g], rank1s[g], rank2s[g], v1s[g], v2s[g],
                            cands[g][0], all_sels[g])
            most = most_g if most is None else jnp.maximum(most, most_g)
        return most

    def finish(h, ls, s1, s2, rank1, rank2, v1, v2, orig, sels):
        top = v1[0:1, :] + v2[0:1, :]
        zsum = functools.reduce(
            lambda a_, b_: a_ + b_,
            [jnp.where(s_ > 0.0, jnp.exp(o - top), 0.0) for s_, o in zip(sels, orig)])
        zinv = 1.0 / jnp.sum(zsum, axis=0, keepdims=True)

        counts = []
        pi = 0
        for i in range(SUBLANES):
            c = None
            for j0 in range(0, _CAND_ROWS[i], SUBLANES):
                part = jnp.sum(sels[pi], axis=0, keepdims=True)
                c = part if c is None else c + part
                pi += 1
            counts.append(c)
        tail = sels[pi]
        for i in range(SUBLANES, PEER_TOPK):
            counts.append(tail[i - SUBLANES:i - SUBLANES + 1, :])
        cnt = jnp.zeros(rank1.shape, F32)
        for i in range(PEER_TOPK):
            cnt = jnp.where(rank1 == float(i), counts[i], cnt)

        cnt_ref[0, h, :, ls] = cnt
        e1_ref[0, h, :, ls] = jnp.exp(s1 - v1[0:1, :]) * zinv
        rk_ref[0, h, :, ls] = rank2.astype(BF16)
        e2_ref[0, h, :, ls] = jnp.exp(s2 - v2[0:1, :]).astype(BF16)
        ranked = jnp.where(rank1 < PEER_TOPK, 1.0, 0.0)
        ranked = jnp.maximum(jnp.sum(ranked, axis=0, keepdims=True),
                             jnp.sum(jnp.where(rank2 < PEER_TOPK, 1.0, 0.0), axis=0, keepdims=True))
        picked = jnp.sum(functools.reduce(lambda a_, b_: a_ + b_, sels), axis=0, keepdims=True)
        return jnp.max(jnp.maximum(ranked, picked))

    n_trips = tm // (n_par * LANES)

    def head_lane_groups(idx, carry):
        h = idx // n_trips
        first = (idx % n_trips) * n_par
        lss = [pl.ds(pl.multiple_of((first + g) * LANES, LANES), LANES) for g in range(n_par)]
        most = compute(h, lss, exact=False)

        @pl.when(most > PEER_TOPK)
        def _():
            compute(h, lss, exact=True)

        return carry

    lax.fori_loop(0, PEER_HEADS * n_trips, head_lane_groups, 0)


def _peer_topk_call(ht, layer, wq_t, k1, k2):
    bsz, d, l = ht.shape
    qd = wq_t.shape[1]
    n_par = 6
    tm = _tile(l, 768, n_par * LANES)
    kern = functools.partial(_peer_topk_kernel, tm=tm, n_par=n_par)
    ospec = pl.BlockSpec((1, PEER_HEADS, PEER_NKEYS, tm), lambda b, i: (b, 0, 0, i))
    oshape = jax.ShapeDtypeStruct((bsz, PEER_HEADS, PEER_NKEYS, l), F32)
    kspec = pl.BlockSpec((None, PEER_HEADS, PEER_NKEYS, PEER_NKEYS), lambda b, i: (layer, 0, 0, 0))
    return pl.pallas_call(
        kern,
        grid=(bsz, l // tm),
        in_specs=[pl.BlockSpec((1, d, tm), lambda b, i: (b, 0, i)),
                  pl.BlockSpec((None, qd, d), lambda b, i: (layer, 0, 0)),
                  kspec, kspec],
        out_specs=[ospec, ospec, ospec, ospec],
        out_shape=[oshape, oshape, jax.ShapeDtypeStruct(oshape.shape, BF16),
                   jax.ShapeDtypeStruct(oshape.shape, BF16)],
        scratch_shapes=[pltpu.VMEM((qd, tm), BF16),
                        pltpu.VMEM((2 * n_par, PEER_TOPK, LANES), F32)],
        compiler_params=_cparams(("parallel", "parallel"), 56),
        name="peer_topk",
    )(ht, wq_t, k1, k2)


def _peer_dense_kernel(ht_ref, u_ref, vt_ref, cnt_ref, e1_ref, rk_ref, e2_ref, o_ref, wz_scr, *, n_sub):
    e = pl.program_id(2)

    @pl.when(e == 0)
    def _():
        o_ref[...] = jnp.zeros(o_ref.shape, F32)

    ht = ht_ref[0]
    per = 4
    n_split = n_sub // per
    for part in range(n_split):
        s = jnp.dot(u_ref[part * per * PEER_NKEYS:(part + 1) * per * PEER_NKEYS, :], ht,
                    preferred_element_type=F32)
        for a in range(part * per, (part + 1) * per):
            rows = slice(a * PEER_NKEYS, (a + 1) * PEER_NKEYS)
            local = slice((a - part * per) * PEER_NKEYS, (a - part * per + 1) * PEER_NKEYS)
            w = None
            for h in range(PEER_HEADS):
                cnt = cnt_ref[0, h, a:a + 1, :].astype(BF16)
                e1 = e1_ref[0, h, a:a + 1, :].astype(BF16)
                term = jnp.where(rk_ref[0, h] < cnt, e2_ref[0, h] * e1, 0.0)
                w = term if w is None else w + term
            wz_scr[rows, :] = w * _gelu(s[local, :].astype(BF16))
    o_ref[0] += jnp.dot(vt_ref[...], wz_scr[...], preferred_element_type=F32)


def _peer_dense_call(ht, layer, u_bf16, vt_bf16, cnt, e1n, rank2, e2):
    bsz, d, l = ht.shape
    n_exp = u_bf16.shape[1]
    tm = _tile(l, 768, LANES)
    n_sub = SUBLANES
    te = n_sub * PEER_NKEYS
    kern = functools.partial(_peer_dense_kernel, n_sub=n_sub)
    aspec = pl.BlockSpec((1, PEER_HEADS, n_sub, tm), lambda b, i, e: (b, 0, e, i))
    fspec = pl.BlockSpec((1, PEER_HEADS, PEER_NKEYS, tm), lambda b, i, e: (b, 0, 0, i))
    return pl.pallas_call(
        kern,
        grid=(bsz, l // tm, n_exp // te),
        in_specs=[pl.BlockSpec((1, d, tm), lambda b, i, e: (b, 0, i)),
                  pl.BlockSpec((None, te, d), lambda b, i, e: (layer, e, 0)),
                  pl.BlockSpec((None, d, te), lambda b, i, e: (layer, 0, e)),
                  aspec, aspec, fspec, fspec],
        out_specs=pl.BlockSpec((1, d, tm), lambda b, i, e: (b, 0, i)),
        out_shape=jax.ShapeDtypeStruct((bsz, d, l), F32),
        scratch_shapes=[pltpu.VMEM((te, tm), BF16)],
        compiler_params=_cparams(("parallel", "parallel", "arbitrary"), 58),
        name="peer_dense",
    )(ht, u_bf16, vt_bf16, cnt, e1n, rank2, e2)


def _resid_kernel(x_ref, yt_ref, gate_ref, o_ref, *, tm, ctx_len, first_tile):
    b = pl.program_id(0)
    i = pl.program_id(1) + first_tile
    row = i * tm + lax.broadcasted_iota(jnp.int32, (tm, 1), 0)
    o_ref[0] = x_ref[0] + _row_select(gate_ref, b, row < ctx_len) * yt_ref[0].T


def _resid_call(x, yt, mod, layer, ctx_len, latent_only):
    bsz, l, d = x.shape
    tm = _tile(math.gcd(l, ctx_len), 384, LANES)
    first_tile = ctx_len // tm if latent_only else 0
    n_tiles = l // tm - first_tile
    kern = functools.partial(_resid_kernel, tm=tm, ctx_len=ctx_len, first_tile=first_tile)
    return pl.pallas_call(
        kern,
        grid=(bsz, n_tiles),
        in_specs=[pl.BlockSpec((1, tm, d), lambda b, i: (b, i + first_tile, 0)),
                  pl.BlockSpec((1, d, tm), lambda b, i: (b, 0, i + first_tile)),
                  pl.BlockSpec((None, 8, d), lambda b, i: (layer, 0, 5))],
        out_specs=pl.BlockSpec((1, tm, d), lambda b, i: (b, i, 0)),
        out_shape=jax.ShapeDtypeStruct((bsz, n_tiles * tm, d), F32),
        compiler_params=_cparams(("parallel", "parallel"), 40),
        name="resid",
    )(x, yt, mod)


def _rope_tables(ctx_len, t_lat):
    rows = t_lat // GRID_W
    row = jnp.repeat(jnp.arange(rows, dtype=F32), GRID_W)
    col = jnp.tile(jnp.arange(GRID_W, dtype=F32), rows)
    inv = ROPE_THETA ** (-jnp.arange(ROPE_PAIRS, dtype=F32) / ROPE_PAIRS)
    ang = jnp.concatenate([row[:, None] * inv] * 2 + [col[:, None] * inv] * 2, axis=1)
    ang = jnp.concatenate([jnp.zeros((ctx_len, HEAD_DIM), F32), ang], axis=0)
    sign = jnp.where((jnp.arange(HEAD_DIM) & ROPE_PAIRS) == 0, -1.0, 1.0).astype(F32)
    return jnp.cos(ang), jnp.sin(ang) * sign


def kernel(x, c, ctx, c_ctx, w_mod, b_mod, norm_mix, norm_ffn, w_in, q_norm, k_norm, lru_conv_w, lru_conv_b, lru_wa, lru_ba, lru_wx, lru_bx, lru_lambda, sc_conv_w, sc_conv_b, w_o_attn, w_o_lru, w_o_sc, w_out, peer_wq, peer_k1, peer_k2, peer_u, peer_v):
    bsz, t_lat, d = x.shape
    ctx_len = ctx.shape[1]
    depth = w_mod.shape[0]
    assert bsz == 2, "modulation rows are laid out as [latent 0, latent 1, context]"

    xs = jnp.concatenate([ctx, x], axis=1)
    s8 = jnp.concatenate([c, c_ctx[None, :], jnp.zeros((8 - bsz - 1, d), F32)], axis=0)
    mod = _mod_call(s8, w_mod, b_mod)
    cos, sin_signed = _rope_tables(ctx_len, t_lat)

    w_in_b = w_in.astype(BF16)
    w_att_b, w_lru_b, w_sc_b = w_o_attn.astype(BF16), w_o_lru.astype(BF16), w_o_sc.astype(BF16)
    w_out_b = w_out.astype(BF16)
    wq_t = jnp.swapaxes(peer_wq, 1, 2).astype(BF16)
    k1_b, k2_b = peer_k1.astype(BF16), peer_k2.astype(BF16)
    u_b = peer_u.astype(BF16)
    vt_b = jnp.swapaxes(peer_v, 1, 2).astype(BF16)

    for l in range(depth):
        p = _inproj_call(xs, mod, l, norm_mix[l], w_in_b, ctx_len)
        qn, ke, ve = _qkprep_call(p, cos, sin_signed, q_norm[l], k_norm[l])
        x_att = _attn_call(qn, ke, ve, q_norm[l], k_norm[l], ctx_len)
        lru_args = (lru_conv_w[l], lru_conv_b[l])
        h_fwd = _lru_call(p, *lru_args, lru_wa[l, 0], lru_ba[l, 0], lru_wx[l, 0], lru_bx[l, 0],
                          lru_lambda[l, 0], ctx_len, reverse=False)
        x_lru = _lru_call(p, *lru_args, lru_wa[l, 1], lru_ba[l, 1], lru_wx[l, 1], lru_bx[l, 1],
                          lru_lambda[l, 1], ctx_len, reverse=True, h_fwd=h_fwd)
        x_sc = _sconv_call(p, sc_conv_w[l], sc_conv_b[l], ctx_len)
        y = _merge_call(x_att, x_lru, x_sc, p, l, w_att_b, w_lru_b, w_sc_b)
        xs, ht = _outproj_call(y, w_out_b, xs, mod, l, norm_ffn[l], ctx_len)
        cnt, e1n, rank2, e2 = _peer_topk_call(ht, l, wq_t, k1_b, k2_b)
        yt = _peer_dense_call(ht, l, u_b, vt_b, cnt, e1n, rank2, e2)
        xs = _resid_call(xs, yt, mod, l, ctx_len, latent_only=(l == depth - 1))
    return xs
```

```python
import functools
import math

import jax
import jax.numpy as jnp
from jax import lax
from jax.experimental import pallas as pl
from jax.experimental.pallas import tpu as pltpu

F32 = jnp.float32
BF16 = jnp.bfloat16

GRID_W = 64
EPS = 1e-6

N_HEADS = 16
N_KV_HEADS = 4
HEAD_DIM = 128
GROUP = N_HEADS // N_KV_HEADS
ROPE_PAIRS = HEAD_DIM // 4
ROPE_THETA = 10000.0

LRU_BLOCK = 128
LRU_CONV = 4
LRU_C = 8.0
SC_CONV = 3

PEER_HEADS = 8
PEER_NKEYS = 128
PEER_TOPK = 16

LANES = 128
SUBLANES = 8
HALO = SUBLANES
NEG_INF = float("-inf")
LOG2E = 1.4426950408889634
GELU_C = math.sqrt(2.0 / math.pi)
Q_SCALE = HEAD_DIM ** -0.5 * LOG2E
SHIFT_MARGIN = 1.02
MIN_DENOM = 2.0 ** -100
ATTN_CHUNKS = 4


def _tile(n, target, mult):
    best = None
    for t in range(mult, min(n, target) + 1, mult):
        if n % t == 0:
            best = t
    assert best is not None, (n, target, mult)
    return best


def _cparams(sem, vmem_mib):
    return pltpu.CompilerParams(dimension_semantics=sem, vmem_limit_bytes=vmem_mib << 20)


def _gelu(x):
    return 0.5 * x * (1.0 + jnp.tanh(GELU_C * (x + 0.044715 * (x * x * x))))


def _row_select(mod_ref, b, is_ctx):
    return jnp.where(is_ctx, mod_ref[2:3, :], mod_ref[pl.ds(b, 1), :])


def _mod_kernel(s_ref, w_ref, b_ref, o_ref):
    s = s_ref[...]
    s = s * jax.nn.sigmoid(s)
    o_ref[0] = jnp.dot(s, w_ref[0], preferred_element_type=F32,
                       precision=lax.Precision.HIGHEST) + b_ref[0]


def _mod_call(s8, w_mod, b_mod):
    depth, d, n = w_mod.shape
    tn = _tile(n, 1024, LANES)
    return pl.pallas_call(
        _mod_kernel,
        grid=(depth, n // tn),
        in_specs=[pl.BlockSpec((8, d), lambda l, j: (0, 0)),
                  pl.BlockSpec((1, d, tn), lambda l, j: (l, 0, j)),
                  pl.BlockSpec((1, 1, tn), lambda l, j: (l, 0, j))],
        out_specs=pl.BlockSpec((1, 8, tn), lambda l, j: (l, 0, j)),
        out_shape=jax.ShapeDtypeStruct((depth, 8, n), F32),
        compiler_params=_cparams(("parallel", "parallel"), 32),
        name="mod",
    )(s8, w_mod, b_mod.reshape(depth, 1, n))


def _inproj_kernel(x_ref, shift_ref, scale_ref, g_ref, w_ref, o_ref, h_scr, *, tm, rows, ctx_len):
    b = pl.program_id(0)
    i = pl.program_id(1)

    @pl.when(pl.program_id(2) == 0)
    def _():
        def norm_rows(c, carry):
            r0 = pl.multiple_of(c * rows, rows)
            x = x_ref[0, pl.ds(r0, rows), :]
            var = jnp.mean(x * x, axis=-1, keepdims=True)
            y = x * lax.rsqrt(var + EPS) * g_ref[...]
            row = i * tm + r0 + lax.broadcasted_iota(jnp.int32, (rows, 1), 0)
            is_ctx = row < ctx_len
            sh = _row_select(shift_ref, b, is_ctx)
            sc = _row_select(scale_ref, b, is_ctx)
            h_scr[pl.ds(r0, rows), :] = (y * (1.0 + sc) + sh).astype(BF16)
            return carry

        lax.fori_loop(0, tm // rows, norm_rows, 0)

    o_ref[0] = jnp.dot(h_scr[...], w_ref[...], preferred_element_type=F32).astype(BF16)


def _inproj_call(x, mod, layer, gain, w_bf16, ctx_len):
    bsz, l, d = x.shape
    n = w_bf16.shape[2]
    tm = _tile(l, 1056, 16)
    rows = _tile(tm, 352, 16)
    tn = 1024
    kern = functools.partial(_inproj_kernel, tm=tm, rows=rows, ctx_len=ctx_len)
    return pl.pallas_call(
        kern,
        grid=(bsz, l // tm, n // tn),
        in_specs=[pl.BlockSpec((1, tm, d), lambda b, i, j: (b, i, 0)),
                  pl.BlockSpec((None, 8, d), lambda b, i, j: (layer, 0, 0)),
                  pl.BlockSpec((None, 8, d), lambda b, i, j: (layer, 0, 1)),
                  pl.BlockSpec((1, d), lambda b, i, j: (0, 0)),
                  pl.BlockSpec((None, d, tn), lambda b, i, j: (layer, 0, j))],
        out_specs=pl.BlockSpec((1, tm, tn), lambda b, i, j: (b, i, j)),
        out_shape=jax.ShapeDtypeStruct((bsz, l, n), BF16),
        scratch_shapes=[pltpu.VMEM((tm, d), BF16)],
        compiler_params=_cparams(("parallel", "parallel", "arbitrary"), 48),
        name="inproj",
    )(x, mod, mod, gain.reshape(1, d), w_bf16)


def _norm_rope(t, gain, cos, sin_signed, lane_lo):
    var = jnp.mean(t * t, axis=-1, keepdims=True)
    y = t * lax.rsqrt(var + EPS) * gain
    swapped = jnp.where(lane_lo, pltpu.roll(y, HEAD_DIM - ROPE_PAIRS, 1), pltpu.roll(y, ROPE_PAIRS, 1))
    return y * cos + swapped * sin_signed


def _qkprep_kernel(q_ref, k_ref, v_ref, cos_ref, sin_ref, qg_ref, kg_ref, qo_ref, ko_ref, vo_ref):
    cos = cos_ref[...]
    sin = sin_ref[...]
    lane = lax.broadcasted_iota(jnp.int32, cos.shape, 1)
    lane_lo = (lane & ROPE_PAIRS) == 0
    qg = qg_ref[...]
    kg = kg_ref[...]
    for h in range(N_HEADS):
        sl = slice(h * HEAD_DIM, (h + 1) * HEAD_DIM)
        t = q_ref[0, :, sl].astype(F32)
        qo_ref[0, :, sl] = (_norm_rope(t, qg, cos, sin, lane_lo) * Q_SCALE).astype(BF16)
    k_tail = jnp.where(lane == 0, 1.0, 0.0).astype(BF16)
    v_tail = jnp.ones(cos.shape, BF16)
    for h in range(N_KV_HEADS):
        sl = slice(h * HEAD_DIM, (h + 1) * HEAD_DIM)
        lo = slice(2 * h * HEAD_DIM, (2 * h + 1) * HEAD_DIM)
        hi = slice((2 * h + 1) * HEAD_DIM, (2 * h + 2) * HEAD_DIM)
        t = k_ref[0, :, sl].astype(F32)
        ko_ref[0, :, lo] = _norm_rope(t, kg, cos, sin, lane_lo).astype(BF16)
        ko_ref[0, :, hi] = k_tail
        vo_ref[0, :, lo] = v_ref[0, :, sl]
        vo_ref[0, :, hi] = v_tail


def _qkprep_call(p, cos, sin_signed, q_gain, k_gain):
    bsz, l, _ = p.shape
    qw = N_HEADS * HEAD_DIM
    kw = N_KV_HEADS * HEAD_DIM
    tm = _tile(l, 768, 16)
    return pl.pallas_call(
        _qkprep_kernel,
        grid=(bsz, l // tm),
        in_specs=[pl.BlockSpec((1, tm, qw), lambda b, i: (b, i, 0)),
                  pl.BlockSpec((1, tm, kw), lambda b, i: (b, i, qw // kw)),
                  pl.BlockSpec((1, tm, kw), lambda b, i: (b, i, qw // kw + 1)),
                  pl.BlockSpec((tm, HEAD_DIM), lambda b, i: (i, 0)),
                  pl.BlockSpec((tm, HEAD_DIM), lambda b, i: (i, 0)),
                  pl.BlockSpec((1, HEAD_DIM), lambda b, i: (0, 0)),
                  pl.BlockSpec((1, HEAD_DIM), lambda b, i: (0, 0))],
        out_specs=[pl.BlockSpec((1, tm, qw), lambda b, i: (b, i, 0)),
                   pl.BlockSpec((1, tm, 2 * kw), lambda b, i: (b, i, 0)),
                   pl.BlockSpec((1, tm, 2 * kw), lambda b, i: (b, i, 0))],
        out_shape=[jax.ShapeDtypeStruct((bsz, l, qw), BF16),
                   jax.ShapeDtypeStruct((bsz, l, 2 * kw), BF16),
                   jax.ShapeDtypeStruct((bsz, l, 2 * kw), BF16)],
        compiler_params=_cparams(("parallel", "parallel"), 40),
        name="qkprep",
    )(p, p, p, cos, sin_signed, q_gain.reshape(1, HEAD_DIM), k_gain.reshape(1, HEAD_DIM))


def _attn_kernel(q_ref, k_ref, v_ref, qg_ref, kg_ref, o_ref, qs_scr, acc_scr, sa_scr, sb_scr, m_scr, l_scr,
                 *, tq, tkc, ctx_len, n_lat_chunks):
    qi = pl.program_id(2)
    q = q_ref[0]
    gq = jnp.max(jnp.abs(qg_ref[...]), axis=-1, keepdims=True)
    gk = jnp.max(jnp.abs(kg_ref[...]), axis=-1, keepdims=True)
    bound = (SHIFT_MARGIN * HEAD_DIM * Q_SCALE) * gq * gk
    lane = lax.broadcasted_iota(jnp.int32, (1, HEAD_DIM), 1)
    tail = jnp.where(lane == 0, -bound, 0.0).astype(BF16)
    for h in range(GROUP):
        qs_scr[h * tq:(h + 1) * tq, :HEAD_DIM] = q[:, h * HEAD_DIM:(h + 1) * HEAD_DIM]
        qs_scr[h * tq:(h + 1) * tq, HEAD_DIM:] = jnp.broadcast_to(tail, (tq, HEAD_DIM))
    n = jnp.where(qi < ctx_len // tq, 0, n_lat_chunks)

    def lat_start(c):
        return pl.multiple_of(ctx_len + c * tkc, math.gcd(ctx_len, tkc))

    def write(out):
        for h in range(GROUP):
            o_ref[0, :, h * HEAD_DIM:(h + 1) * HEAD_DIM] = out[h * tq:(h + 1) * tq, :].astype(BF16)

    def scores(start, size):
        return lax.dot_general(qs_scr[...], k_ref[0, pl.ds(start, size), :], (((1,), (1,)), ((), ())),
                               preferred_element_type=F32)

    def accumulate(s, start, size, first=False):
        p = jnp.exp2(s).astype(BF16)
        pv = jnp.dot(p, v_ref[0, pl.ds(start, size), :], preferred_element_type=F32)
        acc_scr[...] = pv if first else acc_scr[...] + pv

    @pl.when(n == 0)
    def _():
        accumulate(scores(0, ctx_len), 0, ctx_len, first=True)

    @pl.when(n > 0)
    def _():
        bounds = [(0, ctx_len + tkc)] + [(ctx_len + c * tkc, tkc) for c in range(1, n_lat_chunks)]
        bufs = (sa_scr, sb_scr)
        bufs[0][:, :bounds[0][1]] = scores(*bounds[0])
        for c, (start, size) in enumerate(bounds):
            if c + 1 < len(bounds):
                nxt_start, nxt_size = bounds[c + 1]
                bufs[(c + 1) % 2][:, :nxt_size] = scores(nxt_start, nxt_size)
            accumulate(bufs[c % 2][:, :size], start, size, first=(c == 0))

    acc = acc_scr[...]
    den = acc[:, HEAD_DIM:]
    write(acc[:, :HEAD_DIM] / den)

    @pl.when(jnp.logical_not(jnp.min(den) >= MIN_DENOM))
    def _():
        m_scr[...] = jnp.full(m_scr.shape, NEG_INF, F32)
        l_scr[...] = jnp.zeros(l_scr.shape, F32)
        acc_scr[...] = jnp.zeros(acc_scr.shape, F32)

        def online_chunk(start, size):
            k = k_ref[0, pl.ds(start, size), :HEAD_DIM]
            v = v_ref[0, pl.ds(start, size), :HEAD_DIM]
            s = lax.dot_general(qs_scr[:, :HEAD_DIM], k, (((1,), (1,)), ((), ())),
                                preferred_element_type=F32)
            m_old = m_scr[...]
            m_new = jnp.maximum(m_old, jnp.max(s, axis=-1, keepdims=True))
            alpha = jnp.exp2(m_old - m_new)
            p = jnp.exp2(s - m_new)
            l_scr[...] = alpha * l_scr[...] + jnp.sum(p, axis=-1, keepdims=True)
            acc_scr[:, :HEAD_DIM] = alpha * acc_scr[:, :HEAD_DIM] + jnp.dot(
                p.astype(BF16), v, preferred_element_type=F32)
            m_scr[...] = m_new

        online_chunk(0, ctx_len)

        def online_body(c, carry):
            online_chunk(lat_start(c), tkc)
            return carry

        lax.fori_loop(0, n, online_body, 0)
        write(acc_scr[:, :HEAD_DIM] / l_scr[...])


def _attn_call(qn, ke, ve, q_gain, k_gain, ctx_len):
    bsz, l, qw = qn.shape
    tq = 256
    assert ctx_len % tq == 0 and l % tq == 0
    t_lat = l - ctx_len
    assert t_lat % (ATTN_CHUNKS * LANES) == 0
    tkc = t_lat // ATTN_CHUNKS
    gw = GROUP * HEAD_DIM
    ew = 2 * HEAD_DIM
    kern = functools.partial(_attn_kernel, tq=tq, tkc=tkc, ctx_len=ctx_len, n_lat_chunks=t_lat // tkc)
    return pl.pallas_call(
        kern,
        grid=(bsz, N_KV_HEADS, l // tq),
        in_specs=[pl.BlockSpec((1, tq, gw), lambda b, g, i: (b, i, g)),
                  pl.BlockSpec((1, l, ew), lambda b, g, i: (b, 0, g)),
                  pl.BlockSpec((1, l, ew), lambda b, g, i: (b, 0, g)),
                  pl.BlockSpec((1, HEAD_DIM), lambda b, g, i: (0, 0)),
                  pl.BlockSpec((1, HEAD_DIM), lambda b, g, i: (0, 0))],
        out_specs=pl.BlockSpec((1, tq, gw), lambda b, g, i: (b, i, g)),
        out_shape=jax.ShapeDtypeStruct((bsz, l, qw), BF16),
        scratch_shapes=[pltpu.VMEM((GROUP * tq, ew), BF16),
                        pltpu.VMEM((GROUP * tq, ew), F32),
                        pltpu.VMEM((GROUP * tq, ctx_len + tkc), F32),
                        pltpu.VMEM((GROUP * tq, ctx_len + tkc), F32),
                        pltpu.VMEM((GROUP * tq, 1), F32),
                        pltpu.VMEM((GROUP * tq, 1), F32)],
        compiler_params=_cparams(("parallel", "parallel", "arbitrary"), 56),
        name="attn",
    )(qn, ke, ve, q_gain.reshape(1, HEAD_DIM), k_gain.reshape(1, HEAD_DIM))


def _segment_bounds(row, ctx_len, seq_len):
    is_ctx = row < ctx_len
    first = jnp.where(is_ctx, 0, ctx_len)
    last = jnp.where(is_ctx, ctx_len - 1, seq_len - 1)
    return first, last


def _shift_down(x, prev, k, local):
    y = pltpu.roll(x, k, 0)
    for r in range(k):
        y = jnp.where(local == r, prev[HALO - k + r:HALO - k + r + 1, :], y)
    return y


def _shift_up(x, nxt, k, local, tt):
    y = pltpu.roll(x, tt - k, 0)
    for r in range(k):
        y = jnp.where(local == tt - k + r, nxt[r:r + 1, :], y)
    return y


def _halo_shifts(x, prev, nxt):
    tt = x.shape[0]
    sub = lax.broadcasted_iota(jnp.int32, (SUBLANES, x.shape[1]), 0)

    def down(k):
        y = pltpu.roll(x, k, 0)
        head = y[:SUBLANES]
        for r in range(k):
            head = jnp.where(sub == r, prev[HALO - k + r:HALO - k + r + 1, :], head)
        return jnp.concatenate([head, y[SUBLANES:]], axis=0)

    def up(k):
        y = pltpu.roll(x, tt - k, 0)
        tail = y[tt - SUBLANES:]
        for r in range(k):
            tail = jnp.where(sub == SUBLANES - k + r, nxt[r:r + 1, :], tail)
        return jnp.concatenate([y[:tt - SUBLANES], tail], axis=0)

    return down, up


def _scan_tile(a, d, carry, reverse):
    n_groups = a.shape[0] // SUBLANES
    sub = lax.broadcasted_iota(jnp.int32, (SUBLANES, a.shape[1]), 0)
    groups = []
    for v in range(n_groups):
        av = a[v * SUBLANES:(v + 1) * SUBLANES]
        dv = d[v * SUBLANES:(v + 1) * SUBLANES]
        for k in (1, 2, 4):
            keep = sub < SUBLANES - k if reverse else sub >= k
            shift = SUBLANES - k if reverse else k
            a_n = jnp.where(keep, pltpu.roll(av, shift, 0), 1.0)
            d_n = jnp.where(keep, pltpu.roll(dv, shift, 0), 0.0)
            dv = av * d_n + dv
            av = av * a_n
        groups.append((av, dv))
    hs = [None] * n_groups
    for v in (reversed(range(n_groups)) if reverse else range(n_groups)):
        av, dv = groups[v]
        hv = dv + av * carry
        carry = hv[0:1] if reverse else hv[SUBLANES - 1:SUBLANES]
        hs[v] = hv
    return jnp.concatenate(hs, axis=0), carry


def _lru_kernel(*refs, reverse, tt, ctx_len, n_sub):
    if reverse:
        (x_ref, xp_ref, xn_ref, cw_ref, cb_ref, wa_ref, ba_ref, wx_ref, bx_ref, lam_ref,
         hf_ref, g_ref, bg_ref, cg_ref, u_ref, cgp_ref, up_ref, cgn_ref, un_ref, sw_ref, sb_ref,
         o_ref, sc_ref, carry_scr) = refs
    else:
        (x_ref, xp_ref, xn_ref, cw_ref, cb_ref, wa_ref, ba_ref, wx_ref, bx_ref, lam_ref,
         o_ref, carry_scr) = refs
    s = pl.program_id(2)
    nt = pl.num_programs(2)
    ti = jnp.where(s == 0, 0, nt - s) if reverse else s

    @pl.when(s == 0)
    def _():
        carry_scr[...] = jnp.zeros(carry_scr.shape, F32)

    prev_ok = jnp.logical_and(ti != 0, ti * tt != ctx_len)
    next_ok = jnp.logical_and(ti != nt - 1, (ti + 1) * tt != ctx_len)
    x = x_ref[0].astype(F32)
    xp = jnp.where(prev_ok, xp_ref[0].astype(F32), 0.0)
    xn = jnp.where(next_ok, xn_ref[0].astype(F32), 0.0)
    down, up = _halo_shifts(x, xp, xn)
    cw = cw_ref[...]
    u = cb_ref[...] + cw[2:3, :] * x + cw[0:1, :] * down(2) + cw[1:2, :] * down(1) + cw[3:4, :] * up(1)
    if reverse:
        z = cg_ref[0].astype(F32) * u_ref[0].astype(F32)
        zp = jnp.where(prev_ok, cgp_ref[0].astype(F32) * up_ref[0].astype(F32), 0.0)
        zn = jnp.where(next_ok, cgn_ref[0].astype(F32) * un_ref[0].astype(F32), 0.0)
        zdown, zup = _halo_shifts(z, zp, zn)
        sw = sw_ref[...]
        y = sb_ref[...] + sw[1:2, :] * z + sw[0:1, :] * zdown(1) + sw[2:3, :] * zup(1)
        sc_ref[0] = (bg_ref[0].astype(F32) * y).astype(BF16)
    nlam = -lam_ref[...]
    softplus = jnp.maximum(nlam, 0.0) + jnp.log1p(jnp.exp(-jnp.abs(nlam)))

    for j in range(n_sub):
        ls = slice(j * LRU_BLOCK, (j + 1) * LRU_BLOCK)
        uj = u[:, ls]
        ub = uj.astype(BF16)
        r = jax.nn.sigmoid(jnp.dot(ub, wa_ref[j], preferred_element_type=F32) + ba_ref[:, ls])
        i = jax.nn.sigmoid(jnp.dot(ub, wx_ref[j], preferred_element_type=F32) + bx_ref[:, ls])
        a = jnp.exp((-LRU_C) * r * softplus[:, ls])
        d = jnp.sqrt(1.0 - a * a) * (i * uj)
        h, carry = _scan_tile(a, d, carry_scr[:, ls], reverse)
        carry_scr[:, ls] = carry
        if reverse:
            o_ref[0, :, ls] = ((hf_ref[0, :, ls] + h) * _gelu(g_ref[0, :, ls].astype(F32))).astype(BF16)
        else:
            o_ref[0, :, ls] = h


def _lru_call(p, conv_w, conv_b, wa, ba, wx, bx, lam, ctx_len, reverse, h_fwd=None, sc_w=None, sc_b=None):
    bsz, l, _ = p.shape
    c = conv_w.shape[1]
    tt = 256
    assert ctx_len % tt == 0 and l % tt == 0
    nt = l // tt
    n_sub = 8
    cw = n_sub * LRU_BLOCK
    nb = c // cw
    x_col0 = (N_HEADS + 2 * N_KV_HEADS) * HEAD_DIM // cw
    g_col0 = x_col0 + nb
    hb = tt // HALO
    n_hblk = l // HALO

    def tile_of(s):
        return jnp.where(s == 0, 0, nt - s) if reverse else s

    x_map = lambda b, cb, s: (b, tile_of(s), x_col0 + cb)
    prev_map = lambda b, cb, s: (b, jnp.maximum(tile_of(s) * hb - 1, 0), x_col0 + cb)
    next_map = lambda b, cb, s: (b, jnp.minimum((tile_of(s) + 1) * hb, n_hblk - 1), x_col0 + cb)
    vec_map = lambda b, cb, s: (0, cb)
    w_spec = pl.BlockSpec((n_sub, LRU_BLOCK, LRU_BLOCK), lambda b, cb, s: (cb, 0, 0))
    in_specs = [pl.BlockSpec((1, tt, cw), x_map),
                pl.BlockSpec((1, HALO, cw), prev_map),
                pl.BlockSpec((1, HALO, cw), next_map),
                pl.BlockSpec((LRU_CONV, cw), vec_map),
                pl.BlockSpec((1, cw), vec_map),
                w_spec,
                pl.BlockSpec((1, cw), vec_map),
                w_spec,
                pl.BlockSpec((1, cw), vec_map),
                pl.BlockSpec((1, cw), vec_map)]
    args = [p, p, p, conv_w, conv_b.reshape(1, c), wa.astype(BF16), ba.reshape(1, c),
            wx.astype(BF16), bx.reshape(1, c), lam.reshape(1, c)]
    out_map = lambda b, cb, s: (b, tile_of(s), cb)
    if reverse:
        in_specs += [pl.BlockSpec((1, tt, cw), out_map),
                     pl.BlockSpec((1, tt, cw), lambda b, cb, s: (b, tile_of(s), g_col0 + cb))]
        args += [h_fwd, p]
        b_col0 = g_col0 + nb
        col = lambda k: (lambda b, cb, s: (b, tile_of(s), b_col0 + k * nb + cb))
        pcol = lambda k: (lambda b, cb, s: (b, jnp.maximum(tile_of(s) * hb - 1, 0), b_col0 + k * nb + cb))
        ncol = lambda k: (lambda b, cb, s: (b, jnp.minimum((tile_of(s) + 1) * hb, n_hblk - 1),
                                            b_col0 + k * nb + cb))
        in_specs += [pl.BlockSpec((1, tt, cw), col(0)), pl.BlockSpec((1, tt, cw), col(1)),
                     pl.BlockSpec((1, tt, cw), col(2)),
                     pl.BlockSpec((1, HALO, cw), pcol(1)), pl.BlockSpec((1, HALO, cw), pcol(2)),
                     pl.BlockSpec((1, HALO, cw), ncol(1)), pl.BlockSpec((1, HALO, cw), ncol(2)),
                     pl.BlockSpec((SC_CONV, cw), vec_map), pl.BlockSpec((1, cw), vec_map)]
        args += [p, p, p, p, p, p, p, sc_w, sc_b.reshape(1, c)]
        out_dtype = BF16
    else:
        out_dtype = F32
    kern = functools.partial(_lru_kernel, reverse=reverse, tt=tt, ctx_len=ctx_len, n_sub=n_sub)
    out_spec = pl.BlockSpec((1, tt, cw), out_map)
    out_struct = jax.ShapeDtypeStruct((bsz, l, c), out_dtype)
    return pl.pallas_call(
        kern,
        grid=(bsz, nb, nt),
        in_specs=in_specs,
        out_specs=[out_spec, out_spec] if reverse else out_spec,
        out_shape=[out_struct, out_struct] if reverse else out_struct,
        scratch_shapes=[pltpu.VMEM((1, cw), F32)],
        compiler_params=_cparams(("parallel", "parallel", "arbitrary"), 32),
        name="lru_rev" if reverse else "lru_fwd",
    )(*args)


def _sconv_kernel(bg_ref, cg_ref, u_ref, cgp_ref, up_ref, cgn_ref, un_ref, w_ref, b_ref, o_ref,
                  *, tt, ctx_len, seq_len):
    ti = pl.program_id(1)
    z = cg_ref[0].astype(F32) * u_ref[0].astype(F32)
    zp = cgp_ref[0].astype(F32) * up_ref[0].astype(F32)
    zn = cgn_ref[0].astype(F32) * un_ref[0].astype(F32)
    local = lax.broadcasted_iota(jnp.int32, (tt, 1), 0)
    row = ti * tt + local
    first, last = _segment_bounds(row, ctx_len, seq_len)
    w = w_ref[...]
    y = b_ref[...] + w[1:2, :] * z
    y = y + w[0:1, :] * jnp.where(row - 1 >= first, _shift_down(z, zp, 1, local), 0.0)
    y = y + w[2:3, :] * jnp.where(row + 1 <= last, _shift_up(z, zn, 1, local, tt), 0.0)
    o_ref[0] = (bg_ref[0].astype(F32) * y).astype(BF16)


def _sconv_call(p, w, b, ctx_len):
    bsz, l, _ = p.shape
    c = w.shape[1]
    tt = _tile(l, 768, 16)
    tc = 512
    ncb = c // tc
    b_col0 = (N_HEADS * HEAD_DIM + 2 * N_KV_HEADS * HEAD_DIM + 2 * c) // tc
    c_col0 = b_col0 + ncb
    u_col0 = c_col0 + ncb
    hb = tt // HALO
    n_hblk = l // HALO
    prev = lambda i: jnp.maximum(i * hb - 1, 0)
    nxt = lambda i: jnp.minimum((i + 1) * hb, n_hblk - 1)
    kern = functools.partial(_sconv_kernel, tt=tt, ctx_len=ctx_len, seq_len=l)
    return pl.pallas_call(
        kern,
        grid=(bsz, l // tt, ncb),
        in_specs=[pl.BlockSpec((1, tt, tc), lambda b_, i, j: (b_, i, b_col0 + j)),
                  pl.BlockSpec((1, tt, tc), lambda b_, i, j: (b_, i, c_col0 + j)),
                  pl.BlockSpec((1, tt, tc), lambda b_, i, j: (b_, i, u_col0 + j)),
                  pl.BlockSpec((1, HALO, tc), lambda b_, i, j: (b_, prev(i), c_col0 + j)),
                  pl.BlockSpec((1, HALO, tc), lambda b_, i, j: (b_, prev(i), u_col0 + j)),
                  pl.BlockSpec((1, HALO, tc), lambda b_, i, j: (b_, nxt(i), c_col0 + j)),
                  pl.BlockSpec((1, HALO, tc), lambda b_, i, j: (b_, nxt(i), u_col0 + j)),
                  pl.BlockSpec((SC_CONV, tc), lambda b_, i, j: (0, j)),
                  pl.BlockSpec((1, tc), lambda b_, i, j: (0, j))],
        out_specs=pl.BlockSpec((1, tt, tc), lambda b_, i, j: (b_, i, j)),
        out_shape=jax.ShapeDtypeStruct((bsz, l, c), BF16),
        compiler_params=_cparams(("parallel", "parallel", "parallel"), 32),
        name="sconv",
    )(p, p, p, p, p, p, p, w, b.reshape(1, c))


def _merge_kernel(xa_ref, xl_ref, xs_ref, ga_ref, gl_ref, gs_ref, wa_ref, wl_ref, ws_ref, o_ref):
    y = jax.nn.sigmoid(ga_ref[0].astype(F32)) * jnp.dot(xa_ref[0], wa_ref[...], preferred_element_type=F32)
    y = y + jax.nn.sigmoid(gl_ref[0].astype(F32)) * jnp.dot(xl_ref[0], wl_ref[...], preferred_element_type=F32)
    y = y + jax.nn.sigmoid(gs_ref[0].astype(F32)) * jnp.dot(xs_ref[0], ws_ref[...], preferred_element_type=F32)
    o_ref[0] = y.astype(BF16)


def _merge_call(x_att, x_lru, x_sc, p, layer, w_att, w_lru, w_sc):
    bsz, l, d = x_att.shape
    tm = _tile(l, 768, 16)
    tn = 512
    g_col0 = (p.shape[2] - 3 * d) // tn
    nj = d // tn
    xspec = pl.BlockSpec((1, tm, d), lambda b, i, j: (b, i, 0))
    wspec = pl.BlockSpec((None, d, tn), lambda b, i, j: (layer, 0, j))
    gspec = lambda k: pl.BlockSpec((1, tm, tn), lambda b, i, j: (b, i, g_col0 + k * nj + j))
    return pl.pallas_call(
        _merge_kernel,
        grid=(bsz, l // tm, nj),
        in_specs=[xspec, xspec, xspec, gspec(0), gspec(1), gspec(2), wspec, wspec, wspec],
        out_specs=pl.BlockSpec((1, tm, tn), lambda b, i, j: (b, i, j)),
        out_shape=jax.ShapeDtypeStruct((bsz, l, d), BF16),
        compiler_params=_cparams(("parallel", "parallel", "arbitrary"), 48),
        name="merge",
    )(x_att, x_lru, x_sc, p, p, p, w_att, w_lru, w_sc)


def _outproj_kernel(y_ref, w_ref, x_ref, gate_ref, shift_ref, scale_ref, g_ref, xo_ref, ht_ref,
                    *, tm, ctx_len):
    b = pl.program_id(0)
    i = pl.program_id(1)
    row = i * tm + lax.broadcasted_iota(jnp.int32, (tm, 1), 0)
    is_ctx = row < ctx_len
    acc = jnp.dot(y_ref[0], w_ref[...], preferred_element_type=F32)
    xn = x_ref[0] + _row_select(gate_ref, b, is_ctx) * acc
    xo_ref[0] = xn
    var = jnp.mean(xn * xn, axis=-1, keepdims=True)
    h = xn * lax.rsqrt(var + EPS) * g_ref[...]
    h = h * (1.0 + _row_select(scale_ref, b, is_ctx)) + _row_select(shift_ref, b, is_ctx)
    ht_ref[0] = h.T.astype(BF16)


def _outproj_call(y, w_out, x, mod, layer, gain, ctx_len):
    bsz, l, d = x.shape
    tm = _tile(l, 384, LANES)
    kern = functools.partial(_outproj_kernel, tm=tm, ctx_len=ctx_len)
    mspec = lambda k: pl.BlockSpec((None, 8, d), lambda b, i: (layer, 0, k))
    return pl.pallas_call(
        kern,
        grid=(bsz, l // tm),
        in_specs=[pl.BlockSpec((1, tm, d), lambda b, i: (b, i, 0)),
                  pl.BlockSpec((None, d, d), lambda b, i: (layer, 0, 0)),
                  pl.BlockSpec((1, tm, d), lambda b, i: (b, i, 0)),
                  mspec(2), mspec(3), mspec(4),
                  pl.BlockSpec((1, d), lambda b, i: (0, 0))],
        out_specs=[pl.BlockSpec((1, tm, d), lambda b, i: (b, i, 0)),
                   pl.BlockSpec((1, d, tm), lambda b, i: (b, 0, i))],
        out_shape=[jax.ShapeDtypeStruct((bsz, l, d), F32),
                   jax.ShapeDtypeStruct((bsz, d, l), BF16)],
        compiler_params=_cparams(("parallel", "parallel"), 48),
        name="outproj",
    )(y, w_out, x, mod, mod, mod, gain.reshape(1, d))


_CAND_ROWS = tuple(PEER_TOPK // (i + 1) for i in range(PEER_TOPK))


def _top16(s, v_scr):
    n = s.shape[0]
    key = lax.broadcasted_iota(jnp.int32, s.shape, 0)

    def body(r, carry):
        work, rank = carry
        m = jnp.max(work, axis=0, keepdims=True)
        v_scr[pl.ds(r, 1), :] = m
        first = jnp.min(jnp.where(work == m, key, n), axis=0, keepdims=True)
        sel = key == first
        return jnp.where(sel, NEG_INF, work), jnp.where(sel, jnp.asarray(r, F32), rank)

    _, rank = lax.fori_loop(0, PEER_TOPK, body, (s, jnp.full(s.shape, float(PEER_TOPK), F32)))
    return rank


MARK = -(2.0 ** 127)


def _top16_distinct(ss, v_scrs):
    def body(r, works):
        mark = jnp.asarray(r, F32) * (MARK / 32.0) + MARK
        out = []
        for work, v_scr in zip(works, v_scrs):
            m = jnp.max(work, axis=0, keepdims=True)
            v_scr[pl.ds(r, 1), :] = m
            out.append(jnp.where(work == m, mark, work))
        return tuple(out)

    works = lax.fori_loop(0, PEER_TOPK, body, tuple(ss), unroll=True)
    return [jnp.where(w <= MARK, (MARK - w) * (-32.0 / MARK), float(PEER_TOPK)) for w in works]


def _peer_topk_kernel(ht_ref, wq_ref, k1_ref, k2_ref, cnt_ref, e1_ref, rk_ref, e2_ref,
                      q_scr, v_scr, *, tm, n_par):
    half = PEER_NKEYS
    q_scr[...] = jnp.dot(wq_ref[...], ht_ref[0], preferred_element_type=F32).astype(BF16)
    sub = lax.broadcasted_iota(jnp.int32, (SUBLANES, LANES), 0)
    big = PEER_TOPK * PEER_TOPK

    def candidates(v1, v2):
        pieces, poss = [], []
        for i in range(SUBLANES):
            for j0 in range(0, _CAND_ROWS[i], SUBLANES):
                c = v1[i:i + 1, :] + v2[j0:j0 + SUBLANES, :]
                valid = sub + j0 < _CAND_ROWS[i]
                pieces.append(jnp.where(valid, c, NEG_INF))
                poss.append(jnp.where(valid, i * PEER_TOPK + j0 + sub, big))
        pieces.append(v1[SUBLANES:, :] + v2[0:1, :])
        poss.append((sub + SUBLANES) * PEER_TOPK)
        return pieces, poss

    def pick_exact(pieces, poss):
        def pick(_, carry2):
            cs, sels = carry2
            m = functools.reduce(jnp.maximum, cs)
            m = jnp.max(m, axis=0, keepdims=True)
            cand_pos = functools.reduce(jnp.minimum, [jnp.where(c == m, p_, big) for c, p_ in zip(cs, poss)])
            first = jnp.min(cand_pos, axis=0, keepdims=True)
            hit = [p_ == first for p_ in poss]
            cs = tuple(jnp.where(hh, NEG_INF, c) for hh, c in zip(hit, cs))
            sels = tuple(jnp.where(hh, 1.0, s_) for hh, s_ in zip(hit, sels))
            return cs, sels

        zeros = tuple(jnp.zeros((SUBLANES, LANES), F32) for _ in pieces)
        return lax.fori_loop(0, PEER_TOPK, pick, (tuple(pieces), zeros))[1]

    def pick_distinct(groups):
        npc = len(groups[0])

        def pick(_, cs):
            out = []
            for g in range(len(groups)):
                grp = cs[g * npc:(g + 1) * npc]
                m = jnp.max(functools.reduce(jnp.maximum, grp), axis=0, keepdims=True)
                out.extend(jnp.where(c == m, MARK, c) for c in grp)
            return tuple(out)

        marked = lax.fori_loop(0, PEER_TOPK, pick, tuple(c for grp in groups for c in grp), unroll=True)
        return [tuple(jnp.where(c == MARK, 1.0, 0.0) for c in marked[g * npc:(g + 1) * npc])
                for g in range(len(groups))]

    def compute(h, lss, exact):
        base = pl.multiple_of(h * 2 * half, 2 * half)
        n_g = len(lss)
        s1s = [jnp.dot(k1_ref[h], q_scr[pl.ds(base, half), ls], preferred_element_type=F32) for ls in lss]
        s2s = [jnp.dot(k2_ref[h], q_scr[pl.ds(base + half, half), ls], preferred_element_type=F32)
               for ls in lss]
        v1_refs = [v_scr.at[2 * g] for g in range(n_g)]
        v2_refs = [v_scr.at[2 * g + 1] for g in range(n_g)]
        if exact:
            rank1s = [_top16(s, r) for s, r in zip(s1s, v1_refs)]
            rank2s = [_top16(s, r) for s, r in zip(s2s, v2_refs)]
        else:
            ranks = _top16_distinct(tuple(s1s + s2s), tuple(v1_refs + v2_refs))
            rank1s, rank2s = ranks[:n_g], ranks[n_g:]
        v1s = [r[...] for r in v1_refs]
        v2s = [r[...] for r in v2_refs]
        cands = [candidates(v1, v2) for v1, v2 in zip(v1s, v2s)]
        if exact:
            all_sels = [pick_exact(pieces, poss) for pieces, poss in cands]
        else:
            all_sels = pick_distinct([pieces for pieces, _ in cands])
        most = None
        for g in range(n_g):
            most_g = finish(h, lss[g], s1s[g], s2s[g], rank1s[g], rank2s[g], v1s[g], v2s[g],
                            cands[g][0], all_sels[g])
            most = most_g if most is None else jnp.maximum(most, most_g)
        return most

    def finish(h, ls, s1, s2, rank1, rank2, v1, v2, orig, sels):
        top = v1[0:1, :] + v2[0:1, :]
        zsum = functools.reduce(
            lambda a_, b_: a_ + b_,
            [jnp.where(s_ > 0.0, jnp.exp(o - top), 0.0) for s_, o in zip(sels, orig)])
        zinv = 1.0 / jnp.sum(zsum, axis=0, keepdims=True)

        counts = []
        pi = 0
        for i in range(SUBLANES):
            c = None
            for j0 in range(0, _CAND_ROWS[i], SUBLANES):
                part = jnp.sum(sels[pi], axis=0, keepdims=True)
                c = part if c is None else c + part
                pi += 1
            counts.append(c)
        tail = sels[pi]
        for i in range(SUBLANES, PEER_TOPK):
            counts.append(tail[i - SUBLANES:i - SUBLANES + 1, :])
        cnt = jnp.zeros(rank1.shape, F32)
        for i in range(PEER_TOPK):
            cnt = jnp.where(rank1 == float(i), counts[i], cnt)

        cnt_ref[0, h, :, ls] = cnt
        e1_ref[0, h, :, ls] = jnp.exp(s1 - v1[0:1, :]) * zinv
        rk_ref[0, h, :, ls] = rank2.astype(BF16)
        e2_ref[0, h, :, ls] = jnp.exp(s2 - v2[0:1, :]).astype(BF16)
        ranked = jnp.where(rank1 < PEER_TOPK, 1.0, 0.0)
        ranked = jnp.maximum(jnp.sum(ranked, axis=0, keepdims=True),
                             jnp.sum(jnp.where(rank2 < PEER_TOPK, 1.0, 0.0), axis=0, keepdims=True))
        picked = jnp.sum(functools.reduce(lambda a_, b_: a_ + b_, sels), axis=0, keepdims=True)
        return jnp.max(jnp.maximum(ranked, picked))

    n_trips = tm // (n_par * LANES)

    def head_lane_groups(idx, carry):
        h = idx // n_trips
        first = (idx % n_trips) * n_par
        lss = [pl.ds(pl.multiple_of((first + g) * LANES, LANES), LANES) for g in range(n_par)]
        most = compute(h, lss, exact=False)

        @pl.when(most > PEER_TOPK)
        def _():
            compute(h, lss, exact=True)

        return carry

    lax.fori_loop(0, PEER_HEADS * n_trips, head_lane_groups, 0)


def _peer_topk_call(ht, layer, wq_t, k1, k2):
    bsz, d, l = ht.shape
    qd = wq_t.shape[1]
    n_par = 6
    tm = _tile(l, 768, n_par * LANES)
    kern = functools.partial(_peer_topk_kernel, tm=tm, n_par=n_par)
    ospec = pl.BlockSpec((1, PEER_HEADS, PEER_NKEYS, tm), lambda b, i: (b, 0, 0, i))
    oshape = jax.ShapeDtypeStruct((bsz, PEER_HEADS, PEER_NKEYS, l), F32)
    kspec = pl.BlockSpec((None, PEER_HEADS, PEER_NKEYS, PEER_NKEYS), lambda b, i: (layer, 0, 0, 0))
    return pl.pallas_call(
        kern,
        grid=(bsz, l // tm),
        in_specs=[pl.BlockSpec((1, d, tm), lambda b, i: (b, 0, i)),
                  pl.BlockSpec((None, qd, d), lambda b, i: (layer, 0, 0)),
                  kspec, kspec],
        out_specs=[ospec, ospec, ospec, ospec],
        out_shape=[oshape, oshape, jax.ShapeDtypeStruct(oshape.shape, BF16),
                   jax.ShapeDtypeStruct(oshape.shape, BF16)],
        scratch_shapes=[pltpu.VMEM((qd, tm), BF16),
                        pltpu.VMEM((2 * n_par, PEER_TOPK, LANES), F32)],
        compiler_params=_cparams(("parallel", "parallel"), 56),
        name="peer_topk",
    )(ht, wq_t, k1, k2)


def _peer_dense_kernel(ht_ref, u_ref, vt_ref, cnt_ref, e1_ref, rk_ref, e2_ref, o_ref, wz_scr, *, n_sub):
    e = pl.program_id(2)

    @pl.when(e == 0)
    def _():
        o_ref[...] = jnp.zeros(o_ref.shape, F32)

    ht = ht_ref[0]
    per = 4
    n_split = n_sub // per
    for part in range(n_split):
        s = jnp.dot(u_ref[part * per * PEER_NKEYS:(part + 1) * per * PEER_NKEYS, :], ht,
                    preferred_element_type=F32)
        for a in range(part * per, (part + 1) * per):
            rows = slice(a * PEER_NKEYS, (a + 1) * PEER_NKEYS)
            local = slice((a - part * per) * PEER_NKEYS, (a - part * per + 1) * PEER_NKEYS)
            w = None
            for h in range(PEER_HEADS):
                cnt = cnt_ref[0, h, a:a + 1, :].astype(BF16)
                e1 = e1_ref[0, h, a:a + 1, :].astype(BF16)
                term = jnp.where(rk_ref[0, h] < cnt, e2_ref[0, h] * e1, 0.0)
                w = term if w is None else w + term
            wz_scr[rows, :] = w * _gelu(s[local, :].astype(BF16))
    o_ref[0] += jnp.dot(vt_ref[...], wz_scr[...], preferred_element_type=F32)


def _peer_dense_call(ht, layer, u_bf16, vt_bf16, cnt, e1n, rank2, e2):
    bsz, d, l = ht.shape
    n_exp = u_bf16.shape[1]
    tm = _tile(l, 768, LANES)
    n_sub = SUBLANES
    te = n_sub * PEER_NKEYS
    kern = functools.partial(_peer_dense_kernel, n_sub=n_sub)
    aspec = pl.BlockSpec((1, PEER_HEADS, n_sub, tm), lambda b, i, e: (b, 0, e, i))
    fspec = pl.BlockSpec((1, PEER_HEADS, PEER_NKEYS, tm), lambda b, i, e: (b, 0, 0, i))
    return pl.pallas_call(
        kern,
        grid=(bsz, l // tm, n_exp // te),
        in_specs=[pl.BlockSpec((1, d, tm), lambda b, i, e: (b, 0, i)),
                  pl.BlockSpec((None, te, d), lambda b, i, e: (layer, e, 0)),
                  pl.BlockSpec((None, d, te), lambda b, i, e: (layer, 0, e)),
                  aspec, aspec, fspec, fspec],
        out_specs=pl.BlockSpec((1, d, tm), lambda b, i, e: (b, 0, i)),
        out_shape=jax.ShapeDtypeStruct((bsz, d, l), F32),
        scratch_shapes=[pltpu.VMEM((te, tm), BF16)],
        compiler_params=_cparams(("parallel", "parallel", "arbitrary"), 58),
        name="peer_dense",
    )(ht, u_bf16, vt_bf16, cnt, e1n, rank2, e2)


def _resid_kernel(x_ref, yt_ref, gate_ref, o_ref, *, tm, ctx_len, first_tile):
    b = pl.program_id(0)
    i = pl.program_id(1) + first_tile
    row = i * tm + lax.broadcasted_iota(jnp.int32, (tm, 1), 0)
    o_ref[0] = x_ref[0] + _row_select(gate_ref, b, row < ctx_len) * yt_ref[0].T


def _resid_call(x, yt, mod, layer, ctx_len, latent_only):
    bsz, l, d = x.shape
    tm = _tile(math.gcd(l, ctx_len), 384, LANES)
    first_tile = ctx_len // tm if latent_only else 0
    n_tiles = l // tm - first_tile
    kern = functools.partial(_resid_kernel, tm=tm, ctx_len=ctx_len, first_tile=first_tile)
    return pl.pallas_call(
        kern,
        grid=(bsz, n_tiles),
        in_specs=[pl.BlockSpec((1, tm, d), lambda b, i: (b, i + first_tile, 0)),
                  pl.BlockSpec((1, d, tm), lambda b, i: (b, 0, i + first_tile)),
                  pl.BlockSpec((None, 8, d), lambda b, i: (layer, 0, 5))],
        out_specs=pl.BlockSpec((1, tm, d), lambda b, i: (b, i, 0)),
        out_shape=jax.ShapeDtypeStruct((bsz, n_tiles * tm, d), F32),
        compiler_params=_cparams(("parallel", "parallel"), 40),
        name="resid",
    )(x, yt, mod)


def _rope_tables(ctx_len, t_lat):
    rows = t_lat // GRID_W
    row = jnp.repeat(jnp.arange(rows, dtype=F32), GRID_W)
    col = jnp.tile(jnp.arange(GRID_W, dtype=F32), rows)
    inv = ROPE_THETA ** (-jnp.arange(ROPE_PAIRS, dtype=F32) / ROPE_PAIRS)
    ang = jnp.concatenate([row[:, None] * inv] * 2 + [col[:, None] * inv] * 2, axis=1)
    ang = jnp.concatenate([jnp.zeros((ctx_len, HEAD_DIM), F32), ang], axis=0)
    sign = jnp.where((jnp.arange(HEAD_DIM) & ROPE_PAIRS) == 0, -1.0, 1.0).astype(F32)
    return jnp.cos(ang), jnp.sin(ang) * sign


def kernel(x, c, ctx, c_ctx, w_mod, b_mod, norm_mix, norm_ffn, w_in, q_norm, k_norm, lru_conv_w, lru_conv_b, lru_wa, lru_ba, lru_wx, lru_bx, lru_lambda, sc_conv_w, sc_conv_b, w_o_attn, w_o_lru, w_o_sc, w_out, peer_wq, peer_k1, peer_k2, peer_u, peer_v):
    bsz, t_lat, d = x.shape
    ctx_len = ctx.shape[1]
    depth = w_mod.shape[0]
    assert bsz == 2, "modulation rows are laid out as [latent 0, latent 1, context]"

    xs = jnp.concatenate([ctx, x], axis=1)
    s8 = jnp.concatenate([c, c_ctx[None, :], jnp.zeros((8 - bsz - 1, d), F32)], axis=0)
    mod = _mod_call(s8, w_mod, b_mod)
    cos, sin_signed = _rope_tables(ctx_len, t_lat)

    w_in_b = w_in.astype(BF16)
    w_att_b, w_lru_b, w_sc_b = w_o_attn.astype(BF16), w_o_lru.astype(BF16), w_o_sc.astype(BF16)
    w_out_b = w_out.astype(BF16)
    wq_t = jnp.swapaxes(peer_wq, 1, 2).astype(BF16)
    k1_b, k2_b = peer_k1.astype(BF16), peer_k2.astype(BF16)
    u_b = peer_u.astype(BF16)
    vt_b = jnp.swapaxes(peer_v, 1, 2).astype(BF16)

    for l in range(depth):
        p = _inproj_call(xs, mod, l, norm_mix[l], w_in_b, ctx_len)
        qn, ke, ve = _qkprep_call(p, cos, sin_signed, q_norm[l], k_norm[l])
        x_att = _attn_call(qn, ke, ve, q_norm[l], k_norm[l], ctx_len)
        lru_args = (lru_conv_w[l], lru_conv_b[l])
        h_fwd = _lru_call(p, *lru_args, lru_wa[l, 0], lru_ba[l, 0], lru_wx[l, 0], lru_bx[l, 0],
                          lru_lambda[l, 0], ctx_len, reverse=False)
        x_lru, x_sc = _lru_call(p, *lru_args, lru_wa[l, 1], lru_ba[l, 1], lru_wx[l, 1], lru_bx[l, 1],
                                lru_lambda[l, 1], ctx_len, reverse=True, h_fwd=h_fwd,
                                sc_w=sc_conv_w[l], sc_b=sc_conv_b[l])
        y = _merge_call(x_att, x_lru, x_sc, p, l, w_att_b, w_lru_b, w_sc_b)
        xs, ht = _outproj_call(y, w_out_b, xs, mod, l, norm_ffn[l], ctx_len)
        cnt, e1n, rank2, e2 = _peer_topk_call(ht, l, wq_t, k1_b, k2_b)
        yt = _peer_dense_call(ht, l, u_b, vt_b, cnt, e1n, rank2, e2)
        xs = _resid_call(xs, yt, mod, l, ctx_len, latent_only=(l == depth - 1))
    return xs
```
